```python
import math
import jax
import jax.numpy as jnp
from jax import lax
import numpy as np

D_MODEL = 1024
BATCH = 8
SEQ = 16384
DEPTH = 2

N_EVEN = (DEPTH + 1) // 2
N_ODD = DEPTH // 2

SSD_HEADS = 16
SSD_HEAD_DIM = 64
SSD_INNER = SSD_HEADS * SSD_HEAD_DIM
SSD_GROUPS = 4
SSD_STATE = 128
SSD_CONV = 4
SSD_CONV_PAD = (2, 1)
SSD_CHUNK = 128
SSD_XBC = SSD_INNER + 2 * SSD_GROUPS * SSD_STATE
SC_WIDTH = 1024
SC_CONV = 3
MLA_HEADS = 8
MLA_Q_RANK = 256
MLA_KV_RANK = 128
MLA_NOPE = 64
MLA_ROPE = 32
MLA_V = 64
MLA_WIDTH = MLA_HEADS * MLA_V
ROPE_THETA = 10000.0
ATTN_BLOCK = 128
ATTN_SCALE = (MLA_NOPE + MLA_ROPE) ** -0.5
POOL_WINDOWS = (2, 4, 8, 16)
POOL_GROUP = 128
POOL_WIDTH = POOL_GROUP * len(POOL_WINDOWS)
EPS = 1e-5
ALPHA = (2 * DEPTH) ** 0.25
BETA = (8 * DEPTH) ** -0.25

EVEN_SIZES = (SSD_INNER, SSD_XBC, 2 * SSD_HEADS, SC_WIDTH, SC_WIDTH, SC_WIDTH, SC_WIDTH)
ODD_SIZES = (MLA_Q_RANK, MLA_KV_RANK, MLA_ROPE, MLA_WIDTH, POOL_WIDTH, POOL_WIDTH)
EVEN_PROJ = sum(EVEN_SIZES)
ODD_PROJ = sum(ODD_SIZES)
EVEN_OUT = SSD_INNER + SC_WIDTH
ODD_OUT = MLA_WIDTH + POOL_WIDTH

kernel_name = 'hybrid_ssd_shortconv_mla_pool_encoder'

F32 = jnp.float32


def _split(t, sizes):
    return jnp.split(t, [int(s) for s in np.cumsum(sizes)[:-1]], axis=-1)


def _layer_norm(x, g, b):
    xf = x.astype(F32)
    mu = jnp.mean(xf, -1, keepdims=True)
    var = jnp.mean(jnp.square(xf - mu), -1, keepdims=True)
    return ((xf - mu) * lax.rsqrt(var + EPS) * g + b).astype(x.dtype)


def _rms_norm(x, g):
    xf = x.astype(F32)
    return (xf * lax.rsqrt(jnp.mean(xf * xf, -1, keepdims=True) + EPS) * g).astype(x.dtype)


def _depthwise_conv(u, w, pad):
    return lax.conv_general_dilated(
        u, w[:, None, :].astype(u.dtype), window_strides=(1,), padding=(pad,),
        dimension_numbers=('NWC', 'WIO', 'NWC'), feature_group_count=u.shape[-1])


def _segsum_exp(cs):
    q = cs.shape[-1]
    mask = jnp.tril(jnp.ones((q, q), bool))
    return jnp.exp(jnp.where(mask, cs[..., :, None] - cs[..., None, :], -jnp.inf))


def _ssd_chunked(x, dt, a, bm, cm):
    b, S, H, P = x.shape
    G, N = bm.shape[2], bm.shape[3]
    R = H // G
    nc, Q = S // SSD_CHUNK, SSD_CHUNK
    xd = (x.astype(F32) * dt[..., None]).reshape(b, nc, Q, G, R, P)
    cs = jnp.cumsum(jnp.transpose((dt * a).reshape(b, nc, Q, G, R), (0, 1, 3, 4, 2)), axis=-1)
    bq = bm.astype(F32).reshape(b, nc, Q, G, N)
    cq = cm.astype(F32).reshape(b, nc, Q, G, N)
    cb = jnp.einsum('bclgn,bcsgn->bcgls', cq, bq)
    y_diag = jnp.einsum('bcgls,bcgrls,bcsgrp->bclgrp', cb, _segsum_exp(cs), xd)
    decay_states = jnp.exp(cs[..., -1:] - cs)
    states = jnp.einsum('bclgn,bcgrl,bclgrp->bcgrpn', bq, decay_states, xd)
    chunk_tot = cs[..., -1]

    def step(h, inp):
        st, tot = inp
        return h * jnp.exp(tot)[..., None, None] + st, h

    h0 = jnp.zeros((b, G, R, P, N), F32)
    _, h_in = lax.scan(step, h0, (jnp.moveaxis(states, 1, 0), jnp.moveaxis(chunk_tot, 1, 0)))
    h_in = jnp.moveaxis(h_in, 0, 1)
    y_off = jnp.einsum('bclgn,bcgrpn,bcgrl->bclgrp', cq, h_in, jnp.exp(cs))
    return (y_diag + y_off).reshape(b, S, H, P)


def _ssd_bidirectional(xs, bm, cm, dt_raw, a_log, dt_bias, d_skip):
    def one_dir(x_, b_, c_, dtr, k):
        dt = jax.nn.softplus(dtr.astype(F32) + dt_bias[k].astype(F32))
        a = -jnp.exp(a_log[k].astype(F32))
        return _ssd_chunked(x_, dt, a, b_, c_) + d_skip[k].astype(F32)[:, None] * x_.astype(F32)

    flip = lambda t: jnp.flip(t, axis=1)
    y_f = one_dir(xs, bm, cm, dt_raw[:, :, 0], 0)
    y_b = flip(one_dir(flip(xs), flip(bm), flip(cm), flip(dt_raw[:, :, 1]), 1))
    return y_f + y_b


def _even_layer(x, w_in, conv_w, conv_b, a_log, dt_bias, d_skip, norm_g, sc_conv_w, w_out):
    b, S, _ = x.shape
    proj = x @ w_in
    z, xbc, dt_raw, sc_bg, sc_cg, sc_h, sc_gate = _split(proj, EVEN_SIZES)
    xbc = jax.nn.silu(_depthwise_conv(xbc, conv_w, SSD_CONV_PAD) + conv_b)
    xs, bm, cm = _split(xbc, (SSD_INNER, SSD_GROUPS * SSD_STATE, SSD_GROUPS * SSD_STATE))
    y_a = _ssd_bidirectional(
        xs.reshape(b, S, SSD_HEADS, SSD_HEAD_DIM),
        bm.reshape(b, S, SSD_GROUPS, SSD_STATE), cm.reshape(b, S, SSD_GROUPS, SSD_STATE),
        dt_raw.reshape(b, S, 2, SSD_HEADS), a_log, dt_bias, d_skip)
    y_a = _rms_norm(y_a.reshape(b, S, SSD_INNER).astype(x.dtype) * jax.nn.silu(z), norm_g)
    y_b = sc_bg * _depthwise_conv(sc_cg * sc_h, sc_conv_w, (1, 1)) * jax.nn.silu(sc_gate)
    return jnp.concatenate([y_a, y_b], axis=-1) @ w_out


def _rope(t, cos, sin):
    half = t.shape[-1] // 2
    t1, t2 = t[..., :half].astype(F32), t[..., half:].astype(F32)
    return jnp.concatenate([t1 * cos - t2 * sin, t1 * sin + t2 * cos], axis=-1).astype(t.dtype)


def _mla_attention(q_nope, q_rope, k_nope, k_rope, v):
    b, S, H, _ = q_nope.shape
    nb = S // ATTN_BLOCK
    to_blocks = lambda t: jnp.moveaxis(t.reshape((b, nb, ATTN_BLOCK) + t.shape[2:]), 1, 0)

    def block(qs):
        qn, qr = qs
        s = (jnp.einsum('bqhd,bkhd->bhqk', qn, k_nope, preferred_element_type=F32)
             + jnp.einsum('bqhr,bkr->bhqk', qr, k_rope, preferred_element_type=F32)) * ATTN_SCALE
        p = jax.nn.softmax(s, axis=-1).astype(v.dtype)
        return jnp.einsum('bhqk,bkhd->bqhd', p, v)

    o = lax.map(block, (to_blocks(q_nope), to_blocks(q_rope)))
    return jnp.moveaxis(o, 0, 1).reshape(b, S, H * MLA_V)


def _multiscale_pool(u):
    S = u.shape[1]
    cs = jnp.pad(jnp.cumsum(u.astype(F32), axis=1), ((0, 0), (1, 0), (0, 0)))
    pos = jnp.arange(S)
    outs = []
    for gi, w in enumerate(POOL_WINDOWS):
        lo = jnp.clip(pos - w // 2, 0, S)
        hi = jnp.clip(pos + w - w // 2, 0, S)
        cg = cs[..., gi * POOL_GROUP:(gi + 1) * POOL_GROUP]
        cnt = (hi - lo).astype(F32)[None, :, None]
        outs.append((jnp.take(cg, hi, axis=1) - jnp.take(cg, lo, axis=1)) / cnt)
    return (jnp.concatenate(outs, axis=-1) - u.astype(F32)).astype(u.dtype)


def _odd_layer(x, positions, w_in, q_norm_g, w_uq, kv_norm_g, w_ukv, pool_w, pool_scale, w_out):
    b, S, _ = x.shape
    proj = x @ w_in
    cq, ckv, k_rope, gate_c, u_d, gate_d = _split(proj, ODD_SIZES)
    q = (_rms_norm(cq, q_norm_g) @ w_uq).reshape(b, S, MLA_HEADS, MLA_NOPE + MLA_ROPE)
    kv = (_rms_norm(ckv, kv_norm_g) @ w_ukv).reshape(b, S, MLA_HEADS, MLA_NOPE + MLA_V)
    q_nope, q_rope = q[..., :MLA_NOPE], q[..., MLA_NOPE:]
    k_nope, v = kv[..., :MLA_NOPE], kv[..., MLA_NOPE:]
    half = MLA_ROPE // 2
    inv_freq = ROPE_THETA ** (-jnp.arange(half, dtype=F32) / half)
    ang = positions.astype(F32)[..., None] * inv_freq
    cos, sin = jnp.cos(ang), jnp.sin(ang)
    q_rope = _rope(q_rope, cos[:, :, None, :], sin[:, :, None, :])
    k_rope = _rope(k_rope, cos, sin)
    y_c = _mla_attention(q_nope, q_rope, k_nope, k_rope, v) * jax.nn.silu(gate_c)
    pooled = _multiscale_pool(u_d).reshape(b, S, len(POOL_WINDOWS), POOL_GROUP)
    y_d = jnp.einsum('bsgc,gcd->bsgd', pooled, pool_w).reshape(b, S, POOL_WIDTH)
    y_d = y_d * pool_scale * jax.nn.silu(gate_d)
    return jnp.concatenate([y_c, y_d], axis=-1) @ w_out


def _fwd_setup_inputs(seed: int = 0) -> dict:
    key = jax.random.key(seed)
    ks = iter(jax.random.split(key, 32))
    nrm = lambda shape, scale: jax.random.normal(next(ks), shape, F32) * scale
    NE, NO = N_EVEN, N_ODD
    x = jax.random.normal(next(ks), (BATCH, SEQ, D_MODEL), F32)
    positions = (jnp.arange(SEQ, dtype=jnp.int32)[None, :]
                 + jax.random.randint(next(ks), (BATCH, 1), 0, 4096, dtype=jnp.int32))
    dt0 = jnp.exp(jax.random.uniform(next(ks), (NE, 2, SSD_HEADS), F32, math.log(1e-3), math.log(1e-1)))
    ev_dt_bias = dt0 + jnp.log(-jnp.expm1(-dt0))
    ev_a_log = jnp.log(jax.random.uniform(next(ks), (NE, 2, SSD_HEADS), F32, 1.0, 16.0))
    return {
        'x': x,
        'positions': positions,
        'ev_w_in': nrm((NE, D_MODEL, EVEN_PROJ), D_MODEL ** -0.5),
        'ev_conv_w': nrm((NE, SSD_CONV, SSD_XBC), SSD_CONV ** -0.5),
        'ev_conv_b': nrm((NE, SSD_XBC), 0.02),
        'ev_a_log': ev_a_log,
        'ev_dt_bias': ev_dt_bias,
        'ev_d_skip': 1.0 + nrm((NE, 2, SSD_HEADS), 0.05),
        'ev_norm_g': 1.0 + nrm((NE, SSD_INNER), 0.02),
        'ev_sc_conv_w': nrm((NE, SC_CONV, SC_WIDTH), SC_CONV ** -0.5),
        'ev_w_out': nrm((NE, EVEN_OUT, D_MODEL), BETA * EVEN_OUT ** -0.5),
        'ev_ln_g': 1.0 + nrm((NE, D_MODEL), 0.02),
        'ev_ln_b': nrm((NE, D_MODEL), 0.02),
        'od_w_in': nrm((NO, D_MODEL, ODD_PROJ), D_MODEL ** -0.5),
        'od_q_norm_g': 1.0 + nrm((NO, MLA_Q_RANK), 0.02),
        'od_w_uq': nrm((NO, MLA_Q_RANK, MLA_HEADS * (MLA_NOPE + MLA_ROPE)), MLA_Q_RANK ** -0.5),
        'od_kv_norm_g': 1.0 + nrm((NO, MLA_KV_RANK), 0.02),
        'od_w_ukv': nrm((NO, MLA_KV_RANK, MLA_HEADS * (MLA_NOPE + MLA_V)), MLA_KV_RANK ** -0.5),
        'od_pool_w': nrm((NO, len(POOL_WINDOWS), POOL_GROUP, POOL_GROUP), POOL_GROUP ** -0.5),
        'od_pool_scale': 1.0 + nrm((NO, POOL_WIDTH), 0.02),
        'od_w_out': nrm((NO, ODD_OUT, D_MODEL), BETA * ODD_OUT ** -0.5),
        'od_ln_g': 1.0 + nrm((NO, D_MODEL), 0.02),
        'od_ln_b': nrm((NO, D_MODEL), 0.02),
    }


def _fwd_reference(x, positions, ev_w_in, ev_conv_w, ev_conv_b, ev_a_log, ev_dt_bias, ev_d_skip,
              ev_norm_g, ev_sc_conv_w, ev_w_out, ev_ln_g, ev_ln_b, od_w_in, od_q_norm_g, od_w_uq,
              od_kv_norm_g, od_w_ukv, od_pool_w, od_pool_scale, od_w_out, od_ln_g, od_ln_b):
    for layer in range(DEPTH):
        i = layer // 2
        if layer % 2 == 0:
            h = _even_layer(x, ev_w_in[i], ev_conv_w[i], ev_conv_b[i], ev_a_log[i], ev_dt_bias[i],
                            ev_d_skip[i], ev_norm_g[i], ev_sc_conv_w[i], ev_w_out[i])
            x = _layer_norm(ALPHA * x + h, ev_ln_g[i], ev_ln_b[i])
        else:
            h = _odd_layer(x, positions, od_w_in[i], od_q_norm_g[i], od_w_uq[i], od_kv_norm_g[i],
                           od_w_ukv[i], od_pool_w[i], od_pool_scale[i], od_w_out[i])
            x = _layer_norm(ALPHA * x + h, od_ln_g[i], od_ln_b[i])
    return x


import jax as _jax
import jax.numpy as _jnp

TWIN_FORMAT = 'train_step'
FWD_PARAMS = ['x', 'positions', 'ev_w_in', 'ev_conv_w', 'ev_conv_b', 'ev_a_log', 'ev_dt_bias', 'ev_d_skip', 'ev_norm_g', 'ev_sc_conv_w', 'ev_w_out', 'ev_ln_g', 'ev_ln_b', 'od_w_in', 'od_q_norm_g', 'od_w_uq', 'od_kv_norm_g', 'od_w_ukv', 'od_pool_w', 'od_pool_scale', 'od_w_out', 'od_ln_g', 'od_ln_b']
TWIN_WEIGHTS = ['ev_w_in', 'ev_conv_w', 'ev_conv_b', 'ev_a_log', 'ev_dt_bias', 'ev_d_skip', 'ev_norm_g', 'ev_sc_conv_w', 'ev_w_out', 'ev_ln_g', 'ev_ln_b', 'od_w_in', 'od_q_norm_g', 'od_w_uq', 'od_kv_norm_g', 'od_w_ukv', 'od_pool_w', 'od_pool_scale', 'od_w_out', 'od_ln_g', 'od_ln_b']
TWIN_DIFF_INPUT = 'x'
TWIN_INPUTS = ['x', 'positions', 'ev_w_in', 'ev_conv_w', 'ev_conv_b', 'ev_a_log', 'ev_dt_bias', 'ev_d_skip', 'ev_norm_g', 'ev_sc_conv_w', 'ev_w_out', 'ev_ln_g', 'ev_ln_b', 'od_w_in', 'od_q_norm_g', 'od_w_uq', 'od_kv_norm_g', 'od_w_ukv', 'od_pool_w', 'od_pool_scale', 'od_w_out', 'od_ln_g', 'od_ln_b', 'loss_target', 'm_ev_w_in', 'm_ev_conv_w', 'm_ev_conv_b', 'm_ev_a_log', 'm_ev_dt_bias', 'm_ev_d_skip', 'm_ev_norm_g', 'm_ev_sc_conv_w', 'm_ev_w_out', 'm_ev_ln_g', 'm_ev_ln_b', 'm_od_w_in', 'm_od_q_norm_g', 'm_od_w_uq', 'm_od_kv_norm_g', 'm_od_w_ukv', 'm_od_pool_w', 'm_od_pool_scale', 'm_od_w_out', 'm_od_ln_g', 'm_od_ln_b', 'v_ev_w_in', 'v_ev_conv_w', 'v_ev_conv_b', 'v_ev_a_log', 'v_ev_dt_bias', 'v_ev_d_skip', 'v_ev_norm_g', 'v_ev_sc_conv_w', 'v_ev_w_out', 'v_ev_ln_g', 'v_ev_ln_b', 'v_od_w_in', 'v_od_q_norm_g', 'v_od_w_uq', 'v_od_kv_norm_g', 'v_od_w_ukv', 'v_od_pool_w', 'v_od_pool_scale', 'v_od_w_out', 'v_od_ln_g', 'v_od_ln_b']
TWIN_OUTPUTS = ['loss', 'grad_x', 'grad_ev_w_in', 'grad_ev_conv_w', 'grad_ev_conv_b', 'grad_ev_a_log', 'grad_ev_dt_bias', 'grad_ev_d_skip', 'grad_ev_norm_g', 'grad_ev_sc_conv_w', 'grad_ev_w_out', 'grad_ev_ln_g', 'grad_ev_ln_b', 'grad_od_w_in', 'grad_od_q_norm_g', 'grad_od_w_uq', 'grad_od_kv_norm_g', 'grad_od_w_ukv', 'grad_od_pool_w', 'grad_od_pool_scale', 'grad_od_w_out', 'grad_od_ln_g', 'grad_od_ln_b', 'delta_ev_w_in', 'delta_ev_conv_w', 'delta_ev_conv_b', 'delta_ev_a_log', 'delta_ev_dt_bias', 'delta_ev_d_skip', 'delta_ev_norm_g', 'delta_ev_sc_conv_w', 'delta_ev_w_out', 'delta_ev_ln_g', 'delta_ev_ln_b', 'delta_od_w_in', 'delta_od_q_norm_g', 'delta_od_w_uq', 'delta_od_kv_norm_g', 'delta_od_w_ukv', 'delta_od_pool_w', 'delta_od_pool_scale', 'delta_od_w_out', 'delta_od_ln_g', 'delta_od_ln_b', 'new_m_ev_w_in', 'new_m_ev_conv_w', 'new_m_ev_conv_b', 'new_m_ev_a_log', 'new_m_ev_dt_bias', 'new_m_ev_d_skip', 'new_m_ev_norm_g', 'new_m_ev_sc_conv_w', 'new_m_ev_w_out', 'new_m_ev_ln_g', 'new_m_ev_ln_b', 'new_m_od_w_in', 'new_m_od_q_norm_g', 'new_m_od_w_uq', 'new_m_od_kv_norm_g', 'new_m_od_w_ukv', 'new_m_od_pool_w', 'new_m_od_pool_scale', 'new_m_od_w_out', 'new_m_od_ln_g', 'new_m_od_ln_b', 'new_v_ev_w_in', 'new_v_ev_conv_w', 'new_v_ev_conv_b', 'new_v_ev_a_log', 'new_v_ev_dt_bias', 'new_v_ev_d_skip', 'new_v_ev_norm_g', 'new_v_ev_sc_conv_w', 'new_v_ev_w_out', 'new_v_ev_ln_g', 'new_v_ev_ln_b', 'new_v_od_w_in', 'new_v_od_q_norm_g', 'new_v_od_w_uq', 'new_v_od_kv_norm_g', 'new_v_od_w_ukv', 'new_v_od_pool_w', 'new_v_od_pool_scale', 'new_v_od_w_out', 'new_v_od_ln_g', 'new_v_od_ln_b']
TWIN_LEAF_KINDS = {'loss': 'loss', 'grad_x': 'grad_x', 'grad_ev_w_in': 'grad_w', 'grad_ev_conv_w': 'grad_w', 'grad_ev_conv_b': 'grad_w', 'grad_ev_a_log': 'grad_w', 'grad_ev_dt_bias': 'grad_w', 'grad_ev_d_skip': 'grad_w', 'grad_ev_norm_g': 'grad_w', 'grad_ev_sc_conv_w': 'grad_w', 'grad_ev_w_out': 'grad_w', 'grad_ev_ln_g': 'grad_w', 'grad_ev_ln_b': 'grad_w', 'grad_od_w_in': 'grad_w', 'grad_od_q_norm_g': 'grad_w', 'grad_od_w_uq': 'grad_w', 'grad_od_kv_norm_g': 'grad_w', 'grad_od_w_ukv': 'grad_w', 'grad_od_pool_w': 'grad_w', 'grad_od_pool_scale': 'grad_w', 'grad_od_w_out': 'grad_w', 'grad_od_ln_g': 'grad_w', 'grad_od_ln_b': 'grad_w', 'delta_ev_w_in': 'delta_w', 'delta_ev_conv_w': 'delta_w', 'delta_ev_conv_b': 'delta_w', 'delta_ev_a_log': 'delta_w', 'delta_ev_dt_bias': 'delta_w', 'delta_ev_d_skip': 'delta_w', 'delta_ev_norm_g': 'delta_w', 'delta_ev_sc_conv_w': 'delta_w', 'delta_ev_w_out': 'delta_w', 'delta_ev_ln_g': 'delta_w', 'delta_ev_ln_b': 'delta_w', 'delta_od_w_in': 'delta_w', 'delta_od_q_norm_g': 'delta_w', 'delta_od_w_uq': 'delta_w', 'delta_od_kv_norm_g': 'delta_w', 'delta_od_w_ukv': 'delta_w', 'delta_od_pool_w': 'delta_w', 'delta_od_pool_scale': 'delta_w', 'delta_od_w_out': 'delta_w', 'delta_od_ln_g': 'delta_w', 'delta_od_ln_b': 'delta_w', 'new_m_ev_w_in': 'new_m', 'new_m_ev_conv_w': 'new_m', 'new_m_ev_conv_b': 'new_m', 'new_m_ev_a_log': 'new_m', 'new_m_ev_dt_bias': 'new_m', 'new_m_ev_d_skip': 'new_m', 'new_m_ev_norm_g': 'new_m', 'new_m_ev_sc_conv_w': 'new_m', 'new_m_ev_w_out': 'new_m', 'new_m_ev_ln_g': 'new_m', 'new_m_ev_ln_b': 'new_m', 'new_m_od_w_in': 'new_m', 'new_m_od_q_norm_g': 'new_m', 'new_m_od_w_uq': 'new_m', 'new_m_od_kv_norm_g': 'new_m', 'new_m_od_w_ukv': 'new_m', 'new_m_od_pool_w': 'new_m', 'new_m_od_pool_scale': 'new_m', 'new_m_od_w_out': 'new_m', 'new_m_od_ln_g': 'new_m', 'new_m_od_ln_b': 'new_m', 'new_v_ev_w_in': 'new_v', 'new_v_ev_conv_w': 'new_v', 'new_v_ev_conv_b': 'new_v', 'new_v_ev_a_log': 'new_v', 'new_v_ev_dt_bias': 'new_v', 'new_v_ev_d_skip': 'new_v', 'new_v_ev_norm_g': 'new_v', 'new_v_ev_sc_conv_w': 'new_v', 'new_v_ev_w_out': 'new_v', 'new_v_ev_ln_g': 'new_v', 'new_v_ev_ln_b': 'new_v', 'new_v_od_w_in': 'new_v', 'new_v_od_q_norm_g': 'new_v', 'new_v_od_w_uq': 'new_v', 'new_v_od_kv_norm_g': 'new_v', 'new_v_od_w_ukv': 'new_v', 'new_v_od_pool_w': 'new_v', 'new_v_od_pool_scale': 'new_v', 'new_v_od_w_out': 'new_v', 'new_v_od_ln_g': 'new_v', 'new_v_od_ln_b': 'new_v'}


def _forward(args):
    return _fwd_reference(*[args[k] for k in FWD_PARAMS])


def _output_shape():
    def fwd():
        inp = _fwd_setup_inputs(0)
        return _fwd_reference(*[inp[k] for k in FWD_PARAMS])
    out = _jax.eval_shape(fwd)
    return out.shape, out.dtype

N_MICROBATCH = 1
ADAM_LR = 0.001
ADAM_B1 = 0.9
ADAM_B2 = 0.999
ADAM_EPS = 1e-08
ADAM_WD = 0.01
ADAM_STEP = 10
PER_EXAMPLE_BATCH_AXIS = {'x': 0, 'positions': 0, 'loss_target': 0}
SHARED_INPUTS = []
_WEIGHT_DTYPES = {'ev_w_in': _jnp.float32, 'ev_conv_w': _jnp.float32, 'ev_conv_b': _jnp.float32, 'ev_a_log': _jnp.float32, 'ev_dt_bias': _jnp.float32, 'ev_d_skip': _jnp.float32, 'ev_norm_g': _jnp.float32, 'ev_sc_conv_w': _jnp.float32, 'ev_w_out': _jnp.float32, 'ev_ln_g': _jnp.float32, 'ev_ln_b': _jnp.float32, 'od_w_in': _jnp.float32, 'od_q_norm_g': _jnp.float32, 'od_w_uq': _jnp.float32, 'od_kv_norm_g': _jnp.float32, 'od_w_ukv': _jnp.float32, 'od_pool_w': _jnp.float32, 'od_pool_scale': _jnp.float32, 'od_w_out': _jnp.float32, 'od_ln_g': _jnp.float32, 'od_ln_b': _jnp.float32}
MOMENT_SCALE = {'ev_w_in': 6.157596e-02, 'ev_conv_w': 6.548710e-02, 'ev_conv_b': 1.401459e-01, 'ev_a_log': 1.959568e-01, 'ev_dt_bias': 1.158972e-01, 'ev_d_skip': 2.722170e-01, 'ev_norm_g': 8.905528e-02, 'ev_sc_conv_w': 5.115728e-02, 'ev_w_out': 2.270971e-01, 'ev_ln_g': 4.291048e+00, 'ev_ln_b': 1.852620e+00, 'od_w_in': 4.815582e-02, 'od_q_norm_g': 1.238747e-02, 'od_w_uq': 7.535818e-03, 'od_kv_norm_g': 2.652763e-02, 'od_w_ukv': 9.042873e-03, 'od_pool_w': 6.499953e-02, 'od_pool_scale': 6.459389e-02, 'od_w_out': 9.245421e-02, 'od_ln_g': 1.282018e+02, 'od_ln_b': 6.015477e+00}


def _to_microbatches(a, axis):
    t = _jnp.moveaxis(a, axis, 0)
    t = t.reshape((N_MICROBATCH, t.shape[0] // N_MICROBATCH) + t.shape[1:])
    return _jnp.moveaxis(t, 1, axis + 1)


def setup_inputs(seed: int = 0) -> dict:
    inp = _fwd_setup_inputs(seed)
    key = _jax.random.fold_in(_jax.random.key(seed), 7919)
    shape, _ = _output_shape()
    out = dict(inp)
    out["loss_target"] = _jax.random.normal(_jax.random.fold_in(key, 0), shape, _jnp.float32)
    for i, name in enumerate(TWIN_WEIGHTS):
        w = inp[name].astype(_jnp.float32)
        if MOMENT_SCALE is None:
            s = _jnp.sqrt(_jnp.mean(_jnp.square(w)) + 1e-30)
        else:
            s = MOMENT_SCALE[name]
        km, kv = _jax.random.split(_jax.random.fold_in(key, i + 1))
        out[name] = w
        out["m_" + name] = s * _jax.random.normal(km, w.shape, _jnp.float32)
        out["v_" + name] = (s * s) * _jax.random.uniform(kv, w.shape, _jnp.float32, 0.5, 1.5)
    if N_MICROBATCH > 1:
        for name, axis in PER_EXAMPLE_BATCH_AXIS.items():
            out[name] = _to_microbatches(out[name], axis)
    return {'x': out['x'], 'positions': out['positions'], 'ev_w_in': out['ev_w_in'], 'ev_conv_w': out['ev_conv_w'], 'ev_conv_b': out['ev_conv_b'], 'ev_a_log': out['ev_a_log'], 'ev_dt_bias': out['ev_dt_bias'], 'ev_d_skip': out['ev_d_skip'], 'ev_norm_g': out['ev_norm_g'], 'ev_sc_conv_w': out['ev_sc_conv_w'], 'ev_w_out': out['ev_w_out'], 'ev_ln_g': out['ev_ln_g'], 'ev_ln_b': out['ev_ln_b'], 'od_w_in': out['od_w_in'], 'od_q_norm_g': out['od_q_norm_g'], 'od_w_uq': out['od_w_uq'], 'od_kv_norm_g': out['od_kv_norm_g'], 'od_w_ukv': out['od_w_ukv'], 'od_pool_w': out['od_pool_w'], 'od_pool_scale': out['od_pool_scale'], 'od_w_out': out['od_w_out'], 'od_ln_g': out['od_ln_g'], 'od_ln_b': out['od_ln_b'], 'loss_target': out['loss_target'], 'm_ev_w_in': out['m_ev_w_in'], 'm_ev_conv_w': out['m_ev_conv_w'], 'm_ev_conv_b': out['m_ev_conv_b'], 'm_ev_a_log': out['m_ev_a_log'], 'm_ev_dt_bias': out['m_ev_dt_bias'], 'm_ev_d_skip': out['m_ev_d_skip'], 'm_ev_norm_g': out['m_ev_norm_g'], 'm_ev_sc_conv_w': out['m_ev_sc_conv_w'], 'm_ev_w_out': out['m_ev_w_out'], 'm_ev_ln_g': out['m_ev_ln_g'], 'm_ev_ln_b': out['m_ev_ln_b'], 'm_od_w_in': out['m_od_w_in'], 'm_od_q_norm_g': out['m_od_q_norm_g'], 'm_od_w_uq': out['m_od_w_uq'], 'm_od_kv_norm_g': out['m_od_kv_norm_g'], 'm_od_w_ukv': out['m_od_w_ukv'], 'm_od_pool_w': out['m_od_pool_w'], 'm_od_pool_scale': out['m_od_pool_scale'], 'm_od_w_out': out['m_od_w_out'], 'm_od_ln_g': out['m_od_ln_g'], 'm_od_ln_b': out['m_od_ln_b'], 'v_ev_w_in': out['v_ev_w_in'], 'v_ev_conv_w': out['v_ev_conv_w'], 'v_ev_conv_b': out['v_ev_conv_b'], 'v_ev_a_log': out['v_ev_a_log'], 'v_ev_dt_bias': out['v_ev_dt_bias'], 'v_ev_d_skip': out['v_ev_d_skip'], 'v_ev_norm_g': out['v_ev_norm_g'], 'v_ev_sc_conv_w': out['v_ev_sc_conv_w'], 'v_ev_w_out': out['v_ev_w_out'], 'v_ev_ln_g': out['v_ev_ln_g'], 'v_ev_ln_b': out['v_ev_ln_b'], 'v_od_w_in': out['v_od_w_in'], 'v_od_q_norm_g': out['v_od_q_norm_g'], 'v_od_w_uq': out['v_od_w_uq'], 'v_od_kv_norm_g': out['v_od_kv_norm_g'], 'v_od_w_ukv': out['v_od_w_ukv'], 'v_od_pool_w': out['v_od_pool_w'], 'v_od_pool_scale': out['v_od_pool_scale'], 'v_od_w_out': out['v_od_w_out'], 'v_od_ln_g': out['v_od_ln_g'], 'v_od_ln_b': out['v_od_ln_b']}


def _loss(weights, diff, rest, loss_target):
    with _jax.named_scope("forward"):
        args = {**rest, TWIN_DIFF_INPUT: diff, **{k: w.astype(_WEIGHT_DTYPES[k]) for k, w in weights.items()}}
        y = _forward(args)
    with _jax.named_scope("loss_head"):
        err = _jnp.square(y.astype(_jnp.float32) - loss_target)
        return 0.5 * _jnp.sum(_jnp.mean(err, axis=-1)) if err.ndim else 0.5 * err


def _adamw(w, g, m, v):
    m = ADAM_B1 * m + (1.0 - ADAM_B1) * g
    v = ADAM_B2 * v + (1.0 - ADAM_B2) * _jnp.square(g)
    m_hat = m / (1.0 - ADAM_B1 ** ADAM_STEP)
    v_hat = v / (1.0 - ADAM_B2 ** ADAM_STEP)
    delta = -ADAM_LR * (m_hat / (_jnp.sqrt(v_hat) + ADAM_EPS) + ADAM_WD * w)
    return delta, m, v


def reference(x, positions, ev_w_in, ev_conv_w, ev_conv_b, ev_a_log, ev_dt_bias, ev_d_skip, ev_norm_g, ev_sc_conv_w, ev_w_out, ev_ln_g, ev_ln_b, od_w_in, od_q_norm_g, od_w_uq, od_kv_norm_g, od_w_ukv, od_pool_w, od_pool_scale, od_w_out, od_ln_g, od_ln_b, loss_target, m_ev_w_in, m_ev_conv_w, m_ev_conv_b, m_ev_a_log, m_ev_dt_bias, m_ev_d_skip, m_ev_norm_g, m_ev_sc_conv_w, m_ev_w_out, m_ev_ln_g, m_ev_ln_b, m_od_w_in, m_od_q_norm_g, m_od_w_uq, m_od_kv_norm_g, m_od_w_ukv, m_od_pool_w, m_od_pool_scale, m_od_w_out, m_od_ln_g, m_od_ln_b, v_ev_w_in, v_ev_conv_w, v_ev_conv_b, v_ev_a_log, v_ev_dt_bias, v_ev_d_skip, v_ev_norm_g, v_ev_sc_conv_w, v_ev_w_out, v_ev_ln_g, v_ev_ln_b, v_od_w_in, v_od_q_norm_g, v_od_w_uq, v_od_kv_norm_g, v_od_w_ukv, v_od_pool_w, v_od_pool_scale, v_od_w_out, v_od_ln_g, v_od_ln_b):
    given = dict(x=x, positions=positions, ev_w_in=ev_w_in, ev_conv_w=ev_conv_w, ev_conv_b=ev_conv_b, ev_a_log=ev_a_log, ev_dt_bias=ev_dt_bias, ev_d_skip=ev_d_skip, ev_norm_g=ev_norm_g, ev_sc_conv_w=ev_sc_conv_w, ev_w_out=ev_w_out, ev_ln_g=ev_ln_g, ev_ln_b=ev_ln_b, od_w_in=od_w_in, od_q_norm_g=od_q_norm_g, od_w_uq=od_w_uq, od_kv_norm_g=od_kv_norm_g, od_w_ukv=od_w_ukv, od_pool_w=od_pool_w, od_pool_scale=od_pool_scale, od_w_out=od_w_out, od_ln_g=od_ln_g, od_ln_b=od_ln_b, loss_target=loss_target, m_ev_w_in=m_ev_w_in, m_ev_conv_w=m_ev_conv_w, m_ev_conv_b=m_ev_conv_b, m_ev_a_log=m_ev_a_log, m_ev_dt_bias=m_ev_dt_bias, m_ev_d_skip=m_ev_d_skip, m_ev_norm_g=m_ev_norm_g, m_ev_sc_conv_w=m_ev_sc_conv_w, m_ev_w_out=m_ev_w_out, m_ev_ln_g=m_ev_ln_g, m_ev_ln_b=m_ev_ln_b, m_od_w_in=m_od_w_in, m_od_q_norm_g=m_od_q_norm_g, m_od_w_uq=m_od_w_uq, m_od_kv_norm_g=m_od_kv_norm_g, m_od_w_ukv=m_od_w_ukv, m_od_pool_w=m_od_pool_w, m_od_pool_scale=m_od_pool_scale, m_od_w_out=m_od_w_out, m_od_ln_g=m_od_ln_g, m_od_ln_b=m_od_ln_b, v_ev_w_in=v_ev_w_in, v_ev_conv_w=v_ev_conv_w, v_ev_conv_b=v_ev_conv_b, v_ev_a_log=v_ev_a_log, v_ev_dt_bias=v_ev_dt_bias, v_ev_d_skip=v_ev_d_skip, v_ev_norm_g=v_ev_norm_g, v_ev_sc_conv_w=v_ev_sc_conv_w, v_ev_w_out=v_ev_w_out, v_ev_ln_g=v_ev_ln_g, v_ev_ln_b=v_ev_ln_b, v_od_w_in=v_od_w_in, v_od_q_norm_g=v_od_q_norm_g, v_od_w_uq=v_od_w_uq, v_od_kv_norm_g=v_od_kv_norm_g, v_od_w_ukv=v_od_w_ukv, v_od_pool_w=v_od_pool_w, v_od_pool_scale=v_od_pool_scale, v_od_w_out=v_od_w_out, v_od_ln_g=v_od_ln_g, v_od_ln_b=v_od_ln_b)
    weights = {n: given[n] for n in TWIN_WEIGHTS}
    shared = {n: given[n] for n in SHARED_INPUTS}
    per_example = {n: given[n] for n in ['x', 'positions']}
    grad_fn = _jax.value_and_grad(_loss, argnums=(0, 1))

    def one_microbatch(ex, loss_target):
        ex = dict(ex)
        diff = ex.pop(TWIN_DIFF_INPUT)
        return grad_fn(weights, diff, {**shared, **ex}, loss_target)

    if N_MICROBATCH == 1:
        loss, (grad_w, grad_x) = one_microbatch(per_example, given["loss_target"])
    else:
        def body(carry, xs):
            loss_sum, grad_sum = carry
            l_k, (gw_k, gx_k) = one_microbatch(xs[0], xs[1])
            with _jax.named_scope("update"):
                return (loss_sum + l_k, _jax.tree.map(_jnp.add, grad_sum, gw_k)), gx_k

        init = (_jnp.zeros((), _jnp.float32), _jax.tree.map(_jnp.zeros_like, weights))
        (loss, grad_w), grad_x = _jax.lax.scan(body, init, (per_example, given["loss_target"]))
    with _jax.named_scope("update"):
        delta_w, new_m, new_v = {}, {}, {}
        for n in TWIN_WEIGHTS:
            delta_w[n], new_m[n], new_v[n] = _adamw(weights[n], grad_w[n], given["m_" + n], given["v_" + n])
    return (loss, grad_x, *[grad_w[n] for n in TWIN_WEIGHTS], *[delta_w[n] for n in TWIN_WEIGHTS],
            *[new_m[n] for n in TWIN_WEIGHTS], *[new_v[n] for n in TWIN_WEIGHTS])
```

```python
import functools
import math

import jax
import jax.numpy as jnp
import numpy as np
from jax import lax
from jax.experimental import pallas as pl
from jax.experimental.pallas import tpu as pltpu

F32 = jnp.float32
BF16 = jnp.bfloat16
MATMUL_DTYPE = jnp.bfloat16

D_MODEL = 1024
DEPTH = 2
SSD_HEADS, SSD_HEAD_DIM, SSD_GROUPS, SSD_STATE, SSD_CHUNK = 16, 64, 4, 128, 128
SSD_INNER = SSD_HEADS * SSD_HEAD_DIM
SSD_XBC = SSD_INNER + 2 * SSD_GROUPS * SSD_STATE
SC_WIDTH = 1024
MLA_HEADS, MLA_Q_RANK, MLA_KV_RANK, MLA_NOPE, MLA_ROPE, MLA_V = 8, 256, 128, 64, 32, 64
MLA_WIDTH = MLA_HEADS * MLA_V
ROPE_THETA = 10000.0
ATTN_SCALE = (MLA_NOPE + MLA_ROPE) ** -0.5
POOL_WINDOWS = (2, 4, 8, 16)
POOL_GROUP = 128
POOL_WIDTH = POOL_GROUP * len(POOL_WINDOWS)
EPS = 1e-5
ALPHA = (2 * DEPTH) ** 0.25
EVEN_PROJ, ODD_PROJ = 7200, 1952
ADAM_LR, ADAM_B1, ADAM_B2, ADAM_EPS, ADAM_WD, ADAM_STEP = 0.001, 0.9, 0.999, 1e-08, 0.01, 10

LANES = 128
SUBLANES = 8
HALO = SUBLANES
VMEM_LIMIT = 56 * 1024 * 1024

EVEN_P = 7296
ODD_P = 2048
E_Z, E_XBC, E_BG, E_CG, E_H, E_GATE, E_DT = 0, 1024, 3072, 4096, 5120, 6144, 7168
O_CQ, O_CKV, O_KR, O_GC, O_UD, O_GD = 0, 256, 384, 512, 1024, 1536


def _params(sem=None):
    return pltpu.CompilerParams(dimension_semantics=sem, vmem_limit_bytes=VMEM_LIMIT)


def _mm(a, b, dims=(((1,), (0,)), ((), ()))):
    return lax.dot_general(a.astype(MATMUL_DTYPE), b.astype(MATMUL_DTYPE), dims, preferred_element_type=F32)


_NN = (((1,), (0,)), ((), ()))
_NT = (((1,), (1,)), ((), ()))
_TN = (((0,), (0,)), ((), ()))


def _silu(v):
    return v * jax.nn.sigmoid(v)


def _dsilu(v):
    s = jax.nn.sigmoid(v)
    return s * (1.0 + v * (1.0 - s))


def _pick(n, prefs):
    for p in prefs:
        if n % p == 0:
            return p
    return n


def matmul(a, b, mode, out_dtype, name, add=None, tm=None, tn=None, tk=None):
    if mode == "nn":
        (m, k), (k2, n) = a.shape, b.shape
    elif mode == "nt":
        (m, k), (n, k2) = a.shape, b.shape
    else:
        (k, m), (k2, n) = a.shape, b.shape
    assert k == k2, (a.shape, b.shape, mode)
    tm = tm or _pick(m, (512, 256, 128))
    tn = tn or _pick(n, (512, 384, 256, 128))
    tk = tk or _pick(k, (512, 256, 128))
    nk = k // tk
    dims = {"nn": _NN, "nt": _NT, "tn": _TN}[mode]

    def body(a_ref, b_ref, *rest):
        o_ref, acc_ref = rest[-2:]
        kk = pl.program_id(2)

        @pl.when(kk == 0)
        def _():
            acc_ref[...] = jnp.zeros_like(acc_ref) if add is None else rest[0][...].astype(F32)

        acc_ref[...] += _mm(a_ref[...], b_ref[...], dims)

        @pl.when(kk == nk - 1)
        def _():
            o_ref[...] = acc_ref[...].astype(o_ref.dtype)

    a_spec = {"nn": pl.BlockSpec((tm, tk), lambda i, j, kk: (i, kk)),
              "nt": pl.BlockSpec((tm, tk), lambda i, j, kk: (i, kk)),
              "tn": pl.BlockSpec((tk, tm), lambda i, j, kk: (kk, i))}[mode]
    b_spec = {"nn": pl.BlockSpec((tk, tn), lambda i, j, kk: (kk, j)),
              "nt": pl.BlockSpec((tn, tk), lambda i, j, kk: (j, kk)),
              "tn": pl.BlockSpec((tk, tn), lambda i, j, kk: (kk, j))}[mode]
    return pl.pallas_call(
        body, name=name, grid=(m // tm, n // tn, nk),
        in_specs=[a_spec, b_spec] + ([] if add is None else [pl.BlockSpec((tm, tn), lambda i, j, kk: (i, j))]),
        out_specs=pl.BlockSpec((tm, tn), lambda i, j, kk: (i, j)),
        out_shape=jax.ShapeDtypeStruct((m, n), out_dtype),
        scratch_shapes=[pltpu.VMEM((tm, tn), F32)],
        compiler_params=_params(("parallel", "parallel", "arbitrary")),
    )(*((a, b) if add is None else (a, b, add)))


class Rows:
    def __init__(self, s, t):
        assert s % t == 0 and t % HALO == 0
        self.s, self.t, self.n = s, t, s // t

    def tile(self, width, col=0, lead=None):
        cb = col // width
        assert col % width == 0
        if lead is None:
            return pl.BlockSpec((self.t, width), lambda i: (i, cb))
        return pl.BlockSpec((None, self.t, width), lambda i: (lead, i, cb))

    def prev(self, width, col=0, lead=None):
        cb, r = col // width, self.t // HALO
        if lead is None:
            return pl.BlockSpec((HALO, width), lambda i: (jnp.maximum(i * r - 1, 0), cb))
        return pl.BlockSpec((None, HALO, width), lambda i: (lead, jnp.maximum(i * r - 1, 0), cb))

    def next(self, width, col=0, lead=None):
        cb, r, last = col // width, self.t // HALO, self.s // HALO - 1
        if lead is None:
            return pl.BlockSpec((HALO, width), lambda i: (jnp.minimum((i + 1) * r, last), cb))
        return pl.BlockSpec((None, HALO, width), lambda i: (lead, jnp.minimum((i + 1) * r, last), cb))

    def halo(self, width, col=0, lead=None):
        return [self.prev(width, col, lead), self.tile(width, col, lead), self.next(width, col, lead)]

    @staticmethod
    def full(shape):
        nd = len(shape)
        return pl.BlockSpec(tuple(shape), lambda i: (0,) * nd)


def rows_call(name, rows, fn, ins, in_specs, row_outs, acc_outs=()):
    n_row = len(row_outs)

    def body(*refs):
        in_refs = refs[:len(ins)]
        out_refs = refs[len(ins):]
        res = fn(*[r[...] for r in in_refs])
        if not isinstance(res, (tuple, list)):
            res = (res,)
        for r, v in zip(out_refs[:n_row], res[:n_row]):
            r[...] = v.astype(r.dtype)
        if acc_outs:
            first = pl.program_id(0) == 0

            @pl.when(first)
            def _():
                for r, v in zip(out_refs[n_row:], res[n_row:]):
                    r[...] = v.astype(F32)

            @pl.when(jnp.logical_not(first))
            def _():
                for r, v in zip(out_refs[n_row:], res[n_row:]):
                    r[...] += v.astype(F32)

    out_shape = [jax.ShapeDtypeStruct((rows.s, w), dt) for (w, dt) in row_outs]
    out_specs = [rows.tile(w) for (w, dt) in row_outs]
    out_shape += [jax.ShapeDtypeStruct(tuple(sh), F32) for sh in acc_outs]
    out_specs += [Rows.full(sh) for sh in acc_outs]
    return pl.pallas_call(
        body, name=name, grid=(rows.n,), in_specs=list(in_specs), out_specs=out_specs, out_shape=out_shape,
        compiler_params=_params(("arbitrary",)),
    )(*ins)


def _edge_zero(prev, nxt, n_tiles):
    i = pl.program_id(0)
    prev = jnp.where(i == 0, jnp.zeros_like(prev), prev)
    nxt = jnp.where(i == n_tiles - 1, jnp.zeros_like(nxt), nxt)
    return prev, nxt


def _ext(prev, cur, nxt, n_tiles):
    prev, nxt = _edge_zero(prev, nxt, n_tiles)
    return jnp.concatenate([prev.astype(F32), cur.astype(F32), nxt.astype(F32)], axis=0)


def _shift(ext, off):
    n = ext.shape[0]
    t = n - 2 * HALO
    if off == 0:
        return ext[HALO:HALO + t]
    return pltpu.roll(ext, (-off) % n, 0)[HALO:HALO + t]


def _rowsum(v):
    return jnp.sum(v, axis=0, keepdims=True)


CONV_OFFS = (-2, -1, 0, 1)
SC_OFFS = (-1, 0, 1)


def conv_fwd(proj, conv_w, conv_b, rows):
    def fn(p, c, nx, w, b):
        e = _ext(p, c, nx, rows.n)
        pre = b
        for k, off in enumerate(CONV_OFFS):
            pre = pre + w[k:k + 1, :] * _shift(e, off)
        return pre, _silu(pre)

    outs = [rows_call(f"ev_conv_fwd{h}", rows, fn,
                      [proj, proj, proj, conv_w[:, h * 1024:(h + 1) * 1024], conv_b[:, h * 1024:(h + 1) * 1024]],
                      rows.halo(1024, E_XBC + h * 1024) + [Rows.full((4, 1024)), Rows.full((1, 1024))],
                      [(1024, F32), (1024, F32)]) for h in range(2)]
    return outs


def conv_bwd(proj, pre_h, dxs, db, dc, conv_w, rows):
    res = []
    for h in range(2):
        def fn(*a):
            w = a[-1]
            xe = _ext(a[0], a[1], a[2], rows.n)
            pe = jnp.concatenate([a[3], a[4], a[5]], axis=0)
            g = a[6:-1]
            if len(g) == 6:
                du = _ext(g[0], g[1], g[2], rows.n) + _ext(g[3], g[4], g[5], rows.n)
            else:
                dbe = _ext(g[0], g[1], g[2], rows.n) + _ext(g[3], g[4], g[5], rows.n)
                dce = _ext(g[6], g[7], g[8], rows.n) + _ext(g[9], g[10], g[11], rows.n)
                du = jnp.concatenate([dbe, dce], axis=1)
            dpre = du * _dsilu(pe)
            dx = jnp.zeros_like(a[1], dtype=F32)
            dws = []
            for k, off in enumerate(CONV_OFFS):
                dx = dx + w[k:k + 1, :] * _shift(dpre, -off)
                dws.append(_rowsum(_shift(dpre, 0) * _shift(xe, off)))
            dw = jnp.concatenate(dws + [jnp.zeros((4, dx.shape[1]), F32)], axis=0)
            return dx, dw, _rowsum(_shift(dpre, 0))

        if h == 0:
            gi = [dxs] * 6
            gs = rows.halo(1024, 0, lead=0) + rows.halo(1024, 0, lead=1)
        else:
            gi = [db] * 6 + [dc] * 6
            gs = (rows.halo(512, 0, lead=0) + rows.halo(512, 0, lead=1)) * 2
        res.append(rows_call(
            f"ev_conv_bwd{h}", rows, fn,
            [proj] * 3 + [pre_h[h]] * 3 + gi + [conv_w[:, h * 1024:(h + 1) * 1024]],
            rows.halo(1024, E_XBC + h * 1024) + rows.halo(1024) + gs + [Rows.full((4, 1024))],
            [(1024, F32)], [(8, 1024), (1, 1024)]))
    dconv_w = jnp.concatenate([res[0][1][:4], res[1][1][:4]], axis=1)
    dconv_b = jnp.concatenate([res[0][2], res[1][2]], axis=1)
    return [res[0][0], res[1][0]], dconv_w, dconv_b


def _head_row(p):
    return jnp.concatenate([p.reshape(1, 2 * SSD_HEADS), jnp.zeros((1, LANES - 2 * SSD_HEADS), F32)], axis=1)


def _head_unrow(r):
    return r[:, :2 * SSD_HEADS].reshape(2, SSD_HEADS)


def _ssd_pre_fn(dtraw, bias_row, alog_row):
    q = dtraw.shape[0]
    dt = jax.nn.softplus(dtraw + bias_row)
    da = dt * (-jnp.exp(alog_row))
    li = lax.broadcasted_iota(jnp.int32, (q, q), 0)
    si = lax.broadcasted_iota(jnp.int32, (q, q), 1)
    tril = (li >= si).astype(F32)
    csf = lax.dot_general(tril, da, _NN, precision=lax.Precision.HIGHEST, preferred_element_type=F32)
    tot = jnp.sum(da, axis=0, keepdims=True)
    lane = lax.broadcasted_iota(jnp.int32, (1, LANES), 1)
    cs = jnp.where(lane < SSD_HEADS, csf, tot - csf + da)
    return dt, cs


def ssd_pre(proj, bias_row, alog_row, s):
    rows = Rows(s, SSD_CHUNK)
    return rows_call("ev_ssd_pre", rows, _ssd_pre_fn, [proj, bias_row, alog_row],
                     [rows.tile(LANES, E_DT), Rows.full((1, LANES)), Rows.full((1, LANES))],
                     [(LANES, F32), (LANES, F32)])


def ssd_pre_bwd(proj, bias_row, alog_row, ddt, dcs, s):
    rows = Rows(s, SSD_CHUNK)

    def fn(dtraw, b, al, g0, g1, c0, c1):
        _, vjp = jax.vjp(_ssd_pre_fn, dtraw, b, al)
        return vjp((g0 + g1, c0 + c1))

    return rows_call("ev_ssd_pre_bwd", rows, fn, [proj, bias_row, alog_row, ddt, ddt, dcs, dcs],
                     [rows.tile(LANES, E_DT), Rows.full((1, LANES)), Rows.full((1, LANES)),
                      rows.tile(LANES, 0, lead=0), rows.tile(LANES, 0, lead=1),
                      rows.tile(LANES, 0, lead=0), rows.tile(LANES, 0, lead=1)],
                     [(LANES, F32)], [(1, LANES), (1, LANES)])


def _ssd_chunk_fn(x, bm, cm, dt, cs, h_in, dsk_row, d, j):
    q = x.shape[0]
    lane = lax.broadcasted_iota(jnp.int32, (1, LANES), 1)
    sub = lax.broadcasted_iota(jnp.int32, (LANES, 1), 0)
    l0 = SSD_HEADS * d + 2 * j
    half = lane < SSD_HEAD_DIM

    def col(v, l):
        return jnp.sum(jnp.where(lane == l, v, 0.0), axis=1, keepdims=True)

    cst = cs.T

    def row(l):
        return jnp.sum(jnp.where(sub == l, cst, 0.0), axis=0, keepdims=True)

    li = lax.broadcasted_iota(jnp.int32, (q, 1), 0)
    si = lax.broadcasted_iota(jnp.int32, (1, q), 1)
    mask = (li - si) * (1 - 2 * d) >= 0
    end_row = jnp.where(d == 0, q - 1, 0)
    tot = jnp.sum(jnp.where(li == end_row, cs, 0.0), axis=0, keepdims=True)
    dt0, dt1 = col(dt, l0), col(dt, l0 + 1)
    cs0, cs1 = col(cs, l0), col(cs, l0 + 1)
    tot0, tot1 = col(tot, l0), col(tot, l0 + 1)
    xd = x * jnp.where(half, dt0, dt1)
    cb = _mm(cm, bm, _NT)
    w0 = cb * jnp.exp(jnp.where(mask, cs0 - row(l0), -jnp.inf))
    w1 = cb * jnp.exp(jnp.where(mask, cs1 - row(l0 + 1), -jnp.inf))
    y = jnp.where(half, _mm(w0, xd), _mm(w1, xd))
    st = jnp.where(half, _mm(bm * jnp.exp(tot0 - cs0), xd, _TN), _mm(bm * jnp.exp(tot1 - cs1), xd, _TN))
    y = y + _mm(cm, h_in) * jnp.where(half, jnp.exp(cs0), jnp.exp(cs1))
    y = y + jnp.where(half, col(dsk_row, l0), col(dsk_row, l0 + 1)) * x
    h_out = h_in * jnp.where(half, jnp.exp(tot0), jnp.exp(tot1)) + st
    return y, h_out


_N_PAIR = SSD_HEADS // 2
_XS_BLK = SSD_INNER // LANES


def _chunk_of(d, ci, nc, backward):
    up = ci if not backward else nc - 1 - ci
    return up + d * (nc - 1 - 2 * up)


def ssd_fwd(u_h, dt, cs, dsk_row, s):
    nc = s // SSD_CHUNK
    q = SSD_CHUNK

    def body(x_ref, b_ref, c_ref, dt_ref, cs_ref, dsk_ref, y_ref, hs_ref, st_ref):
        d, ci, j = pl.program_id(0), pl.program_id(1), pl.program_id(2)

        @pl.when(ci == 0)
        def _():
            st_ref[j] = jnp.zeros((LANES, LANES), F32)

        h_in = st_ref[j]
        y, h_out = _ssd_chunk_fn(x_ref[...], b_ref[...], c_ref[...], dt_ref[...], cs_ref[...], h_in,
                                 dsk_ref[...], d, j)
        y_ref[...] = y
        hs_ref[...] = h_in
        st_ref[j] = h_out

    ch = lambda d, ci, j: _chunk_of(d, ci, nc, False)
    return pl.pallas_call(
        body, name="ev_ssd_fwd", grid=(2, nc, _N_PAIR),
        in_specs=[pl.BlockSpec((q, LANES), lambda d, ci, j: (ch(d, ci, j), j)),
                  pl.BlockSpec((q, LANES), lambda d, ci, j: (ch(d, ci, j), j // 2)),
                  pl.BlockSpec((q, LANES), lambda d, ci, j: (ch(d, ci, j), 4 + j // 2)),
                  pl.BlockSpec((q, LANES), lambda d, ci, j: (ch(d, ci, j), 0)),
                  pl.BlockSpec((q, LANES), lambda d, ci, j: (ch(d, ci, j), 0)),
                  pl.BlockSpec((1, LANES), lambda d, ci, j: (0, 0))],
        out_specs=[pl.BlockSpec((None, q, LANES), lambda d, ci, j: (d, ch(d, ci, j), j)),
                   pl.BlockSpec((None, None, None, LANES, LANES), lambda d, ci, j: (d, ch(d, ci, j), j, 0, 0))],
        out_shape=[jax.ShapeDtypeStruct((2, s, SSD_INNER), F32),
                   jax.ShapeDtypeStruct((2, nc, _N_PAIR, LANES, LANES), F32)],
        scratch_shapes=[pltpu.VMEM((_N_PAIR, LANES, LANES), F32)],
        compiler_params=_params(("arbitrary", "arbitrary", "arbitrary")),
    )(u_h[0], u_h[1], u_h[1], dt, cs, dsk_row)


def ssd_bwd(u_h, dt, cs, dsk_row, hs, dy, s):
    nc = s // SSD_CHUNK
    q = SSD_CHUNK

    def body(x_ref, b_ref, c_ref, dt_ref, cs_ref, dsk_ref, hs_ref, dy_ref,
             dx_ref, db_ref, dc_ref, ddt_ref, dcs_ref, ddsk_ref, dst_ref):
        d, ci, j = pl.program_id(0), pl.program_id(1), pl.program_id(2)

        @pl.when(ci == 0)
        def _():
            dst_ref[j] = jnp.zeros((LANES, LANES), F32)

        f = functools.partial(_ssd_chunk_fn, d=d, j=j)
        _, vjp = jax.vjp(f, x_ref[...], b_ref[...], c_ref[...], dt_ref[...], cs_ref[...], hs_ref[...],
                         dsk_ref[...])
        dx, dbm, dcm, ddt, dcs, dh, ddsk = vjp((dy_ref[...], dst_ref[j]))
        dx_ref[...] = dx
        dst_ref[j] = dh

        @pl.when(j % 2 == 0)
        def _():
            db_ref[...] = dbm
            dc_ref[...] = dcm

        @pl.when(j % 2 == 1)
        def _():
            db_ref[...] += dbm
            dc_ref[...] += dcm

        @pl.when(j == 0)
        def _():
            ddt_ref[...] = ddt
            dcs_ref[...] = dcs

        @pl.when(j != 0)
        def _():
            ddt_ref[...] += ddt
            dcs_ref[...] += dcs

        @pl.when(jnp.logical_and(ci == 0, j == 0))
        def _():
            ddsk_ref[...] = ddsk

        @pl.when(jnp.logical_not(jnp.logical_and(ci == 0, j == 0)))
        def _():
            ddsk_ref[...] += ddsk

    ch = lambda d, ci, j: _chunk_of(d, ci, nc, True)
    return pl.pallas_call(
        body, name="ev_ssd_bwd", grid=(2, nc, _N_PAIR),
        in_specs=[pl.BlockSpec((q, LANES), lambda d, ci, j: (ch(d, ci, j), j)),
                  pl.BlockSpec((q, LANES), lambda d, ci, j: (ch(d, ci, j), j // 2)),
                  pl.BlockSpec((q, LANES), lambda d, ci, j: (ch(d, ci, j), 4 + j // 2)),
                  pl.BlockSpec((q, LANES), lambda d, ci, j: (ch(d, ci, j), 0)),
                  pl.BlockSpec((q, LANES), lambda d, ci, j: (ch(d, ci, j), 0)),
                  pl.BlockSpec((1, LANES), lambda d, ci, j: (0, 0)),
                  pl.BlockSpec((None, None, None, LANES, LANES), lambda d, ci, j: (d, ch(d, ci, j), j, 0, 0)),
                  pl.BlockSpec((q, LANES), lambda d, ci, j: (ch(d, ci, j), j))],
        out_specs=[pl.BlockSpec((None, q, LANES), lambda d, ci, j: (d, ch(d, ci, j), j)),
                   pl.BlockSpec((None, q, LANES), lambda d, ci, j: (d, ch(d, ci, j), j // 2)),
                   pl.BlockSpec((None, q, LANES), lambda d, ci, j: (d, ch(d, ci, j), j // 2)),
                   pl.BlockSpec((None, q, LANES), lambda d, ci, j: (d, ch(d, ci, j), 0)),
                   pl.BlockSpec((None, q, LANES), lambda d, ci, j: (d, ch(d, ci, j), 0)),
                   pl.BlockSpec((None, 1, LANES), lambda d, ci, j: (d, 0, 0))],
        out_shape=[jax.ShapeDtypeStruct((2, s, SSD_INNER), F32),
                   jax.ShapeDtypeStruct((2, s, SSD_GROUPS * SSD_STATE), F32),
                   jax.ShapeDtypeStruct((2, s, SSD_GROUPS * SSD_STATE), F32),
                   jax.ShapeDtypeStruct((2, s, LANES), F32),
                   jax.ShapeDtypeStruct((2, s, LANES), F32),
                   jax.ShapeDtypeStruct((2, 1, LANES), F32)],
        scratch_shapes=[pltpu.VMEM((_N_PAIR, LANES, LANES), F32)],
        compiler_params=_params(("arbitrary", "arbitrary", "arbitrary")),
    )(u_h[0], u_h[1], u_h[1], dt, cs, dsk_row, hs, dy)


def _gated_rms(ys, z, g):
    t1 = ys * _silu(z)
    return t1 * lax.rsqrt(jnp.mean(t1 * t1, axis=-1, keepdims=True) + EPS) * g


def even_mix_fwd(proj, y2, norm_g, sc_w, rows):
    def fn(yf, yb, z, bg, cgp, cg, cgn, hp, hh, hn, gate, g, w):
        ya = _gated_rms(yf + yb, z, g)
        me = _ext(cgp, cg, cgn, rows.n) * _ext(hp, hh, hn, rows.n)
        cm = sum(w[k:k + 1, :] * _shift(me, off) for k, off in enumerate(SC_OFFS))
        return jnp.concatenate([ya, bg * cm * _silu(gate)], axis=1)

    w = 1024
    return rows_call("ev_mix_fwd", rows, fn,
                     [y2, y2, proj, proj] + [proj] * 6 + [proj, norm_g, sc_w],
                     [rows.tile(w, 0, lead=0), rows.tile(w, 0, lead=1), rows.tile(w, E_Z), rows.tile(w, E_BG)]
                     + rows.halo(w, E_CG) + rows.halo(w, E_H)
                     + [rows.tile(w, E_GATE), Rows.full((1, w)), Rows.full((3, w))],
                     [(2 * w, MATMUL_DTYPE)])[0]


def even_mix_bwd(proj, y2, norm_g, sc_w, dyab, rows):
    w = 1024

    def fn(yf, yb, z, g, sw, dya, *a):
        (dybp, dyb, dybn, bgp, bg, bgn, gtp, gt, gtn, cgp, cg, cgn, hp, hh, hn) = a
        _, vjp = jax.vjp(_gated_rms, yf + yb, z, g)
        dys, dz, dg = vjp(dya.astype(F32))
        n = rows.n
        dye, bge, gte = _ext(dybp, dyb, dybn, n), _ext(bgp, bg, bgn, n), _ext(gtp, gt, gtn, n)
        cge, he = _ext(cgp, cg, cgn, n), _ext(hp, hh, hn, n)
        me = cge * he
        cm = sum(sw[k:k + 1, :] * _shift(me, off) for k, off in enumerate(SC_OFFS))
        dyc = dyb.astype(F32)
        dbg = dyc * cm * _silu(gt)
        dgate = dyc * bg * cm * _dsilu(gt)
        dcme = dye * bge * _silu(gte)
        dm = sum(sw[k:k + 1, :] * _shift(dcme, -off) for k, off in enumerate(SC_OFFS))
        dcm = _shift(dcme, 0)
        dws = [_rowsum(dcm * _shift(me, off)) for off in SC_OFFS]
        dsw = jnp.concatenate(dws + [jnp.zeros((5, w), F32)], axis=0)
        return dys, dz, dbg, dm * hh, dm * cg, dgate, _rowsum(dg), dsw

    return rows_call(
        "ev_mix_bwd", rows, fn,
        [y2, y2, proj, norm_g, sc_w, dyab] + [dyab] * 3 + [proj] * 12,
        [rows.tile(w, 0, lead=0), rows.tile(w, 0, lead=1), rows.tile(w, E_Z), Rows.full((1, w)), Rows.full((3, w)),
         rows.tile(w, 0)] + rows.halo(w, w) + rows.halo(w, E_BG) + rows.halo(w, E_GATE)
        + rows.halo(w, E_CG) + rows.halo(w, E_H),
        [(w, F32)] * 6, [(1, w), (8, w)])


def _res_ln(x, h, g, b):
    v = ALPHA * x + h
    mu = jnp.mean(v, axis=-1, keepdims=True)
    var = jnp.mean(jnp.square(v - mu), axis=-1, keepdims=True)
    return (v - mu) * lax.rsqrt(var + EPS) * g + b


def res_ln_fwd(x, h, g, b, rows, name):
    return rows_call(name, rows, _res_ln, [x, h, g, b],
                     [rows.tile(D_MODEL), rows.tile(D_MODEL), Rows.full((1, D_MODEL)), Rows.full((1, D_MODEL))],
                     [(D_MODEL, F32)])[0]


def res_ln_bwd(x, h, g, b, dy, rows, name):
    def fn(x_, h_, g_, b_, dy_):
        _, vjp = jax.vjp(_res_ln, x_, h_, g_, b_)
        dx, dh, dg, db = vjp(dy_)
        return dx, dh, dg, db

    return rows_call(name, rows, fn, [x, h, g, b, dy],
                     [rows.tile(D_MODEL), rows.tile(D_MODEL), Rows.full((1, D_MODEL)), Rows.full((1, D_MODEL)),
                      rows.tile(D_MODEL)],
                     [(D_MODEL, F32), (D_MODEL, F32)], [(1, D_MODEL), (1, D_MODEL)])


def final_ln_loss(x, h, g, b, target, rows):
    def fn(x_, h_, g_, b_, t_):
        y, vjp = jax.vjp(_res_ln, x_, h_, g_, b_)
        err = y - t_
        dx, dh, dg, db = vjp(err * (1.0 / D_MODEL))
        return dx, dh, dg, db, _rowsum(jnp.square(err)) * (0.5 / D_MODEL)

    return rows_call("od_ln_loss", rows, fn, [x, h, g, b, target],
                     [rows.tile(D_MODEL), rows.tile(D_MODEL), Rows.full((1, D_MODEL)), Rows.full((1, D_MODEL)),
                      rows.tile(D_MODEL)],
                     [(D_MODEL, F32), (D_MODEL, F32)], [(1, D_MODEL), (1, D_MODEL), (1, D_MODEL)])


def pad_even_w_in(w):
    return jnp.concatenate([w[:, :3072], w[:, 3104:], w[:, 3072:3104],
                            jnp.zeros((w.shape[0], EVEN_P - EVEN_PROJ), w.dtype)], axis=1)


def unpad_even_w_in(wp):
    return jnp.concatenate([wp[:, :3072], wp[:, E_DT:E_DT + 32], wp[:, 3072:E_DT]], axis=1)


def even_layer(x, w, rows):
    s = rows.s
    proj = matmul(x, w["w_in_p"], "nn", F32, "ev_proj")
    (pre0, u0), (pre1, u1) = conv_fwd(proj, w["conv_w"], w["conv_b"], rows)
    dt, cs = ssd_pre(proj, w["bias_row"], w["alog_row"], s)
    y2, hs = ssd_fwd((u0, u1), dt, cs, w["dsk_row"], s)
    yab = even_mix_fwd(proj, y2, w["norm_g"], w["sc_w"], rows)
    h = matmul(yab, w["w_out"], "nn", F32, "ev_out")
    x1 = res_ln_fwd(x, h, w["ln_g"], w["ln_b"], rows, "ev_ln")
    return x1, dict(x=x, proj=proj, pre=(pre0, pre1), u=(u0, u1), dt=dt, cs=cs, y2=y2, hs=hs, yab=yab, h=h)


def even_layer_bwd(dx1, w, sv, rows):
    s = rows.s
    dres, dh, dln_g, dln_b = res_ln_bwd(sv["x"], sv["h"], w["ln_g"], w["ln_b"], dx1, rows, "ev_ln_bwd")
    dyab = matmul(dh, w["w_out"], "nt", F32, "ev_out_dx")
    dw_out = matmul(sv["yab"], dh, "tn", F32, "ev_out_dw")
    dys, dz, dbg, dcg, dhh, dgate, dnorm_g, dsw = even_mix_bwd(sv["proj"], sv["y2"], w["norm_g"], w["sc_w"], dyab,
                                                               rows)
    dxs, db, dc, ddt, dcs, ddsk = ssd_bwd(sv["u"], sv["dt"], sv["cs"], w["dsk_row"], sv["hs"], dys, s)
    ddtraw, dbias_row, dalog_row = ssd_pre_bwd(sv["proj"], w["bias_row"], w["alog_row"], ddt, dcs, s)
    (dxbc0, dxbc1), dconv_w, dconv_b = conv_bwd(sv["proj"], sv["pre"], dxs, db, dc, w["conv_w"], rows)
    dproj = jnp.concatenate([dz, dxbc0, dxbc1, dbg, dcg, dhh, dgate, ddtraw], axis=1).astype(MATMUL_DTYPE)
    dx0 = matmul(dproj, w["w_in_p"], "nt", F32, "ev_proj_dx", add=dres)
    dw_in = unpad_even_w_in(matmul(sv["x"], dproj, "tn", F32, "ev_proj_dw"))
    g = dict(ev_w_in=dw_in, ev_conv_w=dconv_w, ev_conv_b=dconv_b,
             ev_a_log=_head_unrow(dalog_row), ev_dt_bias=_head_unrow(dbias_row),
             ev_d_skip=_head_unrow(ddsk[0] + ddsk[1]), ev_norm_g=dnorm_g, ev_sc_conv_w=dsw[:3],
             ev_w_out=dw_out, ev_ln_g=dln_g, ev_ln_b=dln_b)
    return dx0, g


def even_weights(ev_w_in, ev_conv_w, ev_conv_b, ev_a_log, ev_dt_bias, ev_d_skip, ev_norm_g, ev_sc_conv_w,
                 ev_w_out, ev_ln_g, ev_ln_b):
    return dict(w_in_p=pad_even_w_in(ev_w_in).astype(MATMUL_DTYPE), conv_w=ev_conv_w, conv_b=ev_conv_b,
                alog_row=_head_row(ev_a_log), bias_row=_head_row(ev_dt_bias), dsk_row=_head_row(ev_d_skip),
                norm_g=ev_norm_g, sc_w=ev_sc_conv_w, w_out=ev_w_out.astype(MATMUL_DTYPE), ln_g=ev_ln_g,
                ln_b=ev_ln_b)


HEAD_BLK = LANES
ROPE_LO = MLA_NOPE
ROPE_HALF = MLA_ROPE // 2


def _rms(v, g):
    return v * lax.rsqrt(jnp.mean(v * v, axis=-1, keepdims=True) + EPS) * g


def latent_norm_fwd(proj, gq, gkv, rows):
    def fn(cq, ckv, gq_, gkv_):
        return _rms(cq, gq_), _rms(ckv, gkv_)

    return rows_call("od_norm_fwd", rows, fn, [proj, proj, gq, gkv],
                     [rows.tile(MLA_Q_RANK, O_CQ), rows.tile(MLA_KV_RANK, O_CKV), Rows.full((1, MLA_Q_RANK)),
                      Rows.full((1, MLA_KV_RANK))],
                     [(MLA_Q_RANK, MATMUL_DTYPE), (MLA_KV_RANK, MATMUL_DTYPE)])


def latent_norm_bwd(proj, gq, gkv, dcqn, dckvn, rows):
    def fn(cq, ckv, gq_, gkv_, d1, d2):
        _, vjp = jax.vjp(_rms, cq, gq_)
        dcq, dgq = vjp(d1)
        _, vjp2 = jax.vjp(_rms, ckv, gkv_)
        dckv, dgkv = vjp2(d2)
        return dcq, dckv, dgq, dgkv

    return rows_call("od_norm_bwd", rows, fn, [proj, proj, gq, gkv, dcqn, dckvn],
                     [rows.tile(MLA_Q_RANK, O_CQ), rows.tile(MLA_KV_RANK, O_CKV), Rows.full((1, MLA_Q_RANK)),
                      Rows.full((1, MLA_KV_RANK)), rows.tile(MLA_Q_RANK), rows.tile(MLA_KV_RANK)],
                     [(MLA_Q_RANK, F32), (MLA_KV_RANK, F32)], [(1, MLA_Q_RANK), (1, MLA_KV_RANK)])


def rope_rows():
    lane = np.arange(LANES)
    inv = ROPE_THETA ** (-jnp.arange(ROPE_HALF, dtype=F32) / ROPE_HALF)
    on = (lane >= ROPE_LO) & (lane < ROPE_LO + MLA_ROPE)
    freq = jnp.where(on, inv[(lane - ROPE_LO) % ROPE_HALF], 0.0).reshape(1, LANES).astype(F32)
    sign = np.where(on, np.where(lane < ROPE_LO + ROPE_HALF, -1.0, 1.0), 0.0).reshape(1, LANES).astype(np.float32)
    return freq, jnp.asarray(sign)


def _rot_tables(pos, freq, sign):
    ang = pos.astype(F32) * freq
    return jnp.cos(ang), jnp.sin(ang) * sign


def _swap_halves(v):
    lane = lax.broadcasted_iota(jnp.int32, (1, LANES), 1)
    return jnp.where(lane < ROPE_LO + ROPE_HALF, pltpu.roll(v, LANES - ROPE_HALF, 1), pltpu.roll(v, ROPE_HALF, 1))


def rope_fwd(qp, kvp, proj, pos, freq, sign, rows):
    def fn(q, k, kr, v, p, f, sg):
        c, sn = _rot_tables(p, f, sg)
        rk = kr * c + _swap_halves(kr) * sn
        qs, ks = [], []
        for h in range(MLA_HEADS):
            qh = q[:, h * HEAD_BLK:(h + 1) * HEAD_BLK]
            qs.append(qh * c + _swap_halves(qh) * sn)
            ks.append(k[:, h * HEAD_BLK:(h + 1) * HEAD_BLK] + rk)
        return jnp.concatenate(qs, axis=1), jnp.concatenate(ks, axis=1), v

    w = MLA_HEADS * HEAD_BLK
    return rows_call("od_rope_fwd", rows, fn, [qp, kvp, proj, kvp, pos, freq, sign],
                     [rows.tile(w), rows.tile(w, 0), rows.tile(LANES, O_KR), rows.tile(MLA_WIDTH, w),
                      rows.tile(1), Rows.full((1, LANES)), Rows.full((1, LANES))],
                     [(w, MATMUL_DTYPE), (w, MATMUL_DTYPE), (MLA_WIDTH, MATMUL_DTYPE)])


def rope_bwd(dq, dk, dv, pos, freq, sign, rows):
    def fn(dq_, dk_, dv_, p, f, sg):
        c, sn = _rot_tables(p, f, sg)
        on = jnp.abs(sg)
        outs, dkr = [], jnp.zeros((dq_.shape[0], LANES), F32)
        for h in range(MLA_HEADS):
            g = dq_[:, h * HEAD_BLK:(h + 1) * HEAD_BLK]
            outs.append(g * c + _swap_halves(g * sn) * on)
            gk = dk_[:, h * HEAD_BLK:(h + 1) * HEAD_BLK]
            dkr = dkr + gk * c + _swap_halves(gk * sn) * on
        return jnp.concatenate(outs, axis=1), jnp.concatenate([dk_, dv_], axis=1), dkr

    w = MLA_HEADS * HEAD_BLK
    return rows_call("od_rope_bwd", rows, fn, [dq, dk, dv, pos, freq, sign],
                     [rows.tile(w), rows.tile(w), rows.tile(MLA_WIDTH), rows.tile(1), Rows.full((1, LANES)),
                      Rows.full((1, LANES))],
                     [(w, MATMUL_DTYPE), (w + MLA_WIDTH, MATMUL_DTYPE), (LANES, F32)])


_PAIRS = MLA_HEADS // 2
ATT_TQ = 512
ATT_TK = 512


def _att_tiles(s):
    return min(ATT_TQ, s), min(ATT_TK, s)


def attention_fwd(qcat, kcat, v, s):
    tq, tk = _att_tiles(s)
    nq, nk = s // tq, s // tk

    def body(q_ref, k_ref, v_ref, o_ref, lse_ref, m_ref, l_ref, acc_ref):
        kk = pl.program_id(2)

        @pl.when(kk == 0)
        def _():
            m_ref[...] = jnp.full(m_ref.shape, -jnp.inf, F32)
            l_ref[...] = jnp.zeros(l_ref.shape, F32)
            acc_ref[...] = jnp.zeros(acc_ref.shape, F32)

        vv = v_ref[...]
        for hh in range(2):
            sl = slice(hh * HEAD_BLK, (hh + 1) * HEAD_BLK)
            sc = _mm(q_ref[:, sl], k_ref[:, sl], _NT) * ATTN_SCALE
            m_prev = m_ref[hh]
            m_new = jnp.maximum(m_prev, jnp.max(sc, axis=1, keepdims=True))
            alpha = jnp.exp(m_prev - m_new)
            p = jnp.exp(sc - m_new[:, :1])
            l_ref[hh] = alpha * l_ref[hh] + jnp.sum(p, axis=1, keepdims=True)
            acc_ref[hh] = acc_ref[hh] * alpha + _mm(p, vv)
            m_ref[hh] = m_new

        @pl.when(kk == nk - 1)
        def _():
            half = lax.broadcasted_iota(jnp.int32, (1, LANES), 1) < MLA_V
            o_ref[...] = jnp.where(half, acc_ref[0] / l_ref[0], acc_ref[1] / l_ref[1])
            lse_ref[...] = jnp.where(half, m_ref[0] + jnp.log(l_ref[0]), m_ref[1] + jnp.log(l_ref[1]))

    return pl.pallas_call(
        body, name="od_attn_fwd", grid=(_PAIRS, nq, nk),
        in_specs=[pl.BlockSpec((tq, 2 * HEAD_BLK), lambda p, i, kk: (i, p)),
                  pl.BlockSpec((tk, 2 * HEAD_BLK), lambda p, i, kk: (kk, p)),
                  pl.BlockSpec((tk, LANES), lambda p, i, kk: (kk, p))],
        out_specs=[pl.BlockSpec((tq, LANES), lambda p, i, kk: (i, p)),
                   pl.BlockSpec((None, tq, LANES), lambda p, i, kk: (p, i, 0))],
        out_shape=[jax.ShapeDtypeStruct((s, MLA_WIDTH), F32), jax.ShapeDtypeStruct((_PAIRS, s, LANES), F32)],
        scratch_shapes=[pltpu.VMEM((2, tq, LANES), F32)] * 3,
        compiler_params=_params(("parallel", "parallel", "arbitrary")),
    )(qcat, kcat, v)


def _att_probs(q_ref, k_ref, v_ref, do_ref, o_ref, lse_ref, hh):
    lane = lax.broadcasted_iota(jnp.int32, (1, LANES), 1)
    half = lane < MLA_V if hh == 0 else lane >= MLA_V
    sl = slice(hh * HEAD_BLK, (hh + 1) * HEAD_BLK)
    sc = _mm(q_ref[:, sl], k_ref[:, sl], _NT) * ATTN_SCALE
    lse = lse_ref[...]
    p = jnp.exp(sc - lse[:, hh * MLA_V:hh * MLA_V + 1])
    do_h = jnp.where(half, do_ref[...], 0.0)
    delta = jnp.sum(do_h * o_ref[...], axis=1, keepdims=True)
    dp = _mm(do_h, v_ref[...], _NT)
    ds = p * (dp - delta) * ATTN_SCALE
    return p, ds, do_h


def attention_bwd(qcat, kcat, v, o, lse, do, s):
    tq, tk = _att_tiles(s)
    nq, nk = s // tq, s // tk

    def dq_body(q_ref, k_ref, v_ref, do_ref, o_ref, lse_ref, dq_ref):
        kk = pl.program_id(2)

        @pl.when(kk == 0)
        def _():
            dq_ref[...] = jnp.zeros(dq_ref.shape, F32)

        for hh in range(2):
            sl = slice(hh * HEAD_BLK, (hh + 1) * HEAD_BLK)
            _, ds, _ = _att_probs(q_ref, k_ref, v_ref, do_ref, o_ref, lse_ref, hh)
            dq_ref[:, sl] += _mm(ds, k_ref[:, sl])

    q_specs = [pl.BlockSpec((tq, 2 * HEAD_BLK), lambda p, i, kk: (i, p)),
               pl.BlockSpec((tk, 2 * HEAD_BLK), lambda p, i, kk: (kk, p)),
               pl.BlockSpec((tk, LANES), lambda p, i, kk: (kk, p)),
               pl.BlockSpec((tq, LANES), lambda p, i, kk: (i, p)),
               pl.BlockSpec((tq, LANES), lambda p, i, kk: (i, p)),
               pl.BlockSpec((None, tq, LANES), lambda p, i, kk: (p, i, 0))]
    dq = pl.pallas_call(
        dq_body, name="od_attn_dq", grid=(_PAIRS, nq, nk), in_specs=q_specs,
        out_specs=pl.BlockSpec((tq, 2 * HEAD_BLK), lambda p, i, kk: (i, p)),
        out_shape=jax.ShapeDtypeStruct((s, MLA_HEADS * HEAD_BLK), F32),
        compiler_params=_params(("parallel", "parallel", "arbitrary")),
    )(qcat, kcat, v, do, o, lse)

    def dkv_body(q_ref, k_ref, v_ref, do_ref, o_ref, lse_ref, dk_ref, dv_ref):
        qi = pl.program_id(2)

        @pl.when(qi == 0)
        def _():
            dk_ref[...] = jnp.zeros(dk_ref.shape, F32)
            dv_ref[...] = jnp.zeros(dv_ref.shape, F32)

        for hh in range(2):
            sl = slice(hh * HEAD_BLK, (hh + 1) * HEAD_BLK)
            p, ds, do_h = _att_probs(q_ref, k_ref, v_ref, do_ref, o_ref, lse_ref, hh)
            dv_ref[...] += _mm(p, do_h, _TN)
            dk_ref[:, sl] += _mm(ds, q_ref[:, sl], _TN)

    k_specs = [pl.BlockSpec((tq, 2 * HEAD_BLK), lambda p, kk, i: (i, p)),
               pl.BlockSpec((tk, 2 * HEAD_BLK), lambda p, kk, i: (kk, p)),
               pl.BlockSpec((tk, LANES), lambda p, kk, i: (kk, p)),
               pl.BlockSpec((tq, LANES), lambda p, kk, i: (i, p)),
               pl.BlockSpec((tq, LANES), lambda p, kk, i: (i, p)),
               pl.BlockSpec((None, tq, LANES), lambda p, kk, i: (p, i, 0))]
    dk, dv = pl.pallas_call(
        dkv_body, name="od_attn_dkv", grid=(_PAIRS, nk, nq), in_specs=k_specs,
        out_specs=[pl.BlockSpec((tk, 2 * HEAD_BLK), lambda p, kk, i: (kk, p)),
                   pl.BlockSpec((tk, LANES), lambda p, kk, i: (kk, p))],
        out_shape=[jax.ShapeDtypeStruct((s, MLA_HEADS * HEAD_BLK), F32), jax.ShapeDtypeStruct((s, MLA_WIDTH), F32)],
        compiler_params=_params(("parallel", "parallel", "arbitrary")),
    )(qcat, kcat, v, do, o, lse)
    return dq, dk, dv


def _pool_counts(n_rows, first_row, s, w):
    pos = first_row + lax.broadcasted_iota(jnp.int32, (n_rows, 1), 0)
    lo = jnp.clip(pos - w // 2, 0, s)
    hi = jnp.clip(pos + w - w // 2, 0, s)
    return jnp.maximum(hi - lo, 1).astype(F32)


def _window_sum(e, levels, mirrored):
    n = e.shape[0]
    acc = e + pltpu.roll(e, (n - 1) if mirrored else 1, 0)
    step = 1
    for _ in range(levels - 1):
        acc = pltpu.roll(acc, step, 0) + pltpu.roll(acc, n - step, 0)
        step *= 2
    return acc


def _pooled(ue, s, t):
    first = pl.program_id(0) * t
    outs = []
    for gi, w in enumerate(POOL_WINDOWS):
        eg = ue[:, gi * POOL_GROUP:(gi + 1) * POOL_GROUP]
        sm = _window_sum(eg, gi + 1, False)[HALO:HALO + t]
        outs.append(sm / _pool_counts(t, first, s, w) - eg[HALO:HALO + t])
    return outs


def odd_mix_fwd(proj, o, pool_w, pool_scale, rows):
    def fn(o_, gc, up, u, un, gd, pw, ps):
        pooled = _pooled(_ext(up, u, un, rows.n), rows.s, rows.t)
        lin = jnp.concatenate([_mm(pooled[g], pw[g]) for g in range(len(POOL_WINDOWS))], axis=1)
        return jnp.concatenate([o_ * _silu(gc), lin * ps * _silu(gd)], axis=1)

    w = POOL_WIDTH
    return rows_call("od_mix_fwd", rows, fn, [o, proj, proj, proj, proj, proj, pool_w, pool_scale],
                     [rows.tile(w), rows.tile(w, O_GC)] + rows.halo(w, O_UD)
                     + [rows.tile(w, O_GD), Rows.full((4, POOL_GROUP, POOL_GROUP)), Rows.full((1, w))],
                     [(2 * w, MATMUL_DTYPE)])[0]


def odd_mix_bwd(proj, o, pool_w, pool_scale, dycd, rows):
    w = POOL_WIDTH
    ng = len(POOL_WINDOWS)

    def fn(o_, gc, up, u, un, gdp, gd, gdn, pw, ps, dyc, dydp, dyd, dydn):
        n, t, s = rows.n, rows.t, rows.s
        dyc = dyc.astype(F32)
        do = dyc * _silu(gc)
        dgc = dyc * o_ * _dsilu(gc)
        pooled = _pooled(_ext(up, u, un, n), s, t)
        lin = jnp.concatenate([_mm(pooled[g], pw[g]) for g in range(ng)], axis=1)
        dydc = dyd.astype(F32)
        dgd = dydc * lin * ps * _dsilu(gd)
        dps = _rowsum(dydc * lin * _silu(gd))
        dlin_e = _ext(dydp, dyd, dydn, n) * ps * _silu(_ext(gdp, gd, gdn, n))
        first = pl.program_id(0) * t - HALO
        dus, dpws = [], []
        for g, win in enumerate(POOL_WINDOWS):
            sl = slice(g * POOL_GROUP, (g + 1) * POOL_GROUP)
            dle = dlin_e[:, sl]
            dpws.append(_mm(pooled[g], dle[HALO:HALO + t], _TN))
            dpe = _mm(dle, pw[g], _NT)
            gce = dpe / _pool_counts(t + 2 * HALO, first, s, win)
            dus.append(_window_sum(gce, g + 1, True)[HALO:HALO + t] - dpe[HALO:HALO + t])
        return do, dgc, jnp.concatenate(dus, axis=1), dgd, jnp.stack(dpws), dps

    return rows_call("od_mix_bwd", rows, fn,
                     [o, proj, proj, proj, proj, proj, proj, proj, pool_w, pool_scale, dycd, dycd, dycd, dycd],
                     [rows.tile(w), rows.tile(w, O_GC)] + rows.halo(w, O_UD) + rows.halo(w, O_GD)
                     + [Rows.full((ng, POOL_GROUP, POOL_GROUP)), Rows.full((1, w)), rows.tile(w, 0)]
                     + rows.halo(w, w),
                     [(w, F32)] * 4, [(ng, POOL_GROUP, POOL_GROUP), (1, w)])


def pad_odd_w_in(w):
    z = lambda n: jnp.zeros((w.shape[0], n), w.dtype)
    return jnp.concatenate([w[:, :384], z(ROPE_LO), w[:, 384:416], z(LANES - ROPE_LO - MLA_ROPE), w[:, 416:]], axis=1)


def unpad_odd_w_in(wp):
    return jnp.concatenate([wp[:, :384], wp[:, O_KR + ROPE_LO:O_KR + ROPE_LO + MLA_ROPE], wp[:, O_GC:]], axis=1)


def pad_w_uq(w):
    w3 = w.reshape(MLA_Q_RANK, MLA_HEADS, MLA_NOPE + MLA_ROPE)
    w3 = jnp.pad(w3, ((0, 0), (0, 0), (0, HEAD_BLK - MLA_NOPE - MLA_ROPE)))
    return w3.reshape(MLA_Q_RANK, MLA_HEADS * HEAD_BLK)


def unpad_w_uq(wp):
    return wp.reshape(MLA_Q_RANK, MLA_HEADS, HEAD_BLK)[..., :MLA_NOPE + MLA_ROPE].reshape(MLA_Q_RANK, -1)


def pad_w_ukv(w):
    w3 = w.reshape(MLA_KV_RANK, MLA_HEADS, MLA_NOPE + MLA_V)
    kp = jnp.pad(w3[..., :MLA_NOPE], ((0, 0), (0, 0), (0, HEAD_BLK - MLA_NOPE)))
    return jnp.concatenate([kp.reshape(MLA_KV_RANK, -1), w3[..., MLA_NOPE:].reshape(MLA_KV_RANK, -1)], axis=1)


def unpad_w_ukv(wp):
    kp = wp[:, :MLA_HEADS * HEAD_BLK].reshape(MLA_KV_RANK, MLA_HEADS, HEAD_BLK)[..., :MLA_NOPE]
    vp = wp[:, MLA_HEADS * HEAD_BLK:].reshape(MLA_KV_RANK, MLA_HEADS, MLA_V)
    return jnp.concatenate([kp, vp], axis=-1).reshape(MLA_KV_RANK, -1)


def odd_weights(od_w_in, od_q_norm_g, od_w_uq, od_kv_norm_g, od_w_ukv, od_pool_w, od_pool_scale, od_w_out,
                od_ln_g, od_ln_b):
    freq, sign = rope_rows()
    return dict(w_in_p=pad_odd_w_in(od_w_in).astype(MATMUL_DTYPE), gq=od_q_norm_g, gkv=od_kv_norm_g,
                w_uq_p=pad_w_uq(od_w_uq).astype(MATMUL_DTYPE), w_ukv_p=pad_w_ukv(od_w_ukv).astype(MATMUL_DTYPE),
                pool_w=od_pool_w, pool_scale=od_pool_scale, w_out=od_w_out.astype(MATMUL_DTYPE), ln_g=od_ln_g,
                ln_b=od_ln_b, freq=freq, sign=sign)


def odd_layer_loss(x, pos, target, w, rows):
    s = rows.s
    proj = matmul(x, w["w_in_p"], "nn", F32, "od_proj")
    cqn, ckvn = latent_norm_fwd(proj, w["gq"], w["gkv"], rows)
    qp = matmul(cqn, w["w_uq_p"], "nn", F32, "od_q_up")
    kvp = matmul(ckvn, w["w_ukv_p"], "nn", F32, "od_kv_up")
    qcat, kcat, v = rope_fwd(qp, kvp, proj, pos, w["freq"], w["sign"], rows)
    o, lse = attention_fwd(qcat, kcat, v, s)
    ycd = odd_mix_fwd(proj, o, w["pool_w"], w["pool_scale"], rows)
    h = matmul(ycd, w["w_out"], "nn", F32, "od_out")
    dres, dh, dln_g, dln_b, loss_lanes = final_ln_loss(x, h, w["ln_g"], w["ln_b"], target, rows)
    dycd = matmul(dh, w["w_out"], "nt", F32, "od_out_dx")
    dw_out = matmul(ycd, dh, "tn", F32, "od_out_dw")
    do, dgc, dud, dgd, dpool_w, dpool_scale = odd_mix_bwd(proj, o, w["pool_w"], w["pool_scale"], dycd, rows)
    dq, dk, dv = attention_bwd(qcat, kcat, v, o, lse, do, s)
    dqp, dkvp, dkr = rope_bwd(dq, dk, dv, pos, w["freq"], w["sign"], rows)
    dcqn = matmul(dqp, w["w_uq_p"], "nt", F32, "od_q_up_dx")
    dw_uq = unpad_w_uq(matmul(cqn, dqp, "tn", F32, "od_q_up_dw"))
    dckvn = matmul(dkvp, w["w_ukv_p"], "nt", F32, "od_kv_up_dx")
    dw_ukv = unpad_w_ukv(matmul(ckvn, dkvp, "tn", F32, "od_kv_up_dw"))
    dcq, dckv, dgq, dgkv = latent_norm_bwd(proj, w["gq"], w["gkv"], dcqn, dckvn, rows)
    dproj = jnp.concatenate([dcq, dckv, dkr, dgc, dud, dgd], axis=1).astype(MATMUL_DTYPE)
    dx = matmul(dproj, w["w_in_p"], "nt", F32, "od_proj_dx", add=dres)
    dw_in = unpad_odd_w_in(matmul(x, dproj, "tn", F32, "od_proj_dw"))
    g = dict(od_w_in=dw_in, od_q_norm_g=dgq, od_w_uq=dw_uq, od_kv_norm_g=dgkv, od_w_ukv=dw_ukv,
             od_pool_w=dpool_w, od_pool_scale=dpool_scale, od_w_out=dw_out, od_ln_g=dln_g, od_ln_b=dln_b)
    return loss_lanes, dx, g


_MESH = pl.DeviceIdType.MESH
_ANY = pl.BlockSpec(memory_space=pl.ANY)
N_CHIPS = 4


def _push_call(name, ins, out_shapes, plan, n_remote, n_local):
    n_in, n_out = len(ins), len(out_shapes)

    def body(*refs):
        in_refs, out_refs = refs[:n_in], refs[n_in:n_in + n_out]
        send_sems, recv_sems, local_sems = refs[n_in + n_out:]
        x, y, c = lax.axis_index("x"), lax.axis_index("y"), lax.axis_index("c")
        remote, local = plan(in_refs, out_refs, x, y, c)
        assert len(remote) == n_remote and len(local) == n_local
        sends = [pltpu.make_async_remote_copy(src_ref=s, dst_ref=d, send_sem=send_sems.at[k], recv_sem=recv_sems.at[k],
                                              device_id=dev, device_id_type=_MESH)
                 for k, (s, d, dev, _) in enumerate(remote)]
        recvs = [pltpu.make_async_remote_copy(src_ref=s, dst_ref=land, send_sem=send_sems.at[k],
                                              recv_sem=recv_sems.at[k], device_id=dev, device_id_type=_MESH)
                 for k, (s, _, dev, land) in enumerate(remote)]
        locs = [pltpu.make_async_copy(s, d, local_sems.at[k]) for k, (s, d) in enumerate(local)]
        for cp in sends + locs:
            cp.start()
        for cp in recvs:
            cp.wait_recv()
        for cp in sends:
            cp.wait_send()
        for cp in locs:
            cp.wait()

    return pl.pallas_call(
        body, name=name, in_specs=[_ANY] * n_in, out_specs=[_ANY] * n_out, out_shape=list(out_shapes),
        scratch_shapes=[pltpu.SemaphoreType.DMA((n_remote,)), pltpu.SemaphoreType.DMA((n_remote,)),
                        pltpu.SemaphoreType.DMA((max(n_local, 1),))],
    )(*ins)


def _other_chips(x, y):
    return [(1 - x, y), (x, 1 - y), (1 - x, 1 - y)]


def chips_allgather(bufs):
    def plan(in_refs, out_refs, x, y, c):
        me = 2 * x + y
        remote, local = [], []
        for src, out in zip(in_refs, out_refs):
            for (px, py) in _other_chips(x, y):
                remote.append((src, out.at[me], (px, py, c), out.at[2 * px + py]))
            local.append((src, out.at[me]))
        return remote, local

    shapes = [jax.ShapeDtypeStruct((N_CHIPS,) + b.shape, b.dtype) for b in bufs]
    return _push_call("weights_allgather", bufs, shapes, plan, 3 * len(bufs), len(bufs))


def sibling_send(buf, name):
    def plan(in_refs, out_refs, x, y, c):
        return [(in_refs[0], out_refs[0], (x, y, 1 - c), out_refs[0])], []

    return _push_call(name, [buf], [jax.ShapeDtypeStruct(buf.shape, buf.dtype)], plan, 1, 0)[0]


def chips_scatter(buf):
    def plan(in_refs, out_refs, x, y, c):
        me = 2 * x + y
        src, out = in_refs[0], out_refs[0]
        remote = [(src.at[2 * px + py], out.at[me], (px, py, c), out.at[2 * px + py]) for (px, py) in _other_chips(x, y)]
        return remote, [(src.at[me], out.at[me])]

    return _push_call("grads_scatter", [buf], [jax.ShapeDtypeStruct(buf.shape, buf.dtype)], plan, 3, 1)[0]


PACK_ROWS = 1024


def _flat_call(name, fn, ins, in_specs, n_out, rows_total):
    def body(*refs):
        res = fn(*[r[...] for r in refs[:len(ins)]])
        if not isinstance(res, (tuple, list)):
            res = (res,)
        for r, v in zip(refs[len(ins):], res):
            r[...] = v

    blk = pl.BlockSpec((PACK_ROWS, LANES), lambda i: (i, 0))
    return pl.pallas_call(
        body, name=name, grid=(rows_total // PACK_ROWS,), in_specs=in_specs, out_specs=[blk] * n_out,
        out_shape=[jax.ShapeDtypeStruct((rows_total, LANES), F32)] * n_out,
        compiler_params=_params(("parallel",)),
    )(*ins)


def add2(a, b, name):
    n, r, _ = a.shape
    blk = pl.BlockSpec((PACK_ROWS, LANES), lambda i: (i, 0))
    out = _flat_call(name, lambda u, v: u + v, [a.reshape(n * r, LANES), b.reshape(n * r, LANES)], [blk, blk], 1,
                     n * r)[0]
    return out.reshape(a.shape)


def sum_chips(buf):
    r = buf.shape[1]
    specs = [pl.BlockSpec((None, PACK_ROWS, LANES), lambda i, q=q: (q, i, 0)) for q in range(N_CHIPS)]
    return _flat_call("grads_sum", lambda a, b, c, d: ((a + b) + c) + d, [buf] * N_CHIPS, specs, 1, r)[0]


def adamw(w, g, m, v):
    def fn(w_, g_, m_, v_):
        m2 = ADAM_B1 * m_ + (1.0 - ADAM_B1) * g_
        v2 = ADAM_B2 * v_ + (1.0 - ADAM_B2) * jnp.square(g_)
        m_hat = m2 / (1.0 - ADAM_B1 ** ADAM_STEP)
        v_hat = v2 / (1.0 - ADAM_B2 ** ADAM_STEP)
        return -ADAM_LR * (m_hat / (jnp.sqrt(v_hat) + ADAM_EPS) + ADAM_WD * w_), m2, v2

    blk = pl.BlockSpec((PACK_ROWS, LANES), lambda i: (i, 0))
    return _flat_call("adamw", fn, [w, g, m, v], [blk] * 4, 3, w.shape[0])


WEIGHTS = (
    ("ev_w_in", (1, 1024, 7200), 2), ("ev_conv_w", (1, 4, 2048), 2), ("ev_conv_b", (1, 2048), None),
    ("ev_a_log", (1, 2, 16), None), ("ev_dt_bias", (1, 2, 16), None), ("ev_d_skip", (1, 2, 16), None),
    ("ev_norm_g", (1, 1024), None), ("ev_sc_conv_w", (1, 3, 1024), 2), ("ev_w_out", (1, 2048, 1024), 1),
    ("ev_ln_g", (1, 1024), None), ("ev_ln_b", (1, 1024), None), ("od_w_in", (1, 1024, 1952), 2),
    ("od_q_norm_g", (1, 256), 1), ("od_w_uq", (1, 256, 768), 2), ("od_kv_norm_g", (1, 128), None),
    ("od_w_ukv", (1, 128, 1024), 2), ("od_pool_w", (1, 4, 128, 128), None), ("od_pool_scale", (1, 512), 1),
    ("od_w_out", (1, 1024, 1024), 1), ("od_ln_g", (1, 1024), 1), ("od_ln_b", (1, 1024), 1),
)
BIG = ("ev_w_in", "ev_w_out", "od_w_in", "od_w_uq", "od_w_ukv", "od_w_out")


def _block_shape(shape, axis):
    if axis is None:
        return tuple(shape)
    return tuple(d // N_CHIPS if i == axis else d for i, d in enumerate(shape))


def _pack(arrs, quantum):
    flat = jnp.concatenate([a.reshape(-1) for a in arrs])
    n = flat.shape[0]
    padded = -(-n // quantum) * quantum
    return jnp.concatenate([flat, jnp.zeros((padded - n,), flat.dtype)]).reshape(-1, LANES)


def _unpack(flat, shapes):
    out, off = [], 0
    for sh in shapes:
        n = int(np.prod(sh))
        out.append(flat[off:off + n].reshape(sh))
        off += n
    return out


def gather_weights(local):
    sharded = [(n, sh, ax) for (n, sh, ax) in WEIGHTS if ax is not None]
    big = [(n, sh, ax) for (n, sh, ax) in sharded if n in BIG]
    small = [(n, sh, ax) for (n, sh, ax) in sharded if n not in BIG]
    pb = _pack([local[n].astype(MATMUL_DTYPE) for n, _, _ in big], 2 * SUBLANES * LANES)
    ps = _pack([local[n] for n, _, _ in small], SUBLANES * LANES)
    gb, gs = chips_allgather([pb, ps])
    full = {n: local[n] for (n, sh, ax) in WEIGHTS if ax is None}
    for group, g in ((big, gb), (small, gs)):
        parts = [_unpack(g[q].reshape(-1), [_block_shape(sh, ax) for _, sh, ax in group]) for q in range(N_CHIPS)]
        for i, (n, sh, ax) in enumerate(group):
            full[n] = jnp.concatenate([parts[q][i] for q in range(N_CHIPS)], axis=ax)
    return full


_PACK_QUANTUM = 2 * PACK_ROWS * LANES


def reduce_and_update(grads, local_w, local_m, local_v):
    c = lax.axis_index("c")
    blocks = [_block_shape(sh, ax) for _, sh, ax in WEIGHTS]
    packs = []
    for q in range(N_CHIPS):
        parts = []
        for (n, sh, ax), bs in zip(WEIGHTS, blocks):
            g = grads[n].reshape(sh)
            parts.append(g if ax is None else lax.slice_in_dim(g, q * bs[ax], (q + 1) * bs[ax], axis=ax))
        packs.append(_pack(parts, _PACK_QUANTUM))
    rh = packs[0].shape[0] // 2
    halves = [jnp.stack([p[h * rh:(h + 1) * rh] for p in packs]) for h in range(2)]
    both = jnp.stack(halves)
    keep = lax.dynamic_index_in_dim(both, c, 0, keepdims=False)
    give = lax.dynamic_index_in_dim(both, 1 - c, 0, keepdims=False)
    chip_half = add2(keep, sibling_send(give, "grads_to_sibling"), "grads_chip_sum")
    total_half = sum_chips(chips_scatter(chip_half))
    other_half = sibling_send(total_half, "grads_from_sibling")
    first = jnp.where(c == 0, total_half, other_half)
    second = jnp.where(c == 0, other_half, total_half)
    g_pack = jnp.concatenate([first, second], axis=0)
    w_pack, m_pack, v_pack = (_pack([d[n] for n, _, _ in WEIGHTS], _PACK_QUANTUM)
                              for d in (local_w, local_m, local_v))
    delta, new_m, new_v = adamw(w_pack, g_pack, m_pack, v_pack)
    names = [n for n, _, _ in WEIGHTS]
    return tuple(dict(zip(names, _unpack(p.reshape(-1), blocks))) for p in (g_pack, delta, new_m, new_v))


ROW_TILE = 256


def kernel(x, positions, ev_w_in, ev_conv_w, ev_conv_b, ev_a_log, ev_dt_bias, ev_d_skip, ev_norm_g, ev_sc_conv_w, ev_w_out, ev_ln_g, ev_ln_b, od_w_in, od_q_norm_g, od_w_uq, od_kv_norm_g, od_w_ukv, od_pool_w, od_pool_scale, od_w_out, od_ln_g, od_ln_b, loss_target, m_ev_w_in, m_ev_conv_w, m_ev_conv_b, m_ev_a_log, m_ev_dt_bias, m_ev_d_skip, m_ev_norm_g, m_ev_sc_conv_w, m_ev_w_out, m_ev_ln_g, m_ev_ln_b, m_od_w_in, m_od_q_norm_g, m_od_w_uq, m_od_kv_norm_g, m_od_w_ukv, m_od_pool_w, m_od_pool_scale, m_od_w_out, m_od_ln_g, m_od_ln_b, v_ev_w_in, v_ev_conv_w, v_ev_conv_b, v_ev_a_log, v_ev_dt_bias, v_ev_d_skip, v_ev_norm_g, v_ev_sc_conv_w, v_ev_w_out, v_ev_ln_g, v_ev_ln_b, v_od_w_in, v_od_q_norm_g, v_od_w_uq, v_od_kv_norm_g, v_od_w_ukv, v_od_pool_w, v_od_pool_scale, v_od_w_out, v_od_ln_g, v_od_ln_b):
    names = [n for n, _, _ in WEIGHTS]
    local_w = dict(zip(names, (ev_w_in, ev_conv_w, ev_conv_b, ev_a_log, ev_dt_bias, ev_d_skip, ev_norm_g, ev_sc_conv_w, ev_w_out, ev_ln_g, ev_ln_b, od_w_in, od_q_norm_g, od_w_uq, od_kv_norm_g, od_w_ukv, od_pool_w, od_pool_scale, od_w_out, od_ln_g, od_ln_b)))
    local_m = dict(zip(names, (m_ev_w_in, m_ev_conv_w, m_ev_conv_b, m_ev_a_log, m_ev_dt_bias, m_ev_d_skip, m_ev_norm_g, m_ev_sc_conv_w, m_ev_w_out, m_ev_ln_g, m_ev_ln_b, m_od_w_in, m_od_q_norm_g, m_od_w_uq, m_od_kv_norm_g, m_od_w_ukv, m_od_pool_w, m_od_pool_scale, m_od_w_out, m_od_ln_g, m_od_ln_b)))
    local_v = dict(zip(names, (v_ev_w_in, v_ev_conv_w, v_ev_conv_b, v_ev_a_log, v_ev_dt_bias, v_ev_d_skip, v_ev_norm_g, v_ev_sc_conv_w, v_ev_w_out, v_ev_ln_g, v_ev_ln_b, v_od_w_in, v_od_q_norm_g, v_od_w_uq, v_od_kv_norm_g, v_od_w_ukv, v_od_pool_w, v_od_pool_scale, v_od_w_out, v_od_ln_g, v_od_ln_b)))
    s = x.shape[1]
    rows = Rows(s, min(ROW_TILE, s))
    f = gather_weights(local_w)
    ew = even_weights(f["ev_w_in"][0], f["ev_conv_w"][0], f["ev_conv_b"], f["ev_a_log"][0], f["ev_dt_bias"][0],
                      f["ev_d_skip"][0], f["ev_norm_g"], f["ev_sc_conv_w"][0], f["ev_w_out"][0], f["ev_ln_g"],
                      f["ev_ln_b"])
    ow = odd_weights(f["od_w_in"][0], f["od_q_norm_g"], f["od_w_uq"][0], f["od_kv_norm_g"], f["od_w_ukv"][0],
                     f["od_pool_w"][0], f["od_pool_scale"], f["od_w_out"][0], f["od_ln_g"], f["od_ln_b"])
    x1, saved = even_layer(x[0], ew, rows)
    loss_lanes, dx1, g_odd = odd_layer_loss(x1, positions.reshape(s, 1), loss_target[0], ow, rows)
    dx0, g_even = even_layer_bwd(dx1, ew, saved, rows)
    loss = lax.psum(jnp.sum(loss_lanes), ("x", "y", "c"))
    grad, delta, new_m, new_v = reduce_and_update({**g_even, **g_odd}, local_w, local_m, local_v)
    return (loss, dx0[None], *[grad[n] for n in names], *[delta[n] for n in names],
            *[new_m[n] for n in names], *[new_v[n] for n in names])
```

```python
import functools
import math

import jax
import jax.numpy as jnp
import numpy as np
from jax import lax
from jax.experimental import pallas as pl
from jax.experimental.pallas import tpu as pltpu

F32 = jnp.float32
BF16 = jnp.bfloat16
MATMUL_DTYPE = jnp.bfloat16

D_MODEL = 1024
DEPTH = 2
SSD_HEADS, SSD_HEAD_DIM, SSD_GROUPS, SSD_STATE, SSD_CHUNK = 16, 64, 4, 128, 128
SSD_INNER = SSD_HEADS * SSD_HEAD_DIM
SSD_XBC = SSD_INNER + 2 * SSD_GROUPS * SSD_STATE
SC_WIDTH = 1024
MLA_HEADS, MLA_Q_RANK, MLA_KV_RANK, MLA_NOPE, MLA_ROPE, MLA_V = 8, 256, 128, 64, 32, 64
MLA_WIDTH = MLA_HEADS * MLA_V
ROPE_THETA = 10000.0
ATTN_SCALE = (MLA_NOPE + MLA_ROPE) ** -0.5
QSCALE = ATTN_SCALE * math.log2(math.e)
LN2 = math.log(2.0)
POOL_WINDOWS = (2, 4, 8, 16)
POOL_GROUP = 128
POOL_WIDTH = POOL_GROUP * len(POOL_WINDOWS)
EPS = 1e-5
ALPHA = (2 * DEPTH) ** 0.25
EVEN_PROJ, ODD_PROJ = 7200, 1952
ADAM_LR, ADAM_B1, ADAM_B2, ADAM_EPS, ADAM_WD, ADAM_STEP = 0.001, 0.9, 0.999, 1e-08, 0.01, 10

LANES = 128
SUBLANES = 8
HALO = SUBLANES
VMEM_LIMIT = 56 * 1024 * 1024

EVEN_P = 7296
ODD_P = 2048
E_Z, E_XBC, E_BG, E_CG, E_H, E_GATE, E_DT = 0, 1024, 3072, 4096, 5120, 6144, 7168
O_CQ, O_CKV, O_KR, O_GC, O_UD, O_GD = 0, 256, 384, 512, 1024, 1536


def _params(sem=None):
    return pltpu.CompilerParams(dimension_semantics=sem, vmem_limit_bytes=VMEM_LIMIT)


def _mm(a, b, dims=(((1,), (0,)), ((), ()))):
    return lax.dot_general(a.astype(MATMUL_DTYPE), b.astype(MATMUL_DTYPE), dims, preferred_element_type=F32)


_NN = (((1,), (0,)), ((), ()))
_NT = (((1,), (1,)), ((), ()))
_TN = (((0,), (0,)), ((), ()))


def _silu(v):
    return v * jax.nn.sigmoid(v)


def _dsilu(v):
    s = jax.nn.sigmoid(v)
    return s * (1.0 + v * (1.0 - s))


def _pick(n, prefs):
    for p in prefs:
        if n % p == 0:
            return p
    return n


def matmul(a, b, mode, out_dtype, name, add=None, tm=None, tn=None, tk=None):
    if mode == "nn":
        (m, k), (k2, n) = a.shape, b.shape
    elif mode == "nt":
        (m, k), (n, k2) = a.shape, b.shape
    else:
        (k, m), (k2, n) = a.shape, b.shape
    assert k == k2, (a.shape, b.shape, mode)
    wide = (2432, 2048, 1536, 1024, 512, 256, 128)
    if mode == "tn":
        tm = tm or _pick(m, (2432, 2048, 1024, 512, 256, 128))
        tn = tn or _pick(n, wide)
        tk = tk or _pick(k, (512, 256, 128))
    else:
        tm = tm or _pick(m, (512, 256, 128))
        tn = tn or _pick(n, wide)
        tk = tk or _pick(k, wide)
    nk = k // tk
    dims = {"nn": _NN, "nt": _NT, "tn": _TN}[mode]

    def body(a_ref, b_ref, *rest):
        o_ref, acc_ref = rest[-2:]
        kk = pl.program_id(2)

        @pl.when(kk == 0)
        def _():
            acc_ref[...] = jnp.zeros_like(acc_ref) if add is None else rest[0][...].astype(F32)

        acc_ref[...] += _mm(a_ref[...], b_ref[...], dims)

        @pl.when(kk == nk - 1)
        def _():
            o_ref[...] = acc_ref[...].astype(o_ref.dtype)

    a_spec = {"nn": pl.BlockSpec((tm, tk), lambda i, j, kk: (i, kk)),
              "nt": pl.BlockSpec((tm, tk), lambda i, j, kk: (i, kk)),
              "tn": pl.BlockSpec((tk, tm), lambda i, j, kk: (kk, i))}[mode]
    b_spec = {"nn": pl.BlockSpec((tk, tn), lambda i, j, kk: (kk, j)),
              "nt": pl.BlockSpec((tn, tk), lambda i, j, kk: (j, kk)),
              "tn": pl.BlockSpec((tk, tn), lambda i, j, kk: (kk, j))}[mode]
    return pl.pallas_call(
        body, name=name, grid=(m // tm, n // tn, nk),
        in_specs=[a_spec, b_spec] + ([] if add is None else [pl.BlockSpec((tm, tn), lambda i, j, kk: (i, j))]),
        out_specs=pl.BlockSpec((tm, tn), lambda i, j, kk: (i, j)),
        out_shape=jax.ShapeDtypeStruct((m, n), out_dtype),
        scratch_shapes=[pltpu.VMEM((tm, tn), F32)],
        compiler_params=_params(("parallel", "parallel", "arbitrary")),
    )(*((a, b) if add is None else (a, b, add)))


class Rows:
    def __init__(self, s, t):
        assert s % t == 0 and t % HALO == 0
        self.s, self.t, self.n = s, t, s // t

    def tile(self, width, col=0, lead=None):
        cb = col // width
        assert col % width == 0
        if lead is None:
            return pl.BlockSpec((self.t, width), lambda i: (i, cb))
        return pl.BlockSpec((None, self.t, width), lambda i: (lead, i, cb))

    def prev(self, width, col=0, lead=None):
        cb, r = col // width, self.t // HALO
        if lead is None:
            return pl.BlockSpec((HALO, width), lambda i: (jnp.maximum(i * r - 1, 0), cb))
        return pl.BlockSpec((None, HALO, width), lambda i: (lead, jnp.maximum(i * r - 1, 0), cb))

    def next(self, width, col=0, lead=None):
        cb, r, last = col // width, self.t // HALO, self.s // HALO - 1
        if lead is None:
            return pl.BlockSpec((HALO, width), lambda i: (jnp.minimum((i + 1) * r, last), cb))
        return pl.BlockSpec((None, HALO, width), lambda i: (lead, jnp.minimum((i + 1) * r, last), cb))

    def halo(self, width, col=0, lead=None):
        return [self.prev(width, col, lead), self.tile(width, col, lead), self.next(width, col, lead)]

    @staticmethod
    def full(shape):
        nd = len(shape)
        return pl.BlockSpec(tuple(shape), lambda i: (0,) * nd)


def rows_call(name, rows, fn, ins, in_specs, row_outs, acc_outs=()):
    n_row = len(row_outs)

    def body(*refs):
        in_refs = refs[:len(ins)]
        out_refs = refs[len(ins):]
        res = fn(*[r[...] for r in in_refs])
        if not isinstance(res, (tuple, list)):
            res = (res,)
        for r, v in zip(out_refs[:n_row], res[:n_row]):
            r[...] = v.astype(r.dtype)
        if acc_outs:
            first = pl.program_id(0) == 0

            @pl.when(first)
            def _():
                for r, v in zip(out_refs[n_row:], res[n_row:]):
                    r[...] = v.astype(F32)

            @pl.when(jnp.logical_not(first))
            def _():
                for r, v in zip(out_refs[n_row:], res[n_row:]):
                    r[...] += v.astype(F32)

    out_shape = [jax.ShapeDtypeStruct((rows.s, w), dt) for (w, dt) in row_outs]
    out_specs = [rows.tile(w) for (w, dt) in row_outs]
    out_shape += [jax.ShapeDtypeStruct(tuple(sh), F32) for sh in acc_outs]
    out_specs += [Rows.full(sh) for sh in acc_outs]
    return pl.pallas_call(
        body, name=name, grid=(rows.n,), in_specs=list(in_specs), out_specs=out_specs, out_shape=out_shape,
        compiler_params=_params(("arbitrary",)),
    )(*ins)


def _edge_zero(prev, nxt, n_tiles):
    i = pl.program_id(0)
    prev = jnp.where(i == 0, jnp.zeros_like(prev), prev)
    nxt = jnp.where(i == n_tiles - 1, jnp.zeros_like(nxt), nxt)
    return prev, nxt


def _ext(prev, cur, nxt, n_tiles):
    prev, nxt = _edge_zero(prev, nxt, n_tiles)
    return jnp.concatenate([prev.astype(F32), cur.astype(F32), nxt.astype(F32)], axis=0)


def _shift(ext, off):
    n = ext.shape[0]
    t = n - 2 * HALO
    if off == 0:
        return ext[HALO:HALO + t]
    return pltpu.roll(ext, (-off) % n, 0)[HALO:HALO + t]


def _rowsum(v):
    return jnp.sum(v, axis=0, keepdims=True)


CONV_OFFS = (-2, -1, 0, 1)
SC_OFFS = (-1, 0, 1)


def conv_fwd(proj, conv_w, conv_b, rows):
    def fn(p, c, nx, w, b):
        e = _ext(p, c, nx, rows.n)
        pre = b
        for k, off in enumerate(CONV_OFFS):
            pre = pre + w[k:k + 1, :] * _shift(e, off)
        return pre, _silu(pre)

    outs = [rows_call(f"ev_conv_fwd{h}", rows, fn,
                      [proj, proj, proj, conv_w[:, h * 1024:(h + 1) * 1024], conv_b[:, h * 1024:(h + 1) * 1024]],
                      rows.halo(1024, E_XBC + h * 1024) + [Rows.full((4, 1024)), Rows.full((1, 1024))],
                      [(1024, F32), (1024, F32)]) for h in range(2)]
    return outs


def conv_bwd(proj, pre_h, du_h, conv_w, rows):
    res = []
    for h in range(2):
        def fn(*a):
            w = a[-1]
            xe = _ext(a[0], a[1], a[2], rows.n)
            pe = jnp.concatenate([a[3], a[4], a[5]], axis=0)
            g = a[6:-1]
            du = _ext(g[0], g[1], g[2], rows.n) + _ext(g[3], g[4], g[5], rows.n)
            dpre = du * _dsilu(pe)
            dx = jnp.zeros_like(a[1], dtype=F32)
            dws = []
            for k, off in enumerate(CONV_OFFS):
                dx = dx + w[k:k + 1, :] * _shift(dpre, -off)
                dws.append(_rowsum(_shift(dpre, 0) * _shift(xe, off)))
            dw = jnp.concatenate(dws + [jnp.zeros((4, dx.shape[1]), F32)], axis=0)
            return dx, dw, _rowsum(_shift(dpre, 0))

        gi = [du_h[h]] * 6
        gs = rows.halo(1024, 0, lead=0) + rows.halo(1024, 0, lead=1)
        res.append(rows_call(
            f"ev_conv_bwd{h}", rows, fn,
            [proj] * 3 + [pre_h[h]] * 3 + gi + [conv_w[:, h * 1024:(h + 1) * 1024]],
            rows.halo(1024, E_XBC + h * 1024) + rows.halo(1024) + gs + [Rows.full((4, 1024))],
            [(1024, F32)], [(8, 1024), (1, 1024)]))
    dconv_w = jnp.concatenate([res[0][1][:4], res[1][1][:4]], axis=1)
    dconv_b = jnp.concatenate([res[0][2], res[1][2]], axis=1)
    return [res[0][0], res[1][0]], dconv_w, dconv_b


def _head_row(p):
    return jnp.concatenate([p.reshape(1, 2 * SSD_HEADS), jnp.zeros((1, LANES - 2 * SSD_HEADS), F32)], axis=1)


def _head_unrow(r):
    return r[:, :2 * SSD_HEADS].reshape(2, SSD_HEADS)


def _ssd_pre_fn(dtraw, bias_row, alog_row):
    q = dtraw.shape[0]
    dt = jax.nn.softplus(dtraw + bias_row)
    da = dt * (-jnp.exp(alog_row))
    li = lax.broadcasted_iota(jnp.int32, (q, q), 0)
    si = lax.broadcasted_iota(jnp.int32, (q, q), 1)
    tril = (li >= si).astype(F32)
    csf = lax.dot_general(tril, da, _NN, precision=lax.Precision.HIGHEST, preferred_element_type=F32)
    tot = jnp.sum(da, axis=0, keepdims=True)
    lane = lax.broadcasted_iota(jnp.int32, (1, LANES), 1)
    cs = jnp.where(lane < SSD_HEADS, csf, tot - csf + da)
    return dt, cs


def ssd_pre(proj, bias_row, alog_row, s):
    rows = Rows(s, SSD_CHUNK)
    return rows_call("ev_ssd_pre", rows, _ssd_pre_fn, [proj, bias_row, alog_row],
                     [rows.tile(LANES, E_DT), Rows.full((1, LANES)), Rows.full((1, LANES))],
                     [(LANES, F32), (LANES, F32)])


def ssd_pre_bwd(proj, bias_row, alog_row, ddt, dcs, s):
    rows = Rows(s, SSD_CHUNK)

    def fn(dtraw, b, al, g0, g1, c0, c1):
        _, vjp = jax.vjp(_ssd_pre_fn, dtraw, b, al)
        return vjp((g0 + g1, c0 + c1))

    return rows_call("ev_ssd_pre_bwd", rows, fn, [proj, bias_row, alog_row, ddt, ddt, dcs, dcs],
                     [rows.tile(LANES, E_DT), Rows.full((1, LANES)), Rows.full((1, LANES)),
                      rows.tile(LANES, 0, lead=0), rows.tile(LANES, 0, lead=1),
                      rows.tile(LANES, 0, lead=0), rows.tile(LANES, 0, lead=1)],
                     [(LANES, F32)], [(1, LANES), (1, LANES)])


_N_PAIR = SSD_HEADS // 2
_BC = SSD_GROUPS * SSD_STATE


def _ssd_chunk_fn(x, bc, dt, cs, h_in, dsk_row, d):
    q = x.shape[0]
    lane = lax.broadcasted_iota(jnp.int32, (1, LANES), 1)
    sub = lax.broadcasted_iota(jnp.int32, (LANES, 1), 0)
    half = lane < SSD_HEAD_DIM

    def col(v, l):
        return jnp.sum(jnp.where(lane == l, v, 0.0), axis=1, keepdims=True)

    cst = cs.T

    def row(l):
        return jnp.sum(jnp.where(sub == l, cst, 0.0), axis=0, keepdims=True)

    li = lax.broadcasted_iota(jnp.int32, (q, 1), 0)
    si = lax.broadcasted_iota(jnp.int32, (1, q), 1)
    mask = (li - si) * (1 - 2 * d) >= 0
    end_row = jnp.where(d == 0, q - 1, 0)
    tot = jnp.sum(jnp.where(li == end_row, cs, 0.0), axis=0, keepdims=True)
    mine = (lane >= SSD_HEADS * d) & (lane < SSD_HEADS * (d + 1))
    e_cs, e_dec, e_tot = jnp.exp(cs), jnp.exp(jnp.where(mine, tot - cs, 0.0)), jnp.exp(tot)
    ys, hs, cbs = [], [], {}
    for j in range(_N_PAIR):
        g = j // (_N_PAIR // SSD_GROUPS)
        l0 = SSD_HEADS * d + 2 * j
        xj = x[:, j * LANES:(j + 1) * LANES]
        bm = bc[:, g * SSD_STATE:(g + 1) * SSD_STATE]
        cm = bc[:, _BC + g * SSD_STATE:_BC + (g + 1) * SSD_STATE]
        xd = xj * jnp.where(half, col(dt, l0), col(dt, l0 + 1))
        if g not in cbs:
            cbs[g] = _mm(cm, bm, _NT)
        cb = cbs[g]
        w0 = cb * jnp.exp(jnp.where(mask, col(cs, l0) - row(l0), -jnp.inf))
        w1 = cb * jnp.exp(jnp.where(mask, col(cs, l0 + 1) - row(l0 + 1), -jnp.inf))
        y = jnp.where(half, _mm(w0, xd), _mm(w1, xd))
        st = jnp.where(half, _mm(bm * col(e_dec, l0), xd, _TN), _mm(bm * col(e_dec, l0 + 1), xd, _TN))
        y = y + _mm(cm, h_in[j]) * jnp.where(half, col(e_cs, l0), col(e_cs, l0 + 1))
        y = y + jnp.where(half, col(dsk_row, l0), col(dsk_row, l0 + 1)) * xj
        ys.append(y)
        hs.append(h_in[j] * jnp.where(half, col(e_tot, l0), col(e_tot, l0 + 1)) + st)
    return jnp.concatenate(ys, axis=1), jnp.stack(hs)


def _chunk_of(d, ci, nc, backward):
    up = ci if not backward else nc - 1 - ci
    return up + d * (nc - 1 - 2 * up)


def ssd_fwd(u_h, dt, cs, dsk_row, s):
    nc = s // SSD_CHUNK
    q = SSD_CHUNK

    def body(x_ref, bc_ref, dt_ref, cs_ref, dsk_ref, y_ref, hs_ref, st_ref):
        d, ci = pl.program_id(0), pl.program_id(1)

        @pl.when(ci == 0)
        def _():
            st_ref[...] = jnp.zeros(st_ref.shape, F32)

        h_in = st_ref[...]
        y, h_out = _ssd_chunk_fn(x_ref[...], bc_ref[...], dt_ref[...], cs_ref[...], h_in, dsk_ref[...], d)
        y_ref[...] = y
        hs_ref[...] = h_in
        st_ref[...] = h_out

    ch = lambda d, ci: _chunk_of(d, ci, nc, False)
    return pl.pallas_call(
        body, name="ev_ssd_fwd", grid=(2, nc),
        in_specs=[pl.BlockSpec((q, SSD_INNER), lambda d, ci: (ch(d, ci), 0)),
                  pl.BlockSpec((q, 2 * _BC), lambda d, ci: (ch(d, ci), 0)),
                  pl.BlockSpec((q, LANES), lambda d, ci: (ch(d, ci), 0)),
                  pl.BlockSpec((q, LANES), lambda d, ci: (ch(d, ci), 0)),
                  pl.BlockSpec((1, LANES), lambda d, ci: (0, 0))],
        out_specs=[pl.BlockSpec((None, q, SSD_INNER), lambda d, ci: (d, ch(d, ci), 0)),
                   pl.BlockSpec((None, None, _N_PAIR, LANES, LANES), lambda d, ci: (d, ch(d, ci), 0, 0, 0))],
        out_shape=[jax.ShapeDtypeStruct((2, s, SSD_INNER), F32),
                   jax.ShapeDtypeStruct((2, nc, _N_PAIR, LANES, LANES), F32)],
        scratch_shapes=[pltpu.VMEM((_N_PAIR, LANES, LANES), F32)],
        compiler_params=_params(("arbitrary", "arbitrary")),
    )(u_h[0], u_h[1], dt, cs, dsk_row)


def ssd_bwd(u_h, dt, cs, dsk_row, hs, dy, s):
    nc = s // SSD_CHUNK
    q = SSD_CHUNK

    def body(x_ref, bc_ref, dt_ref, cs_ref, dsk_ref, hs_ref, dy_ref,
             dx_ref, dbc_ref, ddt_ref, dcs_ref, ddsk_ref, dst_ref):
        d, ci = pl.program_id(0), pl.program_id(1)

        @pl.when(ci == 0)
        def _():
            dst_ref[...] = jnp.zeros(dst_ref.shape, F32)

        f = functools.partial(_ssd_chunk_fn, d=d)
        _, vjp = jax.vjp(f, x_ref[...], bc_ref[...], dt_ref[...], cs_ref[...], hs_ref[...], dsk_ref[...])
        dx, dbc, ddt, dcs, dh, ddsk = vjp((dy_ref[...], dst_ref[...]))
        dx_ref[...] = dx
        dbc_ref[...] = dbc
        ddt_ref[...] = ddt
        dcs_ref[...] = dcs
        dst_ref[...] = dh

        @pl.when(ci == 0)
        def _():
            ddsk_ref[...] = ddsk

        @pl.when(ci != 0)
        def _():
            ddsk_ref[...] += ddsk

    ch = lambda d, ci: _chunk_of(d, ci, nc, True)
    return pl.pallas_call(
        body, name="ev_ssd_bwd", grid=(2, nc),
        in_specs=[pl.BlockSpec((q, SSD_INNER), lambda d, ci: (ch(d, ci), 0)),
                  pl.BlockSpec((q, 2 * _BC), lambda d, ci: (ch(d, ci), 0)),
                  pl.BlockSpec((q, LANES), lambda d, ci: (ch(d, ci), 0)),
                  pl.BlockSpec((q, LANES), lambda d, ci: (ch(d, ci), 0)),
                  pl.BlockSpec((1, LANES), lambda d, ci: (0, 0)),
                  pl.BlockSpec((None, None, _N_PAIR, LANES, LANES), lambda d, ci: (d, ch(d, ci), 0, 0, 0)),
                  pl.BlockSpec((q, SSD_INNER), lambda d, ci: (ch(d, ci), 0))],
        out_specs=[pl.BlockSpec((None, q, SSD_INNER), lambda d, ci: (d, ch(d, ci), 0)),
                   pl.BlockSpec((None, q, 2 * _BC), lambda d, ci: (d, ch(d, ci), 0)),
                   pl.BlockSpec((None, q, LANES), lambda d, ci: (d, ch(d, ci), 0)),
                   pl.BlockSpec((None, q, LANES), lambda d, ci: (d, ch(d, ci), 0)),
                   pl.BlockSpec((None, 1, LANES), lambda d, ci: (d, 0, 0))],
        out_shape=[jax.ShapeDtypeStruct((2, s, SSD_INNER), F32),
                   jax.ShapeDtypeStruct((2, s, 2 * _BC), F32),
                   jax.ShapeDtypeStruct((2, s, LANES), F32),
                   jax.ShapeDtypeStruct((2, s, LANES), F32),
                   jax.ShapeDtypeStruct((2, 1, LANES), F32)],
        scratch_shapes=[pltpu.VMEM((_N_PAIR, LANES, LANES), F32)],
        compiler_params=_params(("arbitrary", "arbitrary")),
    )(u_h[0], u_h[1], dt, cs, dsk_row, hs, dy)


def _gated_rms(ys, z, g):
    t1 = ys * _silu(z)
    return t1 * lax.rsqrt(jnp.mean(t1 * t1, axis=-1, keepdims=True) + EPS) * g


def even_mix_fwd(proj, y2, norm_g, sc_w, rows):
    def fn(yf, yb, z, bg, cgp, cg, cgn, hp, hh, hn, gate, g, w):
        ya = _gated_rms(yf + yb, z, g)
        me = _ext(cgp, cg, cgn, rows.n) * _ext(hp, hh, hn, rows.n)
        cm = sum(w[k:k + 1, :] * _shift(me, off) for k, off in enumerate(SC_OFFS))
        return jnp.concatenate([ya, bg * cm * _silu(gate)], axis=1)

    w = 1024
    return rows_call("ev_mix_fwd", rows, fn,
                     [y2, y2, proj, proj] + [proj] * 6 + [proj, norm_g, sc_w],
                     [rows.tile(w, 0, lead=0), rows.tile(w, 0, lead=1), rows.tile(w, E_Z), rows.tile(w, E_BG)]
                     + rows.halo(w, E_CG) + rows.halo(w, E_H)
                     + [rows.tile(w, E_GATE), Rows.full((1, w)), Rows.full((3, w))],
                     [(2 * w, MATMUL_DTYPE)])[0]


def even_mix_bwd(proj, y2, norm_g, sc_w, dyab, rows):
    w = 1024

    def fn(yf, yb, z, g, sw, dya, *a):
        (dybp, dyb, dybn, bgp, bg, bgn, gtp, gt, gtn, cgp, cg, cgn, hp, hh, hn) = a
        _, vjp = jax.vjp(_gated_rms, yf + yb, z, g)
        dys, dz, dg = vjp(dya.astype(F32))
        n = rows.n
        dye, bge, gte = _ext(dybp, dyb, dybn, n), _ext(bgp, bg, bgn, n), _ext(gtp, gt, gtn, n)
        cge, he = _ext(cgp, cg, cgn, n), _ext(hp, hh, hn, n)
        me = cge * he
        cm = sum(sw[k:k + 1, :] * _shift(me, off) for k, off in enumerate(SC_OFFS))
        dyc = dyb.astype(F32)
        dbg = dyc * cm * _silu(gt)
        dgate = dyc * bg * cm * _dsilu(gt)
        dcme = dye * bge * _silu(gte)
        dm = sum(sw[k:k + 1, :] * _shift(dcme, -off) for k, off in enumerate(SC_OFFS))
        dcm = _shift(dcme, 0)
        dws = [_rowsum(dcm * _shift(me, off)) for off in SC_OFFS]
        dsw = jnp.concatenate(dws + [jnp.zeros((5, w), F32)], axis=0)
        return dys, dz, dbg, dm * hh, dm * cg, dgate, _rowsum(dg), dsw

    return rows_call(
        "ev_mix_bwd", rows, fn,
        [y2, y2, proj, norm_g, sc_w, dyab] + [dyab] * 3 + [proj] * 12,
        [rows.tile(w, 0, lead=0), rows.tile(w, 0, lead=1), rows.tile(w, E_Z), Rows.full((1, w)), Rows.full((3, w)),
         rows.tile(w, 0)] + rows.halo(w, w) + rows.halo(w, E_BG) + rows.halo(w, E_GATE)
        + rows.halo(w, E_CG) + rows.halo(w, E_H),
        [(w, F32)] * 6, [(1, w), (8, w)])


def _res_ln(x, h, g, b):
    v = ALPHA * x + h
    mu = jnp.mean(v, axis=-1, keepdims=True)
    var = jnp.mean(jnp.square(v - mu), axis=-1, keepdims=True)
    return (v - mu) * lax.rsqrt(var + EPS) * g + b


def res_ln_fwd(x, h, g, b, rows, name):
    return rows_call(name, rows, _res_ln, [x, h, g, b],
                     [rows.tile(D_MODEL), rows.tile(D_MODEL), Rows.full((1, D_MODEL)), Rows.full((1, D_MODEL))],
                     [(D_MODEL, F32)])[0]


def res_ln_bwd(x, h, g, b, dy, rows, name):
    def fn(x_, h_, g_, b_, dy_):
        _, vjp = jax.vjp(_res_ln, x_, h_, g_, b_)
        dx, dh, dg, db = vjp(dy_)
        return dx, dh, dg, db

    return rows_call(name, rows, fn, [x, h, g, b, dy],
                     [rows.tile(D_MODEL), rows.tile(D_MODEL), Rows.full((1, D_MODEL)), Rows.full((1, D_MODEL)),
                      rows.tile(D_MODEL)],
                     [(D_MODEL, F32), (D_MODEL, F32)], [(1, D_MODEL), (1, D_MODEL)])


def final_ln_loss(x, h, g, b, target, rows):
    def fn(x_, h_, g_, b_, t_):
        y, vjp = jax.vjp(_res_ln, x_, h_, g_, b_)
        err = y - t_
        dx, dh, dg, db = vjp(err * (1.0 / D_MODEL))
        return dx, dh, dg, db, _rowsum(jnp.square(err)) * (0.5 / D_MODEL)

    return rows_call("od_ln_loss", rows, fn, [x, h, g, b, target],
                     [rows.tile(D_MODEL), rows.tile(D_MODEL), Rows.full((1, D_MODEL)), Rows.full((1, D_MODEL)),
                      rows.tile(D_MODEL)],
                     [(D_MODEL, F32), (D_MODEL, F32)], [(1, D_MODEL), (1, D_MODEL), (1, D_MODEL)])


def pad_even_w_in(w):
    return jnp.concatenate([w[:, :3072], w[:, 3104:], w[:, 3072:3104],
                            jnp.zeros((w.shape[0], EVEN_P - EVEN_PROJ), w.dtype)], axis=1)


def unpad_even_w_in_t(wpt):
    return jnp.concatenate([wpt[:3072], wpt[E_DT:E_DT + 32], wpt[3072:E_DT]], axis=0)


def even_layer(x, w, rows):
    s = rows.s
    proj = matmul(x, w["w_in_p"], "nn", F32, "ev_proj")
    (pre0, u0), (pre1, u1) = conv_fwd(proj, w["conv_w"], w["conv_b"], rows)
    dt, cs = ssd_pre(proj, w["bias_row"], w["alog_row"], s)
    y2, hs = ssd_fwd((u0, u1), dt, cs, w["dsk_row"], s)
    yab = even_mix_fwd(proj, y2, w["norm_g"], w["sc_w"], rows)
    h = matmul(yab, w["w_out"], "nn", F32, "ev_out")
    x1 = res_ln_fwd(x, h, w["ln_g"], w["ln_b"], rows, "ev_ln")
    return x1, dict(x=x, proj=proj, pre=(pre0, pre1), u=(u0, u1), dt=dt, cs=cs, y2=y2, hs=hs, yab=yab, h=h)


def even_layer_bwd(dx1, w, sv, rows):
    s = rows.s
    dres, dh, dln_g, dln_b = res_ln_bwd(sv["x"], sv["h"], w["ln_g"], w["ln_b"], dx1, rows, "ev_ln_bwd")
    dyab = matmul(dh, w["w_out"], "nt", F32, "ev_out_dx")
    dw_out = matmul(sv["yab"], dh, "tn", F32, "ev_out_dw")
    dys, dz, dbg, dcg, dhh, dgate, dnorm_g, dsw = even_mix_bwd(sv["proj"], sv["y2"], w["norm_g"], w["sc_w"], dyab,
                                                               rows)
    dxs, dbc, ddt, dcs, ddsk = ssd_bwd(sv["u"], sv["dt"], sv["cs"], w["dsk_row"], sv["hs"], dys, s)
    ddtraw, dbias_row, dalog_row = ssd_pre_bwd(sv["proj"], w["bias_row"], w["alog_row"], ddt, dcs, s)
    (dxbc0, dxbc1), dconv_w, dconv_b = conv_bwd(sv["proj"], sv["pre"], (dxs, dbc), w["conv_w"], rows)
    dproj = jnp.concatenate([dz, dxbc0, dxbc1, dbg, dcg, dhh, dgate, ddtraw], axis=1).astype(MATMUL_DTYPE)
    dx0 = matmul(dproj, w["w_in_p"], "nt", F32, "ev_proj_dx", add=dres)
    dw_in = unpad_even_w_in_t(matmul(dproj, sv["x"], "tn", F32, "ev_proj_dw"))
    g = dict(ev_w_in=dw_in, ev_conv_w=dconv_w, ev_conv_b=dconv_b,
             ev_a_log=_head_unrow(dalog_row), ev_dt_bias=_head_unrow(dbias_row),
             ev_d_skip=_head_unrow(ddsk[0] + ddsk[1]), ev_norm_g=dnorm_g, ev_sc_conv_w=dsw[:3],
             ev_w_out=dw_out, ev_ln_g=dln_g, ev_ln_b=dln_b)
    return dx0, g


def even_weights(ev_w_in, ev_conv_w, ev_conv_b, ev_a_log, ev_dt_bias, ev_d_skip, ev_norm_g, ev_sc_conv_w,
                 ev_w_out, ev_ln_g, ev_ln_b):
    return dict(w_in_p=pad_even_w_in(ev_w_in).astype(MATMUL_DTYPE), conv_w=ev_conv_w, conv_b=ev_conv_b,
                alog_row=_head_row(ev_a_log), bias_row=_head_row(ev_dt_bias), dsk_row=_head_row(ev_d_skip),
                norm_g=ev_norm_g, sc_w=ev_sc_conv_w, w_out=ev_w_out.astype(MATMUL_DTYPE), ln_g=ev_ln_g,
                ln_b=ev_ln_b)


HEAD_BLK = LANES
ROPE_LO = MLA_NOPE
ROPE_HALF = MLA_ROPE // 2


def _rms(v, g):
    return v * lax.rsqrt(jnp.mean(v * v, axis=-1, keepdims=True) + EPS) * g


def latent_norm_fwd(proj, gq, gkv, rows):
    def fn(cq, ckv, gq_, gkv_):
        return _rms(cq, gq_), _rms(ckv, gkv_)

    return rows_call("od_norm_fwd", rows, fn, [proj, proj, gq, gkv],
                     [rows.tile(MLA_Q_RANK, O_CQ), rows.tile(MLA_KV_RANK, O_CKV), Rows.full((1, MLA_Q_RANK)),
                      Rows.full((1, MLA_KV_RANK))],
                     [(MLA_Q_RANK, MATMUL_DTYPE), (MLA_KV_RANK, MATMUL_DTYPE)])


def latent_norm_bwd(proj, gq, gkv, dcqn, dckvn, rows):
    def fn(cq, ckv, gq_, gkv_, d1, d2):
        _, vjp = jax.vjp(_rms, cq, gq_)
        dcq, dgq = vjp(d1)
        _, vjp2 = jax.vjp(_rms, ckv, gkv_)
        dckv, dgkv = vjp2(d2)
        return dcq, dckv, dgq, dgkv

    return rows_call("od_norm_bwd", rows, fn, [proj, proj, gq, gkv, dcqn, dckvn],
                     [rows.tile(MLA_Q_RANK, O_CQ), rows.tile(MLA_KV_RANK, O_CKV), Rows.full((1, MLA_Q_RANK)),
                      Rows.full((1, MLA_KV_RANK)), rows.tile(MLA_Q_RANK), rows.tile(MLA_KV_RANK)],
                     [(MLA_Q_RANK, F32), (MLA_KV_RANK, F32)], [(1, MLA_Q_RANK), (1, MLA_KV_RANK)])


def rope_rows():
    lane = np.arange(LANES)
    inv = ROPE_THETA ** (-jnp.arange(ROPE_HALF, dtype=F32) / ROPE_HALF)
    on = (lane >= ROPE_LO) & (lane < ROPE_LO + MLA_ROPE)
    freq = jnp.where(on, inv[(lane - ROPE_LO) % ROPE_HALF], 0.0).reshape(1, LANES).astype(F32)
    sign = np.where(on, np.where(lane < ROPE_LO + ROPE_HALF, -1.0, 1.0), 0.0).reshape(1, LANES).astype(np.float32)
    return freq, jnp.asarray(sign)


def _rot_tables(pos, freq, sign):
    ang = pos.astype(F32) * freq
    return jnp.cos(ang), jnp.sin(ang) * sign


def _swap_halves(v):
    lane = lax.broadcasted_iota(jnp.int32, (1, LANES), 1)
    return jnp.where(lane < ROPE_LO + ROPE_HALF, pltpu.roll(v, LANES - ROPE_HALF, 1), pltpu.roll(v, ROPE_HALF, 1))


def rope_fwd(qp, kvp, proj, pos, freq, sign, rows):
    def fn(q, k, kr, v, p, f, sg):
        c, sn = _rot_tables(p, f, sg)
        rk = kr * c + _swap_halves(kr) * sn
        one = (lax.broadcasted_iota(jnp.int32, (v.shape[0], HEAD_BLK - MLA_V), 1) == 0).astype(F32)
        qs, ks, vs = [], [], []
        for h in range(MLA_HEADS):
            qh = q[:, h * HEAD_BLK:(h + 1) * HEAD_BLK]
            qs.append((qh * c + _swap_halves(qh) * sn) * QSCALE)
            ks.append(k[:, h * HEAD_BLK:(h + 1) * HEAD_BLK] + rk)
            vs += [v[:, h * MLA_V:(h + 1) * MLA_V], one]
        return jnp.concatenate(qs, axis=1), jnp.concatenate(ks, axis=1), jnp.concatenate(vs, axis=1)

    w = MLA_HEADS * HEAD_BLK
    return rows_call("od_rope_fwd", rows, fn, [qp, kvp, proj, kvp, pos, freq, sign],
                     [rows.tile(w), rows.tile(w, 0), rows.tile(LANES, O_KR), rows.tile(MLA_WIDTH, w),
                      rows.tile(1), Rows.full((1, LANES)), Rows.full((1, LANES))],
                     [(w, MATMUL_DTYPE), (w, MATMUL_DTYPE), (w, MATMUL_DTYPE)])


def rope_bwd(dq, dk, dv, pos, freq, sign, rows):
    def fn(dq_, dk_, dv_, p, f, sg):
        c, sn = _rot_tables(p, f, sg)
        on = jnp.abs(sg)
        outs, dks, dvs, dkr = [], [], [], jnp.zeros((dq_.shape[0], LANES), F32)
        for h in range(MLA_HEADS):
            g = dq_[:, h * HEAD_BLK:(h + 1) * HEAD_BLK] * ATTN_SCALE
            outs.append(g * c + _swap_halves(g * sn) * on)
            gk = dk_[:, h * HEAD_BLK:(h + 1) * HEAD_BLK] * LN2
            dks.append(gk)
            dkr = dkr + gk * c + _swap_halves(gk * sn) * on
            dvs.append(dv_[:, h * HEAD_BLK:h * HEAD_BLK + MLA_V])
        return jnp.concatenate(outs, axis=1), jnp.concatenate(dks + dvs, axis=1), dkr

    w = MLA_HEADS * HEAD_BLK
    return rows_call("od_rope_bwd", rows, fn, [dq, dk, dv, pos, freq, sign],
                     [rows.tile(w), rows.tile(w), rows.tile(w), rows.tile(1), Rows.full((1, LANES)),
                      Rows.full((1, LANES))],
                     [(w, MATMUL_DTYPE), (w + MLA_WIDTH, MATMUL_DTYPE), (LANES, F32)])


_PAIRS = MLA_HEADS // 2
ATT_TQ = 512
ATT_TK = 4096
ATT_BWD_TQ = 512
ATT_BWD_TK = 2048


def _att_tiles(s, backward=False):
    if backward:
        return min(ATT_BWD_TQ, s), min(ATT_BWD_TK, s)
    return min(ATT_TQ, s), min(ATT_TK, s)


def attention_fwd(qcat, kcat, vcat, s):
    tq, tk = _att_tiles(s)
    nq, nk = s // tq, s // tk

    def body(q_ref, k_ref, v_ref, o_ref, lse_ref, m_ref, acc_ref):
        kk = pl.program_id(2)

        @pl.when(kk == 0)
        def _():
            m_ref[...] = jnp.full(m_ref.shape, -jnp.inf, F32)
            acc_ref[...] = jnp.zeros(acc_ref.shape, F32)

        sl = [slice(hh * HEAD_BLK, (hh + 1) * HEAD_BLK) for hh in range(2)]
        sc = [_mm(q_ref[:, sl[hh]], k_ref[:, sl[hh]], _NT) for hh in range(2)]
        for hh in range(2):
            m_prev = m_ref[hh]
            m_new = jnp.maximum(m_prev, jnp.max(sc[hh], axis=1, keepdims=True))
            p = jnp.exp2(sc[hh] - m_new[:, :1])
            acc_ref[hh] = acc_ref[hh] * jnp.exp2(m_prev - m_new) + _mm(p, v_ref[:, sl[hh]])
            m_ref[hh] = m_new

        @pl.when(kk == nk - 1)
        def _():
            half = lax.broadcasted_iota(jnp.int32, (1, LANES), 1) < MLA_V
            l0, l1 = acc_ref[0][:, MLA_V:MLA_V + 1], acc_ref[1][:, MLA_V:MLA_V + 1]
            o_ref[...] = jnp.where(half, acc_ref[0] / l0, pltpu.roll(acc_ref[1] / l1, MLA_V, 1))
            lse_ref[...] = jnp.where(half, m_ref[0] + jnp.log2(l0), m_ref[1] + jnp.log2(l1))

    return pl.pallas_call(
        body, name="od_attn_fwd", grid=(_PAIRS, nq, nk),
        in_specs=[pl.BlockSpec((tq, 2 * HEAD_BLK), lambda p, i, kk: (i, p)),
                  pl.BlockSpec((tk, 2 * HEAD_BLK), lambda p, i, kk: (kk, p)),
                  pl.BlockSpec((tk, 2 * HEAD_BLK), lambda p, i, kk: (kk, p))],
        out_specs=[pl.BlockSpec((tq, LANES), lambda p, i, kk: (i, p)),
                   pl.BlockSpec((None, tq, LANES), lambda p, i, kk: (p, i, 0))],
        out_shape=[jax.ShapeDtypeStruct((s, MLA_WIDTH), F32), jax.ShapeDtypeStruct((_PAIRS, s, LANES), F32)],
        scratch_shapes=[pltpu.VMEM((2, tq, LANES), F32)] * 2,
        compiler_params=_params(("parallel", "parallel", "arbitrary")),
    )(qcat, kcat, vcat)


def attention_bwd(qcat, kcat, vcat, o, lse, do, s):
    tq, tk = _att_tiles(s, backward=True)
    nq, nk = s // tq, s // tk

    def body(q_ref, k_ref, v_ref, do_ref, o_ref, lse_ref, dq_ref, dk_ref, dv_ref):
        kk, i = pl.program_id(1), pl.program_id(2)
        lane = lax.broadcasted_iota(jnp.int32, (1, LANES), 1)
        half = lane < MLA_V
        do_p, lse = do_ref[...], lse_ref[...]
        prod = do_p * o_ref[...]
        rows_i = pl.ds(pl.multiple_of(i * tq, tq), tq)
        for hh in range(2):
            sl = slice(hh * HEAD_BLK, (hh + 1) * HEAD_BLK)
            mine = half if hh == 0 else jnp.logical_not(half)
            delta = jnp.sum(jnp.where(mine, prod, 0.0), axis=1, keepdims=True)
            do_h = jnp.where(half, do_p if hh == 0 else pltpu.roll(do_p, MLA_V, 1), 0.0)
            p = jnp.exp2(_mm(q_ref[:, sl], k_ref[:, sl], _NT) - lse[:, hh * MLA_V:hh * MLA_V + 1])
            ds = p * (_mm(do_h, v_ref[:, sl], _NT) - delta)
            dv_h, dk_h, dq_h = _mm(p, do_h, _TN), _mm(ds, q_ref[:, sl], _TN), _mm(ds, k_ref[:, sl])

            @pl.when(i == 0)
            def _():
                dv_ref[:, sl] = dv_h
                dk_ref[:, sl] = dk_h

            @pl.when(i != 0)
            def _():
                dv_ref[:, sl] += dv_h
                dk_ref[:, sl] += dk_h

            @pl.when(kk == 0)
            def _():
                dq_ref[rows_i, sl] = dq_h

            @pl.when(kk != 0)
            def _():
                dq_ref[rows_i, sl] += dq_h

    w = MLA_HEADS * HEAD_BLK
    return pl.pallas_call(
        body, name="od_attn_bwd", grid=(_PAIRS, nk, nq),
        in_specs=[pl.BlockSpec((tq, 2 * HEAD_BLK), lambda p, kk, i: (i, p)),
                  pl.BlockSpec((tk, 2 * HEAD_BLK), lambda p, kk, i: (kk, p)),
                  pl.BlockSpec((tk, 2 * HEAD_BLK), lambda p, kk, i: (kk, p)),
                  pl.BlockSpec((tq, LANES), lambda p, kk, i: (i, p)),
                  pl.BlockSpec((tq, LANES), lambda p, kk, i: (i, p)),
                  pl.BlockSpec((None, tq, LANES), lambda p, kk, i: (p, i, 0))],
        out_specs=[pl.BlockSpec((s, 2 * HEAD_BLK), lambda p, kk, i: (0, p)),
                   pl.BlockSpec((tk, 2 * HEAD_BLK), lambda p, kk, i: (kk, p)),
                   pl.BlockSpec((tk, 2 * HEAD_BLK), lambda p, kk, i: (kk, p))],
        out_shape=[jax.ShapeDtypeStruct((s, w), F32)] * 3,
        compiler_params=_params(("parallel", "arbitrary", "arbitrary")),
    )(qcat, kcat, vcat, do, o, lse)


def _pool_counts(n_rows, first_row, s, w):
    pos = first_row + lax.broadcasted_iota(jnp.int32, (n_rows, 1), 0)
    lo = jnp.clip(pos - w // 2, 0, s)
    hi = jnp.clip(pos + w - w // 2, 0, s)
    return jnp.maximum(hi - lo, 1).astype(F32)


def _window_sum(e, levels, mirrored):
    n = e.shape[0]
    acc = e + pltpu.roll(e, (n - 1) if mirrored else 1, 0)
    step = 1
    for _ in range(levels - 1):
        acc = pltpu.roll(acc, step, 0) + pltpu.roll(acc, n - step, 0)
        step *= 2
    return acc


def _pooled(ue, s, t):
    first = pl.program_id(0) * t
    outs = []
    for gi, w in enumerate(POOL_WINDOWS):
        eg = ue[:, gi * POOL_GROUP:(gi + 1) * POOL_GROUP]
        sm = _window_sum(eg, gi + 1, False)[HALO:HALO + t]
        outs.append(sm / _pool_counts(t, first, s, w) - eg[HALO:HALO + t])
    return outs


def odd_mix_fwd(proj, o, pool_w, pool_scale, rows):
    def fn(o_, gc, up, u, un, gd, pw, ps):
        pooled = _pooled(_ext(up, u, un, rows.n), rows.s, rows.t)
        lin = jnp.concatenate([_mm(pooled[g], pw[g]) for g in range(len(POOL_WINDOWS))], axis=1)
        return jnp.concatenate([o_ * _silu(gc), lin * ps * _silu(gd)], axis=1)

    w = POOL_WIDTH
    return rows_call("od_mix_fwd", rows, fn, [o, proj, proj, proj, proj, proj, pool_w, pool_scale],
                     [rows.tile(w), rows.tile(w, O_GC)] + rows.halo(w, O_UD)
                     + [rows.tile(w, O_GD), Rows.full((4, POOL_GROUP, POOL_GROUP)), Rows.full((1, w))],
                     [(2 * w, MATMUL_DTYPE)])[0]


def odd_mix_bwd(proj, o, pool_w, pool_scale, dycd, rows):
    w = POOL_WIDTH
    ng = len(POOL_WINDOWS)

    def fn(o_, gc, up, u, un, gdp, gd, gdn, pw, ps, dyc, dydp, dyd, dydn):
        n, t, s = rows.n, rows.t, rows.s
        dyc = dyc.astype(F32)
        do = dyc * _silu(gc)
        dgc = dyc * o_ * _dsilu(gc)
        pooled = _pooled(_ext(up, u, un, n), s, t)
        lin = jnp.concatenate([_mm(pooled[g], pw[g]) for g in range(ng)], axis=1)
        dydc = dyd.astype(F32)
        dgd = dydc * lin * ps * _dsilu(gd)
        dps = _rowsum(dydc * lin * _silu(gd))
        dlin_e = _ext(dydp, dyd, dydn, n) * ps * _silu(_ext(gdp, gd, gdn, n))
        first = pl.program_id(0) * t - HALO
        dus, dpws = [], []
        for g, win in enumerate(POOL_WINDOWS):
            sl = slice(g * POOL_GROUP, (g + 1) * POOL_GROUP)
            dle = dlin_e[:, sl]
            dpws.append(_mm(pooled[g], dle[HALO:HALO + t], _TN))
            dpe = _mm(dle, pw[g], _NT)
            gce = dpe / _pool_counts(t + 2 * HALO, first, s, win)
            dus.append(_window_sum(gce, g + 1, True)[HALO:HALO + t] - dpe[HALO:HALO + t])
        return do, dgc, jnp.concatenate(dus, axis=1), dgd, jnp.stack(dpws), dps

    return rows_call("od_mix_bwd", rows, fn,
                     [o, proj, proj, proj, proj, proj, proj, proj, pool_w, pool_scale, dycd, dycd, dycd, dycd],
                     [rows.tile(w), rows.tile(w, O_GC)] + rows.halo(w, O_UD) + rows.halo(w, O_GD)
                     + [Rows.full((ng, POOL_GROUP, POOL_GROUP)), Rows.full((1, w)), rows.tile(w, 0)]
                     + rows.halo(w, w),
                     [(w, F32)] * 4, [(ng, POOL_GROUP, POOL_GROUP), (1, w)])


def pad_odd_w_in(w):
    z = lambda n: jnp.zeros((w.shape[0], n), w.dtype)
    return jnp.concatenate([w[:, :384], z(ROPE_LO), w[:, 384:416], z(LANES - ROPE_LO - MLA_ROPE), w[:, 416:]], axis=1)


def unpad_odd_w_in_t(wpt):
    return jnp.concatenate([wpt[:384], wpt[O_KR + ROPE_LO:O_KR + ROPE_LO + MLA_ROPE], wpt[O_GC:]], axis=0)


def pad_w_uq(w):
    w3 = w.reshape(MLA_Q_RANK, MLA_HEADS, MLA_NOPE + MLA_ROPE)
    w3 = jnp.pad(w3, ((0, 0), (0, 0), (0, HEAD_BLK - MLA_NOPE - MLA_ROPE)))
    return w3.reshape(MLA_Q_RANK, MLA_HEADS * HEAD_BLK)


def unpad_w_uq(wp):
    return wp.reshape(MLA_Q_RANK, MLA_HEADS, HEAD_BLK)[..., :MLA_NOPE + MLA_ROPE].reshape(MLA_Q_RANK, -1)


def pad_w_ukv(w):
    w3 = w.reshape(MLA_KV_RANK, MLA_HEADS, MLA_NOPE + MLA_V)
    kp = jnp.pad(w3[..., :MLA_NOPE], ((0, 0), (0, 0), (0, HEAD_BLK - MLA_NOPE)))
    return jnp.concatenate([kp.reshape(MLA_KV_RANK, -1), w3[..., MLA_NOPE:].reshape(MLA_KV_RANK, -1)], axis=1)


def unpad_w_ukv(wp):
    kp = wp[:, :MLA_HEADS * HEAD_BLK].reshape(MLA_KV_RANK, MLA_HEADS, HEAD_BLK)[..., :MLA_NOPE]
    vp = wp[:, MLA_HEADS * HEAD_BLK:].reshape(MLA_KV_RANK, MLA_HEADS, MLA_V)
    return jnp.concatenate([kp, vp], axis=-1).reshape(MLA_KV_RANK, -1)


def odd_weights(od_w_in, od_q_norm_g, od_w_uq, od_kv_norm_g, od_w_ukv, od_pool_w, od_pool_scale, od_w_out,
                od_ln_g, od_ln_b):
    freq, sign = rope_rows()
    return dict(w_in_p=pad_odd_w_in(od_w_in).astype(MATMUL_DTYPE), gq=od_q_norm_g, gkv=od_kv_norm_g,
                w_uq_p=pad_w_uq(od_w_uq).astype(MATMUL_DTYPE), w_ukv_p=pad_w_ukv(od_w_ukv).astype(MATMUL_DTYPE),
                pool_w=od_pool_w, pool_scale=od_pool_scale, w_out=od_w_out.astype(MATMUL_DTYPE), ln_g=od_ln_g,
                ln_b=od_ln_b, freq=freq, sign=sign)


def odd_layer_loss(x, pos, target, w, rows):
    s = rows.s
    proj = matmul(x, w["w_in_p"], "nn", F32, "od_proj")
    cqn, ckvn = latent_norm_fwd(proj, w["gq"], w["gkv"], rows)
    qp = matmul(cqn, w["w_uq_p"], "nn", F32, "od_q_up")
    kvp = matmul(ckvn, w["w_ukv_p"], "nn", F32, "od_kv_up")
    qcat, kcat, v = rope_fwd(qp, kvp, proj, pos, w["freq"], w["sign"], rows)
    o, lse = attention_fwd(qcat, kcat, v, s)
    ycd = odd_mix_fwd(proj, o, w["pool_w"], w["pool_scale"], rows)
    h = matmul(ycd, w["w_out"], "nn", F32, "od_out")
    dres, dh, dln_g, dln_b, loss_lanes = final_ln_loss(x, h, w["ln_g"], w["ln_b"], target, rows)
    dycd = matmul(dh, w["w_out"], "nt", F32, "od_out_dx")
    dw_out = matmul(ycd, dh, "tn", F32, "od_out_dw")
    do, dgc, dud, dgd, dpool_w, dpool_scale = odd_mix_bwd(proj, o, w["pool_w"], w["pool_scale"], dycd, rows)
    dq, dk, dv = attention_bwd(qcat, kcat, v, o, lse, do, s)
    dqp, dkvp, dkr = rope_bwd(dq, dk, dv, pos, w["freq"], w["sign"], rows)
    dcqn = matmul(dqp, w["w_uq_p"], "nt", F32, "od_q_up_dx")
    dw_uq = unpad_w_uq(matmul(cqn, dqp, "tn", F32, "od_q_up_dw"))
    dckvn = matmul(dkvp, w["w_ukv_p"], "nt", F32, "od_kv_up_dx")
    dw_ukv = unpad_w_ukv(matmul(ckvn, dkvp, "tn", F32, "od_kv_up_dw"))
    dcq, dckv, dgq, dgkv = latent_norm_bwd(proj, w["gq"], w["gkv"], dcqn, dckvn, rows)
    dproj = jnp.concatenate([dcq, dckv, dkr, dgc, dud, dgd], axis=1).astype(MATMUL_DTYPE)
    dx = matmul(dproj, w["w_in_p"], "nt", F32, "od_proj_dx", add=dres)
    dw_in = unpad_odd_w_in_t(matmul(dproj, x, "tn", F32, "od_proj_dw"))
    g = dict(od_w_in=dw_in, od_q_norm_g=dgq, od_w_uq=dw_uq, od_kv_norm_g=dgkv, od_w_ukv=dw_ukv,
             od_pool_w=dpool_w, od_pool_scale=dpool_scale, od_w_out=dw_out, od_ln_g=dln_g, od_ln_b=dln_b)
    return loss_lanes, dx, g


_MESH = pl.DeviceIdType.MESH
_ANY = pl.BlockSpec(memory_space=pl.ANY)
N_CHIPS = 4


def _push_call(name, ins, out_shapes, plan, n_remote, n_local):
    n_in, n_out = len(ins), len(out_shapes)

    def body(*refs):
        in_refs, out_refs = refs[:n_in], refs[n_in:n_in + n_out]
        send_sems, recv_sems, local_sems = refs[n_in + n_out:]
        x, y, c = lax.axis_index("x"), lax.axis_index("y"), lax.axis_index("c")
        remote, local = plan(in_refs, out_refs, x, y, c)
        assert len(remote) == n_remote and len(local) == n_local
        sends = [pltpu.make_async_remote_copy(src_ref=s, dst_ref=d, send_sem=send_sems.at[k], recv_sem=recv_sems.at[k],
                                              device_id=dev, device_id_type=_MESH)
                 for k, (s, d, dev, _) in enumerate(remote)]
        recvs = [pltpu.make_async_remote_copy(src_ref=s, dst_ref=land, send_sem=send_sems.at[k],
                                              recv_sem=recv_sems.at[k], device_id=dev, device_id_type=_MESH)
                 for k, (s, _, dev, land) in enumerate(remote)]
        locs = [pltpu.make_async_copy(s, d, local_sems.at[k]) for k, (s, d) in enumerate(local)]
        for cp in sends + locs:
            cp.start()
        for cp in recvs:
            cp.wait_recv()
        for cp in sends:
            cp.wait_send()
        for cp in locs:
            cp.wait()

    return pl.pallas_call(
        body, name=name, in_specs=[_ANY] * n_in, out_specs=[_ANY] * n_out, out_shape=list(out_shapes),
        scratch_shapes=[pltpu.SemaphoreType.DMA((n_remote,)), pltpu.SemaphoreType.DMA((n_remote,)),
                        pltpu.SemaphoreType.DMA((max(n_local, 1),))],
    )(*ins)


def _other_chips(x, y):
    return [(1 - x, y), (x, 1 - y), (1 - x, 1 - y)]


def chips_allgather(bufs):
    def plan(in_refs, out_refs, x, y, c):
        me = 2 * x + y
        remote, local = [], []
        for src, out in zip(in_refs, out_refs):
            for (px, py) in _other_chips(x, y):
                remote.append((src, out.at[me], (px, py, c), out.at[2 * px + py]))
            local.append((src, out.at[me]))
        return remote, local

    shapes = [jax.ShapeDtypeStruct((N_CHIPS,) + b.shape, b.dtype) for b in bufs]
    return _push_call("weights_allgather", bufs, shapes, plan, 3 * len(bufs), len(bufs))


def sibling_send(buf, name):
    def plan(in_refs, out_refs, x, y, c):
        return [(in_refs[0], out_refs[0], (x, y, 1 - c), out_refs[0])], []

    return _push_call(name, [buf], [jax.ShapeDtypeStruct(buf.shape, buf.dtype)], plan, 1, 0)[0]


def chips_scatter(buf):
    def plan(in_refs, out_refs, x, y, c):
        me = 2 * x + y
        src, out = in_refs[0], out_refs[0]
        remote = [(src.at[2 * px + py], out.at[me], (px, py, c), out.at[2 * px + py]) for (px, py) in _other_chips(x, y)]
        return remote, [(src.at[me], out.at[me])]

    return _push_call("grads_scatter", [buf], [jax.ShapeDtypeStruct(buf.shape, buf.dtype)], plan, 3, 1)[0]


PACK_W = 1024
PACK_BLK = 128


def _ew_call(name, fn, ins, in_specs, out_shape, out_spec, n_out, steps):
    def body(*refs):
        res = fn(*[r[...] for r in refs[:len(ins)]])
        if not isinstance(res, (tuple, list)):
            res = (res,)
        for r, v in zip(refs[len(ins):], res):
            r[...] = v

    return pl.pallas_call(
        body, name=name, grid=(steps,), in_specs=in_specs, out_specs=[out_spec] * n_out,
        out_shape=[jax.ShapeDtypeStruct(out_shape, F32)] * n_out,
        compiler_params=_params(("parallel",)),
    )(*ins)


def add2(a, b, name):
    n, r, w = a.shape
    blk = pl.BlockSpec((PACK_BLK, w), lambda i: (i, 0))
    out = _ew_call(name, lambda u, v: u + v, [a.reshape(n * r, w), b.reshape(n * r, w)], [blk, blk], (n * r, w), blk,
                   1, n * r // PACK_BLK)[0]
    return out.reshape(a.shape)


def sum_chips(buf):
    _, r, w = buf.shape
    specs = [pl.BlockSpec((None, PACK_BLK, w), lambda i, q=q: (q, i, 0)) for q in range(N_CHIPS)]
    return _ew_call("grads_sum", lambda a, b, c, d: ((a + b) + c) + d, [buf] * N_CHIPS, specs, (r, w),
                    pl.BlockSpec((PACK_BLK, w), lambda i: (i, 0)), 1, r // PACK_BLK)[0]


def adamw(w, g, m, v, name):
    def fn(w_, g_, m_, v_):
        m2 = ADAM_B1 * m_ + (1.0 - ADAM_B1) * g_
        v2 = ADAM_B2 * v_ + (1.0 - ADAM_B2) * jnp.square(g_)
        m_hat = m2 / (1.0 - ADAM_B1 ** ADAM_STEP)
        v_hat = v2 / (1.0 - ADAM_B2 ** ADAM_STEP)
        return -ADAM_LR * (m_hat / (jnp.sqrt(v_hat) + ADAM_EPS) + ADAM_WD * w_), m2, v2

    r, c = w.shape
    br = r if r <= 512 else _pick(r, (256, 128, 64, 32, 16, 8))
    blk = pl.BlockSpec((br, c), lambda i: (i, 0))
    return _ew_call(name, fn, [w, g, m, v], [blk] * 4, (r, c), blk, 3, r // br)


WEIGHTS = (
    ("ev_w_in", (1, 1024, 7200), 2), ("ev_conv_w", (1, 4, 2048), 2), ("ev_conv_b", (1, 2048), None),
    ("ev_a_log", (1, 2, 16), None), ("ev_dt_bias", (1, 2, 16), None), ("ev_d_skip", (1, 2, 16), None),
    ("ev_norm_g", (1, 1024), None), ("ev_sc_conv_w", (1, 3, 1024), 2), ("ev_w_out", (1, 2048, 1024), 1),
    ("ev_ln_g", (1, 1024), None), ("ev_ln_b", (1, 1024), None), ("od_w_in", (1, 1024, 1952), 2),
    ("od_q_norm_g", (1, 256), 1), ("od_w_uq", (1, 256, 768), 2), ("od_kv_norm_g", (1, 128), None),
    ("od_w_ukv", (1, 128, 1024), 2), ("od_pool_w", (1, 4, 128, 128), None), ("od_pool_scale", (1, 512), 1),
    ("od_w_out", (1, 1024, 1024), 1), ("od_ln_g", (1, 1024), 1), ("od_ln_b", (1, 1024), 1),
)
BIG = ("ev_w_in", "ev_w_out", "od_w_in", "od_w_uq", "od_w_ukv", "od_w_out")


def _block_shape(shape, axis):
    if axis is None:
        return tuple(shape)
    return tuple(d // N_CHIPS if i == axis else d for i, d in enumerate(shape))


def _pack(arrs, quantum):
    flat = jnp.concatenate([a.reshape(-1) for a in arrs])
    n = flat.shape[0]
    padded = -(-n // quantum) * quantum
    return jnp.concatenate([flat, jnp.zeros((padded - n,), flat.dtype)]).reshape(-1, LANES)


def _unpack(flat, shapes):
    out, off = [], 0
    for sh in shapes:
        n = int(np.prod(sh))
        out.append(flat[off:off + n].reshape(sh))
        off += n
    return out


def gather_weights(local):
    sharded = [(n, sh, ax) for (n, sh, ax) in WEIGHTS if ax is not None]
    big = [(n, sh, ax) for (n, sh, ax) in sharded if n in BIG]
    small = [(n, sh, ax) for (n, sh, ax) in sharded if n not in BIG]
    pb = _pack([local[n].astype(MATMUL_DTYPE) for n, _, _ in big], 2 * SUBLANES * LANES)
    ps = _pack([local[n] for n, _, _ in small], SUBLANES * LANES)
    gb, gs = chips_allgather([pb, ps])
    full = {n: local[n] for (n, sh, ax) in WEIGHTS if ax is None}
    for group, g in ((big, gb), (small, gs)):
        parts = [_unpack(g[q].reshape(-1), [_block_shape(sh, ax) for _, sh, ax in group]) for q in range(N_CHIPS)]
        for i, (n, sh, ax) in enumerate(group):
            full[n] = jnp.concatenate([parts[q][i] for q in range(N_CHIPS)], axis=ax)
    return full


WIDE = (("ev_w_in", True), ("ev_w_out", False), ("od_w_in", True), ("od_w_out", False))


def reduce_and_update(grads, local_w, local_m, local_v):
    c = lax.axis_index("c")
    wide_names = [n for n, _ in WIDE]
    tail = [(n, sh, ax) for (n, sh, ax) in WEIGHTS if n not in wide_names]
    tail_blocks = [_block_shape(sh, ax) for _, sh, ax in tail]
    n_tail = sum(int(np.prod(b)) for b in tail_blocks)
    wide_rows = [grads[n].shape[0] // N_CHIPS for n in wide_names]
    quantum = 2 * PACK_BLK
    total = -(-(sum(wide_rows) + -(-n_tail // PACK_W)) // quantum) * quantum
    tail_rows = total - sum(wide_rows)

    def tail_pack(pieces):
        flat = jnp.concatenate([p.reshape(-1) for p in pieces] + [jnp.zeros((tail_rows * PACK_W - n_tail,), F32)])
        return flat.reshape(tail_rows, PACK_W)

    packs = []
    for q in range(N_CHIPS):
        parts = [lax.slice_in_dim(grads[n], q * r, (q + 1) * r, axis=0) for n, r in zip(wide_names, wide_rows)]
        pieces = []
        for (n, sh, ax), bs in zip(tail, tail_blocks):
            g = grads[n].reshape(sh)
            pieces.append(g if ax is None else lax.slice_in_dim(g, q * bs[ax], (q + 1) * bs[ax], axis=ax))
        packs.append(jnp.concatenate(parts + [tail_pack(pieces)], axis=0))
    packs = jnp.stack(packs)
    rh = total // 2
    keep = lax.dynamic_slice_in_dim(packs, c * rh, rh, axis=1)
    give = lax.dynamic_slice_in_dim(packs, (1 - c) * rh, rh, axis=1)
    chip_half = add2(keep, sibling_send(give, "grads_to_sibling"), "grads_chip_sum")
    total_half = sum_chips(chips_scatter(chip_half))
    other_half = sibling_send(total_half, "grads_from_sibling")
    g_pack = jnp.concatenate([jnp.where(c == 0, total_half, other_half),
                              jnp.where(c == 0, other_half, total_half)], axis=0)
    outs = ({}, {}, {}, {})
    off = 0
    for (n, transposed), r in zip(WIDE, wide_rows):
        g = g_pack[off:off + r]
        off += r
        g = g.T if transposed else g
        shape = local_w[n].shape
        res = adamw(local_w[n].reshape(g.shape), g, local_m[n].reshape(g.shape), local_v[n].reshape(g.shape),
                    "adamw_" + n)
        for d, a in zip(outs, (g, *res)):
            d[n] = a.reshape(shape)
    g_tail = g_pack[off:]
    res = adamw(*[tail_pack([d[n] for n, _, _ in tail]) if d is not None else g_tail
                  for d in (local_w, None, local_m, local_v)], "adamw_small")
    for d, a in zip(outs, (g_tail, *res)):
        d.update(zip([n for n, _, _ in tail], _unpack(a.reshape(-1), tail_blocks)))
    return outs


ROW_TILE = 256


def kernel(x, positions, ev_w_in, ev_conv_w, ev_conv_b, ev_a_log, ev_dt_bias, ev_d_skip, ev_norm_g, ev_sc_conv_w, ev_w_out, ev_ln_g, ev_ln_b, od_w_in, od_q_norm_g, od_w_uq, od_kv_norm_g, od_w_ukv, od_pool_w, od_pool_scale, od_w_out, od_ln_g, od_ln_b, loss_target, m_ev_w_in, m_ev_conv_w, m_ev_conv_b, m_ev_a_log, m_ev_dt_bias, m_ev_d_skip, m_ev_norm_g, m_ev_sc_conv_w, m_ev_w_out, m_ev_ln_g, m_ev_ln_b, m_od_w_in, m_od_q_norm_g, m_od_w_uq, m_od_kv_norm_g, m_od_w_ukv, m_od_pool_w, m_od_pool_scale, m_od_w_out, m_od_ln_g, m_od_ln_b, v_ev_w_in, v_ev_conv_w, v_ev_conv_b, v_ev_a_log, v_ev_dt_bias, v_ev_d_skip, v_ev_norm_g, v_ev_sc_conv_w, v_ev_w_out, v_ev_ln_g, v_ev_ln_b, v_od_w_in, v_od_q_norm_g, v_od_w_uq, v_od_kv_norm_g, v_od_w_ukv, v_od_pool_w, v_od_pool_scale, v_od_w_out, v_od_ln_g, v_od_ln_b):
    names = [n for n, _, _ in WEIGHTS]
    local_w = dict(zip(names, (ev_w_in, ev_conv_w, ev_conv_b, ev_a_log, ev_dt_bias, ev_d_skip, ev_norm_g, ev_sc_conv_w, ev_w_out, ev_ln_g, ev_ln_b, od_w_in, od_q_norm_g, od_w_uq, od_kv_norm_g, od_w_ukv, od_pool_w, od_pool_scale, od_w_out, od_ln_g, od_ln_b)))
    local_m = dict(zip(names, (m_ev_w_in, m_ev_conv_w, m_ev_conv_b, m_ev_a_log, m_ev_dt_bias, m_ev_d_skip, m_ev_norm_g, m_ev_sc_conv_w, m_ev_w_out, m_ev_ln_g, m_ev_ln_b, m_od_w_in, m_od_q_norm_g, m_od_w_uq, m_od_kv_norm_g, m_od_w_ukv, m_od_pool_w, m_od_pool_scale, m_od_w_out, m_od_ln_g, m_od_ln_b)))
    local_v = dict(zip(names, (v_ev_w_in, v_ev_conv_w, v_ev_conv_b, v_ev_a_log, v_ev_dt_bias, v_ev_d_skip, v_ev_norm_g, v_ev_sc_conv_w, v_ev_w_out, v_ev_ln_g, v_ev_ln_b, v_od_w_in, v_od_q_norm_g, v_od_w_uq, v_od_kv_norm_g, v_od_w_ukv, v_od_pool_w, v_od_pool_scale, v_od_w_out, v_od_ln_g, v_od_ln_b)))
    s = x.shape[1]
    rows = Rows(s, min(ROW_TILE, s))
    f = gather_weights(local_w)
    ew = even_weights(f["ev_w_in"][0], f["ev_conv_w"][0], f["ev_conv_b"], f["ev_a_log"][0], f["ev_dt_bias"][0],
                      f["ev_d_skip"][0], f["ev_norm_g"], f["ev_sc_conv_w"][0], f["ev_w_out"][0], f["ev_ln_g"],
                      f["ev_ln_b"])
    ow = odd_weights(f["od_w_in"][0], f["od_q_norm_g"], f["od_w_uq"][0], f["od_kv_norm_g"], f["od_w_ukv"][0],
                     f["od_pool_w"][0], f["od_pool_scale"], f["od_w_out"][0], f["od_ln_g"], f["od_ln_b"])
    x1, saved = even_layer(x[0], ew, rows)
    loss_lanes, dx1, g_odd = odd_layer_loss(x1, positions.reshape(s, 1), loss_target[0], ow, rows)
    dx0, g_even = even_layer_bwd(dx1, ew, saved, rows)
    loss = lax.psum(jnp.sum(loss_lanes), ("x", "y", "c"))
    grad, delta, new_m, new_v = reduce_and_update({**g_even, **g_odd}, local_w, local_m, local_v)
    return (loss, dx0[None], *[grad[n] for n in names], *[delta[n] for n in names],
            *[new_m[n] for n in names], *[new_v[n] for n in names])
```

```python
import functools
import math

import jax
import jax.numpy as jnp
import numpy as np
from jax import lax
from jax.experimental import pallas as pl
from jax.experimental.pallas import tpu as pltpu

F32 = jnp.float32
BF16 = jnp.bfloat16
MATMUL_DTYPE = jnp.bfloat16

D_MODEL = 1024
DEPTH = 2
SSD_HEADS, SSD_HEAD_DIM, SSD_GROUPS, SSD_STATE, SSD_CHUNK = 16, 64, 4, 128, 128
SSD_INNER = SSD_HEADS * SSD_HEAD_DIM
SSD_XBC = SSD_INNER + 2 * SSD_GROUPS * SSD_STATE
SC_WIDTH = 1024
MLA_HEADS, MLA_Q_RANK, MLA_KV_RANK, MLA_NOPE, MLA_ROPE, MLA_V = 8, 256, 128, 64, 32, 64
MLA_WIDTH = MLA_HEADS * MLA_V
ROPE_THETA = 10000.0
ATTN_SCALE = (MLA_NOPE + MLA_ROPE) ** -0.5
QSCALE = ATTN_SCALE * math.log2(math.e)
LN2 = math.log(2.0)
POOL_WINDOWS = (2, 4, 8, 16)
POOL_GROUP = 128
POOL_WIDTH = POOL_GROUP * len(POOL_WINDOWS)
EPS = 1e-5
ALPHA = (2 * DEPTH) ** 0.25
EVEN_PROJ, ODD_PROJ = 7200, 1952
ADAM_LR, ADAM_B1, ADAM_B2, ADAM_EPS, ADAM_WD, ADAM_STEP = 0.001, 0.9, 0.999, 1e-08, 0.01, 10

LANES = 128
SUBLANES = 8
HALO = SUBLANES
VMEM_LIMIT = 56 * 1024 * 1024

EVEN_P = 7296
ODD_P = 2048
E_Z, E_XBC, E_BG, E_CG, E_H, E_GATE, E_DT = 0, 1024, 3072, 4096, 5120, 6144, 7168
O_CQ, O_CKV, O_KR, O_GC, O_UD, O_GD = 0, 256, 384, 512, 1024, 1536


def _params(sem=None):
    return pltpu.CompilerParams(dimension_semantics=sem, vmem_limit_bytes=VMEM_LIMIT)


def _mm(a, b, dims=(((1,), (0,)), ((), ()))):
    return lax.dot_general(a.astype(MATMUL_DTYPE), b.astype(MATMUL_DTYPE), dims, preferred_element_type=F32)


_NN = (((1,), (0,)), ((), ()))
_NT = (((1,), (1,)), ((), ()))
_TN = (((0,), (0,)), ((), ()))


def _silu(v):
    return v * jax.nn.sigmoid(v)


def _dsilu(v):
    s = jax.nn.sigmoid(v)
    return s * (1.0 + v * (1.0 - s))


def _pick(n, prefs):
    for p in prefs:
        if n % p == 0:
            return p
    return n


def matmul(a, b, mode, out_dtype, name, add=None, tm=None, tn=None, tk=None):
    if mode == "nn":
        (m, k), (k2, n) = a.shape, b.shape
    elif mode == "nt":
        (m, k), (n, k2) = a.shape, b.shape
    else:
        (k, m), (k2, n) = a.shape, b.shape
    assert k == k2, (a.shape, b.shape, mode)
    wide = (2432, 2048, 1536, 1024, 512, 256, 128)
    if mode == "tn":
        tm = tm or _pick(m, (2432, 2048, 1024, 512, 256, 128))
        tn = tn or _pick(n, wide)
        tk = tk or _pick(k, (512, 256, 128))
    else:
        tm = tm or _pick(m, (512, 256, 128))
        tn = tn or _pick(n, wide)
        tk = tk or _pick(k, wide)
    nk = k // tk
    dims = {"nn": _NN, "nt": _NT, "tn": _TN}[mode]

    def body(a_ref, b_ref, *rest):
        o_ref, acc_ref = rest[-2:]
        kk = pl.program_id(2)

        @pl.when(kk == 0)
        def _():
            acc_ref[...] = jnp.zeros_like(acc_ref) if add is None else rest[0][...].astype(F32)

        acc_ref[...] += _mm(a_ref[...], b_ref[...], dims)

        @pl.when(kk == nk - 1)
        def _():
            o_ref[...] = acc_ref[...].astype(o_ref.dtype)

    a_spec = {"nn": pl.BlockSpec((tm, tk), lambda i, j, kk: (i, kk)),
              "nt": pl.BlockSpec((tm, tk), lambda i, j, kk: (i, kk)),
              "tn": pl.BlockSpec((tk, tm), lambda i, j, kk: (kk, i))}[mode]
    b_spec = {"nn": pl.BlockSpec((tk, tn), lambda i, j, kk: (kk, j)),
              "nt": pl.BlockSpec((tn, tk), lambda i, j, kk: (j, kk)),
              "tn": pl.BlockSpec((tk, tn), lambda i, j, kk: (kk, j))}[mode]
    return pl.pallas_call(
        body, name=name, grid=(m // tm, n // tn, nk),
        in_specs=[a_spec, b_spec] + ([] if add is None else [pl.BlockSpec((tm, tn), lambda i, j, kk: (i, j))]),
        out_specs=pl.BlockSpec((tm, tn), lambda i, j, kk: (i, j)),
        out_shape=jax.ShapeDtypeStruct((m, n), out_dtype),
        scratch_shapes=[pltpu.VMEM((tm, tn), F32)],
        compiler_params=_params(("parallel", "parallel", "arbitrary")),
    )(*((a, b) if add is None else (a, b, add)))


class Rows:
    def __init__(self, s, t):
        assert s % t == 0 and t % HALO == 0
        self.s, self.t, self.n = s, t, s // t

    def tile(self, width, col=0, lead=None):
        cb = col // width
        assert col % width == 0
        if lead is None:
            return pl.BlockSpec((self.t, width), lambda i: (i, cb))
        return pl.BlockSpec((None, self.t, width), lambda i: (lead, i, cb))

    def prev(self, width, col=0, lead=None):
        cb, r = col // width, self.t // HALO
        if lead is None:
            return pl.BlockSpec((HALO, width), lambda i: (jnp.maximum(i * r - 1, 0), cb))
        return pl.BlockSpec((None, HALO, width), lambda i: (lead, jnp.maximum(i * r - 1, 0), cb))

    def next(self, width, col=0, lead=None):
        cb, r, last = col // width, self.t // HALO, self.s // HALO - 1
        if lead is None:
            return pl.BlockSpec((HALO, width), lambda i: (jnp.minimum((i + 1) * r, last), cb))
        return pl.BlockSpec((None, HALO, width), lambda i: (lead, jnp.minimum((i + 1) * r, last), cb))

    def halo(self, width, col=0, lead=None):
        return [self.prev(width, col, lead), self.tile(width, col, lead), self.next(width, col, lead)]

    @staticmethod
    def full(shape):
        nd = len(shape)
        return pl.BlockSpec(tuple(shape), lambda i: (0,) * nd)


def rows_call(name, rows, fn, ins, in_specs, row_outs, acc_outs=()):
    n_row = len(row_outs)

    def body(*refs):
        in_refs = refs[:len(ins)]
        out_refs = refs[len(ins):]
        res = fn(*[r[...] for r in in_refs])
        if not isinstance(res, (tuple, list)):
            res = (res,)
        for r, v in zip(out_refs[:n_row], res[:n_row]):
            r[...] = v.astype(r.dtype)
        if acc_outs:
            first = pl.program_id(0) == 0

            @pl.when(first)
            def _():
                for r, v in zip(out_refs[n_row:], res[n_row:]):
                    r[...] = v.astype(F32)

            @pl.when(jnp.logical_not(first))
            def _():
                for r, v in zip(out_refs[n_row:], res[n_row:]):
                    r[...] += v.astype(F32)

    out_shape = [jax.ShapeDtypeStruct((rows.s, w), dt) for (w, dt) in row_outs]
    out_specs = [rows.tile(w) for (w, dt) in row_outs]
    out_shape += [jax.ShapeDtypeStruct(tuple(sh), F32) for sh in acc_outs]
    out_specs += [Rows.full(sh) for sh in acc_outs]
    return pl.pallas_call(
        body, name=name, grid=(rows.n,), in_specs=list(in_specs), out_specs=out_specs, out_shape=out_shape,
        compiler_params=_params(("arbitrary",)),
    )(*ins)


def _edge_zero(prev, nxt, n_tiles):
    i = pl.program_id(0)
    prev = jnp.where(i == 0, jnp.zeros_like(prev), prev)
    nxt = jnp.where(i == n_tiles - 1, jnp.zeros_like(nxt), nxt)
    return prev, nxt


def _ext(prev, cur, nxt, n_tiles):
    prev, nxt = _edge_zero(prev, nxt, n_tiles)
    return jnp.concatenate([prev.astype(F32), cur.astype(F32), nxt.astype(F32)], axis=0)


def _shift(ext, off):
    n = ext.shape[0]
    t = n - 2 * HALO
    if off == 0:
        return ext[HALO:HALO + t]
    return pltpu.roll(ext, (-off) % n, 0)[HALO:HALO + t]


def _rowsum(v):
    return jnp.sum(v, axis=0, keepdims=True)


CONV_OFFS = (-2, -1, 0, 1)
SC_OFFS = (-1, 0, 1)


def conv_fwd(proj, conv_w, conv_b, rows):
    def fn(p, c, nx, w, b):
        e = _ext(p, c, nx, rows.n)
        pre = b
        for k, off in enumerate(CONV_OFFS):
            pre = pre + w[k:k + 1, :] * _shift(e, off)
        return pre, _silu(pre)

    outs = [rows_call(f"ev_conv_fwd{h}", rows, fn,
                      [proj, proj, proj, conv_w[:, h * 1024:(h + 1) * 1024], conv_b[:, h * 1024:(h + 1) * 1024]],
                      rows.halo(1024, E_XBC + h * 1024) + [Rows.full((4, 1024)), Rows.full((1, 1024))],
                      [(1024, F32), (1024, F32)]) for h in range(2)]
    return outs


def conv_bwd(proj, pre_h, du_h, conv_w, rows):
    res = []
    for h in range(2):
        def fn(*a):
            w = a[-1]
            xe = _ext(a[0], a[1], a[2], rows.n)
            pe = jnp.concatenate([a[3], a[4], a[5]], axis=0)
            g = a[6:-1]
            du = _ext(g[0], g[1], g[2], rows.n) + _ext(g[3], g[4], g[5], rows.n)
            dpre = du * _dsilu(pe)
            dx = jnp.zeros_like(a[1], dtype=F32)
            dws = []
            for k, off in enumerate(CONV_OFFS):
                dx = dx + w[k:k + 1, :] * _shift(dpre, -off)
                dws.append(_rowsum(_shift(dpre, 0) * _shift(xe, off)))
            dw = jnp.concatenate(dws + [jnp.zeros((4, dx.shape[1]), F32)], axis=0)
            return dx, dw, _rowsum(_shift(dpre, 0))

        gi = [du_h[h][0]] * 3 + [du_h[h][1]] * 3
        gs = rows.halo(1024) * 2
        res.append(rows_call(
            f"ev_conv_bwd{h}", rows, fn,
            [proj] * 3 + [pre_h[h]] * 3 + gi + [conv_w[:, h * 1024:(h + 1) * 1024]],
            rows.halo(1024, E_XBC + h * 1024) + rows.halo(1024) + gs + [Rows.full((4, 1024))],
            [(1024, MATMUL_DTYPE)], [(8, 1024), (1, 1024)]))
    dconv_w = jnp.concatenate([res[0][1][:4], res[1][1][:4]], axis=1)
    dconv_b = jnp.concatenate([res[0][2], res[1][2]], axis=1)
    return [res[0][0], res[1][0]], dconv_w, dconv_b


def _head_row(p):
    return jnp.concatenate([p.reshape(1, 2 * SSD_HEADS), jnp.zeros((1, LANES - 2 * SSD_HEADS), F32)], axis=1)


def _head_unrow(r):
    return r[:, :2 * SSD_HEADS].reshape(2, SSD_HEADS)


def _ssd_pre_fn(dtraw, bias_row, alog_row):
    q = dtraw.shape[0]
    dt = jax.nn.softplus(dtraw + bias_row)
    da = dt * (-jnp.exp(alog_row))
    li = lax.broadcasted_iota(jnp.int32, (q, q), 0)
    si = lax.broadcasted_iota(jnp.int32, (q, q), 1)
    tril = (li >= si).astype(F32)
    csf = lax.dot_general(tril, da, _NN, precision=lax.Precision.HIGHEST, preferred_element_type=F32)
    tot = jnp.sum(da, axis=0, keepdims=True)
    lane = lax.broadcasted_iota(jnp.int32, (1, LANES), 1)
    cs = jnp.where(lane < SSD_HEADS, csf, tot - csf + da)
    return dt, cs


def ssd_pre(proj, bias_row, alog_row, s):
    rows = Rows(s, SSD_CHUNK)
    return rows_call("ev_ssd_pre", rows, _ssd_pre_fn, [proj, bias_row, alog_row],
                     [rows.tile(LANES, E_DT), Rows.full((1, LANES)), Rows.full((1, LANES))],
                     [(LANES, F32), (LANES, F32)])


def ssd_pre_bwd(proj, bias_row, alog_row, ddt, dcs, s):
    rows = Rows(s, SSD_CHUNK)

    def fn(dtraw, b, al, g0, g1, c0, c1):
        _, vjp = jax.vjp(_ssd_pre_fn, dtraw, b, al)
        return vjp((g0 + g1, c0 + c1))

    return rows_call("ev_ssd_pre_bwd", rows, fn, [proj, bias_row, alog_row, ddt[0], ddt[1], dcs[0], dcs[1]],
                     [rows.tile(LANES, E_DT), Rows.full((1, LANES)), Rows.full((1, LANES))] + [rows.tile(LANES)] * 4,
                     [(LANES, MATMUL_DTYPE)], [(1, LANES), (1, LANES)])


_N_PAIR = SSD_HEADS // 2
_BC = SSD_GROUPS * SSD_STATE


def _ssd_chunk_fn(x, bc, dt, cs, h_in, dsk_row, d):
    q = x.shape[0]
    lane = lax.broadcasted_iota(jnp.int32, (1, LANES), 1)
    half = lane < SSD_HEAD_DIM
    lo, hi = half.astype(F32), 1.0 - half.astype(F32)
    cst = cs.T
    li = lax.broadcasted_iota(jnp.int32, (q, 1), 0)
    si = lax.broadcasted_iota(jnp.int32, (1, q), 1)
    mask = li >= si if d == 0 else li <= si
    end = q - 1 if d == 0 else 0
    tot = cs[end:end + 1, :]
    mine = (lane >= SSD_HEADS * d) & (lane < SSD_HEADS * (d + 1))
    e_cs, e_dec, e_tot = jnp.exp(cs), jnp.exp(jnp.where(mine, tot - cs, 0.0)), jnp.exp(tot)

    def col(v, l):
        return v[:, l:l + 1]

    def by_head(v, l):
        return jnp.where(half, col(v, l), col(v, l + 1))

    per_group = _N_PAIR // SSD_GROUPS
    ys, hs = [], []
    for g in range(SSD_GROUPS):
        bm = bc[:, g * SSD_STATE:(g + 1) * SSD_STATE]
        cm = bc[:, _BC + g * SSD_STATE:_BC + (g + 1) * SSD_STATE]
        cb = _mm(cm, bm, _NT)
        pairs = range(g * per_group, (g + 1) * per_group)
        h_g = [h_in[j * SSD_STATE:(j + 1) * SSD_STATE, :] for j in pairs]
        y_off = _mm(cm, jnp.concatenate(h_g, axis=1))
        for k, j in enumerate(pairs):
            l0 = SSD_HEADS * d + 2 * j
            xj = x[:, j * LANES:(j + 1) * LANES]
            xd = xj * by_head(dt, l0)
            xcat = jnp.concatenate([xd * lo, xd * hi], axis=0)
            w0 = cb * jnp.exp(jnp.where(mask, col(cs, l0) - cst[l0:l0 + 1, :], -jnp.inf))
            w1 = cb * jnp.exp(jnp.where(mask, col(cs, l0 + 1) - cst[l0 + 1:l0 + 2, :], -jnp.inf))
            y = _mm(jnp.concatenate([w0, w1], axis=1), xcat)
            st = _mm(jnp.concatenate([bm * col(e_dec, l0), bm * col(e_dec, l0 + 1)], axis=0), xcat, _TN)
            y = y + y_off[:, k * LANES:(k + 1) * LANES] * by_head(e_cs, l0) + by_head(dsk_row, l0) * xj
            ys.append(y)
            hs.append(h_g[k] * by_head(e_tot, l0) + st)
    return jnp.concatenate(ys, axis=1), jnp.concatenate(hs, axis=0)


def _chunk_of(d, ci, nc, backward):
    up = ci if not backward else nc - 1 - ci
    return up + d * (nc - 1 - 2 * up)


_ST_ROWS = _N_PAIR * SSD_STATE


def _ssd_fwd_dir(u_h, dt, cs, dsk_row, s, d):
    nc = s // SSD_CHUNK
    q = SSD_CHUNK

    def body(x_ref, bc_ref, dt_ref, cs_ref, dsk_ref, y_ref, hs_ref, st_ref):
        @pl.when(pl.program_id(0) == 0)
        def _():
            st_ref[...] = jnp.zeros(st_ref.shape, F32)

        h_in = st_ref[...]
        y, h_out = _ssd_chunk_fn(x_ref[...], bc_ref[...], dt_ref[...], cs_ref[...], h_in, dsk_ref[...], d)
        y_ref[...] = y
        hs_ref[...] = h_in
        st_ref[...] = h_out

    ch = lambda ci: _chunk_of(d, ci, nc, False)
    return pl.pallas_call(
        body, name=f"ev_ssd_fwd{d}", grid=(nc,),
        in_specs=[pl.BlockSpec((q, SSD_INNER), lambda ci: (ch(ci), 0)),
                  pl.BlockSpec((q, 2 * _BC), lambda ci: (ch(ci), 0)),
                  pl.BlockSpec((q, LANES), lambda ci: (ch(ci), 0)),
                  pl.BlockSpec((q, LANES), lambda ci: (ch(ci), 0)),
                  pl.BlockSpec((1, LANES), lambda ci: (0, 0))],
        out_specs=[pl.BlockSpec((q, SSD_INNER), lambda ci: (ch(ci), 0)),
                   pl.BlockSpec((None, _ST_ROWS, LANES), lambda ci: (ch(ci), 0, 0))],
        out_shape=[jax.ShapeDtypeStruct((s, SSD_INNER), F32), jax.ShapeDtypeStruct((nc, _ST_ROWS, LANES), F32)],
        scratch_shapes=[pltpu.VMEM((_ST_ROWS, LANES), F32)],
        compiler_params=_params(("arbitrary",)),
    )(u_h[0], u_h[1], dt, cs, dsk_row)


def ssd_fwd(u_h, dt, cs, dsk_row, s):
    ys, hss = zip(*[_ssd_fwd_dir(u_h, dt, cs, dsk_row, s, d) for d in range(2)])
    return ys, hss


def _ssd_bwd_dir(u_h, dt, cs, dsk_row, hs, dy, s, d):
    nc = s // SSD_CHUNK
    q = SSD_CHUNK

    def body(x_ref, bc_ref, dt_ref, cs_ref, dsk_ref, hs_ref, dy_ref,
             dx_ref, dbc_ref, ddt_ref, dcs_ref, ddsk_ref, dst_ref):
        ci = pl.program_id(0)

        @pl.when(ci == 0)
        def _():
            dst_ref[...] = jnp.zeros(dst_ref.shape, F32)

        f = functools.partial(_ssd_chunk_fn, d=d)
        _, vjp = jax.vjp(f, x_ref[...], bc_ref[...], dt_ref[...], cs_ref[...], hs_ref[...], dsk_ref[...])
        dx, dbc, ddt, dcs, dh, ddsk = vjp((dy_ref[...], dst_ref[...]))
        dx_ref[...] = dx
        dbc_ref[...] = dbc
        ddt_ref[...] = ddt
        dcs_ref[...] = dcs
        dst_ref[...] = dh

        @pl.when(ci == 0)
        def _():
            ddsk_ref[...] = ddsk

        @pl.when(ci != 0)
        def _():
            ddsk_ref[...] += ddsk

    ch = lambda ci: _chunk_of(d, ci, nc, True)
    row_blk = lambda w: pl.BlockSpec((q, w), lambda ci: (ch(ci), 0))
    return pl.pallas_call(
        body, name=f"ev_ssd_bwd{d}", grid=(nc,),
        in_specs=[row_blk(SSD_INNER), row_blk(2 * _BC), row_blk(LANES), row_blk(LANES),
                  pl.BlockSpec((1, LANES), lambda ci: (0, 0)),
                  pl.BlockSpec((None, _ST_ROWS, LANES), lambda ci: (ch(ci), 0, 0)), row_blk(SSD_INNER)],
        out_specs=[row_blk(SSD_INNER), row_blk(2 * _BC), row_blk(LANES), row_blk(LANES),
                   pl.BlockSpec((1, LANES), lambda ci: (0, 0))],
        out_shape=[jax.ShapeDtypeStruct((s, SSD_INNER), F32), jax.ShapeDtypeStruct((s, 2 * _BC), F32),
                   jax.ShapeDtypeStruct((s, LANES), F32), jax.ShapeDtypeStruct((s, LANES), F32),
                   jax.ShapeDtypeStruct((1, LANES), F32)],
        scratch_shapes=[pltpu.VMEM((_ST_ROWS, LANES), F32)],
        compiler_params=_params(("arbitrary",)),
    )(u_h[0], u_h[1], dt, cs, dsk_row, hs, dy)


def ssd_bwd(u_h, dt, cs, dsk_row, hs, dy, s):
    return zip(*[_ssd_bwd_dir(u_h, dt, cs, dsk_row, hs[d], dy, s, d) for d in range(2)])


def _gated_rms(ys, z, g):
    t1 = ys * _silu(z)
    return t1 * lax.rsqrt(jnp.mean(t1 * t1, axis=-1, keepdims=True) + EPS) * g


def even_mix_fwd(proj, y2, norm_g, sc_w, rows):
    def fn(yf, yb, z, bg, cgp, cg, cgn, hp, hh, hn, gate, g, w):
        ya = _gated_rms(yf + yb, z, g)
        me = _ext(cgp, cg, cgn, rows.n) * _ext(hp, hh, hn, rows.n)
        cm = sum(w[k:k + 1, :] * _shift(me, off) for k, off in enumerate(SC_OFFS))
        return jnp.concatenate([ya, bg * cm * _silu(gate)], axis=1)

    w = 1024
    return rows_call("ev_mix_fwd", rows, fn,
                     [y2[0], y2[1], proj, proj] + [proj] * 6 + [proj, norm_g, sc_w],
                     [rows.tile(w), rows.tile(w), rows.tile(w, E_Z), rows.tile(w, E_BG)]
                     + rows.halo(w, E_CG) + rows.halo(w, E_H)
                     + [rows.tile(w, E_GATE), Rows.full((1, w)), Rows.full((3, w))],
                     [(2 * w, MATMUL_DTYPE)])[0]


def even_mix_bwd(proj, y2, norm_g, sc_w, dyab, rows):
    w = 1024

    def fn(yf, yb, z, g, sw, dya, *a):
        (dybp, dyb, dybn, bgp, bg, bgn, gtp, gt, gtn, cgp, cg, cgn, hp, hh, hn) = a
        _, vjp = jax.vjp(_gated_rms, yf + yb, z, g)
        dys, dz, dg = vjp(dya.astype(F32))
        n = rows.n
        dye, bge, gte = _ext(dybp, dyb, dybn, n), _ext(bgp, bg, bgn, n), _ext(gtp, gt, gtn, n)
        cge, he = _ext(cgp, cg, cgn, n), _ext(hp, hh, hn, n)
        me = cge * he
        cm = sum(sw[k:k + 1, :] * _shift(me, off) for k, off in enumerate(SC_OFFS))
        dyc = dyb.astype(F32)
        dbg = dyc * cm * _silu(gt)
        dgate = dyc * bg * cm * _dsilu(gt)
        dcme = dye * bge * _silu(gte)
        dm = sum(sw[k:k + 1, :] * _shift(dcme, -off) for k, off in enumerate(SC_OFFS))
        dcm = _shift(dcme, 0)
        dws = [_rowsum(dcm * _shift(me, off)) for off in SC_OFFS]
        dsw = jnp.concatenate(dws + [jnp.zeros((5, w), F32)], axis=0)
        return dys, dz, dbg, dm * hh, dm * cg, dgate, _rowsum(dg), dsw

    return rows_call(
        "ev_mix_bwd", rows, fn,
        [y2[0], y2[1], proj, norm_g, sc_w, dyab] + [dyab] * 3 + [proj] * 12,
        [rows.tile(w), rows.tile(w), rows.tile(w, E_Z), Rows.full((1, w)), Rows.full((3, w)),
         rows.tile(w, 0)] + rows.halo(w, w) + rows.halo(w, E_BG) + rows.halo(w, E_GATE)
        + rows.halo(w, E_CG) + rows.halo(w, E_H),
        [(w, F32)] + [(w, MATMUL_DTYPE)] * 5, [(1, w), (8, w)])


def _res_ln(x, h, g, b):
    v = ALPHA * x + h
    mu = jnp.mean(v, axis=-1, keepdims=True)
    var = jnp.mean(jnp.square(v - mu), axis=-1, keepdims=True)
    return (v - mu) * lax.rsqrt(var + EPS) * g + b


def res_ln_fwd(x, h, g, b, rows, name):
    return rows_call(name, rows, _res_ln, [x, h, g, b],
                     [rows.tile(D_MODEL), rows.tile(D_MODEL), Rows.full((1, D_MODEL)), Rows.full((1, D_MODEL))],
                     [(D_MODEL, F32)])[0]


def res_ln_bwd(x, h, g, b, dy, rows, name):
    def fn(x_, h_, g_, b_, dy_):
        _, vjp = jax.vjp(_res_ln, x_, h_, g_, b_)
        dx, dh, dg, db = vjp(dy_)
        return dx, dh, dg, db

    return rows_call(name, rows, fn, [x, h, g, b, dy],
                     [rows.tile(D_MODEL), rows.tile(D_MODEL), Rows.full((1, D_MODEL)), Rows.full((1, D_MODEL)),
                      rows.tile(D_MODEL)],
                     [(D_MODEL, F32), (D_MODEL, F32)], [(1, D_MODEL), (1, D_MODEL)])


def final_ln_loss(x, h, g, b, target, rows):
    def fn(x_, h_, g_, b_, t_):
        y, vjp = jax.vjp(_res_ln, x_, h_, g_, b_)
        err = y - t_
        dx, dh, dg, db = vjp(err * (1.0 / D_MODEL))
        return dx, dh, dg, db, _rowsum(jnp.square(err)) * (0.5 / D_MODEL)

    return rows_call("od_ln_loss", rows, fn, [x, h, g, b, target],
                     [rows.tile(D_MODEL), rows.tile(D_MODEL), Rows.full((1, D_MODEL)), Rows.full((1, D_MODEL)),
                      rows.tile(D_MODEL)],
                     [(D_MODEL, F32), (D_MODEL, F32)], [(1, D_MODEL), (1, D_MODEL), (1, D_MODEL)])


def pad_even_w_in(w):
    return jnp.concatenate([w[:, :3072], w[:, 3104:], w[:, 3072:3104],
                            jnp.zeros((w.shape[0], EVEN_P - EVEN_PROJ), w.dtype)], axis=1)


def matmul_pieces_nt(pieces, w, name, add):
    (m, wd), n, npc = pieces[0].shape, w.shape[0], len(pieces)
    tm = _pick(m, (512, 256, 128))

    def body(*refs):
        p_refs = refs[:npc]
        w_ref, add_ref, o_ref, acc_ref = refs[npc:]
        kk = pl.program_id(1)

        @pl.when(kk == 0)
        def _():
            acc_ref[...] = add_ref[...].astype(F32)

        for p in range(npc):
            @pl.when(kk == p)
            def _(p=p):
                acc_ref[...] += _mm(p_refs[p][...], w_ref[...], _NT)

        @pl.when(kk == npc - 1)
        def _():
            o_ref[...] = acc_ref[...]

    return pl.pallas_call(
        body, name=name, grid=(m // tm, npc),
        in_specs=[pl.BlockSpec((tm, wd), lambda i, kk: (i, 0))] * npc
        + [pl.BlockSpec((n, wd), lambda i, kk: (0, kk)), pl.BlockSpec((tm, n), lambda i, kk: (i, 0))],
        out_specs=pl.BlockSpec((tm, n), lambda i, kk: (i, 0)),
        out_shape=jax.ShapeDtypeStruct((m, n), F32),
        scratch_shapes=[pltpu.VMEM((tm, n), F32)],
        compiler_params=_params(("parallel", "arbitrary")),
    )(*pieces, w, add)


def even_layer(x, w, rows):
    s = rows.s
    xb = x.astype(MATMUL_DTYPE)
    proj = matmul(xb, w["w_in_p"], "nn", F32, "ev_proj")
    (pre0, u0), (pre1, u1) = conv_fwd(proj, w["conv_w"], w["conv_b"], rows)
    dt, cs = ssd_pre(proj, w["bias_row"], w["alog_row"], s)
    y2, hs = ssd_fwd((u0, u1), dt, cs, w["dsk_row"], s)
    yab = even_mix_fwd(proj, y2, w["norm_g"], w["sc_w"], rows)
    h = matmul(yab, w["w_out"], "nn", F32, "ev_out")
    x1 = res_ln_fwd(x, h, w["ln_g"], w["ln_b"], rows, "ev_ln")
    return x1, dict(x=x, xb=xb, proj=proj, pre=(pre0, pre1), u=(u0, u1), dt=dt, cs=cs, y2=y2, hs=hs, yab=yab,
                    h=h)


def even_layer_bwd(dx1, w, sv, rows):
    s = rows.s
    dres, dh, dln_g, dln_b = res_ln_bwd(sv["x"], sv["h"], w["ln_g"], w["ln_b"], dx1, rows, "ev_ln_bwd")
    dyab = matmul(dh, w["w_out"], "nt", F32, "ev_out_dx")
    dw_out = matmul(sv["yab"], dh, "tn", F32, "ev_out_dw")
    dys, dz, dbg, dcg, dhh, dgate, dnorm_g, dsw = even_mix_bwd(sv["proj"], sv["y2"], w["norm_g"], w["sc_w"], dyab,
                                                               rows)
    dxs, dbc, ddt, dcs, ddsk = ssd_bwd(sv["u"], sv["dt"], sv["cs"], w["dsk_row"], sv["hs"], dys, s)
    ddtraw, dbias_row, dalog_row = ssd_pre_bwd(sv["proj"], w["bias_row"], w["alog_row"], ddt, dcs, s)
    (dxbc0, dxbc1), dconv_w, dconv_b = conv_bwd(sv["proj"], sv["pre"], (dxs, dbc), w["conv_w"], rows)
    pieces = [dz, dxbc0, dxbc1, dbg, dcg, dhh, dgate]
    dx0 = matmul_pieces_nt(pieces, w["w_in_p"], "ev_proj_dx", add=dres)
    dx0 = matmul(ddtraw, w["w_in_p"][:, E_DT:], "nt", F32, "ev_proj_dx_dt", add=dx0)
    dws = [matmul(p, sv["xb"], "tn", F32, f"ev_proj_dw{i}") for i, p in enumerate(pieces)]
    dw_dt = matmul(ddtraw, sv["xb"], "tn", F32, "ev_proj_dw_dt")
    dw_in = jnp.concatenate(dws[:3] + [dw_dt[:2 * SSD_HEADS]] + dws[3:], axis=0)
    g = dict(ev_w_in=dw_in, ev_conv_w=dconv_w, ev_conv_b=dconv_b,
             ev_a_log=_head_unrow(dalog_row), ev_dt_bias=_head_unrow(dbias_row),
             ev_d_skip=_head_unrow(ddsk[0] + ddsk[1]), ev_norm_g=dnorm_g, ev_sc_conv_w=dsw[:3],
             ev_w_out=dw_out, ev_ln_g=dln_g, ev_ln_b=dln_b)
    return dx0, g


def even_weights(ev_w_in, ev_conv_w, ev_conv_b, ev_a_log, ev_dt_bias, ev_d_skip, ev_norm_g, ev_sc_conv_w,
                 ev_w_out, ev_ln_g, ev_ln_b):
    return dict(w_in_p=pad_even_w_in(ev_w_in).astype(MATMUL_DTYPE), conv_w=ev_conv_w, conv_b=ev_conv_b,
                alog_row=_head_row(ev_a_log), bias_row=_head_row(ev_dt_bias), dsk_row=_head_row(ev_d_skip),
                norm_g=ev_norm_g, sc_w=ev_sc_conv_w, w_out=ev_w_out.astype(MATMUL_DTYPE), ln_g=ev_ln_g,
                ln_b=ev_ln_b)


HEAD_BLK = LANES
ROPE_LO = MLA_NOPE
ROPE_HALF = MLA_ROPE // 2


def _rms(v, g):
    return v * lax.rsqrt(jnp.mean(v * v, axis=-1, keepdims=True) + EPS) * g


def latent_norm_fwd(proj, gq, gkv, rows):
    def fn(cq, ckv, gq_, gkv_):
        return _rms(cq, gq_), _rms(ckv, gkv_)

    return rows_call("od_norm_fwd", rows, fn, [proj, proj, gq, gkv],
                     [rows.tile(MLA_Q_RANK, O_CQ), rows.tile(MLA_KV_RANK, O_CKV), Rows.full((1, MLA_Q_RANK)),
                      Rows.full((1, MLA_KV_RANK))],
                     [(MLA_Q_RANK, MATMUL_DTYPE), (MLA_KV_RANK, MATMUL_DTYPE)])


def latent_norm_bwd(proj, gq, gkv, dcqn, dckvn, rows):
    def fn(cq, ckv, gq_, gkv_, d1, d2):
        _, vjp = jax.vjp(_rms, cq, gq_)
        dcq, dgq = vjp(d1)
        _, vjp2 = jax.vjp(_rms, ckv, gkv_)
        dckv, dgkv = vjp2(d2)
        return dcq, dckv, dgq, dgkv

    return rows_call("od_norm_bwd", rows, fn, [proj, proj, gq, gkv, dcqn, dckvn],
                     [rows.tile(MLA_Q_RANK, O_CQ), rows.tile(MLA_KV_RANK, O_CKV), Rows.full((1, MLA_Q_RANK)),
                      Rows.full((1, MLA_KV_RANK)), rows.tile(MLA_Q_RANK), rows.tile(MLA_KV_RANK)],
                     [(MLA_Q_RANK, F32), (MLA_KV_RANK, F32)], [(1, MLA_Q_RANK), (1, MLA_KV_RANK)])


def rope_rows():
    lane = np.arange(LANES)
    inv = ROPE_THETA ** (-jnp.arange(ROPE_HALF, dtype=F32) / ROPE_HALF)
    on = (lane >= ROPE_LO) & (lane < ROPE_LO + MLA_ROPE)
    freq = jnp.where(on, inv[(lane - ROPE_LO) % ROPE_HALF], 0.0).reshape(1, LANES).astype(F32)
    sign = np.where(on, np.where(lane < ROPE_LO + ROPE_HALF, -1.0, 1.0), 0.0).reshape(1, LANES).astype(np.float32)
    return freq, jnp.asarray(sign)


def _rot_tables(pos, freq, sign):
    ang = pos.astype(F32) * freq
    return jnp.cos(ang), jnp.sin(ang) * sign


def _swap_halves(v):
    lane = lax.broadcasted_iota(jnp.int32, (1, LANES), 1)
    return jnp.where(lane < ROPE_LO + ROPE_HALF, pltpu.roll(v, LANES - ROPE_HALF, 1), pltpu.roll(v, ROPE_HALF, 1))


def rope_fwd(qp, kvp, proj, pos, freq, sign, rows):
    def fn(q, k, kr, v, p, f, sg):
        c, sn = _rot_tables(p, f, sg)
        rk = kr * c + _swap_halves(kr) * sn
        one = (lax.broadcasted_iota(jnp.int32, (v.shape[0], HEAD_BLK - MLA_V), 1) == 0).astype(F32)
        qs, ks, vs = [], [], []
        for h in range(MLA_HEADS):
            qh = q[:, h * HEAD_BLK:(h + 1) * HEAD_BLK]
            qs.append((qh * c + _swap_halves(qh) * sn) * QSCALE)
            ks.append(k[:, h * HEAD_BLK:(h + 1) * HEAD_BLK] + rk)
            vs += [v[:, h * MLA_V:(h + 1) * MLA_V], one]
        return jnp.concatenate(qs, axis=1), jnp.concatenate(ks, axis=1), jnp.concatenate(vs, axis=1)

    w = MLA_HEADS * HEAD_BLK
    return rows_call("od_rope_fwd", rows, fn, [qp, kvp, proj, kvp, pos, freq, sign],
                     [rows.tile(w), rows.tile(w, 0), rows.tile(LANES, O_KR), rows.tile(MLA_WIDTH, w),
                      rows.tile(1), Rows.full((1, LANES)), Rows.full((1, LANES))],
                     [(w, MATMUL_DTYPE), (w, MATMUL_DTYPE), (w, MATMUL_DTYPE)])


def rope_bwd(dq, dk, dv, pos, freq, sign, rows):
    def fn(dq_, dk_, dv_, p, f, sg):
        c, sn = _rot_tables(p, f, sg)
        on = jnp.abs(sg)
        outs, dks, dvs, dkr = [], [], [], jnp.zeros((dq_.shape[0], LANES), F32)
        for h in range(MLA_HEADS):
            g = dq_[:, h * HEAD_BLK:(h + 1) * HEAD_BLK] * ATTN_SCALE
            outs.append(g * c + _swap_halves(g * sn) * on)
            gk = dk_[:, h * HEAD_BLK:(h + 1) * HEAD_BLK] * LN2
            dks.append(gk)
            dkr = dkr + gk * c + _swap_halves(gk * sn) * on
            dvs.append(dv_[:, h * HEAD_BLK:h * HEAD_BLK + MLA_V])
        return jnp.concatenate(outs, axis=1), jnp.concatenate(dks + dvs, axis=1), dkr

    w = MLA_HEADS * HEAD_BLK
    return rows_call("od_rope_bwd", rows, fn, [dq, dk, dv, pos, freq, sign],
                     [rows.tile(w), rows.tile(w), rows.tile(w), rows.tile(1), Rows.full((1, LANES)),
                      Rows.full((1, LANES))],
                     [(w, MATMUL_DTYPE), (w + MLA_WIDTH, MATMUL_DTYPE), (LANES, F32)])


_PAIRS = MLA_HEADS // 2
ATT_TQ = 512
ATT_TK = 8192
ATT_BWD_TQ = 512
ATT_BWD_TK = 2048


def _att_tiles(s, backward=False):
    if backward:
        return min(ATT_BWD_TQ, s), min(ATT_BWD_TK, s)
    return min(ATT_TQ, s), min(ATT_TK, s)


def attention_fwd(qcat, kcat, vcat, s):
    tq, tk = _att_tiles(s)
    nq, nk = s // tq, s // tk

    def body(q_ref, k_ref, v_ref, o_ref, lse_ref, m_ref, acc_ref):
        kk = pl.program_id(2)

        @pl.when(kk == 0)
        def _():
            m_ref[...] = jnp.full(m_ref.shape, -jnp.inf, F32)
            acc_ref[...] = jnp.zeros(acc_ref.shape, F32)

        sl = [slice(hh * HEAD_BLK, (hh + 1) * HEAD_BLK) for hh in range(2)]
        sc = [_mm(q_ref[:, sl[hh]], k_ref[:, sl[hh]], _NT) for hh in range(2)]
        for hh in range(2):
            m_prev = m_ref[hh]
            m_new = jnp.maximum(m_prev, jnp.max(sc[hh], axis=1, keepdims=True))
            p = jnp.exp2(sc[hh] - m_new[:, :1])
            acc_ref[hh] = acc_ref[hh] * jnp.exp2(m_prev - m_new) + _mm(p, v_ref[:, sl[hh]])
            m_ref[hh] = m_new

        @pl.when(kk == nk - 1)
        def _():
            half = lax.broadcasted_iota(jnp.int32, (1, LANES), 1) < MLA_V
            l0, l1 = acc_ref[0][:, MLA_V:MLA_V + 1], acc_ref[1][:, MLA_V:MLA_V + 1]
            o_ref[...] = jnp.where(half, acc_ref[0] / l0, pltpu.roll(acc_ref[1] / l1, MLA_V, 1))
            lse_ref[...] = jnp.where(half, m_ref[0] + jnp.log2(l0), m_ref[1] + jnp.log2(l1))

    return pl.pallas_call(
        body, name="od_attn_fwd", grid=(_PAIRS, nq, nk),
        in_specs=[pl.BlockSpec((tq, 2 * HEAD_BLK), lambda p, i, kk: (i, p)),
                  pl.BlockSpec((tk, 2 * HEAD_BLK), lambda p, i, kk: (kk, p)),
                  pl.BlockSpec((tk, 2 * HEAD_BLK), lambda p, i, kk: (kk, p))],
        out_specs=[pl.BlockSpec((tq, LANES), lambda p, i, kk: (i, p)),
                   pl.BlockSpec((None, tq, LANES), lambda p, i, kk: (p, i, 0))],
        out_shape=[jax.ShapeDtypeStruct((s, MLA_WIDTH), F32), jax.ShapeDtypeStruct((_PAIRS, s, LANES), F32)],
        scratch_shapes=[pltpu.VMEM((2, tq, LANES), F32)] * 2,
        compiler_params=_params(("parallel", "parallel", "arbitrary")),
    )(qcat, kcat, vcat)


def attention_bwd(qcat, kcat, vcat, o, lse, do, s):
    tq, tk = _att_tiles(s, backward=True)
    nq, nk = s // tq, s // tk

    def body(q_ref, k_ref, v_ref, do_ref, o_ref, lse_ref, dq_ref, dk_ref, dv_ref):
        kk, i = pl.program_id(1), pl.program_id(2)
        lane = lax.broadcasted_iota(jnp.int32, (1, LANES), 1)
        half = lane < MLA_V
        do_p, lse = do_ref[...], lse_ref[...]
        prod = do_p * o_ref[...]
        rows_i = pl.ds(pl.multiple_of(i * tq, tq), tq)
        for hh in range(2):
            sl = slice(hh * HEAD_BLK, (hh + 1) * HEAD_BLK)
            mine = half if hh == 0 else jnp.logical_not(half)
            delta = jnp.sum(jnp.where(mine, prod, 0.0), axis=1, keepdims=True)
            do_h = jnp.where(half, do_p if hh == 0 else pltpu.roll(do_p, MLA_V, 1), 0.0)
            p = jnp.exp2(_mm(q_ref[:, sl], k_ref[:, sl], _NT) - lse[:, hh * MLA_V:hh * MLA_V + 1])
            ds = p * (_mm(do_h, v_ref[:, sl], _NT) - delta)
            dv_h, dk_h, dq_h = _mm(p, do_h, _TN), _mm(ds, q_ref[:, sl], _TN), _mm(ds, k_ref[:, sl])

            @pl.when(i == 0)
            def _():
                dv_ref[:, sl] = dv_h
                dk_ref[:, sl] = dk_h

            @pl.when(i != 0)
            def _():
                dv_ref[:, sl] += dv_h
                dk_ref[:, sl] += dk_h

            @pl.when(kk == 0)
            def _():
                dq_ref[rows_i, sl] = dq_h

            @pl.when(kk != 0)
            def _():
                dq_ref[rows_i, sl] += dq_h

    w = MLA_HEADS * HEAD_BLK
    return pl.pallas_call(
        body, name="od_attn_bwd", grid=(_PAIRS, nk, nq),
        in_specs=[pl.BlockSpec((tq, 2 * HEAD_BLK), lambda p, kk, i: (i, p)),
                  pl.BlockSpec((tk, 2 * HEAD_BLK), lambda p, kk, i: (kk, p)),
                  pl.BlockSpec((tk, 2 * HEAD_BLK), lambda p, kk, i: (kk, p)),
                  pl.BlockSpec((tq, LANES), lambda p, kk, i: (i, p)),
                  pl.BlockSpec((tq, LANES), lambda p, kk, i: (i, p)),
                  pl.BlockSpec((None, tq, LANES), lambda p, kk, i: (p, i, 0))],
        out_specs=[pl.BlockSpec((s, 2 * HEAD_BLK), lambda p, kk, i: (0, p)),
                   pl.BlockSpec((tk, 2 * HEAD_BLK), lambda p, kk, i: (kk, p)),
                   pl.BlockSpec((tk, 2 * HEAD_BLK), lambda p, kk, i: (kk, p))],
        out_shape=[jax.ShapeDtypeStruct((s, w), F32)] * 3,
        compiler_params=_params(("parallel", "arbitrary", "arbitrary")),
    )(qcat, kcat, vcat, do, o, lse)


def _pool_counts(n_rows, first_row, s, w):
    pos = first_row + lax.broadcasted_iota(jnp.int32, (n_rows, 1), 0)
    lo = jnp.clip(pos - w // 2, 0, s)
    hi = jnp.clip(pos + w - w // 2, 0, s)
    return jnp.maximum(hi - lo, 1).astype(F32)


def _window_sum(e, levels, mirrored):
    n = e.shape[0]
    acc = e + pltpu.roll(e, (n - 1) if mirrored else 1, 0)
    step = 1
    for _ in range(levels - 1):
        acc = pltpu.roll(acc, step, 0) + pltpu.roll(acc, n - step, 0)
        step *= 2
    return acc


def _pooled(ue, s, t):
    first = pl.program_id(0) * t
    outs = []
    for gi, w in enumerate(POOL_WINDOWS):
        eg = ue[:, gi * POOL_GROUP:(gi + 1) * POOL_GROUP]
        sm = _window_sum(eg, gi + 1, False)[HALO:HALO + t]
        outs.append(sm / _pool_counts(t, first, s, w) - eg[HALO:HALO + t])
    return outs


def odd_mix_fwd(proj, o, pool_w, pool_scale, rows):
    def fn(o_, gc, up, u, un, gd, pw, ps):
        pooled = _pooled(_ext(up, u, un, rows.n), rows.s, rows.t)
        lin = jnp.concatenate([_mm(pooled[g], pw[g]) for g in range(len(POOL_WINDOWS))], axis=1)
        return jnp.concatenate([o_ * _silu(gc), lin * ps * _silu(gd)], axis=1)

    w = POOL_WIDTH
    return rows_call("od_mix_fwd", rows, fn, [o, proj, proj, proj, proj, proj, pool_w, pool_scale],
                     [rows.tile(w), rows.tile(w, O_GC)] + rows.halo(w, O_UD)
                     + [rows.tile(w, O_GD), Rows.full((4, POOL_GROUP, POOL_GROUP)), Rows.full((1, w))],
                     [(2 * w, MATMUL_DTYPE)])[0]


def odd_mix_bwd(proj, o, pool_w, pool_scale, dycd, rows):
    w = POOL_WIDTH
    ng = len(POOL_WINDOWS)

    def fn(o_, gc, up, u, un, gdp, gd, gdn, pw, ps, dyc, dydp, dyd, dydn):
        n, t, s = rows.n, rows.t, rows.s
        dyc = dyc.astype(F32)
        do = dyc * _silu(gc)
        dgc = dyc * o_ * _dsilu(gc)
        pooled = _pooled(_ext(up, u, un, n), s, t)
        lin = jnp.concatenate([_mm(pooled[g], pw[g]) for g in range(ng)], axis=1)
        dydc = dyd.astype(F32)
        dgd = dydc * lin * ps * _dsilu(gd)
        dps = _rowsum(dydc * lin * _silu(gd))
        dlin_e = _ext(dydp, dyd, dydn, n) * ps * _silu(_ext(gdp, gd, gdn, n))
        first = pl.program_id(0) * t - HALO
        dus, dpws = [], []
        for g, win in enumerate(POOL_WINDOWS):
            sl = slice(g * POOL_GROUP, (g + 1) * POOL_GROUP)
            dle = dlin_e[:, sl]
            dpws.append(_mm(pooled[g], dle[HALO:HALO + t], _TN))
            dpe = _mm(dle, pw[g], _NT)
            gce = dpe / _pool_counts(t + 2 * HALO, first, s, win)
            dus.append(_window_sum(gce, g + 1, True)[HALO:HALO + t] - dpe[HALO:HALO + t])
        return do, dgc, jnp.concatenate(dus, axis=1), dgd, jnp.stack(dpws), dps

    return rows_call("od_mix_bwd", rows, fn,
                     [o, proj, proj, proj, proj, proj, proj, proj, pool_w, pool_scale, dycd, dycd, dycd, dycd],
                     [rows.tile(w), rows.tile(w, O_GC)] + rows.halo(w, O_UD) + rows.halo(w, O_GD)
                     + [Rows.full((ng, POOL_GROUP, POOL_GROUP)), Rows.full((1, w)), rows.tile(w, 0)]
                     + rows.halo(w, w),
                     [(w, F32)] * 4, [(ng, POOL_GROUP, POOL_GROUP), (1, w)])


def pad_odd_w_in(w):
    z = lambda n: jnp.zeros((w.shape[0], n), w.dtype)
    return jnp.concatenate([w[:, :384], z(ROPE_LO), w[:, 384:416], z(LANES - ROPE_LO - MLA_ROPE), w[:, 416:]], axis=1)


def unpad_odd_w_in_t(wpt):
    return jnp.concatenate([wpt[:384], wpt[O_KR + ROPE_LO:O_KR + ROPE_LO + MLA_ROPE], wpt[O_GC:]], axis=0)


def pad_w_uq(w):
    w3 = w.reshape(MLA_Q_RANK, MLA_HEADS, MLA_NOPE + MLA_ROPE)
    w3 = jnp.pad(w3, ((0, 0), (0, 0), (0, HEAD_BLK - MLA_NOPE - MLA_ROPE)))
    return w3.reshape(MLA_Q_RANK, MLA_HEADS * HEAD_BLK)


def unpad_w_uq(wp):
    return wp.reshape(MLA_Q_RANK, MLA_HEADS, HEAD_BLK)[..., :MLA_NOPE + MLA_ROPE].reshape(MLA_Q_RANK, -1)


def pad_w_ukv(w):
    w3 = w.reshape(MLA_KV_RANK, MLA_HEADS, MLA_NOPE + MLA_V)
    kp = jnp.pad(w3[..., :MLA_NOPE], ((0, 0), (0, 0), (0, HEAD_BLK - MLA_NOPE)))
    return jnp.concatenate([kp.reshape(MLA_KV_RANK, -1), w3[..., MLA_NOPE:].reshape(MLA_KV_RANK, -1)], axis=1)


def unpad_w_ukv(wp):
    kp = wp[:, :MLA_HEADS * HEAD_BLK].reshape(MLA_KV_RANK, MLA_HEADS, HEAD_BLK)[..., :MLA_NOPE]
    vp = wp[:, MLA_HEADS * HEAD_BLK:].reshape(MLA_KV_RANK, MLA_HEADS, MLA_V)
    return jnp.concatenate([kp, vp], axis=-1).reshape(MLA_KV_RANK, -1)


def odd_weights(od_w_in, od_q_norm_g, od_w_uq, od_kv_norm_g, od_w_ukv, od_pool_w, od_pool_scale, od_w_out,
                od_ln_g, od_ln_b):
    freq, sign = rope_rows()
    return dict(w_in_p=pad_odd_w_in(od_w_in).astype(MATMUL_DTYPE), gq=od_q_norm_g, gkv=od_kv_norm_g,
                w_uq_p=pad_w_uq(od_w_uq).astype(MATMUL_DTYPE), w_ukv_p=pad_w_ukv(od_w_ukv).astype(MATMUL_DTYPE),
                pool_w=od_pool_w, pool_scale=od_pool_scale, w_out=od_w_out.astype(MATMUL_DTYPE), ln_g=od_ln_g,
                ln_b=od_ln_b, freq=freq, sign=sign)


def odd_layer_loss(x, pos, target, w, rows):
    s = rows.s
    proj = matmul(x, w["w_in_p"], "nn", F32, "od_proj")
    cqn, ckvn = latent_norm_fwd(proj, w["gq"], w["gkv"], rows)
    qp = matmul(cqn, w["w_uq_p"], "nn", F32, "od_q_up")
    kvp = matmul(ckvn, w["w_ukv_p"], "nn", F32, "od_kv_up")
    qcat, kcat, v = rope_fwd(qp, kvp, proj, pos, w["freq"], w["sign"], rows)
    o, lse = attention_fwd(qcat, kcat, v, s)
    ycd = odd_mix_fwd(proj, o, w["pool_w"], w["pool_scale"], rows)
    h = matmul(ycd, w["w_out"], "nn", F32, "od_out")
    dres, dh, dln_g, dln_b, loss_lanes = final_ln_loss(x, h, w["ln_g"], w["ln_b"], target, rows)
    dycd = matmul(dh, w["w_out"], "nt", F32, "od_out_dx")
    dw_out = matmul(ycd, dh, "tn", F32, "od_out_dw")
    do, dgc, dud, dgd, dpool_w, dpool_scale = odd_mix_bwd(proj, o, w["pool_w"], w["pool_scale"], dycd, rows)
    dq, dk, dv = attention_bwd(qcat, kcat, v, o, lse, do, s)
    dqp, dkvp, dkr = rope_bwd(dq, dk, dv, pos, w["freq"], w["sign"], rows)
    dcqn = matmul(dqp, w["w_uq_p"], "nt", F32, "od_q_up_dx")
    dw_uq = unpad_w_uq(matmul(cqn, dqp, "tn", F32, "od_q_up_dw"))
    dckvn = matmul(dkvp, w["w_ukv_p"], "nt", F32, "od_kv_up_dx")
    dw_ukv = unpad_w_ukv(matmul(ckvn, dkvp, "tn", F32, "od_kv_up_dw"))
    dcq, dckv, dgq, dgkv = latent_norm_bwd(proj, w["gq"], w["gkv"], dcqn, dckvn, rows)
    dproj = jnp.concatenate([dcq, dckv, dkr, dgc, dud, dgd], axis=1).astype(MATMUL_DTYPE)
    dx = matmul(dproj, w["w_in_p"], "nt", F32, "od_proj_dx", add=dres)
    dw_in = unpad_odd_w_in_t(matmul(dproj, x, "tn", F32, "od_proj_dw"))
    g = dict(od_w_in=dw_in, od_q_norm_g=dgq, od_w_uq=dw_uq, od_kv_norm_g=dgkv, od_w_ukv=dw_ukv,
             od_pool_w=dpool_w, od_pool_scale=dpool_scale, od_w_out=dw_out, od_ln_g=dln_g, od_ln_b=dln_b)
    return loss_lanes, dx, g


_MESH = pl.DeviceIdType.MESH
_ANY = pl.BlockSpec(memory_space=pl.ANY)
N_CHIPS = 4


def _push_call(name, ins, out_shapes, plan, n_remote, n_local):
    n_in, n_out = len(ins), len(out_shapes)

    def body(*refs):
        in_refs, out_refs = refs[:n_in], refs[n_in:n_in + n_out]
        send_sems, recv_sems, local_sems = refs[n_in + n_out:]
        x, y, c = lax.axis_index("x"), lax.axis_index("y"), lax.axis_index("c")
        remote, local = plan(in_refs, out_refs, x, y, c)
        assert len(remote) == n_remote and len(local) == n_local
        sends = [pltpu.make_async_remote_copy(src_ref=s, dst_ref=d, send_sem=send_sems.at[k], recv_sem=recv_sems.at[k],
                                              device_id=dev, device_id_type=_MESH)
                 for k, (s, d, dev, _) in enumerate(remote)]
        recvs = [pltpu.make_async_remote_copy(src_ref=s, dst_ref=land, send_sem=send_sems.at[k],
                                              recv_sem=recv_sems.at[k], device_id=dev, device_id_type=_MESH)
                 for k, (s, _, dev, land) in enumerate(remote)]
        locs = [pltpu.make_async_copy(s, d, local_sems.at[k]) for k, (s, d) in enumerate(local)]
        for cp in sends + locs:
            cp.start()
        for cp in recvs:
            cp.wait_recv()
        for cp in sends:
            cp.wait_send()
        for cp in locs:
            cp.wait()

    return pl.pallas_call(
        body, name=name, in_specs=[_ANY] * n_in, out_specs=[_ANY] * n_out, out_shape=list(out_shapes),
        scratch_shapes=[pltpu.SemaphoreType.DMA((n_remote,)), pltpu.SemaphoreType.DMA((n_remote,)),
                        pltpu.SemaphoreType.DMA((max(n_local, 1),))],
    )(*ins)


def _other_chips(x, y):
    return [(1 - x, y), (x, 1 - y), (1 - x, 1 - y)]


def chips_allgather(bufs):
    def plan(in_refs, out_refs, x, y, c):
        me = 2 * x + y
        remote, local = [], []
        for src, out in zip(in_refs, out_refs):
            for (px, py) in _other_chips(x, y):
                remote.append((src, out.at[me], (px, py, c), out.at[2 * px + py]))
            local.append((src, out.at[me]))
        return remote, local

    shapes = [jax.ShapeDtypeStruct((N_CHIPS,) + b.shape, b.dtype) for b in bufs]
    return _push_call("weights_allgather", bufs, shapes, plan, 3 * len(bufs), len(bufs))


def sibling_send(buf, name):
    def plan(in_refs, out_refs, x, y, c):
        return [(in_refs[0], out_refs[0], (x, y, 1 - c), out_refs[0])], []

    return _push_call(name, [buf], [jax.ShapeDtypeStruct(buf.shape, buf.dtype)], plan, 1, 0)[0]


def chips_scatter(buf):
    def plan(in_refs, out_refs, x, y, c):
        me = 2 * x + y
        src, out = in_refs[0], out_refs[0]
        remote = [(src.at[2 * px + py], out.at[me], (px, py, c), out.at[2 * px + py]) for (px, py) in _other_chips(x, y)]
        return remote, [(src.at[me], out.at[me])]

    return _push_call("grads_scatter", [buf], [jax.ShapeDtypeStruct(buf.shape, buf.dtype)], plan, 3, 1)[0]


PACK_W = 1024
PACK_BLK = 128


def _ew_call(name, fn, ins, in_specs, out_shape, out_spec, n_out, steps):
    def body(*refs):
        res = fn(*[r[...] for r in refs[:len(ins)]])
        if not isinstance(res, (tuple, list)):
            res = (res,)
        for r, v in zip(refs[len(ins):], res):
            r[...] = v

    return pl.pallas_call(
        body, name=name, grid=(steps,), in_specs=in_specs, out_specs=[out_spec] * n_out,
        out_shape=[jax.ShapeDtypeStruct(out_shape, F32)] * n_out,
        compiler_params=_params(("parallel",)),
    )(*ins)


def add2(a, b, name):
    n, r, w = a.shape
    blk = pl.BlockSpec((PACK_BLK, w), lambda i: (i, 0))
    out = _ew_call(name, lambda u, v: u + v, [a.reshape(n * r, w), b.reshape(n * r, w)], [blk, blk], (n * r, w), blk,
                   1, n * r // PACK_BLK)[0]
    return out.reshape(a.shape)


def sum_chips(buf):
    _, r, w = buf.shape
    specs = [pl.BlockSpec((None, PACK_BLK, w), lambda i, q=q: (q, i, 0)) for q in range(N_CHIPS)]
    return _ew_call("grads_sum", lambda a, b, c, d: ((a + b) + c) + d, [buf] * N_CHIPS, specs, (r, w),
                    pl.BlockSpec((PACK_BLK, w), lambda i: (i, 0)), 1, r // PACK_BLK)[0]


def adamw(w, g, m, v, name):
    def fn(w_, g_, m_, v_):
        m2 = ADAM_B1 * m_ + (1.0 - ADAM_B1) * g_
        v2 = ADAM_B2 * v_ + (1.0 - ADAM_B2) * jnp.square(g_)
        m_hat = m2 / (1.0 - ADAM_B1 ** ADAM_STEP)
        v_hat = v2 / (1.0 - ADAM_B2 ** ADAM_STEP)
        return -ADAM_LR * (m_hat / (jnp.sqrt(v_hat) + ADAM_EPS) + ADAM_WD * w_), m2, v2

    r, c = w.shape
    br = r if r <= 512 else _pick(r, (256, 128, 64, 32, 16, 8))
    blk = pl.BlockSpec((br, c), lambda i: (i, 0))
    return _ew_call(name, fn, [w, g, m, v], [blk] * 4, (r, c), blk, 3, r // br)


WEIGHTS = (
    ("ev_w_in", (1, 1024, 7200), 2), ("ev_conv_w", (1, 4, 2048), 2), ("ev_conv_b", (1, 2048), None),
    ("ev_a_log", (1, 2, 16), None), ("ev_dt_bias", (1, 2, 16), None), ("ev_d_skip", (1, 2, 16), None),
    ("ev_norm_g", (1, 1024), None), ("ev_sc_conv_w", (1, 3, 1024), 2), ("ev_w_out", (1, 2048, 1024), 1),
    ("ev_ln_g", (1, 1024), None), ("ev_ln_b", (1, 1024), None), ("od_w_in", (1, 1024, 1952), 2),
    ("od_q_norm_g", (1, 256), 1), ("od_w_uq", (1, 256, 768), 2), ("od_kv_norm_g", (1, 128), None),
    ("od_w_ukv", (1, 128, 1024), 2), ("od_pool_w", (1, 4, 128, 128), None), ("od_pool_scale", (1, 512), 1),
    ("od_w_out", (1, 1024, 1024), 1), ("od_ln_g", (1, 1024), 1), ("od_ln_b", (1, 1024), 1),
)
BIG = ("ev_w_in", "ev_w_out", "od_w_in", "od_w_uq", "od_w_ukv", "od_w_out")


def _block_shape(shape, axis):
    if axis is None:
        return tuple(shape)
    return tuple(d // N_CHIPS if i == axis else d for i, d in enumerate(shape))


def _pack(arrs, quantum):
    flat = jnp.concatenate([a.reshape(-1) for a in arrs])
    n = flat.shape[0]
    padded = -(-n // quantum) * quantum
    return jnp.concatenate([flat, jnp.zeros((padded - n,), flat.dtype)]).reshape(-1, LANES)


def _unpack(flat, shapes):
    out, off = [], 0
    for sh in shapes:
        n = int(np.prod(sh))
        out.append(flat[off:off + n].reshape(sh))
        off += n
    return out


def gather_weights(local):
    sharded = [(n, sh, ax) for (n, sh, ax) in WEIGHTS if ax is not None]
    big = [(n, sh, ax) for (n, sh, ax) in sharded if n in BIG]
    small = [(n, sh, ax) for (n, sh, ax) in sharded if n not in BIG]
    pb = _pack([local[n].astype(MATMUL_DTYPE) for n, _, _ in big], 2 * SUBLANES * LANES)
    ps = _pack([local[n] for n, _, _ in small], SUBLANES * LANES)
    gb, gs = chips_allgather([pb, ps])
    full = {n: local[n] for (n, sh, ax) in WEIGHTS if ax is None}
    for group, g in ((big, gb), (small, gs)):
        parts = [_unpack(g[q].reshape(-1), [_block_shape(sh, ax) for _, sh, ax in group]) for q in range(N_CHIPS)]
        for i, (n, sh, ax) in enumerate(group):
            full[n] = jnp.concatenate([parts[q][i] for q in range(N_CHIPS)], axis=ax)
    return full


WIDE = (("ev_w_in", True), ("ev_w_out", False), ("od_w_in", True), ("od_w_out", False))


def reduce_and_update(grads, local_w, local_m, local_v):
    c = lax.axis_index("c")
    wide_names = [n for n, _ in WIDE]
    tail = [(n, sh, ax) for (n, sh, ax) in WEIGHTS if n not in wide_names]
    tail_blocks = [_block_shape(sh, ax) for _, sh, ax in tail]
    n_tail = sum(int(np.prod(b)) for b in tail_blocks)
    wide_rows = [grads[n].shape[0] // N_CHIPS for n in wide_names]
    quantum = 2 * PACK_BLK
    total = -(-(sum(wide_rows) + -(-n_tail // PACK_W)) // quantum) * quantum
    tail_rows = total - sum(wide_rows)

    def tail_pack(pieces):
        flat = jnp.concatenate([p.reshape(-1) for p in pieces] + [jnp.zeros((tail_rows * PACK_W - n_tail,), F32)])
        return flat.reshape(tail_rows, PACK_W)

    packs = []
    for q in range(N_CHIPS):
        parts = [lax.slice_in_dim(grads[n], q * r, (q + 1) * r, axis=0) for n, r in zip(wide_names, wide_rows)]
        pieces = []
        for (n, sh, ax), bs in zip(tail, tail_blocks):
            g = grads[n].reshape(sh)
            pieces.append(g if ax is None else lax.slice_in_dim(g, q * bs[ax], (q + 1) * bs[ax], axis=ax))
        packs.append(jnp.concatenate(parts + [tail_pack(pieces)], axis=0))
    packs = jnp.stack(packs)
    rh = total // 2
    keep = lax.dynamic_slice_in_dim(packs, c * rh, rh, axis=1)
    give = lax.dynamic_slice_in_dim(packs, (1 - c) * rh, rh, axis=1)
    chip_half = add2(keep, sibling_send(give, "grads_to_sibling"), "grads_chip_sum")
    total_half = sum_chips(chips_scatter(chip_half))
    other_half = sibling_send(total_half, "grads_from_sibling")
    g_pack = jnp.concatenate([jnp.where(c == 0, total_half, other_half),
                              jnp.where(c == 0, other_half, total_half)], axis=0)
    outs = ({}, {}, {}, {})
    off = 0
    for (n, transposed), r in zip(WIDE, wide_rows):
        g = g_pack[off:off + r]
        off += r
        g = g.T if transposed else g
        shape = local_w[n].shape
        res = adamw(local_w[n].reshape(g.shape), g, local_m[n].reshape(g.shape), local_v[n].reshape(g.shape),
                    "adamw_" + n)
        for d, a in zip(outs, (g, *res)):
            d[n] = a.reshape(shape)
    g_tail = g_pack[off:]
    res = adamw(*[tail_pack([d[n] for n, _, _ in tail]) if d is not None else g_tail
                  for d in (local_w, None, local_m, local_v)], "adamw_small")
    for d, a in zip(outs, (g_tail, *res)):
        d.update(zip([n for n, _, _ in tail], _unpack(a.reshape(-1), tail_blocks)))
    return outs


ROW_TILE = 256


def kernel(x, positions, ev_w_in, ev_conv_w, ev_conv_b, ev_a_log, ev_dt_bias, ev_d_skip, ev_norm_g, ev_sc_conv_w, ev_w_out, ev_ln_g, ev_ln_b, od_w_in, od_q_norm_g, od_w_uq, od_kv_norm_g, od_w_ukv, od_pool_w, od_pool_scale, od_w_out, od_ln_g, od_ln_b, loss_target, m_ev_w_in, m_ev_conv_w, m_ev_conv_b, m_ev_a_log, m_ev_dt_bias, m_ev_d_skip, m_ev_norm_g, m_ev_sc_conv_w, m_ev_w_out, m_ev_ln_g, m_ev_ln_b, m_od_w_in, m_od_q_norm_g, m_od_w_uq, m_od_kv_norm_g, m_od_w_ukv, m_od_pool_w, m_od_pool_scale, m_od_w_out, m_od_ln_g, m_od_ln_b, v_ev_w_in, v_ev_conv_w, v_ev_conv_b, v_ev_a_log, v_ev_dt_bias, v_ev_d_skip, v_ev_norm_g, v_ev_sc_conv_w, v_ev_w_out, v_ev_ln_g, v_ev_ln_b, v_od_w_in, v_od_q_norm_g, v_od_w_uq, v_od_kv_norm_g, v_od_w_ukv, v_od_pool_w, v_od_pool_scale, v_od_w_out, v_od_ln_g, v_od_ln_b):
    names = [n for n, _, _ in WEIGHTS]
    local_w = dict(zip(names, (ev_w_in, ev_conv_w, ev_conv_b, ev_a_log, ev_dt_bias, ev_d_skip, ev_norm_g, ev_sc_conv_w, ev_w_out, ev_ln_g, ev_ln_b, od_w_in, od_q_norm_g, od_w_uq, od_kv_norm_g, od_w_ukv, od_pool_w, od_pool_scale, od_w_out, od_ln_g, od_ln_b)))
    local_m = dict(zip(names, (m_ev_w_in, m_ev_conv_w, m_ev_conv_b, m_ev_a_log, m_ev_dt_bias, m_ev_d_skip, m_ev_norm_g, m_ev_sc_conv_w, m_ev_w_out, m_ev_ln_g, m_ev_ln_b, m_od_w_in, m_od_q_norm_g, m_od_w_uq, m_od_kv_norm_g, m_od_w_ukv, m_od_pool_w, m_od_pool_scale, m_od_w_out, m_od_ln_g, m_od_ln_b)))
    local_v = dict(zip(names, (v_ev_w_in, v_ev_conv_w, v_ev_conv_b, v_ev_a_log, v_ev_dt_bias, v_ev_d_skip, v_ev_norm_g, v_ev_sc_conv_w, v_ev_w_out, v_ev_ln_g, v_ev_ln_b, v_od_w_in, v_od_q_norm_g, v_od_w_uq, v_od_kv_norm_g, v_od_w_ukv, v_od_pool_w, v_od_pool_scale, v_od_w_out, v_od_ln_g, v_od_ln_b)))
    s = x.shape[1]
    rows = Rows(s, min(ROW_TILE, s))
    f = gather_weights(local_w)
    ew = even_weights(f["ev_w_in"][0], f["ev_conv_w"][0], f["ev_conv_b"], f["ev_a_log"][0], f["ev_dt_bias"][0],
                      f["ev_d_skip"][0], f["ev_norm_g"], f["ev_sc_conv_w"][0], f["ev_w_out"][0], f["ev_ln_g"],
                      f["ev_ln_b"])
    ow = odd_weights(f["od_w_in"][0], f["od_q_norm_g"], f["od_w_uq"][0], f["od_kv_norm_g"], f["od_w_ukv"][0],
                     f["od_pool_w"][0], f["od_pool_scale"], f["od_w_out"][0], f["od_ln_g"], f["od_ln_b"])
    x1, saved = even_layer(x[0], ew, rows)
    loss_lanes, dx1, g_odd = odd_layer_loss(x1, positions.reshape(s, 1), loss_target[0], ow, rows)
    dx0, g_even = even_layer_bwd(dx1, ew, saved, rows)
    loss = lax.psum(jnp.sum(loss_lanes), ("x", "y", "c"))
    grad, delta, new_m, new_v = reduce_and_update({**g_even, **g_odd}, local_w, local_m, local_v)
    return (loss, dx0[None], *[grad[n] for n in names], *[delta[n] for n in names],
            *[new_m[n] for n in names], *[new_v[n] for n in names])
```

```python
import functools
import math

import jax
import jax.numpy as jnp
import numpy as np
from jax import lax
from jax.experimental import pallas as pl
from jax.experimental.pallas import tpu as pltpu

F32 = jnp.float32
BF16 = jnp.bfloat16
MATMUL_DTYPE = jnp.bfloat16

D_MODEL = 1024
DEPTH = 2
SSD_HEADS, SSD_HEAD_DIM, SSD_GROUPS, SSD_STATE, SSD_CHUNK = 16, 64, 4, 128, 128
SSD_INNER = SSD_HEADS * SSD_HEAD_DIM
SSD_XBC = SSD_INNER + 2 * SSD_GROUPS * SSD_STATE
SC_WIDTH = 1024
MLA_HEADS, MLA_Q_RANK, MLA_KV_RANK, MLA_NOPE, MLA_ROPE, MLA_V = 8, 256, 128, 64, 32, 64
MLA_WIDTH = MLA_HEADS * MLA_V
ROPE_THETA = 10000.0
ATTN_SCALE = (MLA_NOPE + MLA_ROPE) ** -0.5
QSCALE = ATTN_SCALE * math.log2(math.e)
LN2 = math.log(2.0)
POOL_WINDOWS = (2, 4, 8, 16)
POOL_GROUP = 128
POOL_WIDTH = POOL_GROUP * len(POOL_WINDOWS)
EPS = 1e-5
ALPHA = (2 * DEPTH) ** 0.25
EVEN_PROJ, ODD_PROJ = 7200, 1952
ADAM_LR, ADAM_B1, ADAM_B2, ADAM_EPS, ADAM_WD, ADAM_STEP = 0.001, 0.9, 0.999, 1e-08, 0.01, 10

LANES = 128
SUBLANES = 8
HALO = SUBLANES
VMEM_LIMIT = 56 * 1024 * 1024

EVEN_P = 7296
ODD_P = 2048
E_Z, E_XBC, E_BG, E_CG, E_H, E_GATE, E_DT = 0, 1024, 3072, 4096, 5120, 6144, 7168
O_CQ, O_CKV, O_KR, O_GC, O_UD, O_GD = 0, 256, 384, 512, 1024, 1536


def _params(sem=None):
    return pltpu.CompilerParams(dimension_semantics=sem, vmem_limit_bytes=VMEM_LIMIT)


def _mm(a, b, dims=(((1,), (0,)), ((), ()))):
    return lax.dot_general(a.astype(MATMUL_DTYPE), b.astype(MATMUL_DTYPE), dims, preferred_element_type=F32)


_NN = (((1,), (0,)), ((), ()))
_NT = (((1,), (1,)), ((), ()))
_TN = (((0,), (0,)), ((), ()))


def _silu(v):
    return v * jax.nn.sigmoid(v)


def _dsilu(v):
    s = jax.nn.sigmoid(v)
    return s * (1.0 + v * (1.0 - s))


def _pick(n, prefs):
    for p in prefs:
        if n % p == 0:
            return p
    return n


def matmul(a, b, mode, out_dtype, name, add=None, tm=None, tn=None, tk=None):
    if mode == "nn":
        (m, k), (k2, n) = a.shape, b.shape
    elif mode == "nt":
        (m, k), (n, k2) = a.shape, b.shape
    else:
        (k, m), (k2, n) = a.shape, b.shape
    assert k == k2, (a.shape, b.shape, mode)
    wide = (2432, 2048, 1536, 1024, 512, 256, 128)
    if mode == "tn":
        tm = tm or _pick(m, (2432, 2048, 1024, 512, 256, 128))
        tn = tn or _pick(n, wide)
        tk = tk or _pick(k, (512, 256, 128))
    else:
        tm = tm or _pick(m, (512, 256, 128))
        tn = tn or _pick(n, wide)
        tk = tk or _pick(k, wide)
    nk = k // tk
    dims = {"nn": _NN, "nt": _NT, "tn": _TN}[mode]

    def body(a_ref, b_ref, *rest):
        o_ref, acc_ref = rest[-2:]
        kk = pl.program_id(2)

        @pl.when(kk == 0)
        def _():
            acc_ref[...] = jnp.zeros_like(acc_ref) if add is None else rest[0][...].astype(F32)

        acc_ref[...] += _mm(a_ref[...], b_ref[...], dims)

        @pl.when(kk == nk - 1)
        def _():
            o_ref[...] = acc_ref[...].astype(o_ref.dtype)

    a_spec = {"nn": pl.BlockSpec((tm, tk), lambda i, j, kk: (i, kk)),
              "nt": pl.BlockSpec((tm, tk), lambda i, j, kk: (i, kk)),
              "tn": pl.BlockSpec((tk, tm), lambda i, j, kk: (kk, i))}[mode]
    b_spec = {"nn": pl.BlockSpec((tk, tn), lambda i, j, kk: (kk, j)),
              "nt": pl.BlockSpec((tn, tk), lambda i, j, kk: (j, kk)),
              "tn": pl.BlockSpec((tk, tn), lambda i, j, kk: (kk, j))}[mode]
    return pl.pallas_call(
        body, name=name, grid=(m // tm, n // tn, nk),
        in_specs=[a_spec, b_spec] + ([] if add is None else [pl.BlockSpec((tm, tn), lambda i, j, kk: (i, j))]),
        out_specs=pl.BlockSpec((tm, tn), lambda i, j, kk: (i, j)),
        out_shape=jax.ShapeDtypeStruct((m, n), out_dtype),
        scratch_shapes=[pltpu.VMEM((tm, tn), F32)],
        compiler_params=_params(("parallel", "parallel", "arbitrary")),
    )(*((a, b) if add is None else (a, b, add)))


class Rows:
    def __init__(self, s, t):
        assert s % t == 0 and t % HALO == 0
        self.s, self.t, self.n = s, t, s // t

    def tile(self, width, col=0, lead=None):
        cb = col // width
        assert col % width == 0
        if lead is None:
            return pl.BlockSpec((self.t, width), lambda i: (i, cb))
        return pl.BlockSpec((None, self.t, width), lambda i: (lead, i, cb))

    def prev(self, width, col=0, lead=None):
        cb, r = col // width, self.t // HALO
        if lead is None:
            return pl.BlockSpec((HALO, width), lambda i: (jnp.maximum(i * r - 1, 0), cb))
        return pl.BlockSpec((None, HALO, width), lambda i: (lead, jnp.maximum(i * r - 1, 0), cb))

    def next(self, width, col=0, lead=None):
        cb, r, last = col // width, self.t // HALO, self.s // HALO - 1
        if lead is None:
            return pl.BlockSpec((HALO, width), lambda i: (jnp.minimum((i + 1) * r, last), cb))
        return pl.BlockSpec((None, HALO, width), lambda i: (lead, jnp.minimum((i + 1) * r, last), cb))

    def halo(self, width, col=0, lead=None):
        return [self.prev(width, col, lead), self.tile(width, col, lead), self.next(width, col, lead)]

    @staticmethod
    def full(shape):
        nd = len(shape)
        return pl.BlockSpec(tuple(shape), lambda i: (0,) * nd)


def rows_call(name, rows, fn, ins, in_specs, row_outs, acc_outs=()):
    n_row = len(row_outs)

    def body(*refs):
        in_refs = refs[:len(ins)]
        out_refs = refs[len(ins):]
        res = fn(*[r[...] for r in in_refs])
        if not isinstance(res, (tuple, list)):
            res = (res,)
        for r, v in zip(out_refs[:n_row], res[:n_row]):
            r[...] = v.astype(r.dtype)
        if acc_outs:
            first = pl.program_id(0) == 0

            @pl.when(first)
            def _():
                for r, v in zip(out_refs[n_row:], res[n_row:]):
                    r[...] = v.astype(F32)

            @pl.when(jnp.logical_not(first))
            def _():
                for r, v in zip(out_refs[n_row:], res[n_row:]):
                    r[...] += v.astype(F32)

    out_shape = [jax.ShapeDtypeStruct((rows.s, w), dt) for (w, dt) in row_outs]
    out_specs = [rows.tile(w) for (w, dt) in row_outs]
    out_shape += [jax.ShapeDtypeStruct(tuple(sh), F32) for sh in acc_outs]
    out_specs += [Rows.full(sh) for sh in acc_outs]
    return pl.pallas_call(
        body, name=name, grid=(rows.n,), in_specs=list(in_specs), out_specs=out_specs, out_shape=out_shape,
        compiler_params=_params(("arbitrary",)),
    )(*ins)


def _edge_zero(prev, nxt, n_tiles):
    i = pl.program_id(0)
    prev = jnp.where(i == 0, jnp.zeros_like(prev), prev)
    nxt = jnp.where(i == n_tiles - 1, jnp.zeros_like(nxt), nxt)
    return prev, nxt


def _ext(prev, cur, nxt, n_tiles):
    prev, nxt = _edge_zero(prev, nxt, n_tiles)
    return jnp.concatenate([prev.astype(F32), cur.astype(F32), nxt.astype(F32)], axis=0)


def _shift(ext, off):
    n = ext.shape[0]
    t = n - 2 * HALO
    if off == 0:
        return ext[HALO:HALO + t]
    return pltpu.roll(ext, (-off) % n, 0)[HALO:HALO + t]


def _rowsum(v):
    return jnp.sum(v, axis=0, keepdims=True)


CONV_OFFS = (-2, -1, 0, 1)
SC_OFFS = (-1, 0, 1)


def conv_fwd(proj, conv_w, conv_b, rows):
    def fn(p, c, nx, w, b):
        e = _ext(p, c, nx, rows.n)
        pre = b
        for k, off in enumerate(CONV_OFFS):
            pre = pre + w[k:k + 1, :] * _shift(e, off)
        return pre, _silu(pre)

    outs = [rows_call(f"ev_conv_fwd{h}", rows, fn,
                      [proj, proj, proj, conv_w[:, h * 1024:(h + 1) * 1024], conv_b[:, h * 1024:(h + 1) * 1024]],
                      rows.halo(1024, E_XBC + h * 1024) + [Rows.full((4, 1024)), Rows.full((1, 1024))],
                      [(1024, F32), (1024, F32)]) for h in range(2)]
    return outs


def conv_bwd(proj, pre_h, du_h, conv_w, rows):
    res = []
    for h in range(2):
        def fn(*a):
            w = a[-1]
            xe = _ext(a[0], a[1], a[2], rows.n)
            pe = jnp.concatenate([a[3], a[4], a[5]], axis=0)
            g = a[6:-1]
            du = _ext(g[0], g[1], g[2], rows.n) + _ext(g[3], g[4], g[5], rows.n)
            dpre = du * _dsilu(pe)
            dx = jnp.zeros_like(a[1], dtype=F32)
            dws = []
            for k, off in enumerate(CONV_OFFS):
                dx = dx + w[k:k + 1, :] * _shift(dpre, -off)
                dws.append(_rowsum(_shift(dpre, 0) * _shift(xe, off)))
            dw = jnp.concatenate(dws + [jnp.zeros((4, dx.shape[1]), F32)], axis=0)
            return dx, dw, _rowsum(_shift(dpre, 0))

        gi = [du_h[h][0]] * 3 + [du_h[h][1]] * 3
        gs = rows.halo(1024) * 2
        res.append(rows_call(
            f"ev_conv_bwd{h}", rows, fn,
            [proj] * 3 + [pre_h[h]] * 3 + gi + [conv_w[:, h * 1024:(h + 1) * 1024]],
            rows.halo(1024, E_XBC + h * 1024) + rows.halo(1024) + gs + [Rows.full((4, 1024))],
            [(1024, MATMUL_DTYPE)], [(8, 1024), (1, 1024)]))
    dconv_w = jnp.concatenate([res[0][1][:4], res[1][1][:4]], axis=1)
    dconv_b = jnp.concatenate([res[0][2], res[1][2]], axis=1)
    return [res[0][0], res[1][0]], dconv_w, dconv_b


def _head_row(p):
    return jnp.concatenate([p.reshape(1, 2 * SSD_HEADS), jnp.zeros((1, LANES - 2 * SSD_HEADS), F32)], axis=1)


def _head_unrow(r):
    return r[:, :2 * SSD_HEADS].reshape(2, SSD_HEADS)


def _ssd_pre_fn(dtraw, bias_row, alog_row):
    q = dtraw.shape[0]
    dt = jax.nn.softplus(dtraw + bias_row)
    da = dt * (-jnp.exp(alog_row))
    li = lax.broadcasted_iota(jnp.int32, (q, q), 0)
    si = lax.broadcasted_iota(jnp.int32, (q, q), 1)
    tril = (li >= si).astype(F32)
    csf = lax.dot_general(tril, da, _NN, precision=lax.Precision.HIGHEST, preferred_element_type=F32)
    tot = jnp.sum(da, axis=0, keepdims=True)
    lane = lax.broadcasted_iota(jnp.int32, (1, LANES), 1)
    cs = jnp.where(lane < SSD_HEADS, csf, tot - csf + da)
    return dt, cs


def ssd_pre(proj, bias_row, alog_row, s):
    rows = Rows(s, SSD_CHUNK)
    return rows_call("ev_ssd_pre", rows, _ssd_pre_fn, [proj, bias_row, alog_row],
                     [rows.tile(LANES, E_DT), Rows.full((1, LANES)), Rows.full((1, LANES))],
                     [(LANES, F32), (LANES, F32)])


def ssd_pre_bwd(proj, bias_row, alog_row, ddt, dcs, s):
    rows = Rows(s, SSD_CHUNK)

    def fn(dtraw, b, al, g0, g1, c0, c1):
        _, vjp = jax.vjp(_ssd_pre_fn, dtraw, b, al)
        return vjp((g0 + g1, c0 + c1))

    return rows_call("ev_ssd_pre_bwd", rows, fn, [proj, bias_row, alog_row, ddt[0], ddt[1], dcs[0], dcs[1]],
                     [rows.tile(LANES, E_DT), Rows.full((1, LANES)), Rows.full((1, LANES))] + [rows.tile(LANES)] * 4,
                     [(LANES, MATMUL_DTYPE)], [(1, LANES), (1, LANES)])


_N_PAIR = SSD_HEADS // 2
_BC = SSD_GROUPS * SSD_STATE


def _ssd_chunk_fn(x, bc, dt, cs, h_in, dsk_row, d):
    q = x.shape[0]
    lane = lax.broadcasted_iota(jnp.int32, (1, LANES), 1)
    half = lane < SSD_HEAD_DIM
    lo, hi = half.astype(F32), 1.0 - half.astype(F32)
    cst = cs.T
    li = lax.broadcasted_iota(jnp.int32, (q, 1), 0)
    si = lax.broadcasted_iota(jnp.int32, (1, q), 1)
    mask = li >= si if d == 0 else li <= si
    end = q - 1 if d == 0 else 0
    tot = cs[end:end + 1, :]
    mine = (lane >= SSD_HEADS * d) & (lane < SSD_HEADS * (d + 1))
    e_cs, e_dec, e_tot = jnp.exp(cs), jnp.exp(jnp.where(mine, tot - cs, 0.0)), jnp.exp(tot)
    e_dd = e_dec * dt
    dtt = dt.T

    def col(v, l):
        return v[:, l:l + 1]

    def by_head(v, l):
        return jnp.where(half, col(v, l), col(v, l + 1))

    per_group = _N_PAIR // SSD_GROUPS
    ys, hs = [], []
    for g in range(SSD_GROUPS):
        bm = bc[:, g * SSD_STATE:(g + 1) * SSD_STATE]
        cm = bc[:, _BC + g * SSD_STATE:_BC + (g + 1) * SSD_STATE]
        cb = _mm(cm, bm, _NT)
        pairs = range(g * per_group, (g + 1) * per_group)
        h_g = [h_in[j * SSD_STATE:(j + 1) * SSD_STATE, :] for j in pairs]
        y_off = _mm(cm, jnp.concatenate(h_g, axis=1))
        for k, j in enumerate(pairs):
            l0 = SSD_HEADS * d + 2 * j
            xj = x[:, j * LANES:(j + 1) * LANES]
            xcat = jnp.concatenate([xj * lo, xj * hi], axis=0)
            w0 = cb * jnp.exp(jnp.where(mask, col(cs, l0) - cst[l0:l0 + 1, :], -jnp.inf)) * dtt[l0:l0 + 1, :]
            w1 = cb * jnp.exp(jnp.where(mask, col(cs, l0 + 1) - cst[l0 + 1:l0 + 2, :], -jnp.inf)) * dtt[l0 + 1:l0 + 2, :]
            y = _mm(jnp.concatenate([w0, w1], axis=1), xcat)
            st = _mm(jnp.concatenate([bm * col(e_dd, l0), bm * col(e_dd, l0 + 1)], axis=0), xcat, _TN)
            y = y + y_off[:, k * LANES:(k + 1) * LANES] * by_head(e_cs, l0) + by_head(dsk_row, l0) * xj
            ys.append(y)
            hs.append(h_g[k] * by_head(e_tot, l0) + st)
    return jnp.concatenate(ys, axis=1), jnp.concatenate(hs, axis=0)


def _chunk_of(d, ci, nc, backward):
    up = ci if not backward else nc - 1 - ci
    return up + d * (nc - 1 - 2 * up)


_ST_ROWS = _N_PAIR * SSD_STATE


def _ssd_fwd_dir(u_h, dt, cs, dsk_row, s, d):
    nc = s // SSD_CHUNK
    q = SSD_CHUNK

    def body(x_ref, bc_ref, dt_ref, cs_ref, dsk_ref, y_ref, hs_ref, st_ref):
        @pl.when(pl.program_id(0) == 0)
        def _():
            st_ref[...] = jnp.zeros(st_ref.shape, F32)

        h_in = st_ref[...]
        y, h_out = _ssd_chunk_fn(x_ref[...], bc_ref[...], dt_ref[...], cs_ref[...], h_in, dsk_ref[...], d)
        y_ref[...] = y
        hs_ref[...] = h_in
        st_ref[...] = h_out

    ch = lambda ci: _chunk_of(d, ci, nc, False)
    return pl.pallas_call(
        body, name=f"ev_ssd_fwd{d}", grid=(nc,),
        in_specs=[pl.BlockSpec((q, SSD_INNER), lambda ci: (ch(ci), 0)),
                  pl.BlockSpec((q, 2 * _BC), lambda ci: (ch(ci), 0)),
                  pl.BlockSpec((q, LANES), lambda ci: (ch(ci), 0)),
                  pl.BlockSpec((q, LANES), lambda ci: (ch(ci), 0)),
                  pl.BlockSpec((1, LANES), lambda ci: (0, 0))],
        out_specs=[pl.BlockSpec((q, SSD_INNER), lambda ci: (ch(ci), 0)),
                   pl.BlockSpec((None, _ST_ROWS, LANES), lambda ci: (ch(ci), 0, 0))],
        out_shape=[jax.ShapeDtypeStruct((s, SSD_INNER), F32), jax.ShapeDtypeStruct((nc, _ST_ROWS, LANES), F32)],
        scratch_shapes=[pltpu.VMEM((_ST_ROWS, LANES), F32)],
        compiler_params=_params(("arbitrary",)),
    )(u_h[0], u_h[1], dt, cs, dsk_row)


def ssd_fwd(u_h, dt, cs, dsk_row, s):
    ys, hss = zip(*[_ssd_fwd_dir(u_h, dt, cs, dsk_row, s, d) for d in range(2)])
    return ys, hss


def _ssd_bwd_dir(u_h, dt, cs, dsk_row, hs, dy, s, d):
    nc = s // SSD_CHUNK
    q = SSD_CHUNK

    def body(x_ref, bc_ref, dt_ref, cs_ref, dsk_ref, hs_ref, dy_ref,
             dx_ref, dbc_ref, ddt_ref, dcs_ref, ddsk_ref, dst_ref):
        ci = pl.program_id(0)

        @pl.when(ci == 0)
        def _():
            dst_ref[...] = jnp.zeros(dst_ref.shape, F32)

        f = functools.partial(_ssd_chunk_fn, d=d)
        _, vjp = jax.vjp(f, x_ref[...], bc_ref[...], dt_ref[...], cs_ref[...], hs_ref[...], dsk_ref[...])
        dx, dbc, ddt, dcs, dh, ddsk = vjp((dy_ref[...], dst_ref[...]))
        dx_ref[...] = dx
        dbc_ref[...] = dbc
        ddt_ref[...] = ddt
        dcs_ref[...] = dcs
        dst_ref[...] = dh

        @pl.when(ci == 0)
        def _():
            ddsk_ref[...] = ddsk

        @pl.when(ci != 0)
        def _():
            ddsk_ref[...] += ddsk

    ch = lambda ci: _chunk_of(d, ci, nc, True)
    row_blk = lambda w: pl.BlockSpec((q, w), lambda ci: (ch(ci), 0))
    return pl.pallas_call(
        body, name=f"ev_ssd_bwd{d}", grid=(nc,),
        in_specs=[row_blk(SSD_INNER), row_blk(2 * _BC), row_blk(LANES), row_blk(LANES),
                  pl.BlockSpec((1, LANES), lambda ci: (0, 0)),
                  pl.BlockSpec((None, _ST_ROWS, LANES), lambda ci: (ch(ci), 0, 0)), row_blk(SSD_INNER)],
        out_specs=[row_blk(SSD_INNER), row_blk(2 * _BC), row_blk(LANES), row_blk(LANES),
                   pl.BlockSpec((1, LANES), lambda ci: (0, 0))],
        out_shape=[jax.ShapeDtypeStruct((s, SSD_INNER), F32), jax.ShapeDtypeStruct((s, 2 * _BC), F32),
                   jax.ShapeDtypeStruct((s, LANES), F32), jax.ShapeDtypeStruct((s, LANES), F32),
                   jax.ShapeDtypeStruct((1, LANES), F32)],
        scratch_shapes=[pltpu.VMEM((_ST_ROWS, LANES), F32)],
        compiler_params=_params(("arbitrary",)),
    )(u_h[0], u_h[1], dt, cs, dsk_row, hs, dy)


def ssd_bwd(u_h, dt, cs, dsk_row, hs, dy, s):
    return zip(*[_ssd_bwd_dir(u_h, dt, cs, dsk_row, hs[d], dy, s, d) for d in range(2)])


def _gated_rms(ys, z, g):
    t1 = ys * _silu(z)
    return t1 * lax.rsqrt(jnp.mean(t1 * t1, axis=-1, keepdims=True) + EPS) * g


def even_mix_fwd(proj, y2, norm_g, sc_w, rows):
    def fn(yf, yb, z, bg, cgp, cg, cgn, hp, hh, hn, gate, g, w):
        ya = _gated_rms(yf + yb, z, g)
        me = _ext(cgp, cg, cgn, rows.n) * _ext(hp, hh, hn, rows.n)
        cm = sum(w[k:k + 1, :] * _shift(me, off) for k, off in enumerate(SC_OFFS))
        return jnp.concatenate([ya, bg * cm * _silu(gate)], axis=1)

    w = 1024
    return rows_call("ev_mix_fwd", rows, fn,
                     [y2[0], y2[1], proj, proj] + [proj] * 6 + [proj, norm_g, sc_w],
                     [rows.tile(w), rows.tile(w), rows.tile(w, E_Z), rows.tile(w, E_BG)]
                     + rows.halo(w, E_CG) + rows.halo(w, E_H)
                     + [rows.tile(w, E_GATE), Rows.full((1, w)), Rows.full((3, w))],
                     [(2 * w, MATMUL_DTYPE)])[0]


def even_mix_bwd(proj, y2, norm_g, sc_w, dyab, rows):
    w = 1024

    def fn(yf, yb, z, g, sw, dya, *a):
        (dybp, dyb, dybn, bgp, bg, bgn, gtp, gt, gtn, cgp, cg, cgn, hp, hh, hn) = a
        _, vjp = jax.vjp(_gated_rms, yf + yb, z, g)
        dys, dz, dg = vjp(dya.astype(F32))
        n = rows.n
        dye, bge, gte = _ext(dybp, dyb, dybn, n), _ext(bgp, bg, bgn, n), _ext(gtp, gt, gtn, n)
        cge, he = _ext(cgp, cg, cgn, n), _ext(hp, hh, hn, n)
        me = cge * he
        cm = sum(sw[k:k + 1, :] * _shift(me, off) for k, off in enumerate(SC_OFFS))
        dyc = dyb.astype(F32)
        dbg = dyc * cm * _silu(gt)
        dgate = dyc * bg * cm * _dsilu(gt)
        dcme = dye * bge * _silu(gte)
        dm = sum(sw[k:k + 1, :] * _shift(dcme, -off) for k, off in enumerate(SC_OFFS))
        dcm = _shift(dcme, 0)
        dws = [_rowsum(dcm * _shift(me, off)) for off in SC_OFFS]
        dsw = jnp.concatenate(dws + [jnp.zeros((5, w), F32)], axis=0)
        return dys, dz, dbg, dm * hh, dm * cg, dgate, _rowsum(dg), dsw

    return rows_call(
        "ev_mix_bwd", rows, fn,
        [y2[0], y2[1], proj, norm_g, sc_w, dyab] + [dyab] * 3 + [proj] * 12,
        [rows.tile(w), rows.tile(w), rows.tile(w, E_Z), Rows.full((1, w)), Rows.full((3, w)),
         rows.tile(w, 0)] + rows.halo(w, w) + rows.halo(w, E_BG) + rows.halo(w, E_GATE)
        + rows.halo(w, E_CG) + rows.halo(w, E_H),
        [(w, F32)] + [(w, MATMUL_DTYPE)] * 5, [(1, w), (8, w)])


def _res_ln(x, h, g, b):
    v = ALPHA * x + h
    mu = jnp.mean(v, axis=-1, keepdims=True)
    var = jnp.mean(jnp.square(v - mu), axis=-1, keepdims=True)
    return (v - mu) * lax.rsqrt(var + EPS) * g + b


def res_ln_fwd(x, h, g, b, rows, name):
    return rows_call(name, rows, _res_ln, [x, h, g, b],
                     [rows.tile(D_MODEL), rows.tile(D_MODEL), Rows.full((1, D_MODEL)), Rows.full((1, D_MODEL))],
                     [(D_MODEL, F32)])[0]


def res_ln_bwd(x, h, g, b, dy, rows, name):
    def fn(x_, h_, g_, b_, dy_):
        _, vjp = jax.vjp(_res_ln, x_, h_, g_, b_)
        dx, dh, dg, db = vjp(dy_)
        return dx, dh, dg, db

    return rows_call(name, rows, fn, [x, h, g, b, dy],
                     [rows.tile(D_MODEL), rows.tile(D_MODEL), Rows.full((1, D_MODEL)), Rows.full((1, D_MODEL)),
                      rows.tile(D_MODEL)],
                     [(D_MODEL, F32), (D_MODEL, F32)], [(1, D_MODEL), (1, D_MODEL)])


def final_ln_loss(x, h, g, b, target, rows):
    def fn(x_, h_, g_, b_, t_):
        y, vjp = jax.vjp(_res_ln, x_, h_, g_, b_)
        err = y - t_
        dx, dh, dg, db = vjp(err * (1.0 / D_MODEL))
        return dx, dh, dg, db, _rowsum(jnp.square(err)) * (0.5 / D_MODEL)

    return rows_call("od_ln_loss", rows, fn, [x, h, g, b, target],
                     [rows.tile(D_MODEL), rows.tile(D_MODEL), Rows.full((1, D_MODEL)), Rows.full((1, D_MODEL)),
                      rows.tile(D_MODEL)],
                     [(D_MODEL, F32), (D_MODEL, F32)], [(1, D_MODEL), (1, D_MODEL), (1, D_MODEL)])


def pad_even_w_in(w):
    return jnp.concatenate([w[:, :3072], w[:, 3104:], w[:, 3072:3104],
                            jnp.zeros((w.shape[0], EVEN_P - EVEN_PROJ), w.dtype)], axis=1)


def matmul_pieces_nt(pieces, w, name, add):
    m, n, npc = pieces[0].shape[0], w.shape[0], len(pieces)
    widths = [p.shape[1] for p in pieces]
    assert sum(widths) == w.shape[1]
    tm = _pick(m, (256, 128))

    def body(*refs):
        p_refs = refs[:npc]
        w_ref, add_ref, o_ref = refs[npc:]
        acc, off = add_ref[...].astype(F32), 0
        for p_ref, wd in zip(p_refs, widths):
            acc = acc + _mm(p_ref[...], w_ref[:, off:off + wd], _NT)
            off += wd
        o_ref[...] = acc

    return pl.pallas_call(
        body, name=name, grid=(m // tm,),
        in_specs=[pl.BlockSpec((tm, wd), lambda i: (i, 0)) for wd in widths]
        + [pl.BlockSpec(w.shape, lambda i: (0, 0), pipeline_mode=pl.Buffered(1)),
           pl.BlockSpec((tm, n), lambda i: (i, 0))],
        out_specs=pl.BlockSpec((tm, n), lambda i: (i, 0)),
        out_shape=jax.ShapeDtypeStruct((m, n), F32),
        compiler_params=_params(("parallel",)),
    )(*pieces, w, add)


def even_layer(x, w, rows):
    s = rows.s
    xb = x.astype(MATMUL_DTYPE)
    proj = matmul(xb, w["w_in_p"], "nn", F32, "ev_proj")
    (pre0, u0), (pre1, u1) = conv_fwd(proj, w["conv_w"], w["conv_b"], rows)
    dt, cs = ssd_pre(proj, w["bias_row"], w["alog_row"], s)
    y2, hs = ssd_fwd((u0, u1), dt, cs, w["dsk_row"], s)
    yab = even_mix_fwd(proj, y2, w["norm_g"], w["sc_w"], rows)
    h = matmul(yab, w["w_out"], "nn", F32, "ev_out")
    x1 = res_ln_fwd(x, h, w["ln_g"], w["ln_b"], rows, "ev_ln")
    return x1, dict(x=x, xb=xb, proj=proj, pre=(pre0, pre1), u=(u0, u1), dt=dt, cs=cs, y2=y2, hs=hs, yab=yab,
                    h=h)


def even_layer_bwd(dx1, w, sv, rows):
    s = rows.s
    dres, dh, dln_g, dln_b = res_ln_bwd(sv["x"], sv["h"], w["ln_g"], w["ln_b"], dx1, rows, "ev_ln_bwd")
    dyab = matmul(dh, w["w_out"], "nt", F32, "ev_out_dx")
    dw_out = matmul(sv["yab"], dh, "tn", F32, "ev_out_dw")
    dys, dz, dbg, dcg, dhh, dgate, dnorm_g, dsw = even_mix_bwd(sv["proj"], sv["y2"], w["norm_g"], w["sc_w"], dyab,
                                                               rows)
    dxs, dbc, ddt, dcs, ddsk = ssd_bwd(sv["u"], sv["dt"], sv["cs"], w["dsk_row"], sv["hs"], dys, s)
    ddtraw, dbias_row, dalog_row = ssd_pre_bwd(sv["proj"], w["bias_row"], w["alog_row"], ddt, dcs, s)
    (dxbc0, dxbc1), dconv_w, dconv_b = conv_bwd(sv["proj"], sv["pre"], (dxs, dbc), w["conv_w"], rows)
    pieces = [dz, dxbc0, dxbc1, dbg, dcg, dhh, dgate]
    dx0 = matmul_pieces_nt(pieces + [ddtraw], w["w_in_p"], "ev_proj_dx", add=dres)
    dws = [matmul(p, sv["xb"], "tn", F32, f"ev_proj_dw{i}") for i, p in enumerate(pieces)]
    dw_dt = matmul(ddtraw, sv["xb"], "tn", F32, "ev_proj_dw_dt")
    dw_in = jnp.concatenate(dws[:3] + [dw_dt[:2 * SSD_HEADS]] + dws[3:], axis=0)
    g = dict(ev_w_in=dw_in, ev_conv_w=dconv_w, ev_conv_b=dconv_b,
             ev_a_log=_head_unrow(dalog_row), ev_dt_bias=_head_unrow(dbias_row),
             ev_d_skip=_head_unrow(ddsk[0] + ddsk[1]), ev_norm_g=dnorm_g, ev_sc_conv_w=dsw[:3],
             ev_w_out=dw_out, ev_ln_g=dln_g, ev_ln_b=dln_b)
    return dx0, g


def even_weights(ev_w_in, ev_conv_w, ev_conv_b, ev_a_log, ev_dt_bias, ev_d_skip, ev_norm_g, ev_sc_conv_w,
                 ev_w_out, ev_ln_g, ev_ln_b):
    return dict(w_in_p=pad_even_w_in(ev_w_in).astype(MATMUL_DTYPE), conv_w=ev_conv_w, conv_b=ev_conv_b,
                alog_row=_head_row(ev_a_log), bias_row=_head_row(ev_dt_bias), dsk_row=_head_row(ev_d_skip),
                norm_g=ev_norm_g, sc_w=ev_sc_conv_w, w_out=ev_w_out.astype(MATMUL_DTYPE), ln_g=ev_ln_g,
                ln_b=ev_ln_b)


HEAD_BLK = LANES
ROPE_LO = MLA_NOPE
ROPE_HALF = MLA_ROPE // 2


def _rms(v, g):
    return v * lax.rsqrt(jnp.mean(v * v, axis=-1, keepdims=True) + EPS) * g


def latent_norm_fwd(proj, gq, gkv, rows):
    def fn(cq, ckv, gq_, gkv_):
        return _rms(cq, gq_), _rms(ckv, gkv_)

    return rows_call("od_norm_fwd", rows, fn, [proj, proj, gq, gkv],
                     [rows.tile(MLA_Q_RANK, O_CQ), rows.tile(MLA_KV_RANK, O_CKV), Rows.full((1, MLA_Q_RANK)),
                      Rows.full((1, MLA_KV_RANK))],
                     [(MLA_Q_RANK, MATMUL_DTYPE), (MLA_KV_RANK, MATMUL_DTYPE)])


def latent_norm_bwd(proj, gq, gkv, dcqn, dckvn, rows):
    def fn(cq, ckv, gq_, gkv_, d1, d2):
        _, vjp = jax.vjp(_rms, cq, gq_)
        dcq, dgq = vjp(d1)
        _, vjp2 = jax.vjp(_rms, ckv, gkv_)
        dckv, dgkv = vjp2(d2)
        return dcq, dckv, dgq, dgkv

    return rows_call("od_norm_bwd", rows, fn, [proj, proj, gq, gkv, dcqn, dckvn],
                     [rows.tile(MLA_Q_RANK, O_CQ), rows.tile(MLA_KV_RANK, O_CKV), Rows.full((1, MLA_Q_RANK)),
                      Rows.full((1, MLA_KV_RANK)), rows.tile(MLA_Q_RANK), rows.tile(MLA_KV_RANK)],
                     [(MLA_Q_RANK, F32), (MLA_KV_RANK, F32)], [(1, MLA_Q_RANK), (1, MLA_KV_RANK)])


def rope_rows():
    lane = np.arange(LANES)
    inv = ROPE_THETA ** (-jnp.arange(ROPE_HALF, dtype=F32) / ROPE_HALF)
    on = (lane >= ROPE_LO) & (lane < ROPE_LO + MLA_ROPE)
    freq = jnp.where(on, inv[(lane - ROPE_LO) % ROPE_HALF], 0.0).reshape(1, LANES).astype(F32)
    sign = np.where(on, np.where(lane < ROPE_LO + ROPE_HALF, -1.0, 1.0), 0.0).reshape(1, LANES).astype(np.float32)
    return freq, jnp.asarray(sign)


def _rot_tables(pos, freq, sign):
    ang = pos.astype(F32) * freq
    return jnp.cos(ang), jnp.sin(ang) * sign


def _swap_halves(v):
    lane = lax.broadcasted_iota(jnp.int32, (1, LANES), 1)
    return jnp.where(lane < ROPE_LO + ROPE_HALF, pltpu.roll(v, LANES - ROPE_HALF, 1), pltpu.roll(v, ROPE_HALF, 1))


def rope_fwd(qp, kvp, proj, pos, freq, sign, rows):
    def fn(q, k, kr, v, p, f, sg):
        c, sn = _rot_tables(p, f, sg)
        rk = kr * c + _swap_halves(kr) * sn
        one = (lax.broadcasted_iota(jnp.int32, (v.shape[0], HEAD_BLK - MLA_V), 1) == 0).astype(F32)
        qs, ks, vs = [], [], []
        for h in range(MLA_HEADS):
            qh = q[:, h * HEAD_BLK:(h + 1) * HEAD_BLK]
            qs.append((qh * c + _swap_halves(qh) * sn) * QSCALE)
            ks.append(k[:, h * HEAD_BLK:(h + 1) * HEAD_BLK] + rk)
            vs += [v[:, h * MLA_V:(h + 1) * MLA_V], one]
        return jnp.concatenate(qs, axis=1), jnp.concatenate(ks, axis=1), jnp.concatenate(vs, axis=1)

    w = MLA_HEADS * HEAD_BLK
    return rows_call("od_rope_fwd", rows, fn, [qp, kvp, proj, kvp, pos, freq, sign],
                     [rows.tile(w), rows.tile(w, 0), rows.tile(LANES, O_KR), rows.tile(MLA_WIDTH, w),
                      rows.tile(1), Rows.full((1, LANES)), Rows.full((1, LANES))],
                     [(w, MATMUL_DTYPE), (w, MATMUL_DTYPE), (w, MATMUL_DTYPE)])


def rope_bwd(dq, dk, dv, pos, freq, sign, rows):
    def fn(dq_, dk_, dv_, p, f, sg):
        c, sn = _rot_tables(p, f, sg)
        on = jnp.abs(sg)
        outs, dks, dvs, dkr = [], [], [], jnp.zeros((dq_.shape[0], LANES), F32)
        for h in range(MLA_HEADS):
            g = dq_[:, h * HEAD_BLK:(h + 1) * HEAD_BLK] * ATTN_SCALE
            outs.append(g * c + _swap_halves(g * sn) * on)
            gk = dk_[:, h * HEAD_BLK:(h + 1) * HEAD_BLK] * LN2
            dks.append(gk)
            dkr = dkr + gk * c + _swap_halves(gk * sn) * on
            dvs.append(dv_[:, h * HEAD_BLK:h * HEAD_BLK + MLA_V])
        return jnp.concatenate(outs, axis=1), jnp.concatenate(dks + dvs, axis=1), dkr

    w = MLA_HEADS * HEAD_BLK
    return rows_call("od_rope_bwd", rows, fn, [dq, dk, dv, pos, freq, sign],
                     [rows.tile(w), rows.tile(w), rows.tile(w), rows.tile(1), Rows.full((1, LANES)),
                      Rows.full((1, LANES))],
                     [(w, MATMUL_DTYPE), (w + MLA_WIDTH, MATMUL_DTYPE), (LANES, F32)])


_PAIRS = MLA_HEADS // 2
ATT_TQ = 512
ATT_TK = 8192
ATT_BWD_TQ = 512
ATT_BWD_TK = 2048


def _att_tiles(s, backward=False):
    if backward:
        return min(ATT_BWD_TQ, s), min(ATT_BWD_TK, s)
    return min(ATT_TQ, s), min(ATT_TK, s)


def attention_fwd(qcat, kcat, vcat, s):
    tq, tk = _att_tiles(s)
    nq, nk = s // tq, s // tk

    def body(q_ref, k_ref, v_ref, o_ref, lse_ref, m_ref, acc_ref):
        kk = pl.program_id(2)

        @pl.when(kk == 0)
        def _():
            m_ref[...] = jnp.full(m_ref.shape, -jnp.inf, F32)
            acc_ref[...] = jnp.zeros(acc_ref.shape, F32)

        sl = [slice(hh * HEAD_BLK, (hh + 1) * HEAD_BLK) for hh in range(2)]
        sc = [_mm(q_ref[:, sl[hh]], k_ref[:, sl[hh]], _NT) for hh in range(2)]
        for hh in range(2):
            m_prev = m_ref[hh]
            m_new = jnp.maximum(m_prev, jnp.max(sc[hh], axis=1, keepdims=True))
            p = jnp.exp2(sc[hh] - m_new[:, :1])
            acc_ref[hh] = acc_ref[hh] * jnp.exp2(m_prev - m_new) + _mm(p, v_ref[:, sl[hh]])
            m_ref[hh] = m_new

        @pl.when(kk == nk - 1)
        def _():
            half = lax.broadcasted_iota(jnp.int32, (1, LANES), 1) < MLA_V
            l0, l1 = acc_ref[0][:, MLA_V:MLA_V + 1], acc_ref[1][:, MLA_V:MLA_V + 1]
            o_ref[...] = jnp.where(half, acc_ref[0] / l0, pltpu.roll(acc_ref[1] / l1, MLA_V, 1))
            lse_ref[...] = jnp.where(half, m_ref[0] + jnp.log2(l0), m_ref[1] + jnp.log2(l1))

    return pl.pallas_call(
        body, name="od_attn_fwd", grid=(_PAIRS, nq, nk),
        in_specs=[pl.BlockSpec((tq, 2 * HEAD_BLK), lambda p, i, kk: (i, p)),
                  pl.BlockSpec((tk, 2 * HEAD_BLK), lambda p, i, kk: (kk, p)),
                  pl.BlockSpec((tk, 2 * HEAD_BLK), lambda p, i, kk: (kk, p))],
        out_specs=[pl.BlockSpec((tq, LANES), lambda p, i, kk: (i, p)),
                   pl.BlockSpec((None, tq, LANES), lambda p, i, kk: (p, i, 0))],
        out_shape=[jax.ShapeDtypeStruct((s, MLA_WIDTH), F32), jax.ShapeDtypeStruct((_PAIRS, s, LANES), F32)],
        scratch_shapes=[pltpu.VMEM((2, tq, LANES), F32)] * 2,
        compiler_params=_params(("parallel", "parallel", "arbitrary")),
    )(qcat, kcat, vcat)


def attention_bwd(qcat, kcat, vcat, o, lse, do, s):
    tq, tk = _att_tiles(s, backward=True)
    nq, nk = s // tq, s // tk

    def body(q_ref, k_ref, v_ref, do_ref, o_ref, lse_ref, dq_ref, dk_ref, dv_ref):
        kk, i = pl.program_id(1), pl.program_id(2)
        lane = lax.broadcasted_iota(jnp.int32, (1, LANES), 1)
        half = lane < MLA_V
        do_p, lse = do_ref[...], lse_ref[...]
        prod = do_p * o_ref[...]
        rows_i = pl.ds(pl.multiple_of(i * tq, tq), tq)
        for hh in range(2):
            sl = slice(hh * HEAD_BLK, (hh + 1) * HEAD_BLK)
            mine = half if hh == 0 else jnp.logical_not(half)
            delta = jnp.sum(jnp.where(mine, prod, 0.0), axis=1, keepdims=True)
            do_h = jnp.where(half, do_p if hh == 0 else pltpu.roll(do_p, MLA_V, 1), 0.0)
            p = jnp.exp2(_mm(q_ref[:, sl], k_ref[:, sl], _NT) - lse[:, hh * MLA_V:hh * MLA_V + 1])
            ds = p * (_mm(do_h, v_ref[:, sl], _NT) - delta)
            dv_h, dk_h, dq_h = _mm(p, do_h, _TN), _mm(ds, q_ref[:, sl], _TN), _mm(ds, k_ref[:, sl])

            @pl.when(i == 0)
            def _():
                dv_ref[:, sl] = dv_h
                dk_ref[:, sl] = dk_h

            @pl.when(i != 0)
            def _():
                dv_ref[:, sl] += dv_h
                dk_ref[:, sl] += dk_h

            @pl.when(kk == 0)
            def _():
                dq_ref[rows_i, sl] = dq_h

            @pl.when(kk != 0)
            def _():
                dq_ref[rows_i, sl] += dq_h

    w = MLA_HEADS * HEAD_BLK
    return pl.pallas_call(
        body, name="od_attn_bwd", grid=(_PAIRS, nk, nq),
        in_specs=[pl.BlockSpec((tq, 2 * HEAD_BLK), lambda p, kk, i: (i, p)),
                  pl.BlockSpec((tk, 2 * HEAD_BLK), lambda p, kk, i: (kk, p)),
                  pl.BlockSpec((tk, 2 * HEAD_BLK), lambda p, kk, i: (kk, p)),
                  pl.BlockSpec((tq, LANES), lambda p, kk, i: (i, p)),
                  pl.BlockSpec((tq, LANES), lambda p, kk, i: (i, p)),
                  pl.BlockSpec((None, tq, LANES), lambda p, kk, i: (p, i, 0))],
        out_specs=[pl.BlockSpec((s, 2 * HEAD_BLK), lambda p, kk, i: (0, p)),
                   pl.BlockSpec((tk, 2 * HEAD_BLK), lambda p, kk, i: (kk, p)),
                   pl.BlockSpec((tk, 2 * HEAD_BLK), lambda p, kk, i: (kk, p))],
        out_shape=[jax.ShapeDtypeStruct((s, w), F32)] * 3,
        compiler_params=_params(("parallel", "arbitrary", "arbitrary")),
    )(qcat, kcat, vcat, do, o, lse)


def _pool_counts(n_rows, first_row, s, w):
    pos = first_row + lax.broadcasted_iota(jnp.int32, (n_rows, 1), 0)
    lo = jnp.clip(pos - w // 2, 0, s)
    hi = jnp.clip(pos + w - w // 2, 0, s)
    return jnp.maximum(hi - lo, 1).astype(F32)


def _window_sum(e, levels, mirrored):
    n = e.shape[0]
    acc = e + pltpu.roll(e, (n - 1) if mirrored else 1, 0)
    step = 1
    for _ in range(levels - 1):
        acc = pltpu.roll(acc, step, 0) + pltpu.roll(acc, n - step, 0)
        step *= 2
    return acc


def _pooled(ue, s, t):
    first = pl.program_id(0) * t
    outs = []
    for gi, w in enumerate(POOL_WINDOWS):
        eg = ue[:, gi * POOL_GROUP:(gi + 1) * POOL_GROUP]
        sm = _window_sum(eg, gi + 1, False)[HALO:HALO + t]
        outs.append(sm / _pool_counts(t, first, s, w) - eg[HALO:HALO + t])
    return outs


def odd_mix_fwd(proj, o, pool_w, pool_scale, rows):
    def fn(o_, gc, up, u, un, gd, pw, ps):
        pooled = _pooled(_ext(up, u, un, rows.n), rows.s, rows.t)
        lin = jnp.concatenate([_mm(pooled[g], pw[g]) for g in range(len(POOL_WINDOWS))], axis=1)
        return jnp.concatenate([o_ * _silu(gc), lin * ps * _silu(gd)], axis=1)

    w = POOL_WIDTH
    return rows_call("od_mix_fwd", rows, fn, [o, proj, proj, proj, proj, proj, pool_w, pool_scale],
                     [rows.tile(w), rows.tile(w, O_GC)] + rows.halo(w, O_UD)
                     + [rows.tile(w, O_GD), Rows.full((4, POOL_GROUP, POOL_GROUP)), Rows.full((1, w))],
                     [(2 * w, MATMUL_DTYPE)])[0]


def odd_mix_bwd(proj, o, pool_w, pool_scale, dycd, rows):
    w = POOL_WIDTH
    ng = len(POOL_WINDOWS)

    def fn(o_, gc, up, u, un, gdp, gd, gdn, pw, ps, dyc, dydp, dyd, dydn):
        n, t, s = rows.n, rows.t, rows.s
        dyc = dyc.astype(F32)
        do = dyc * _silu(gc)
        dgc = dyc * o_ * _dsilu(gc)
        pooled = _pooled(_ext(up, u, un, n), s, t)
        lin = jnp.concatenate([_mm(pooled[g], pw[g]) for g in range(ng)], axis=1)
        dydc = dyd.astype(F32)
        dgd = dydc * lin * ps * _dsilu(gd)
        dps = _rowsum(dydc * lin * _silu(gd))
        dlin_e = _ext(dydp, dyd, dydn, n) * ps * _silu(_ext(gdp, gd, gdn, n))
        first = pl.program_id(0) * t - HALO
        dus, dpws = [], []
        for g, win in enumerate(POOL_WINDOWS):
            sl = slice(g * POOL_GROUP, (g + 1) * POOL_GROUP)
            dle = dlin_e[:, sl]
            dpws.append(_mm(pooled[g], dle[HALO:HALO + t], _TN))
            dpe = _mm(dle, pw[g], _NT)
            gce = dpe / _pool_counts(t + 2 * HALO, first, s, win)
            dus.append(_window_sum(gce, g + 1, True)[HALO:HALO + t] - dpe[HALO:HALO + t])
        return do, dgc, jnp.concatenate(dus, axis=1), dgd, jnp.stack(dpws), dps

    return rows_call("od_mix_bwd", rows, fn,
                     [o, proj, proj, proj, proj, proj, proj, proj, pool_w, pool_scale, dycd, dycd, dycd, dycd],
                     [rows.tile(w), rows.tile(w, O_GC)] + rows.halo(w, O_UD) + rows.halo(w, O_GD)
                     + [Rows.full((ng, POOL_GROUP, POOL_GROUP)), Rows.full((1, w)), rows.tile(w, 0)]
                     + rows.halo(w, w),
                     [(w, F32)] * 4, [(ng, POOL_GROUP, POOL_GROUP), (1, w)])


def pad_odd_w_in(w):
    z = lambda n: jnp.zeros((w.shape[0], n), w.dtype)
    return jnp.concatenate([w[:, :384], z(ROPE_LO), w[:, 384:416], z(LANES - ROPE_LO - MLA_ROPE), w[:, 416:]], axis=1)


def unpad_odd_w_in_t(wpt):
    return jnp.concatenate([wpt[:384], wpt[O_KR + ROPE_LO:O_KR + ROPE_LO + MLA_ROPE], wpt[O_GC:]], axis=0)


def pad_w_uq(w):
    w3 = w.reshape(MLA_Q_RANK, MLA_HEADS, MLA_NOPE + MLA_ROPE)
    w3 = jnp.pad(w3, ((0, 0), (0, 0), (0, HEAD_BLK - MLA_NOPE - MLA_ROPE)))
    return w3.reshape(MLA_Q_RANK, MLA_HEADS * HEAD_BLK)


def unpad_w_uq(wp):
    return wp.reshape(MLA_Q_RANK, MLA_HEADS, HEAD_BLK)[..., :MLA_NOPE + MLA_ROPE].reshape(MLA_Q_RANK, -1)


def pad_w_ukv(w):
    w3 = w.reshape(MLA_KV_RANK, MLA_HEADS, MLA_NOPE + MLA_V)
    kp = jnp.pad(w3[..., :MLA_NOPE], ((0, 0), (0, 0), (0, HEAD_BLK - MLA_NOPE)))
    return jnp.concatenate([kp.reshape(MLA_KV_RANK, -1), w3[..., MLA_NOPE:].reshape(MLA_KV_RANK, -1)], axis=1)


def unpad_w_ukv(wp):
    kp = wp[:, :MLA_HEADS * HEAD_BLK].reshape(MLA_KV_RANK, MLA_HEADS, HEAD_BLK)[..., :MLA_NOPE]
    vp = wp[:, MLA_HEADS * HEAD_BLK:].reshape(MLA_KV_RANK, MLA_HEADS, MLA_V)
    return jnp.concatenate([kp, vp], axis=-1).reshape(MLA_KV_RANK, -1)


def odd_weights(od_w_in, od_q_norm_g, od_w_uq, od_kv_norm_g, od_w_ukv, od_pool_w, od_pool_scale, od_w_out,
                od_ln_g, od_ln_b):
    freq, sign = rope_rows()
    return dict(w_in_p=pad_odd_w_in(od_w_in).astype(MATMUL_DTYPE), gq=od_q_norm_g, gkv=od_kv_norm_g,
                w_uq_p=pad_w_uq(od_w_uq).astype(MATMUL_DTYPE), w_ukv_p=pad_w_ukv(od_w_ukv).astype(MATMUL_DTYPE),
                pool_w=od_pool_w, pool_scale=od_pool_scale, w_out=od_w_out.astype(MATMUL_DTYPE), ln_g=od_ln_g,
                ln_b=od_ln_b, freq=freq, sign=sign)


def odd_layer_loss(x, pos, target, w, rows):
    s = rows.s
    proj = matmul(x, w["w_in_p"], "nn", F32, "od_proj")
    cqn, ckvn = latent_norm_fwd(proj, w["gq"], w["gkv"], rows)
    qp = matmul(cqn, w["w_uq_p"], "nn", F32, "od_q_up")
    kvp = matmul(ckvn, w["w_ukv_p"], "nn", F32, "od_kv_up")
    qcat, kcat, v = rope_fwd(qp, kvp, proj, pos, w["freq"], w["sign"], rows)
    o, lse = attention_fwd(qcat, kcat, v, s)
    ycd = odd_mix_fwd(proj, o, w["pool_w"], w["pool_scale"], rows)
    h = matmul(ycd, w["w_out"], "nn", F32, "od_out")
    dres, dh, dln_g, dln_b, loss_lanes = final_ln_loss(x, h, w["ln_g"], w["ln_b"], target, rows)
    dycd = matmul(dh, w["w_out"], "nt", F32, "od_out_dx")
    dw_out = matmul(ycd, dh, "tn", F32, "od_out_dw")
    do, dgc, dud, dgd, dpool_w, dpool_scale = odd_mix_bwd(proj, o, w["pool_w"], w["pool_scale"], dycd, rows)
    dq, dk, dv = attention_bwd(qcat, kcat, v, o, lse, do, s)
    dqp, dkvp, dkr = rope_bwd(dq, dk, dv, pos, w["freq"], w["sign"], rows)
    dcqn = matmul(dqp, w["w_uq_p"], "nt", F32, "od_q_up_dx")
    dw_uq = unpad_w_uq(matmul(cqn, dqp, "tn", F32, "od_q_up_dw"))
    dckvn = matmul(dkvp, w["w_ukv_p"], "nt", F32, "od_kv_up_dx")
    dw_ukv = unpad_w_ukv(matmul(ckvn, dkvp, "tn", F32, "od_kv_up_dw"))
    dcq, dckv, dgq, dgkv = latent_norm_bwd(proj, w["gq"], w["gkv"], dcqn, dckvn, rows)
    dproj = jnp.concatenate([dcq, dckv, dkr, dgc, dud, dgd], axis=1).astype(MATMUL_DTYPE)
    dx = matmul(dproj, w["w_in_p"], "nt", F32, "od_proj_dx", add=dres)
    dw_in = unpad_odd_w_in_t(matmul(dproj, x, "tn", F32, "od_proj_dw"))
    g = dict(od_w_in=dw_in, od_q_norm_g=dgq, od_w_uq=dw_uq, od_kv_norm_g=dgkv, od_w_ukv=dw_ukv,
             od_pool_w=dpool_w, od_pool_scale=dpool_scale, od_w_out=dw_out, od_ln_g=dln_g, od_ln_b=dln_b)
    return loss_lanes, dx, g


_MESH = pl.DeviceIdType.MESH
_ANY = pl.BlockSpec(memory_space=pl.ANY)
N_CHIPS = 4


def _push_call(name, ins, out_shapes, plan, n_remote, n_local):
    n_in, n_out = len(ins), len(out_shapes)

    def body(*refs):
        in_refs, out_refs = refs[:n_in], refs[n_in:n_in + n_out]
        send_sems, recv_sems, local_sems = refs[n_in + n_out:]
        x, y, c = lax.axis_index("x"), lax.axis_index("y"), lax.axis_index("c")
        remote, local = plan(in_refs, out_refs, x, y, c)
        assert len(remote) == n_remote and len(local) == n_local
        sends = [pltpu.make_async_remote_copy(src_ref=s, dst_ref=d, send_sem=send_sems.at[k], recv_sem=recv_sems.at[k],
                                              device_id=dev, device_id_type=_MESH)
                 for k, (s, d, dev, _) in enumerate(remote)]
        recvs = [pltpu.make_async_remote_copy(src_ref=s, dst_ref=land, send_sem=send_sems.at[k],
                                              recv_sem=recv_sems.at[k], device_id=dev, device_id_type=_MESH)
                 for k, (s, _, dev, land) in enumerate(remote)]
        locs = [pltpu.make_async_copy(s, d, local_sems.at[k]) for k, (s, d) in enumerate(local)]
        for cp in sends + locs:
            cp.start()
        for cp in recvs:
            cp.wait_recv()
        for cp in sends:
            cp.wait_send()
        for cp in locs:
            cp.wait()

    return pl.pallas_call(
        body, name=name, in_specs=[_ANY] * n_in, out_specs=[_ANY] * n_out, out_shape=list(out_shapes),
        scratch_shapes=[pltpu.SemaphoreType.DMA((n_remote,)), pltpu.SemaphoreType.DMA((n_remote,)),
                        pltpu.SemaphoreType.DMA((max(n_local, 1),))],
    )(*ins)


def _other_chips(x, y):
    return [(1 - x, y), (x, 1 - y), (1 - x, 1 - y)]


def chips_allgather(bufs):
    def plan(in_refs, out_refs, x, y, c):
        me = 2 * x + y
        remote, local = [], []
        for src, out in zip(in_refs, out_refs):
            for (px, py) in _other_chips(x, y):
                remote.append((src, out.at[me], (px, py, c), out.at[2 * px + py]))
            local.append((src, out.at[me]))
        return remote, local

    shapes = [jax.ShapeDtypeStruct((N_CHIPS,) + b.shape, b.dtype) for b in bufs]
    return _push_call("weights_allgather", bufs, shapes, plan, 3 * len(bufs), len(bufs))


def sibling_send(buf, name):
    def plan(in_refs, out_refs, x, y, c):
        return [(in_refs[0], out_refs[0], (x, y, 1 - c), out_refs[0])], []

    return _push_call(name, [buf], [jax.ShapeDtypeStruct(buf.shape, buf.dtype)], plan, 1, 0)[0]


def chips_scatter(buf):
    def plan(in_refs, out_refs, x, y, c):
        me = 2 * x + y
        src, out = in_refs[0], out_refs[0]
        remote = [(src.at[2 * px + py], out.at[me], (px, py, c), out.at[2 * px + py]) for (px, py) in _other_chips(x, y)]
        return remote, [(src.at[me], out.at[me])]

    return _push_call("grads_scatter", [buf], [jax.ShapeDtypeStruct(buf.shape, buf.dtype)], plan, 3, 1)[0]


PACK_W = 1024
PACK_BLK = 128


def _ew_call(name, fn, ins, in_specs, out_shape, out_spec, n_out, steps):
    def body(*refs):
        res = fn(*[r[...] for r in refs[:len(ins)]])
        if not isinstance(res, (tuple, list)):
            res = (res,)
        for r, v in zip(refs[len(ins):], res):
            r[...] = v

    return pl.pallas_call(
        body, name=name, grid=(steps,), in_specs=in_specs, out_specs=[out_spec] * n_out,
        out_shape=[jax.ShapeDtypeStruct(out_shape, F32)] * n_out,
        compiler_params=_params(("parallel",)),
    )(*ins)


def add2(a, b, name):
    n, r, w = a.shape
    blk = pl.BlockSpec((PACK_BLK, w), lambda i: (i, 0))
    out = _ew_call(name, lambda u, v: u + v, [a.reshape(n * r, w), b.reshape(n * r, w)], [blk, blk], (n * r, w), blk,
                   1, n * r // PACK_BLK)[0]
    return out.reshape(a.shape)


def sum_chips(buf):
    _, r, w = buf.shape
    specs = [pl.BlockSpec((None, PACK_BLK, w), lambda i, q=q: (q, i, 0)) for q in range(N_CHIPS)]
    return _ew_call("grads_sum", lambda a, b, c, d: ((a + b) + c) + d, [buf] * N_CHIPS, specs, (r, w),
                    pl.BlockSpec((PACK_BLK, w), lambda i: (i, 0)), 1, r // PACK_BLK)[0]


def adamw(w, g, m, v, name):
    def fn(w_, g_, m_, v_):
        m2 = ADAM_B1 * m_ + (1.0 - ADAM_B1) * g_
        v2 = ADAM_B2 * v_ + (1.0 - ADAM_B2) * jnp.square(g_)
        m_hat = m2 / (1.0 - ADAM_B1 ** ADAM_STEP)
        v_hat = v2 / (1.0 - ADAM_B2 ** ADAM_STEP)
        return -ADAM_LR * (m_hat / (jnp.sqrt(v_hat) + ADAM_EPS) + ADAM_WD * w_), m2, v2

    r, c = w.shape
    br = r if r <= 512 else _pick(r, (256, 128, 64, 32, 16, 8))
    blk = pl.BlockSpec((br, c), lambda i: (i, 0))
    return _ew_call(name, fn, [w, g, m, v], [blk] * 4, (r, c), blk, 3, r // br)


WEIGHTS = (
    ("ev_w_in", (1, 1024, 7200), 2), ("ev_conv_w", (1, 4, 2048), 2), ("ev_conv_b", (1, 2048), None),
    ("ev_a_log", (1, 2, 16), None), ("ev_dt_bias", (1, 2, 16), None), ("ev_d_skip", (1, 2, 16), None),
    ("ev_norm_g", (1, 1024), None), ("ev_sc_conv_w", (1, 3, 1024), 2), ("ev_w_out", (1, 2048, 1024), 1),
    ("ev_ln_g", (1, 1024), None), ("ev_ln_b", (1, 1024), None), ("od_w_in", (1, 1024, 1952), 2),
    ("od_q_norm_g", (1, 256), 1), ("od_w_uq", (1, 256, 768), 2), ("od_kv_norm_g", (1, 128), None),
    ("od_w_ukv", (1, 128, 1024), 2), ("od_pool_w", (1, 4, 128, 128), None), ("od_pool_scale", (1, 512), 1),
    ("od_w_out", (1, 1024, 1024), 1), ("od_ln_g", (1, 1024), 1), ("od_ln_b", (1, 1024), 1),
)
BIG = ("ev_w_in", "ev_w_out", "od_w_in", "od_w_uq", "od_w_ukv", "od_w_out")


def _block_shape(shape, axis):
    if axis is None:
        return tuple(shape)
    return tuple(d // N_CHIPS if i == axis else d for i, d in enumerate(shape))


def _pack(arrs, quantum):
    flat = jnp.concatenate([a.reshape(-1) for a in arrs])
    n = flat.shape[0]
    padded = -(-n // quantum) * quantum
    return jnp.concatenate([flat, jnp.zeros((padded - n,), flat.dtype)]).reshape(-1, LANES)


def _unpack(flat, shapes):
    out, off = [], 0
    for sh in shapes:
        n = int(np.prod(sh))
        out.append(flat[off:off + n].reshape(sh))
        off += n
    return out


def gather_weights(local):
    sharded = [(n, sh, ax) for (n, sh, ax) in WEIGHTS if ax is not None]
    big = [(n, sh, ax) for (n, sh, ax) in sharded if n in BIG]
    small = [(n, sh, ax) for (n, sh, ax) in sharded if n not in BIG]
    pb = _pack([local[n].astype(MATMUL_DTYPE) for n, _, _ in big], 2 * SUBLANES * LANES)
    ps = _pack([local[n] for n, _, _ in small], SUBLANES * LANES)
    gb, gs = chips_allgather([pb, ps])
    full = {n: local[n] for (n, sh, ax) in WEIGHTS if ax is None}
    for group, g in ((big, gb), (small, gs)):
        parts = [_unpack(g[q].reshape(-1), [_block_shape(sh, ax) for _, sh, ax in group]) for q in range(N_CHIPS)]
        for i, (n, sh, ax) in enumerate(group):
            full[n] = jnp.concatenate([parts[q][i] for q in range(N_CHIPS)], axis=ax)
    return full


WIDE = (("ev_w_in", True), ("ev_w_out", False), ("od_w_in", True), ("od_w_out", False))


def reduce_and_update(grads, local_w, local_m, local_v):
    c = lax.axis_index("c")
    wide_names = [n for n, _ in WIDE]
    tail = [(n, sh, ax) for (n, sh, ax) in WEIGHTS if n not in wide_names]
    tail_blocks = [_block_shape(sh, ax) for _, sh, ax in tail]
    n_tail = sum(int(np.prod(b)) for b in tail_blocks)
    wide_rows = [grads[n].shape[0] // N_CHIPS for n in wide_names]
    quantum = 2 * PACK_BLK
    total = -(-(sum(wide_rows) + -(-n_tail // PACK_W)) // quantum) * quantum
    tail_rows = total - sum(wide_rows)

    def tail_pack(pieces):
        flat = jnp.concatenate([p.reshape(-1) for p in pieces] + [jnp.zeros((tail_rows * PACK_W - n_tail,), F32)])
        return flat.reshape(tail_rows, PACK_W)

    packs = []
    for q in range(N_CHIPS):
        parts = [lax.slice_in_dim(grads[n], q * r, (q + 1) * r, axis=0) for n, r in zip(wide_names, wide_rows)]
        pieces = []
        for (n, sh, ax), bs in zip(tail, tail_blocks):
            g = grads[n].reshape(sh)
            pieces.append(g if ax is None else lax.slice_in_dim(g, q * bs[ax], (q + 1) * bs[ax], axis=ax))
        packs.append(jnp.concatenate(parts + [tail_pack(pieces)], axis=0))
    packs = jnp.stack(packs)
    rh = total // 2
    keep = lax.dynamic_slice_in_dim(packs, c * rh, rh, axis=1)
    give = lax.dynamic_slice_in_dim(packs, (1 - c) * rh, rh, axis=1)
    chip_half = add2(keep, sibling_send(give, "grads_to_sibling"), "grads_chip_sum")
    total_half = sum_chips(chips_scatter(chip_half))
    other_half = sibling_send(total_half, "grads_from_sibling")
    g_pack = jnp.concatenate([jnp.where(c == 0, total_half, other_half),
                              jnp.where(c == 0, other_half, total_half)], axis=0)
    outs = ({}, {}, {}, {})
    off = 0
    for (n, transposed), r in zip(WIDE, wide_rows):
        g = g_pack[off:off + r]
        off += r
        g = g.T if transposed else g
        shape = local_w[n].shape
        res = adamw(local_w[n].reshape(g.shape), g, local_m[n].reshape(g.shape), local_v[n].reshape(g.shape),
                    "adamw_" + n)
        for d, a in zip(outs, (g, *res)):
            d[n] = a.reshape(shape)
    g_tail = g_pack[off:]
    res = adamw(*[tail_pack([d[n] for n, _, _ in tail]) if d is not None else g_tail
                  for d in (local_w, None, local_m, local_v)], "adamw_small")
    for d, a in zip(outs, (g_tail, *res)):
        d.update(zip([n for n, _, _ in tail], _unpack(a.reshape(-1), tail_blocks)))
    return outs


ROW_TILE = 256


def kernel(x, positions, ev_w_in, ev_conv_w, ev_conv_b, ev_a_log, ev_dt_bias, ev_d_skip, ev_norm_g, ev_sc_conv_w, ev_w_out, ev_ln_g, ev_ln_b, od_w_in, od_q_norm_g, od_w_uq, od_kv_norm_g, od_w_ukv, od_pool_w, od_pool_scale, od_w_out, od_ln_g, od_ln_b, loss_target, m_ev_w_in, m_ev_conv_w, m_ev_conv_b, m_ev_a_log, m_ev_dt_bias, m_ev_d_skip, m_ev_norm_g, m_ev_sc_conv_w, m_ev_w_out, m_ev_ln_g, m_ev_ln_b, m_od_w_in, m_od_q_norm_g, m_od_w_uq, m_od_kv_norm_g, m_od_w_ukv, m_od_pool_w, m_od_pool_scale, m_od_w_out, m_od_ln_g, m_od_ln_b, v_ev_w_in, v_ev_conv_w, v_ev_conv_b, v_ev_a_log, v_ev_dt_bias, v_ev_d_skip, v_ev_norm_g, v_ev_sc_conv_w, v_ev_w_out, v_ev_ln_g, v_ev_ln_b, v_od_w_in, v_od_q_norm_g, v_od_w_uq, v_od_kv_norm_g, v_od_w_ukv, v_od_pool_w, v_od_pool_scale, v_od_w_out, v_od_ln_g, v_od_ln_b):
    names = [n for n, _, _ in WEIGHTS]
    local_w = dict(zip(names, (ev_w_in, ev_conv_w, ev_conv_b, ev_a_log, ev_dt_bias, ev_d_skip, ev_norm_g, ev_sc_conv_w, ev_w_out, ev_ln_g, ev_ln_b, od_w_in, od_q_norm_g, od_w_uq, od_kv_norm_g, od_w_ukv, od_pool_w, od_pool_scale, od_w_out, od_ln_g, od_ln_b)))
    local_m = dict(zip(names, (m_ev_w_in, m_ev_conv_w, m_ev_conv_b, m_ev_a_log, m_ev_dt_bias, m_ev_d_skip, m_ev_norm_g, m_ev_sc_conv_w, m_ev_w_out, m_ev_ln_g, m_ev_ln_b, m_od_w_in, m_od_q_norm_g, m_od_w_uq, m_od_kv_norm_g, m_od_w_ukv, m_od_pool_w, m_od_pool_scale, m_od_w_out, m_od_ln_g, m_od_ln_b)))
    local_v = dict(zip(names, (v_ev_w_in, v_ev_conv_w, v_ev_conv_b, v_ev_a_log, v_ev_dt_bias, v_ev_d_skip, v_ev_norm_g, v_ev_sc_conv_w, v_ev_w_out, v_ev_ln_g, v_ev_ln_b, v_od_w_in, v_od_q_norm_g, v_od_w_uq, v_od_kv_norm_g, v_od_w_ukv, v_od_pool_w, v_od_pool_scale, v_od_w_out, v_od_ln_g, v_od_ln_b)))
    s = x.shape[1]
    rows = Rows(s, min(ROW_TILE, s))
    f = gather_weights(local_w)
    ew = even_weights(f["ev_w_in"][0], f["ev_conv_w"][0], f["ev_conv_b"], f["ev_a_log"][0], f["ev_dt_bias"][0],
                      f["ev_d_skip"][0], f["ev_norm_g"], f["ev_sc_conv_w"][0], f["ev_w_out"][0], f["ev_ln_g"],
                      f["ev_ln_b"])
    ow = odd_weights(f["od_w_in"][0], f["od_q_norm_g"], f["od_w_uq"][0], f["od_kv_norm_g"], f["od_w_ukv"][0],
                     f["od_pool_w"][0], f["od_pool_scale"], f["od_w_out"][0], f["od_ln_g"], f["od_ln_b"])
    x1, saved = even_layer(x[0], ew, rows)
    loss_lanes, dx1, g_odd = odd_layer_loss(x1, positions.reshape(s, 1), loss_target[0], ow, rows)
    dx0, g_even = even_layer_bwd(dx1, ew, saved, rows)
    loss = lax.psum(jnp.sum(loss_lanes), ("x", "y", "c"))
    grad, delta, new_m, new_v = reduce_and_update({**g_even, **g_odd}, local_w, local_m, local_v)
    return (loss, dx0[None], *[grad[n] for n in names], *[delta[n] for n in names],
            *[new_m[n] for n in names], *[new_v[n] for n in names])
```

```python
import functools
import math

import jax
import jax.numpy as jnp
import numpy as np
from jax import lax
from jax.experimental import pallas as pl
from jax.experimental.pallas import tpu as pltpu

F32 = jnp.float32
BF16 = jnp.bfloat16
MATMUL_DTYPE = jnp.bfloat16

D_MODEL = 1024
DEPTH = 2
SSD_HEADS, SSD_HEAD_DIM, SSD_GROUPS, SSD_STATE, SSD_CHUNK = 16, 64, 4, 128, 128
SSD_INNER = SSD_HEADS * SSD_HEAD_DIM
SSD_XBC = SSD_INNER + 2 * SSD_GROUPS * SSD_STATE
SC_WIDTH = 1024
MLA_HEADS, MLA_Q_RANK, MLA_KV_RANK, MLA_NOPE, MLA_ROPE, MLA_V = 8, 256, 128, 64, 32, 64
MLA_WIDTH = MLA_HEADS * MLA_V
ROPE_THETA = 10000.0
ATTN_SCALE = (MLA_NOPE + MLA_ROPE) ** -0.5
QSCALE = ATTN_SCALE * math.log2(math.e)
LN2 = math.log(2.0)
POOL_WINDOWS = (2, 4, 8, 16)
POOL_GROUP = 128
POOL_WIDTH = POOL_GROUP * len(POOL_WINDOWS)
EPS = 1e-5
ALPHA = (2 * DEPTH) ** 0.25
EVEN_PROJ, ODD_PROJ = 7200, 1952
ADAM_LR, ADAM_B1, ADAM_B2, ADAM_EPS, ADAM_WD, ADAM_STEP = 0.001, 0.9, 0.999, 1e-08, 0.01, 10

LANES = 128
SUBLANES = 8
HALO = SUBLANES
VMEM_LIMIT = 56 * 1024 * 1024

EVEN_P = 7296
ODD_P = 2048
E_Z, E_XBC, E_BG, E_CG, E_H, E_GATE, E_DT = 0, 1024, 3072, 4096, 5120, 6144, 7168
O_CQ, O_CKV, O_KR, O_GC, O_UD, O_GD = 0, 256, 384, 512, 1024, 1536


def _params(sem=None):
    return pltpu.CompilerParams(dimension_semantics=sem, vmem_limit_bytes=VMEM_LIMIT)


def _mm(a, b, dims=(((1,), (0,)), ((), ()))):
    return lax.dot_general(a.astype(MATMUL_DTYPE), b.astype(MATMUL_DTYPE), dims, preferred_element_type=F32)


_NN = (((1,), (0,)), ((), ()))
_NT = (((1,), (1,)), ((), ()))
_TN = (((0,), (0,)), ((), ()))


def _silu(v):
    return v * jax.nn.sigmoid(v)


def _dsilu(v):
    s = jax.nn.sigmoid(v)
    return s * (1.0 + v * (1.0 - s))


def _pick(n, prefs):
    for p in prefs:
        if n % p == 0:
            return p
    return n


def matmul(a, b, mode, out_dtype, name, add=None, tm=None, tn=None, tk=None):
    if mode == "nn":
        (m, k), (k2, n) = a.shape, b.shape
    elif mode == "nt":
        (m, k), (n, k2) = a.shape, b.shape
    else:
        (k, m), (k2, n) = a.shape, b.shape
    assert k == k2, (a.shape, b.shape, mode)
    wide = (2432, 2048, 1536, 1024, 512, 256, 128)
    if mode == "tn":
        tm = tm or _pick(m, (2432, 2048, 1024, 512, 256, 128))
        tn = tn or _pick(n, wide)
        tk = tk or _pick(k, (512, 256, 128))
    else:
        tm = tm or _pick(m, (512, 256, 128))
        tn = tn or _pick(n, wide)
        tk = tk or _pick(k, wide)
    nk = k // tk
    dims = {"nn": _NN, "nt": _NT, "tn": _TN}[mode]

    def body(a_ref, b_ref, *rest):
        o_ref, acc_ref = rest[-2:]
        kk = pl.program_id(2)

        @pl.when(kk == 0)
        def _():
            acc_ref[...] = jnp.zeros_like(acc_ref) if add is None else rest[0][...].astype(F32)

        acc_ref[...] += _mm(a_ref[...], b_ref[...], dims)

        @pl.when(kk == nk - 1)
        def _():
            o_ref[...] = acc_ref[...].astype(o_ref.dtype)

    a_spec = {"nn": pl.BlockSpec((tm, tk), lambda i, j, kk: (i, kk)),
              "nt": pl.BlockSpec((tm, tk), lambda i, j, kk: (i, kk)),
              "tn": pl.BlockSpec((tk, tm), lambda i, j, kk: (kk, i))}[mode]
    b_spec = {"nn": pl.BlockSpec((tk, tn), lambda i, j, kk: (kk, j)),
              "nt": pl.BlockSpec((tn, tk), lambda i, j, kk: (j, kk)),
              "tn": pl.BlockSpec((tk, tn), lambda i, j, kk: (kk, j))}[mode]
    return pl.pallas_call(
        body, name=name, grid=(m // tm, n // tn, nk),
        in_specs=[a_spec, b_spec] + ([] if add is None else [pl.BlockSpec((tm, tn), lambda i, j, kk: (i, j))]),
        out_specs=pl.BlockSpec((tm, tn), lambda i, j, kk: (i, j)),
        out_shape=jax.ShapeDtypeStruct((m, n), out_dtype),
        scratch_shapes=[pltpu.VMEM((tm, tn), F32)],
        compiler_params=_params(("parallel", "parallel", "arbitrary")),
    )(*((a, b) if add is None else (a, b, add)))


class Rows:
    def __init__(self, s, t):
        assert s % t == 0 and t % HALO == 0
        self.s, self.t, self.n = s, t, s // t

    def tile(self, width, col=0, lead=None):
        cb = col // width
        assert col % width == 0
        if lead is None:
            return pl.BlockSpec((self.t, width), lambda i: (i, cb))
        return pl.BlockSpec((None, self.t, width), lambda i: (lead, i, cb))

    def prev(self, width, col=0, lead=None):
        cb, r = col // width, self.t // HALO
        if lead is None:
            return pl.BlockSpec((HALO, width), lambda i: (jnp.maximum(i * r - 1, 0), cb))
        return pl.BlockSpec((None, HALO, width), lambda i: (lead, jnp.maximum(i * r - 1, 0), cb))

    def next(self, width, col=0, lead=None):
        cb, r, last = col // width, self.t // HALO, self.s // HALO - 1
        if lead is None:
            return pl.BlockSpec((HALO, width), lambda i: (jnp.minimum((i + 1) * r, last), cb))
        return pl.BlockSpec((None, HALO, width), lambda i: (lead, jnp.minimum((i + 1) * r, last), cb))

    def halo(self, width, col=0, lead=None):
        return [self.prev(width, col, lead), self.tile(width, col, lead), self.next(width, col, lead)]

    @staticmethod
    def full(shape):
        nd = len(shape)
        return pl.BlockSpec(tuple(shape), lambda i: (0,) * nd)


def rows_call(name, rows, fn, ins, in_specs, row_outs, acc_outs=()):
    n_row = len(row_outs)

    def body(*refs):
        in_refs = refs[:len(ins)]
        out_refs = refs[len(ins):]
        res = fn(*[r[...] for r in in_refs])
        if not isinstance(res, (tuple, list)):
            res = (res,)
        for r, v in zip(out_refs[:n_row], res[:n_row]):
            r[...] = v.astype(r.dtype)
        if acc_outs:
            first = pl.program_id(0) == 0

            @pl.when(first)
            def _():
                for r, v in zip(out_refs[n_row:], res[n_row:]):
                    r[...] = v.astype(F32)

            @pl.when(jnp.logical_not(first))
            def _():
                for r, v in zip(out_refs[n_row:], res[n_row:]):
                    r[...] += v.astype(F32)

    out_shape = [jax.ShapeDtypeStruct((rows.s, w), dt) for (w, dt) in row_outs]
    out_specs = [rows.tile(w) for (w, dt) in row_outs]
    out_shape += [jax.ShapeDtypeStruct(tuple(sh), F32) for sh in acc_outs]
    out_specs += [Rows.full(sh) for sh in acc_outs]
    return pl.pallas_call(
        body, name=name, grid=(rows.n,), in_specs=list(in_specs), out_specs=out_specs, out_shape=out_shape,
        compiler_params=_params(("arbitrary",)),
    )(*ins)


def _edge_zero(prev, nxt, n_tiles):
    i = pl.program_id(0)
    prev = jnp.where(i == 0, jnp.zeros_like(prev), prev)
    nxt = jnp.where(i == n_tiles - 1, jnp.zeros_like(nxt), nxt)
    return prev, nxt


def _ext(prev, cur, nxt, n_tiles):
    prev, nxt = _edge_zero(prev, nxt, n_tiles)
    return jnp.concatenate([prev.astype(F32), cur.astype(F32), nxt.astype(F32)], axis=0)


def _shift(ext, off):
    n = ext.shape[0]
    t = n - 2 * HALO
    if off == 0:
        return ext[HALO:HALO + t]
    return pltpu.roll(ext, (-off) % n, 0)[HALO:HALO + t]


def _rowsum(v):
    return jnp.sum(v, axis=0, keepdims=True)


CONV_OFFS = (-2, -1, 0, 1)
SC_OFFS = (-1, 0, 1)


def conv_fwd(proj, conv_w, conv_b, rows):
    def fn(p, c, nx, w, b):
        e = _ext(p, c, nx, rows.n)
        pre = b
        for k, off in enumerate(CONV_OFFS):
            pre = pre + w[k:k + 1, :] * _shift(e, off)
        return pre, _silu(pre)

    outs = [rows_call(f"ev_conv_fwd{h}", rows, fn,
                      [proj, proj, proj, conv_w[:, h * 1024:(h + 1) * 1024], conv_b[:, h * 1024:(h + 1) * 1024]],
                      rows.halo(1024, E_XBC + h * 1024) + [Rows.full((4, 1024)), Rows.full((1, 1024))],
                      [(1024, F32), (1024, F32)]) for h in range(2)]
    return outs


def conv_bwd(proj, pre_h, du_h, conv_w, rows):
    res = []
    for h in range(2):
        def fn(*a):
            w = a[-1]
            xe = _ext(a[0], a[1], a[2], rows.n)
            pe = jnp.concatenate([a[3], a[4], a[5]], axis=0)
            g = a[6:-1]
            du = _ext(g[0], g[1], g[2], rows.n) + _ext(g[3], g[4], g[5], rows.n)
            dpre = du * _dsilu(pe)
            dx = jnp.zeros_like(a[1], dtype=F32)
            dws = []
            for k, off in enumerate(CONV_OFFS):
                dx = dx + w[k:k + 1, :] * _shift(dpre, -off)
                dws.append(_rowsum(_shift(dpre, 0) * _shift(xe, off)))
            dw = jnp.concatenate(dws + [jnp.zeros((4, dx.shape[1]), F32)], axis=0)
            return dx, dw, _rowsum(_shift(dpre, 0))

        gi = [du_h[h][0]] * 3 + [du_h[h][1]] * 3
        gs = rows.halo(1024) * 2
        res.append(rows_call(
            f"ev_conv_bwd{h}", rows, fn,
            [proj] * 3 + [pre_h[h]] * 3 + gi + [conv_w[:, h * 1024:(h + 1) * 1024]],
            rows.halo(1024, E_XBC + h * 1024) + rows.halo(1024) + gs + [Rows.full((4, 1024))],
            [(1024, MATMUL_DTYPE)], [(8, 1024), (1, 1024)]))
    dconv_w = jnp.concatenate([res[0][1][:4], res[1][1][:4]], axis=1)
    dconv_b = jnp.concatenate([res[0][2], res[1][2]], axis=1)
    return [res[0][0], res[1][0]], dconv_w, dconv_b


def _head_row(p):
    return jnp.concatenate([p.reshape(1, 2 * SSD_HEADS), jnp.zeros((1, LANES - 2 * SSD_HEADS), F32)], axis=1)


def _head_unrow(r):
    return r[:, :2 * SSD_HEADS].reshape(2, SSD_HEADS)


def _ssd_pre_fn(dtraw, bias_row, alog_row):
    q = dtraw.shape[0]
    dt = jax.nn.softplus(dtraw + bias_row)
    da = dt * (-jnp.exp(alog_row))
    li = lax.broadcasted_iota(jnp.int32, (q, q), 0)
    si = lax.broadcasted_iota(jnp.int32, (q, q), 1)
    tril = (li >= si).astype(F32)
    csf = lax.dot_general(tril, da, _NN, precision=lax.Precision.HIGHEST, preferred_element_type=F32)
    tot = jnp.sum(da, axis=0, keepdims=True)
    lane = lax.broadcasted_iota(jnp.int32, (1, LANES), 1)
    cs = jnp.where(lane < SSD_HEADS, csf, tot - csf + da)
    return dt, cs


def ssd_pre(proj, bias_row, alog_row, s):
    rows = Rows(s, SSD_CHUNK)
    return rows_call("ev_ssd_pre", rows, _ssd_pre_fn, [proj, bias_row, alog_row],
                     [rows.tile(LANES, E_DT), Rows.full((1, LANES)), Rows.full((1, LANES))],
                     [(LANES, F32), (LANES, F32)])


def ssd_pre_bwd(proj, bias_row, alog_row, ddt, dcs, s):
    rows = Rows(s, SSD_CHUNK)

    def fn(dtraw, b, al, g0, g1, c0, c1):
        _, vjp = jax.vjp(_ssd_pre_fn, dtraw, b, al)
        return vjp((g0 + g1, c0 + c1))

    return rows_call("ev_ssd_pre_bwd", rows, fn, [proj, bias_row, alog_row, ddt[0], ddt[1], dcs[0], dcs[1]],
                     [rows.tile(LANES, E_DT), Rows.full((1, LANES)), Rows.full((1, LANES))] + [rows.tile(LANES)] * 4,
                     [(LANES, MATMUL_DTYPE)], [(1, LANES), (1, LANES)])


_N_PAIR = SSD_HEADS // 2
_BC = SSD_GROUPS * SSD_STATE


def _ssd_chunk_fn(x, bc, dt, cs, h_in, dsk_row, d):
    q = x.shape[0]
    lane = lax.broadcasted_iota(jnp.int32, (1, LANES), 1)
    half = lane < SSD_HEAD_DIM
    lo, hi = half.astype(F32), 1.0 - half.astype(F32)
    cst = cs.T
    li = lax.broadcasted_iota(jnp.int32, (q, 1), 0)
    si = lax.broadcasted_iota(jnp.int32, (1, q), 1)
    mask = li >= si if d == 0 else li <= si
    end = q - 1 if d == 0 else 0
    tot = cs[end:end + 1, :]
    mine = (lane >= SSD_HEADS * d) & (lane < SSD_HEADS * (d + 1))
    e_cs, e_dec, e_tot = jnp.exp(cs), jnp.exp(jnp.where(mine, tot - cs, 0.0)), jnp.exp(tot)
    e_dd = e_dec * dt
    dtt = dt.T

    def col(v, l):
        return v[:, l:l + 1]

    def by_head(v, l):
        return jnp.where(half, col(v, l), col(v, l + 1))

    per_group = _N_PAIR // SSD_GROUPS
    ys, hs = [], []
    for g in range(SSD_GROUPS):
        bm = bc[:, g * SSD_STATE:(g + 1) * SSD_STATE]
        cm = bc[:, _BC + g * SSD_STATE:_BC + (g + 1) * SSD_STATE]
        cb = _mm(cm, bm, _NT)
        pairs = range(g * per_group, (g + 1) * per_group)
        h_g = [h_in[j * SSD_STATE:(j + 1) * SSD_STATE, :] for j in pairs]
        y_off = _mm(cm, jnp.concatenate(h_g, axis=1))
        for k, j in enumerate(pairs):
            l0 = SSD_HEADS * d + 2 * j
            xj = x[:, j * LANES:(j + 1) * LANES]
            xcat = jnp.concatenate([xj * lo, xj * hi], axis=0)
            w0 = cb * jnp.exp(jnp.where(mask, col(cs, l0) - cst[l0:l0 + 1, :], -jnp.inf)) * dtt[l0:l0 + 1, :]
            w1 = cb * jnp.exp(jnp.where(mask, col(cs, l0 + 1) - cst[l0 + 1:l0 + 2, :], -jnp.inf)) * dtt[l0 + 1:l0 + 2, :]
            y = _mm(jnp.concatenate([w0, w1], axis=1), xcat)
            st = _mm(jnp.concatenate([bm * col(e_dd, l0), bm * col(e_dd, l0 + 1)], axis=0), xcat, _TN)
            y = y + y_off[:, k * LANES:(k + 1) * LANES] * by_head(e_cs, l0) + by_head(dsk_row, l0) * xj
            ys.append(y)
            hs.append(h_g[k] * by_head(e_tot, l0) + st)
    return jnp.concatenate(ys, axis=1), jnp.concatenate(hs, axis=0)


def _chunk_of(d, ci, nc, backward):
    up = ci if not backward else nc - 1 - ci
    return up + d * (nc - 1 - 2 * up)


_ST_ROWS = _N_PAIR * SSD_STATE


def _ssd_fwd_dir(u_h, dt, cs, dsk_row, s, d):
    nc = s // SSD_CHUNK
    q = SSD_CHUNK

    def body(x_ref, bc_ref, dt_ref, cs_ref, dsk_ref, y_ref, hs_ref, st_ref):
        @pl.when(pl.program_id(0) == 0)
        def _():
            st_ref[...] = jnp.zeros(st_ref.shape, F32)

        h_in = st_ref[...]
        y, h_out = _ssd_chunk_fn(x_ref[...], bc_ref[...], dt_ref[...], cs_ref[...], h_in, dsk_ref[...], d)
        y_ref[...] = y
        hs_ref[...] = h_in
        st_ref[...] = h_out

    ch = lambda ci: _chunk_of(d, ci, nc, False)
    return pl.pallas_call(
        body, name=f"ev_ssd_fwd{d}", grid=(nc,),
        in_specs=[pl.BlockSpec((q, SSD_INNER), lambda ci: (ch(ci), 0)),
                  pl.BlockSpec((q, 2 * _BC), lambda ci: (ch(ci), 0)),
                  pl.BlockSpec((q, LANES), lambda ci: (ch(ci), 0)),
                  pl.BlockSpec((q, LANES), lambda ci: (ch(ci), 0)),
                  pl.BlockSpec((1, LANES), lambda ci: (0, 0))],
        out_specs=[pl.BlockSpec((q, SSD_INNER), lambda ci: (ch(ci), 0)),
                   pl.BlockSpec((None, _ST_ROWS, LANES), lambda ci: (ch(ci), 0, 0))],
        out_shape=[jax.ShapeDtypeStruct((s, SSD_INNER), F32), jax.ShapeDtypeStruct((nc, _ST_ROWS, LANES), F32)],
        scratch_shapes=[pltpu.VMEM((_ST_ROWS, LANES), F32)],
        compiler_params=_params(("arbitrary",)),
    )(u_h[0], u_h[1], dt, cs, dsk_row)


def ssd_fwd(u_h, dt, cs, dsk_row, s):
    ys, hss = zip(*[_ssd_fwd_dir(u_h, dt, cs, dsk_row, s, d) for d in range(2)])
    return ys, hss


def _ssd_bwd_dir(u_h, dt, cs, dsk_row, hs, dy, s, d):
    nc = s // SSD_CHUNK
    q = SSD_CHUNK

    def body(x_ref, bc_ref, dt_ref, cs_ref, dsk_ref, hs_ref, dy_ref,
             dx_ref, dbc_ref, ddt_ref, dcs_ref, ddsk_ref, dst_ref):
        ci = pl.program_id(0)

        @pl.when(ci == 0)
        def _():
            dst_ref[...] = jnp.zeros(dst_ref.shape, F32)

        f = functools.partial(_ssd_chunk_fn, d=d)
        _, vjp = jax.vjp(f, x_ref[...], bc_ref[...], dt_ref[...], cs_ref[...], hs_ref[...], dsk_ref[...])
        dx, dbc, ddt, dcs, dh, ddsk = vjp((dy_ref[...], dst_ref[...]))
        dx_ref[...] = dx
        dbc_ref[...] = dbc
        ddt_ref[...] = ddt
        dcs_ref[...] = dcs
        dst_ref[...] = dh

        @pl.when(ci == 0)
        def _():
            ddsk_ref[...] = ddsk

        @pl.when(ci != 0)
        def _():
            ddsk_ref[...] += ddsk

    ch = lambda ci: _chunk_of(d, ci, nc, True)
    row_blk = lambda w: pl.BlockSpec((q, w), lambda ci: (ch(ci), 0))
    return pl.pallas_call(
        body, name=f"ev_ssd_bwd{d}", grid=(nc,),
        in_specs=[row_blk(SSD_INNER), row_blk(2 * _BC), row_blk(LANES), row_blk(LANES),
                  pl.BlockSpec((1, LANES), lambda ci: (0, 0)),
                  pl.BlockSpec((None, _ST_ROWS, LANES), lambda ci: (ch(ci), 0, 0)), row_blk(SSD_INNER)],
        out_specs=[row_blk(SSD_INNER), row_blk(2 * _BC), row_blk(LANES), row_blk(LANES),
                   pl.BlockSpec((1, LANES), lambda ci: (0, 0))],
        out_shape=[jax.ShapeDtypeStruct((s, SSD_INNER), F32), jax.ShapeDtypeStruct((s, 2 * _BC), F32),
                   jax.ShapeDtypeStruct((s, LANES), F32), jax.ShapeDtypeStruct((s, LANES), F32),
                   jax.ShapeDtypeStruct((1, LANES), F32)],
        scratch_shapes=[pltpu.VMEM((_ST_ROWS, LANES), F32)],
        compiler_params=_params(("arbitrary",)),
    )(u_h[0], u_h[1], dt, cs, dsk_row, hs, dy)


def ssd_bwd(u_h, dt, cs, dsk_row, hs, dy, s):
    return zip(*[_ssd_bwd_dir(u_h, dt, cs, dsk_row, hs[d], dy, s, d) for d in range(2)])


def _gated_rms(ys, z, g):
    t1 = ys * _silu(z)
    return t1 * lax.rsqrt(jnp.mean(t1 * t1, axis=-1, keepdims=True) + EPS) * g


def even_mix_fwd(proj, y2, norm_g, sc_w, rows):
    def fn(yf, yb, z, bg, cgp, cg, cgn, hp, hh, hn, gate, g, w):
        ya = _gated_rms(yf + yb, z, g)
        me = _ext(cgp, cg, cgn, rows.n) * _ext(hp, hh, hn, rows.n)
        cm = sum(w[k:k + 1, :] * _shift(me, off) for k, off in enumerate(SC_OFFS))
        return jnp.concatenate([ya, bg * cm * _silu(gate)], axis=1)

    w = 1024
    return rows_call("ev_mix_fwd", rows, fn,
                     [y2[0], y2[1], proj, proj] + [proj] * 6 + [proj, norm_g, sc_w],
                     [rows.tile(w), rows.tile(w), rows.tile(w, E_Z), rows.tile(w, E_BG)]
                     + rows.halo(w, E_CG) + rows.halo(w, E_H)
                     + [rows.tile(w, E_GATE), Rows.full((1, w)), Rows.full((3, w))],
                     [(2 * w, MATMUL_DTYPE)])[0]


def even_mix_bwd(proj, y2, norm_g, sc_w, dyab, rows):
    w = 1024

    def fn(yf, yb, z, g, sw, dya, *a):
        (dybp, dyb, dybn, bgp, bg, bgn, gtp, gt, gtn, cgp, cg, cgn, hp, hh, hn) = a
        _, vjp = jax.vjp(_gated_rms, yf + yb, z, g)
        dys, dz, dg = vjp(dya.astype(F32))
        n = rows.n
        dye, bge, gte = _ext(dybp, dyb, dybn, n), _ext(bgp, bg, bgn, n), _ext(gtp, gt, gtn, n)
        cge, he = _ext(cgp, cg, cgn, n), _ext(hp, hh, hn, n)
        me = cge * he
        cm = sum(sw[k:k + 1, :] * _shift(me, off) for k, off in enumerate(SC_OFFS))
        dyc = dyb.astype(F32)
        dbg = dyc * cm * _silu(gt)
        dgate = dyc * bg * cm * _dsilu(gt)
        dcme = dye * bge * _silu(gte)
        dm = sum(sw[k:k + 1, :] * _shift(dcme, -off) for k, off in enumerate(SC_OFFS))
        dcm = _shift(dcme, 0)
        dws = [_rowsum(dcm * _shift(me, off)) for off in SC_OFFS]
        dsw = jnp.concatenate(dws + [jnp.zeros((5, w), F32)], axis=0)
        return dys, dz, dbg, dm * hh, dm * cg, dgate, _rowsum(dg), dsw

    return rows_call(
        "ev_mix_bwd", rows, fn,
        [y2[0], y2[1], proj, norm_g, sc_w, dyab] + [dyab] * 3 + [proj] * 12,
        [rows.tile(w), rows.tile(w), rows.tile(w, E_Z), Rows.full((1, w)), Rows.full((3, w)),
         rows.tile(w, 0)] + rows.halo(w, w) + rows.halo(w, E_BG) + rows.halo(w, E_GATE)
        + rows.halo(w, E_CG) + rows.halo(w, E_H),
        [(w, F32)] + [(w, MATMUL_DTYPE)] * 5, [(1, w), (8, w)])


def _res_ln(x, h, g, b):
    v = ALPHA * x + h
    mu = jnp.mean(v, axis=-1, keepdims=True)
    var = jnp.mean(jnp.square(v - mu), axis=-1, keepdims=True)
    return (v - mu) * lax.rsqrt(var + EPS) * g + b


def res_ln_fwd(x, h, g, b, rows, name):
    return rows_call(name, rows, _res_ln, [x, h, g, b],
                     [rows.tile(D_MODEL), rows.tile(D_MODEL), Rows.full((1, D_MODEL)), Rows.full((1, D_MODEL))],
                     [(D_MODEL, F32)])[0]


def res_ln_bwd(x, h, g, b, dy, rows, name):
    def fn(x_, h_, g_, b_, dy_):
        _, vjp = jax.vjp(_res_ln, x_, h_, g_, b_)
        dx, dh, dg, db = vjp(dy_)
        return dx, dh, dg, db

    return rows_call(name, rows, fn, [x, h, g, b, dy],
                     [rows.tile(D_MODEL), rows.tile(D_MODEL), Rows.full((1, D_MODEL)), Rows.full((1, D_MODEL)),
                      rows.tile(D_MODEL)],
                     [(D_MODEL, F32), (D_MODEL, F32)], [(1, D_MODEL), (1, D_MODEL)])


def final_ln_loss(x, h, g, b, target, rows):
    def fn(x_, h_, g_, b_, t_):
        y, vjp = jax.vjp(_res_ln, x_, h_, g_, b_)
        err = y - t_
        dx, dh, dg, db = vjp(err * (1.0 / D_MODEL))
        return dx, dh, dg, db, _rowsum(jnp.square(err)) * (0.5 / D_MODEL)

    return rows_call("od_ln_loss", rows, fn, [x, h, g, b, target],
                     [rows.tile(D_MODEL), rows.tile(D_MODEL), Rows.full((1, D_MODEL)), Rows.full((1, D_MODEL)),
                      rows.tile(D_MODEL)],
                     [(D_MODEL, F32), (D_MODEL, F32)], [(1, D_MODEL), (1, D_MODEL), (1, D_MODEL)])


def pad_even_w_in(w):
    return jnp.concatenate([w[:, :3072], w[:, 3104:], w[:, 3072:3104],
                            jnp.zeros((w.shape[0], EVEN_P - EVEN_PROJ), w.dtype)], axis=1)


def matmul_pieces_nt(pieces, w, name, add):
    m, n, npc = pieces[0].shape[0], w.shape[0], len(pieces)
    widths = [p.shape[1] for p in pieces]
    assert sum(widths) == w.shape[1]
    tm = _pick(m, (256, 128))

    def body(*refs):
        p_refs = refs[:npc]
        w_ref, add_ref, o_ref = refs[npc:]
        acc, off = add_ref[...].astype(F32), 0
        for p_ref, wd in zip(p_refs, widths):
            acc = acc + _mm(p_ref[...], w_ref[:, off:off + wd], _NT)
            off += wd
        o_ref[...] = acc

    return pl.pallas_call(
        body, name=name, grid=(m // tm,),
        in_specs=[pl.BlockSpec((tm, wd), lambda i: (i, 0)) for wd in widths]
        + [pl.BlockSpec(w.shape, lambda i: (0, 0), pipeline_mode=pl.Buffered(1)),
           pl.BlockSpec((tm, n), lambda i: (i, 0))],
        out_specs=pl.BlockSpec((tm, n), lambda i: (i, 0)),
        out_shape=jax.ShapeDtypeStruct((m, n), F32),
        compiler_params=_params(("parallel",)),
    )(*pieces, w, add)


def even_layer(x, w, rows):
    s = rows.s
    xb = x.astype(MATMUL_DTYPE)
    proj = matmul(xb, w["w_in_p"], "nn", F32, "ev_proj")
    (pre0, u0), (pre1, u1) = conv_fwd(proj, w["conv_w"], w["conv_b"], rows)
    dt, cs = ssd_pre(proj, w["bias_row"], w["alog_row"], s)
    y2, hs = ssd_fwd((u0, u1), dt, cs, w["dsk_row"], s)
    yab = even_mix_fwd(proj, y2, w["norm_g"], w["sc_w"], rows)
    h = matmul(yab, w["w_out"], "nn", F32, "ev_out")
    x1 = res_ln_fwd(x, h, w["ln_g"], w["ln_b"], rows, "ev_ln")
    return x1, dict(x=x, xb=xb, proj=proj, pre=(pre0, pre1), u=(u0, u1), dt=dt, cs=cs, y2=y2, hs=hs, yab=yab,
                    h=h)


def even_layer_bwd(dx1, w, sv, rows):
    s = rows.s
    dres, dh, dln_g, dln_b = res_ln_bwd(sv["x"], sv["h"], w["ln_g"], w["ln_b"], dx1, rows, "ev_ln_bwd")
    dyab = matmul(dh, w["w_out"], "nt", F32, "ev_out_dx")
    dw_out = matmul(sv["yab"], dh, "tn", F32, "ev_out_dw")
    dys, dz, dbg, dcg, dhh, dgate, dnorm_g, dsw = even_mix_bwd(sv["proj"], sv["y2"], w["norm_g"], w["sc_w"], dyab,
                                                               rows)
    dxs, dbc, ddt, dcs, ddsk = ssd_bwd(sv["u"], sv["dt"], sv["cs"], w["dsk_row"], sv["hs"], dys, s)
    ddtraw, dbias_row, dalog_row = ssd_pre_bwd(sv["proj"], w["bias_row"], w["alog_row"], ddt, dcs, s)
    (dxbc0, dxbc1), dconv_w, dconv_b = conv_bwd(sv["proj"], sv["pre"], (dxs, dbc), w["conv_w"], rows)
    pieces = [dz, dxbc0, dxbc1, dbg, dcg, dhh, dgate]
    dx0 = matmul_pieces_nt(pieces + [ddtraw], w["w_in_p"], "ev_proj_dx", add=dres)
    dws = [matmul(p, sv["xb"], "tn", F32, f"ev_proj_dw{i}") for i, p in enumerate(pieces)]
    dw_dt = matmul(ddtraw, sv["xb"], "tn", F32, "ev_proj_dw_dt")
    dw_in = jnp.concatenate(dws[:3] + [dw_dt[:2 * SSD_HEADS]] + dws[3:], axis=0)
    g = dict(ev_w_in=dw_in, ev_conv_w=dconv_w, ev_conv_b=dconv_b,
             ev_a_log=_head_unrow(dalog_row), ev_dt_bias=_head_unrow(dbias_row),
             ev_d_skip=_head_unrow(ddsk[0] + ddsk[1]), ev_norm_g=dnorm_g, ev_sc_conv_w=dsw[:3],
             ev_w_out=dw_out, ev_ln_g=dln_g, ev_ln_b=dln_b)
    return dx0, g


def even_weights(ev_w_in, ev_conv_w, ev_conv_b, ev_a_log, ev_dt_bias, ev_d_skip, ev_norm_g, ev_sc_conv_w,
                 ev_w_out, ev_ln_g, ev_ln_b):
    return dict(w_in_p=pad_even_w_in(ev_w_in).astype(MATMUL_DTYPE), conv_w=ev_conv_w, conv_b=ev_conv_b,
                alog_row=_head_row(ev_a_log), bias_row=_head_row(ev_dt_bias), dsk_row=_head_row(ev_d_skip),
                norm_g=ev_norm_g, sc_w=ev_sc_conv_w, w_out=ev_w_out.astype(MATMUL_DTYPE), ln_g=ev_ln_g,
                ln_b=ev_ln_b)


HEAD_BLK = LANES
ROPE_LO = MLA_NOPE
ROPE_HALF = MLA_ROPE // 2


def _rms(v, g):
    return v * lax.rsqrt(jnp.mean(v * v, axis=-1, keepdims=True) + EPS) * g


def latent_norm_fwd(proj, gq, gkv, rows):
    def fn(cq, ckv, gq_, gkv_):
        return _rms(cq, gq_), _rms(ckv, gkv_)

    return rows_call("od_norm_fwd", rows, fn, [proj, proj, gq, gkv],
                     [rows.tile(MLA_Q_RANK, O_CQ), rows.tile(MLA_KV_RANK, O_CKV), Rows.full((1, MLA_Q_RANK)),
                      Rows.full((1, MLA_KV_RANK))],
                     [(MLA_Q_RANK, MATMUL_DTYPE), (MLA_KV_RANK, MATMUL_DTYPE)])


def latent_norm_bwd(proj, gq, gkv, dcqn, dckvn, rows):
    def fn(cq, ckv, gq_, gkv_, d1, d2):
        _, vjp = jax.vjp(_rms, cq, gq_)
        dcq, dgq = vjp(d1)
        _, vjp2 = jax.vjp(_rms, ckv, gkv_)
        dckv, dgkv = vjp2(d2)
        return dcq, dckv, dgq, dgkv

    return rows_call("od_norm_bwd", rows, fn, [proj, proj, gq, gkv, dcqn, dckvn],
                     [rows.tile(MLA_Q_RANK, O_CQ), rows.tile(MLA_KV_RANK, O_CKV), Rows.full((1, MLA_Q_RANK)),
                      Rows.full((1, MLA_KV_RANK)), rows.tile(MLA_Q_RANK), rows.tile(MLA_KV_RANK)],
                     [(MLA_Q_RANK, F32), (MLA_KV_RANK, F32)], [(1, MLA_Q_RANK), (1, MLA_KV_RANK)])


def rope_rows():
    lane = np.arange(LANES)
    inv = ROPE_THETA ** (-jnp.arange(ROPE_HALF, dtype=F32) / ROPE_HALF)
    on = (lane >= ROPE_LO) & (lane < ROPE_LO + MLA_ROPE)
    freq = jnp.where(on, inv[(lane - ROPE_LO) % ROPE_HALF], 0.0).reshape(1, LANES).astype(F32)
    sign = np.where(on, np.where(lane < ROPE_LO + ROPE_HALF, -1.0, 1.0), 0.0).reshape(1, LANES).astype(np.float32)
    return freq, jnp.asarray(sign)


def _rot_tables(pos, freq, sign):
    ang = pos.astype(F32) * freq
    return jnp.cos(ang), jnp.sin(ang) * sign


def _swap_halves(v):
    lane = lax.broadcasted_iota(jnp.int32, (1, LANES), 1)
    return jnp.where(lane < ROPE_LO + ROPE_HALF, pltpu.roll(v, LANES - ROPE_HALF, 1), pltpu.roll(v, ROPE_HALF, 1))


def rope_fwd(qp, kvp, proj, pos, freq, sign, rows):
    def fn(q, k, kr, v, p, f, sg):
        c, sn = _rot_tables(p, f, sg)
        rk = kr * c + _swap_halves(kr) * sn
        one = (lax.broadcasted_iota(jnp.int32, (v.shape[0], HEAD_BLK - MLA_V), 1) == 0).astype(F32)
        qs, ks, vs = [], [], []
        for h in range(MLA_HEADS):
            qh = q[:, h * HEAD_BLK:(h + 1) * HEAD_BLK]
            qs.append((qh * c + _swap_halves(qh) * sn) * QSCALE)
            ks.append(k[:, h * HEAD_BLK:(h + 1) * HEAD_BLK] + rk)
            vs += [v[:, h * MLA_V:(h + 1) * MLA_V], one]
        return jnp.concatenate(qs, axis=1), jnp.concatenate(ks, axis=1), jnp.concatenate(vs, axis=1)

    w = MLA_HEADS * HEAD_BLK
    return rows_call("od_rope_fwd", rows, fn, [qp, kvp, proj, kvp, pos, freq, sign],
                     [rows.tile(w), rows.tile(w, 0), rows.tile(LANES, O_KR), rows.tile(MLA_WIDTH, w),
                      rows.tile(1), Rows.full((1, LANES)), Rows.full((1, LANES))],
                     [(w, MATMUL_DTYPE), (w, MATMUL_DTYPE), (w, MATMUL_DTYPE)])


def rope_bwd(dq, dk, dv, pos, freq, sign, rows):
    def fn(dq_, dk_, dv_, p, f, sg):
        c, sn = _rot_tables(p, f, sg)
        on = jnp.abs(sg)
        outs, dks, dvs, dkr = [], [], [], jnp.zeros((dq_.shape[0], LANES), F32)
        for h in range(MLA_HEADS):
            g = dq_[:, h * HEAD_BLK:(h + 1) * HEAD_BLK] * ATTN_SCALE
            outs.append(g * c + _swap_halves(g * sn) * on)
            gk = dk_[:, h * HEAD_BLK:(h + 1) * HEAD_BLK] * LN2
            dks.append(gk)
            dkr = dkr + gk * c + _swap_halves(gk * sn) * on
            dvs.append(dv_[:, h * HEAD_BLK:h * HEAD_BLK + MLA_V])
        return jnp.concatenate(outs, axis=1), jnp.concatenate(dks + dvs, axis=1), dkr

    w = MLA_HEADS * HEAD_BLK
    return rows_call("od_rope_bwd", rows, fn, [dq, dk, dv, pos, freq, sign],
                     [rows.tile(w), rows.tile(w), rows.tile(w), rows.tile(1), Rows.full((1, LANES)),
                      Rows.full((1, LANES))],
                     [(w, MATMUL_DTYPE), (w + MLA_WIDTH, MATMUL_DTYPE), (LANES, F32)])


_PAIRS = MLA_HEADS // 2
ATT_TQ = 512
ATT_TK = 4096
ATT_BWD_TQ = 1024
ATT_BWD_TK = 1024


def _att_tiles(s, backward=False):
    if backward:
        return min(ATT_BWD_TQ, s), min(ATT_BWD_TK, s)
    return min(ATT_TQ, s), min(ATT_TK, s)


def attention_fwd(qcat, kcat, vcat, s):
    tq, tk = _att_tiles(s)
    nq, nk = s // tq, s // tk

    def body(q_ref, k_ref, v_ref, o_ref, lse_ref, p_ref, mt_ref, m_ref, acc_ref):
        kk = pl.program_id(2)
        half = lax.broadcasted_iota(jnp.int32, (1, LANES), 1) < MLA_V

        @pl.when(kk == 0)
        def _():
            m_ref[...] = jnp.full(m_ref.shape, -jnp.inf, F32)
            acc_ref[...] = jnp.zeros(acc_ref.shape, F32)

        sl = [slice(hh * HEAD_BLK, (hh + 1) * HEAD_BLK) for hh in range(2)]
        sc = [_mm(q_ref[:, sl[hh]], k_ref[:, sl[hh]], _NT) for hh in range(2)]
        for hh in range(2):
            m_prev = m_ref[hh]
            m_new = jnp.maximum(m_prev, jnp.max(sc[hh], axis=1, keepdims=True))
            p = jnp.exp2(sc[hh] - m_new[:, :1]).astype(p_ref.dtype)
            p_ref[hh] = p
            acc_ref[hh] = acc_ref[hh] * jnp.exp2(m_prev - m_new) + _mm(p, v_ref[:, sl[hh]])
            m_ref[hh] = m_new
        mt_ref[...] = jnp.where(half, m_ref[0], m_ref[1])

        @pl.when(kk == nk - 1)
        def _():
            l0, l1 = acc_ref[0][:, MLA_V:MLA_V + 1], acc_ref[1][:, MLA_V:MLA_V + 1]
            o_ref[...] = jnp.where(half, acc_ref[0] / l0, pltpu.roll(acc_ref[1] / l1, MLA_V, 1))
            lse_ref[...] = jnp.where(half, m_ref[0] + jnp.log2(l0), m_ref[1] + jnp.log2(l1))

    return pl.pallas_call(
        body, name="od_attn_fwd", grid=(_PAIRS, nq, nk),
        in_specs=[pl.BlockSpec((tq, 2 * HEAD_BLK), lambda p, i, kk: (i, p)),
                  pl.BlockSpec((tk, 2 * HEAD_BLK), lambda p, i, kk: (kk, p)),
                  pl.BlockSpec((tk, 2 * HEAD_BLK), lambda p, i, kk: (kk, p))],
        out_specs=[pl.BlockSpec((tq, LANES), lambda p, i, kk: (i, p)),
                   pl.BlockSpec((None, tq, LANES), lambda p, i, kk: (p, i, 0)),
                   pl.BlockSpec((2, tq, tk), lambda p, i, kk: (p, i, kk)),
                   pl.BlockSpec((None, None, tq, LANES), lambda p, i, kk: (p, kk, i, 0))],
        out_shape=[jax.ShapeDtypeStruct((s, MLA_WIDTH), F32), jax.ShapeDtypeStruct((_PAIRS, s, LANES), F32),
                   jax.ShapeDtypeStruct((MLA_HEADS, s, s), MATMUL_DTYPE),
                   jax.ShapeDtypeStruct((_PAIRS, nk, s, LANES), F32)],
        scratch_shapes=[pltpu.VMEM((2, tq, LANES), F32)] * 2,
        compiler_params=_params(("parallel", "parallel", "arbitrary")),
    )(qcat, kcat, vcat)


def attention_bwd(qcat, kcat, vcat, o, lse, pst, mt, do, s):
    tq, tk = _att_tiles(s, backward=True)
    nq, nk = s // tq, s // tk
    per_fwd_tile = _att_tiles(s)[1] // tk
    assert per_fwd_tile * tk == _att_tiles(s)[1]

    def body(q_ref, k_ref, v_ref, do_ref, o_ref, lse_ref, p_ref, mt_ref, dq_ref, dk_ref, dv_ref):
        kk, i = pl.program_id(1), pl.program_id(2)
        lane = lax.broadcasted_iota(jnp.int32, (1, LANES), 1)
        half = lane < MLA_V
        c = jnp.exp2(mt_ref[...] - lse_ref[...])
        do_p = do_ref[...] * c
        prod = do_p * o_ref[...]
        rows_i = pl.ds(pl.multiple_of(i * tq, tq), tq)
        for hh in range(2):
            sl = slice(hh * HEAD_BLK, (hh + 1) * HEAD_BLK)
            mine = half if hh == 0 else jnp.logical_not(half)
            delta = jnp.sum(jnp.where(mine, prod, 0.0), axis=1, keepdims=True)
            do_h = jnp.where(half, do_p if hh == 0 else pltpu.roll(do_p, MLA_V, 1), 0.0)
            p = p_ref[hh]
            ds = p.astype(F32) * (_mm(do_h, v_ref[:, sl], _NT) - delta)
            dv_h, dk_h, dq_h = _mm(p, do_h, _TN), _mm(ds, q_ref[:, sl], _TN), _mm(ds, k_ref[:, sl])

            @pl.when(i == 0)
            def _():
                dv_ref[:, sl] = dv_h
                dk_ref[:, sl] = dk_h

            @pl.when(i != 0)
            def _():
                dv_ref[:, sl] += dv_h
                dk_ref[:, sl] += dk_h

            @pl.when(kk == 0)
            def _():
                dq_ref[rows_i, sl] = dq_h

            @pl.when(kk != 0)
            def _():
                dq_ref[rows_i, sl] += dq_h

    w = MLA_HEADS * HEAD_BLK
    return pl.pallas_call(
        body, name="od_attn_bwd", grid=(_PAIRS, nk, nq),
        in_specs=[pl.BlockSpec((tq, 2 * HEAD_BLK), lambda p, kk, i: (i, p)),
                  pl.BlockSpec((tk, 2 * HEAD_BLK), lambda p, kk, i: (kk, p)),
                  pl.BlockSpec((tk, 2 * HEAD_BLK), lambda p, kk, i: (kk, p)),
                  pl.BlockSpec((tq, LANES), lambda p, kk, i: (i, p)),
                  pl.BlockSpec((tq, LANES), lambda p, kk, i: (i, p)),
                  pl.BlockSpec((None, tq, LANES), lambda p, kk, i: (p, i, 0)),
                  pl.BlockSpec((2, tq, tk), lambda p, kk, i: (p, i, kk)),
                  pl.BlockSpec((None, None, tq, LANES), lambda p, kk, i: (p, kk // per_fwd_tile, i, 0))],
        out_specs=[pl.BlockSpec((s, 2 * HEAD_BLK), lambda p, kk, i: (0, p), pipeline_mode=pl.Buffered(1)),
                   pl.BlockSpec((tk, 2 * HEAD_BLK), lambda p, kk, i: (kk, p)),
                   pl.BlockSpec((tk, 2 * HEAD_BLK), lambda p, kk, i: (kk, p))],
        out_shape=[jax.ShapeDtypeStruct((s, w), F32)] * 3,
        compiler_params=_params(("parallel", "arbitrary", "arbitrary")),
    )(qcat, kcat, vcat, do, o, lse, pst, mt)


def _pool_counts(n_rows, first_row, s, w):
    pos = first_row + lax.broadcasted_iota(jnp.int32, (n_rows, 1), 0)
    lo = jnp.clip(pos - w // 2, 0, s)
    hi = jnp.clip(pos + w - w // 2, 0, s)
    return jnp.maximum(hi - lo, 1).astype(F32)


def _window_sum(e, levels, mirrored):
    n = e.shape[0]
    acc = e + pltpu.roll(e, (n - 1) if mirrored else 1, 0)
    step = 1
    for _ in range(levels - 1):
        acc = pltpu.roll(acc, step, 0) + pltpu.roll(acc, n - step, 0)
        step *= 2
    return acc


def _pooled(ue, s, t):
    first = pl.program_id(0) * t
    outs = []
    for gi, w in enumerate(POOL_WINDOWS):
        eg = ue[:, gi * POOL_GROUP:(gi + 1) * POOL_GROUP]
        sm = _window_sum(eg, gi + 1, False)[HALO:HALO + t]
        outs.append(sm / _pool_counts(t, first, s, w) - eg[HALO:HALO + t])
    return outs


def odd_mix_fwd(proj, o, pool_w, pool_scale, rows):
    def fn(o_, gc, up, u, un, gd, pw, ps):
        pooled = _pooled(_ext(up, u, un, rows.n), rows.s, rows.t)
        lin = jnp.concatenate([_mm(pooled[g], pw[g]) for g in range(len(POOL_WINDOWS))], axis=1)
        return jnp.concatenate([o_ * _silu(gc), lin * ps * _silu(gd)], axis=1)

    w = POOL_WIDTH
    return rows_call("od_mix_fwd", rows, fn, [o, proj, proj, proj, proj, proj, pool_w, pool_scale],
                     [rows.tile(w), rows.tile(w, O_GC)] + rows.halo(w, O_UD)
                     + [rows.tile(w, O_GD), Rows.full((4, POOL_GROUP, POOL_GROUP)), Rows.full((1, w))],
                     [(2 * w, MATMUL_DTYPE)])[0]


def odd_mix_bwd(proj, o, pool_w, pool_scale, dycd, rows):
    w = POOL_WIDTH
    ng = len(POOL_WINDOWS)

    def fn(o_, gc, up, u, un, gdp, gd, gdn, pw, ps, dyc, dydp, dyd, dydn):
        n, t, s = rows.n, rows.t, rows.s
        dyc = dyc.astype(F32)
        do = dyc * _silu(gc)
        dgc = dyc * o_ * _dsilu(gc)
        pooled = _pooled(_ext(up, u, un, n), s, t)
        lin = jnp.concatenate([_mm(pooled[g], pw[g]) for g in range(ng)], axis=1)
        dydc = dyd.astype(F32)
        dgd = dydc * lin * ps * _dsilu(gd)
        dps = _rowsum(dydc * lin * _silu(gd))
        dlin_e = _ext(dydp, dyd, dydn, n) * ps * _silu(_ext(gdp, gd, gdn, n))
        first = pl.program_id(0) * t - HALO
        dus, dpws = [], []
        for g, win in enumerate(POOL_WINDOWS):
            sl = slice(g * POOL_GROUP, (g + 1) * POOL_GROUP)
            dle = dlin_e[:, sl]
            dpws.append(_mm(pooled[g], dle[HALO:HALO + t], _TN))
            dpe = _mm(dle, pw[g], _NT)
            gce = dpe / _pool_counts(t + 2 * HALO, first, s, win)
            dus.append(_window_sum(gce, g + 1, True)[HALO:HALO + t] - dpe[HALO:HALO + t])
        return do, dgc, jnp.concatenate(dus, axis=1), dgd, jnp.stack(dpws), dps

    return rows_call("od_mix_bwd", rows, fn,
                     [o, proj, proj, proj, proj, proj, proj, proj, pool_w, pool_scale, dycd, dycd, dycd, dycd],
                     [rows.tile(w), rows.tile(w, O_GC)] + rows.halo(w, O_UD) + rows.halo(w, O_GD)
                     + [Rows.full((ng, POOL_GROUP, POOL_GROUP)), Rows.full((1, w)), rows.tile(w, 0)]
                     + rows.halo(w, w),
                     [(w, F32)] * 4, [(ng, POOL_GROUP, POOL_GROUP), (1, w)])


def pad_odd_w_in(w):
    z = lambda n: jnp.zeros((w.shape[0], n), w.dtype)
    return jnp.concatenate([w[:, :384], z(ROPE_LO), w[:, 384:416], z(LANES - ROPE_LO - MLA_ROPE), w[:, 416:]], axis=1)


def unpad_odd_w_in_t(wpt):
    return jnp.concatenate([wpt[:384], wpt[O_KR + ROPE_LO:O_KR + ROPE_LO + MLA_ROPE], wpt[O_GC:]], axis=0)


def pad_w_uq(w):
    w3 = w.reshape(MLA_Q_RANK, MLA_HEADS, MLA_NOPE + MLA_ROPE)
    w3 = jnp.pad(w3, ((0, 0), (0, 0), (0, HEAD_BLK - MLA_NOPE - MLA_ROPE)))
    return w3.reshape(MLA_Q_RANK, MLA_HEADS * HEAD_BLK)


def unpad_w_uq(wp):
    return wp.reshape(MLA_Q_RANK, MLA_HEADS, HEAD_BLK)[..., :MLA_NOPE + MLA_ROPE].reshape(MLA_Q_RANK, -1)


def pad_w_ukv(w):
    w3 = w.reshape(MLA_KV_RANK, MLA_HEADS, MLA_NOPE + MLA_V)
    kp = jnp.pad(w3[..., :MLA_NOPE], ((0, 0), (0, 0), (0, HEAD_BLK - MLA_NOPE)))
    return jnp.concatenate([kp.reshape(MLA_KV_RANK, -1), w3[..., MLA_NOPE:].reshape(MLA_KV_RANK, -1)], axis=1)


def unpad_w_ukv(wp):
    kp = wp[:, :MLA_HEADS * HEAD_BLK].reshape(MLA_KV_RANK, MLA_HEADS, HEAD_BLK)[..., :MLA_NOPE]
    vp = wp[:, MLA_HEADS * HEAD_BLK:].reshape(MLA_KV_RANK, MLA_HEADS, MLA_V)
    return jnp.concatenate([kp, vp], axis=-1).reshape(MLA_KV_RANK, -1)


def odd_weights(od_w_in, od_q_norm_g, od_w_uq, od_kv_norm_g, od_w_ukv, od_pool_w, od_pool_scale, od_w_out,
                od_ln_g, od_ln_b):
    freq, sign = rope_rows()
    return dict(w_in_p=pad_odd_w_in(od_w_in).astype(MATMUL_DTYPE), gq=od_q_norm_g, gkv=od_kv_norm_g,
                w_uq_p=pad_w_uq(od_w_uq).astype(MATMUL_DTYPE), w_ukv_p=pad_w_ukv(od_w_ukv).astype(MATMUL_DTYPE),
                pool_w=od_pool_w, pool_scale=od_pool_scale, w_out=od_w_out.astype(MATMUL_DTYPE), ln_g=od_ln_g,
                ln_b=od_ln_b, freq=freq, sign=sign)


def odd_layer_loss(x, pos, target, w, rows):
    s = rows.s
    proj = matmul(x, w["w_in_p"], "nn", F32, "od_proj")
    cqn, ckvn = latent_norm_fwd(proj, w["gq"], w["gkv"], rows)
    qp = matmul(cqn, w["w_uq_p"], "nn", F32, "od_q_up")
    kvp = matmul(ckvn, w["w_ukv_p"], "nn", F32, "od_kv_up")
    qcat, kcat, v = rope_fwd(qp, kvp, proj, pos, w["freq"], w["sign"], rows)
    o, lse, pst, mt = attention_fwd(qcat, kcat, v, s)
    ycd = odd_mix_fwd(proj, o, w["pool_w"], w["pool_scale"], rows)
    h = matmul(ycd, w["w_out"], "nn", F32, "od_out")
    dres, dh, dln_g, dln_b, loss_lanes = final_ln_loss(x, h, w["ln_g"], w["ln_b"], target, rows)
    dycd = matmul(dh, w["w_out"], "nt", F32, "od_out_dx")
    dw_out = matmul(ycd, dh, "tn", F32, "od_out_dw")
    do, dgc, dud, dgd, dpool_w, dpool_scale = odd_mix_bwd(proj, o, w["pool_w"], w["pool_scale"], dycd, rows)
    dq, dk, dv = attention_bwd(qcat, kcat, v, o, lse, pst, mt, do, s)
    dqp, dkvp, dkr = rope_bwd(dq, dk, dv, pos, w["freq"], w["sign"], rows)
    dcqn = matmul(dqp, w["w_uq_p"], "nt", F32, "od_q_up_dx")
    dw_uq = unpad_w_uq(matmul(cqn, dqp, "tn", F32, "od_q_up_dw"))
    dckvn = matmul(dkvp, w["w_ukv_p"], "nt", F32, "od_kv_up_dx")
    dw_ukv = unpad_w_ukv(matmul(ckvn, dkvp, "tn", F32, "od_kv_up_dw"))
    dcq, dckv, dgq, dgkv = latent_norm_bwd(proj, w["gq"], w["gkv"], dcqn, dckvn, rows)
    dproj = jnp.concatenate([dcq, dckv, dkr, dgc, dud, dgd], axis=1).astype(MATMUL_DTYPE)
    dx = matmul(dproj, w["w_in_p"], "nt", F32, "od_proj_dx", add=dres)
    dw_in = unpad_odd_w_in_t(matmul(dproj, x, "tn", F32, "od_proj_dw"))
    g = dict(od_w_in=dw_in, od_q_norm_g=dgq, od_w_uq=dw_uq, od_kv_norm_g=dgkv, od_w_ukv=dw_ukv,
             od_pool_w=dpool_w, od_pool_scale=dpool_scale, od_w_out=dw_out, od_ln_g=dln_g, od_ln_b=dln_b)
    return loss_lanes, dx, g


_MESH = pl.DeviceIdType.MESH
_ANY = pl.BlockSpec(memory_space=pl.ANY)
N_CHIPS = 4


def _push_call(name, ins, out_shapes, plan, n_remote, n_local):
    n_in, n_out = len(ins), len(out_shapes)

    def body(*refs):
        in_refs, out_refs = refs[:n_in], refs[n_in:n_in + n_out]
        send_sems, recv_sems, local_sems = refs[n_in + n_out:]
        x, y, c = lax.axis_index("x"), lax.axis_index("y"), lax.axis_index("c")
        remote, local = plan(in_refs, out_refs, x, y, c)
        assert len(remote) == n_remote and len(local) == n_local
        sends = [pltpu.make_async_remote_copy(src_ref=s, dst_ref=d, send_sem=send_sems.at[k], recv_sem=recv_sems.at[k],
                                              device_id=dev, device_id_type=_MESH)
                 for k, (s, d, dev, _) in enumerate(remote)]
        recvs = [pltpu.make_async_remote_copy(src_ref=s, dst_ref=land, send_sem=send_sems.at[k],
                                              recv_sem=recv_sems.at[k], device_id=dev, device_id_type=_MESH)
                 for k, (s, _, dev, land) in enumerate(remote)]
        locs = [pltpu.make_async_copy(s, d, local_sems.at[k]) for k, (s, d) in enumerate(local)]
        for cp in sends + locs:
            cp.start()
        for cp in recvs:
            cp.wait_recv()
        for cp in sends:
            cp.wait_send()
        for cp in locs:
            cp.wait()

    return pl.pallas_call(
        body, name=name, in_specs=[_ANY] * n_in, out_specs=[_ANY] * n_out, out_shape=list(out_shapes),
        scratch_shapes=[pltpu.SemaphoreType.DMA((n_remote,)), pltpu.SemaphoreType.DMA((n_remote,)),
                        pltpu.SemaphoreType.DMA((max(n_local, 1),))],
    )(*ins)


def _other_chips(x, y):
    return [(1 - x, y), (x, 1 - y), (1 - x, 1 - y)]


def chips_allgather(bufs):
    def plan(in_refs, out_refs, x, y, c):
        me = 2 * x + y
        remote, local = [], []
        for src, out in zip(in_refs, out_refs):
            for (px, py) in _other_chips(x, y):
                remote.append((src, out.at[me], (px, py, c), out.at[2 * px + py]))
            local.append((src, out.at[me]))
        return remote, local

    shapes = [jax.ShapeDtypeStruct((N_CHIPS,) + b.shape, b.dtype) for b in bufs]
    return _push_call("weights_allgather", bufs, shapes, plan, 3 * len(bufs), len(bufs))


def sibling_send(buf, name):
    def plan(in_refs, out_refs, x, y, c):
        return [(in_refs[0], out_refs[0], (x, y, 1 - c), out_refs[0])], []

    return _push_call(name, [buf], [jax.ShapeDtypeStruct(buf.shape, buf.dtype)], plan, 1, 0)[0]


def chips_scatter(buf):
    def plan(in_refs, out_refs, x, y, c):
        me = 2 * x + y
        src, out = in_refs[0], out_refs[0]
        remote = [(src.at[2 * px + py], out.at[me], (px, py, c), out.at[2 * px + py]) for (px, py) in _other_chips(x, y)]
        return remote, [(src.at[me], out.at[me])]

    return _push_call("grads_scatter", [buf], [jax.ShapeDtypeStruct(buf.shape, buf.dtype)], plan, 3, 1)[0]


PACK_W = 1024
PACK_BLK = 128


def _ew_call(name, fn, ins, in_specs, out_shape, out_spec, n_out, steps):
    def body(*refs):
        res = fn(*[r[...] for r in refs[:len(ins)]])
        if not isinstance(res, (tuple, list)):
            res = (res,)
        for r, v in zip(refs[len(ins):], res):
            r[...] = v

    return pl.pallas_call(
        body, name=name, grid=(steps,), in_specs=in_specs, out_specs=[out_spec] * n_out,
        out_shape=[jax.ShapeDtypeStruct(out_shape, F32)] * n_out,
        compiler_params=_params(("parallel",)),
    )(*ins)


def add2(a, b, name):
    n, r, w = a.shape
    blk = pl.BlockSpec((PACK_BLK, w), lambda i: (i, 0))
    out = _ew_call(name, lambda u, v: u + v, [a.reshape(n * r, w), b.reshape(n * r, w)], [blk, blk], (n * r, w), blk,
                   1, n * r // PACK_BLK)[0]
    return out.reshape(a.shape)


def sum_chips(buf):
    _, r, w = buf.shape
    specs = [pl.BlockSpec((None, PACK_BLK, w), lambda i, q=q: (q, i, 0)) for q in range(N_CHIPS)]
    return _ew_call("grads_sum", lambda a, b, c, d: ((a + b) + c) + d, [buf] * N_CHIPS, specs, (r, w),
                    pl.BlockSpec((PACK_BLK, w), lambda i: (i, 0)), 1, r // PACK_BLK)[0]


def adamw(w, g, m, v, name):
    def fn(w_, g_, m_, v_):
        m2 = ADAM_B1 * m_ + (1.0 - ADAM_B1) * g_
        v2 = ADAM_B2 * v_ + (1.0 - ADAM_B2) * jnp.square(g_)
        m_hat = m2 / (1.0 - ADAM_B1 ** ADAM_STEP)
        v_hat = v2 / (1.0 - ADAM_B2 ** ADAM_STEP)
        return -ADAM_LR * (m_hat / (jnp.sqrt(v_hat) + ADAM_EPS) + ADAM_WD * w_), m2, v2

    r, c = w.shape
    br = r if r <= 512 else _pick(r, (256, 128, 64, 32, 16, 8))
    blk = pl.BlockSpec((br, c), lambda i: (i, 0))
    return _ew_call(name, fn, [w, g, m, v], [blk] * 4, (r, c), blk, 3, r // br)


WEIGHTS = (
    ("ev_w_in", (1, 1024, 7200), 2), ("ev_conv_w", (1, 4, 2048), 2), ("ev_conv_b", (1, 2048), None),
    ("ev_a_log", (1, 2, 16), None), ("ev_dt_bias", (1, 2, 16), None), ("ev_d_skip", (1, 2, 16), None),
    ("ev_norm_g", (1, 1024), None), ("ev_sc_conv_w", (1, 3, 1024), 2), ("ev_w_out", (1, 2048, 1024), 1),
    ("ev_ln_g", (1, 1024), None), ("ev_ln_b", (1, 1024), None), ("od_w_in", (1, 1024, 1952), 2),
    ("od_q_norm_g", (1, 256), 1), ("od_w_uq", (1, 256, 768), 2), ("od_kv_norm_g", (1, 128), None),
    ("od_w_ukv", (1, 128, 1024), 2), ("od_pool_w", (1, 4, 128, 128), None), ("od_pool_scale", (1, 512), 1),
    ("od_w_out", (1, 1024, 1024), 1), ("od_ln_g", (1, 1024), 1), ("od_ln_b", (1, 1024), 1),
)
BIG = ("ev_w_in", "ev_w_out", "od_w_in", "od_w_uq", "od_w_ukv", "od_w_out")


def _block_shape(shape, axis):
    if axis is None:
        return tuple(shape)
    return tuple(d // N_CHIPS if i == axis else d for i, d in enumerate(shape))


def _pack(arrs, quantum):
    flat = jnp.concatenate([a.reshape(-1) for a in arrs])
    n = flat.shape[0]
    padded = -(-n // quantum) * quantum
    return jnp.concatenate([flat, jnp.zeros((padded - n,), flat.dtype)]).reshape(-1, LANES)


def _unpack(flat, shapes):
    out, off = [], 0
    for sh in shapes:
        n = int(np.prod(sh))
        out.append(flat[off:off + n].reshape(sh))
        off += n
    return out


def gather_weights(local):
    sharded = [(n, sh, ax) for (n, sh, ax) in WEIGHTS if ax is not None]
    big = [(n, sh, ax) for (n, sh, ax) in sharded if n in BIG]
    small = [(n, sh, ax) for (n, sh, ax) in sharded if n not in BIG]
    pb = _pack([local[n].astype(MATMUL_DTYPE) for n, _, _ in big], 2 * SUBLANES * LANES)
    ps = _pack([local[n] for n, _, _ in small], SUBLANES * LANES)
    gb, gs = chips_allgather([pb, ps])
    full = {n: local[n] for (n, sh, ax) in WEIGHTS if ax is None}
    for group, g in ((big, gb), (small, gs)):
        parts = [_unpack(g[q].reshape(-1), [_block_shape(sh, ax) for _, sh, ax in group]) for q in range(N_CHIPS)]
        for i, (n, sh, ax) in enumerate(group):
            full[n] = jnp.concatenate([parts[q][i] for q in range(N_CHIPS)], axis=ax)
    return full


WIDE = (("ev_w_in", True), ("ev_w_out", False), ("od_w_in", True), ("od_w_out", False))


def reduce_and_update(grads, local_w, local_m, local_v):
    c = lax.axis_index("c")
    wide_names = [n for n, _ in WIDE]
    tail = [(n, sh, ax) for (n, sh, ax) in WEIGHTS if n not in wide_names]
    tail_blocks = [_block_shape(sh, ax) for _, sh, ax in tail]
    n_tail = sum(int(np.prod(b)) for b in tail_blocks)
    wide_rows = [grads[n].shape[0] // N_CHIPS for n in wide_names]
    quantum = 2 * PACK_BLK
    total = -(-(sum(wide_rows) + -(-n_tail // PACK_W)) // quantum) * quantum
    tail_rows = total - sum(wide_rows)

    def tail_pack(pieces):
        flat = jnp.concatenate([p.reshape(-1) for p in pieces] + [jnp.zeros((tail_rows * PACK_W - n_tail,), F32)])
        return flat.reshape(tail_rows, PACK_W)

    packs = []
    for q in range(N_CHIPS):
        parts = [lax.slice_in_dim(grads[n], q * r, (q + 1) * r, axis=0) for n, r in zip(wide_names, wide_rows)]
        pieces = []
        for (n, sh, ax), bs in zip(tail, tail_blocks):
            g = grads[n].reshape(sh)
            pieces.append(g if ax is None else lax.slice_in_dim(g, q * bs[ax], (q + 1) * bs[ax], axis=ax))
        packs.append(jnp.concatenate(parts + [tail_pack(pieces)], axis=0))
    packs = jnp.stack(packs)
    rh = total // 2
    keep = lax.dynamic_slice_in_dim(packs, c * rh, rh, axis=1)
    give = lax.dynamic_slice_in_dim(packs, (1 - c) * rh, rh, axis=1)
    chip_half = add2(keep, sibling_send(give, "grads_to_sibling"), "grads_chip_sum")
    total_half = sum_chips(chips_scatter(chip_half))
    other_half = sibling_send(total_half, "grads_from_sibling")
    g_pack = jnp.concatenate([jnp.where(c == 0, total_half, other_half),
                              jnp.where(c == 0, other_half, total_half)], axis=0)
    outs = ({}, {}, {}, {})
    off = 0
    for (n, transposed), r in zip(WIDE, wide_rows):
        g = g_pack[off:off + r]
        off += r
        g = g.T if transposed else g
        shape = local_w[n].shape
        res = adamw(local_w[n].reshape(g.shape), g, local_m[n].reshape(g.shape), local_v[n].reshape(g.shape),
                    "adamw_" + n)
        for d, a in zip(outs, (g, *res)):
            d[n] = a.reshape(shape)
    g_tail = g_pack[off:]
    res = adamw(*[tail_pack([d[n] for n, _, _ in tail]) if d is not None else g_tail
                  for d in (local_w, None, local_m, local_v)], "adamw_small")
    for d, a in zip(outs, (g_tail, *res)):
        d.update(zip([n for n, _, _ in tail], _unpack(a.reshape(-1), tail_blocks)))
    return outs


ROW_TILE = 256


def kernel(x, positions, ev_w_in, ev_conv_w, ev_conv_b, ev_a_log, ev_dt_bias, ev_d_skip, ev_norm_g, ev_sc_conv_w, ev_w_out, ev_ln_g, ev_ln_b, od_w_in, od_q_norm_g, od_w_uq, od_kv_norm_g, od_w_ukv, od_pool_w, od_pool_scale, od_w_out, od_ln_g, od_ln_b, loss_target, m_ev_w_in, m_ev_conv_w, m_ev_conv_b, m_ev_a_log, m_ev_dt_bias, m_ev_d_skip, m_ev_norm_g, m_ev_sc_conv_w, m_ev_w_out, m_ev_ln_g, m_ev_ln_b, m_od_w_in, m_od_q_norm_g, m_od_w_uq, m_od_kv_norm_g, m_od_w_ukv, m_od_pool_w, m_od_pool_scale, m_od_w_out, m_od_ln_g, m_od_ln_b, v_ev_w_in, v_ev_conv_w, v_ev_conv_b, v_ev_a_log, v_ev_dt_bias, v_ev_d_skip, v_ev_norm_g, v_ev_sc_conv_w, v_ev_w_out, v_ev_ln_g, v_ev_ln_b, v_od_w_in, v_od_q_norm_g, v_od_w_uq, v_od_kv_norm_g, v_od_w_ukv, v_od_pool_w, v_od_pool_scale, v_od_w_out, v_od_ln_g, v_od_ln_b):
    names = [n for n, _, _ in WEIGHTS]
    local_w = dict(zip(names, (ev_w_in, ev_conv_w, ev_conv_b, ev_a_log, ev_dt_bias, ev_d_skip, ev_norm_g, ev_sc_conv_w, ev_w_out, ev_ln_g, ev_ln_b, od_w_in, od_q_norm_g, od_w_uq, od_kv_norm_g, od_w_ukv, od_pool_w, od_pool_scale, od_w_out, od_ln_g, od_ln_b)))
    local_m = dict(zip(names, (m_ev_w_in, m_ev_conv_w, m_ev_conv_b, m_ev_a_log, m_ev_dt_bias, m_ev_d_skip, m_ev_norm_g, m_ev_sc_conv_w, m_ev_w_out, m_ev_ln_g, m_ev_ln_b, m_od_w_in, m_od_q_norm_g, m_od_w_uq, m_od_kv_norm_g, m_od_w_ukv, m_od_pool_w, m_od_pool_scale, m_od_w_out, m_od_ln_g, m_od_ln_b)))
    local_v = dict(zip(names, (v_ev_w_in, v_ev_conv_w, v_ev_conv_b, v_ev_a_log, v_ev_dt_bias, v_ev_d_skip, v_ev_norm_g, v_ev_sc_conv_w, v_ev_w_out, v_ev_ln_g, v_ev_ln_b, v_od_w_in, v_od_q_norm_g, v_od_w_uq, v_od_kv_norm_g, v_od_w_ukv, v_od_pool_w, v_od_pool_scale, v_od_w_out, v_od_ln_g, v_od_ln_b)))
    s = x.shape[1]
    rows = Rows(s, min(ROW_TILE, s))
    f = gather_weights(local_w)
    ew = even_weights(f["ev_w_in"][0], f["ev_conv_w"][0], f["ev_conv_b"], f["ev_a_log"][0], f["ev_dt_bias"][0],
                      f["ev_d_skip"][0], f["ev_norm_g"], f["ev_sc_conv_w"][0], f["ev_w_out"][0], f["ev_ln_g"],
                      f["ev_ln_b"])
    ow = odd_weights(f["od_w_in"][0], f["od_q_norm_g"], f["od_w_uq"][0], f["od_kv_norm_g"], f["od_w_ukv"][0],
                     f["od_pool_w"][0], f["od_pool_scale"], f["od_w_out"][0], f["od_ln_g"], f["od_ln_b"])
    x1, saved = even_layer(x[0], ew, rows)
    loss_lanes, dx1, g_odd = odd_layer_loss(x1, positions.reshape(s, 1), loss_target[0], ow, rows)
    dx0, g_even = even_layer_bwd(dx1, ew, saved, rows)
    loss = lax.psum(jnp.sum(loss_lanes), ("x", "y", "c"))
    grad, delta, new_m, new_v = reduce_and_update({**g_even, **g_odd}, local_w, local_m, local_v)
    return (loss, dx0[None], *[grad[n] for n in names], *[delta[n] for n in names],
            *[new_m[n] for n in names], *[new_v[n] for n in names])
```

```python
import functools
import math

import jax
import jax.numpy as jnp
import numpy as np
from jax import lax
from jax.experimental import pallas as pl
from jax.experimental.pallas import tpu as pltpu

F32 = jnp.float32
BF16 = jnp.bfloat16
MATMUL_DTYPE = jnp.bfloat16

D_MODEL = 1024
DEPTH = 2
SSD_HEADS, SSD_HEAD_DIM, SSD_GROUPS, SSD_STATE, SSD_CHUNK = 16, 64, 4, 128, 128
SSD_INNER = SSD_HEADS * SSD_HEAD_DIM
SSD_XBC = SSD_INNER + 2 * SSD_GROUPS * SSD_STATE
SC_WIDTH = 1024
MLA_HEADS, MLA_Q_RANK, MLA_KV_RANK, MLA_NOPE, MLA_ROPE, MLA_V = 8, 256, 128, 64, 32, 64
MLA_WIDTH = MLA_HEADS * MLA_V
ROPE_THETA = 10000.0
ATTN_SCALE = (MLA_NOPE + MLA_ROPE) ** -0.5
QSCALE = ATTN_SCALE * math.log2(math.e)
LN2 = math.log(2.0)
POOL_WINDOWS = (2, 4, 8, 16)
POOL_GROUP = 128
POOL_WIDTH = POOL_GROUP * len(POOL_WINDOWS)
EPS = 1e-5
ALPHA = (2 * DEPTH) ** 0.25
EVEN_PROJ, ODD_PROJ = 7200, 1952
ADAM_LR, ADAM_B1, ADAM_B2, ADAM_EPS, ADAM_WD, ADAM_STEP = 0.001, 0.9, 0.999, 1e-08, 0.01, 10

LANES = 128
SUBLANES = 8
HALO = SUBLANES
VMEM_LIMIT = 56 * 1024 * 1024

EVEN_P = 7296
ODD_P = 2048
E_Z, E_XBC, E_BG, E_CG, E_H, E_GATE, E_DT = 0, 1024, 3072, 4096, 5120, 6144, 7168
O_CQ, O_CKV, O_KR, O_GC, O_UD, O_GD = 0, 256, 384, 512, 1024, 1536


def _params(sem=None):
    return pltpu.CompilerParams(dimension_semantics=sem, vmem_limit_bytes=VMEM_LIMIT)


def _mm(a, b, dims=(((1,), (0,)), ((), ()))):
    return lax.dot_general(a.astype(MATMUL_DTYPE), b.astype(MATMUL_DTYPE), dims, preferred_element_type=F32)


_NN = (((1,), (0,)), ((), ()))
_NT = (((1,), (1,)), ((), ()))
_TN = (((0,), (0,)), ((), ()))


def _silu(v):
    return v * jax.nn.sigmoid(v)


def _dsilu(v):
    s = jax.nn.sigmoid(v)
    return s * (1.0 + v * (1.0 - s))


def _pick(n, prefs):
    for p in prefs:
        if n % p == 0:
            return p
    return n


def matmul(a, b, mode, out_dtype, name, add=None, tm=None, tn=None, tk=None):
    if mode == "nn":
        (m, k), (k2, n) = a.shape, b.shape
    elif mode == "nt":
        (m, k), (n, k2) = a.shape, b.shape
    else:
        (k, m), (k2, n) = a.shape, b.shape
    assert k == k2, (a.shape, b.shape, mode)
    wide = (2432, 2048, 1536, 1024, 512, 256, 128)
    if mode == "tn":
        tm = tm or _pick(m, (2432, 2048, 1024, 512, 256, 128))
        tn = tn or _pick(n, wide)
        tk = tk or _pick(k, (512, 256, 128))
    else:
        tm = tm or _pick(m, (512, 256, 128))
        tn = tn or _pick(n, wide)
        tk = tk or _pick(k, wide)
    nk = k // tk
    dims = {"nn": _NN, "nt": _NT, "tn": _TN}[mode]

    def body(a_ref, b_ref, *rest):
        o_ref, acc_ref = rest[-2:]
        kk = pl.program_id(2)

        @pl.when(kk == 0)
        def _():
            acc_ref[...] = jnp.zeros_like(acc_ref) if add is None else rest[0][...].astype(F32)

        acc_ref[...] += _mm(a_ref[...], b_ref[...], dims)

        @pl.when(kk == nk - 1)
        def _():
            o_ref[...] = acc_ref[...].astype(o_ref.dtype)

    a_spec = {"nn": pl.BlockSpec((tm, tk), lambda i, j, kk: (i, kk)),
              "nt": pl.BlockSpec((tm, tk), lambda i, j, kk: (i, kk)),
              "tn": pl.BlockSpec((tk, tm), lambda i, j, kk: (kk, i))}[mode]
    b_spec = {"nn": pl.BlockSpec((tk, tn), lambda i, j, kk: (kk, j)),
              "nt": pl.BlockSpec((tn, tk), lambda i, j, kk: (j, kk)),
              "tn": pl.BlockSpec((tk, tn), lambda i, j, kk: (kk, j))}[mode]
    return pl.pallas_call(
        body, name=name, grid=(m // tm, n // tn, nk),
        in_specs=[a_spec, b_spec] + ([] if add is None else [pl.BlockSpec((tm, tn), lambda i, j, kk: (i, j))]),
        out_specs=pl.BlockSpec((tm, tn), lambda i, j, kk: (i, j)),
        out_shape=jax.ShapeDtypeStruct((m, n), out_dtype),
        scratch_shapes=[pltpu.VMEM((tm, tn), F32)],
        compiler_params=_params(("parallel", "parallel", "arbitrary")),
    )(*((a, b) if add is None else (a, b, add)))


class Rows:
    def __init__(self, s, t):
        assert s % t == 0 and t % HALO == 0
        self.s, self.t, self.n = s, t, s // t

    def tile(self, width, col=0, lead=None):
        cb = col // width
        assert col % width == 0
        if lead is None:
            return pl.BlockSpec((self.t, width), lambda i: (i, cb))
        return pl.BlockSpec((None, self.t, width), lambda i: (lead, i, cb))

    def prev(self, width, col=0, lead=None):
        cb, r = col // width, self.t // HALO
        if lead is None:
            return pl.BlockSpec((HALO, width), lambda i: (jnp.maximum(i * r - 1, 0), cb))
        return pl.BlockSpec((None, HALO, width), lambda i: (lead, jnp.maximum(i * r - 1, 0), cb))

    def next(self, width, col=0, lead=None):
        cb, r, last = col // width, self.t // HALO, self.s // HALO - 1
        if lead is None:
            return pl.BlockSpec((HALO, width), lambda i: (jnp.minimum((i + 1) * r, last), cb))
        return pl.BlockSpec((None, HALO, width), lambda i: (lead, jnp.minimum((i + 1) * r, last), cb))

    def halo(self, width, col=0, lead=None):
        return [self.prev(width, col, lead), self.tile(width, col, lead), self.next(width, col, lead)]

    @staticmethod
    def full(shape):
        nd = len(shape)
        return pl.BlockSpec(tuple(shape), lambda i: (0,) * nd)


def rows_call(name, rows, fn, ins, in_specs, row_outs, acc_outs=()):
    n_row = len(row_outs)

    def body(*refs):
        in_refs = refs[:len(ins)]
        out_refs = refs[len(ins):]
        res = fn(*[r[...] for r in in_refs])
        if not isinstance(res, (tuple, list)):
            res = (res,)
        for r, v in zip(out_refs[:n_row], res[:n_row]):
            r[...] = v.astype(r.dtype)
        if acc_outs:
            first = pl.program_id(0) == 0

            @pl.when(first)
            def _():
                for r, v in zip(out_refs[n_row:], res[n_row:]):
                    r[...] = v.astype(F32)

            @pl.when(jnp.logical_not(first))
            def _():
                for r, v in zip(out_refs[n_row:], res[n_row:]):
                    r[...] += v.astype(F32)

    out_shape = [jax.ShapeDtypeStruct((rows.s, w), dt) for (w, dt) in row_outs]
    out_specs = [rows.tile(w) for (w, dt) in row_outs]
    out_shape += [jax.ShapeDtypeStruct(tuple(sh), F32) for sh in acc_outs]
    out_specs += [Rows.full(sh) for sh in acc_outs]
    return pl.pallas_call(
        body, name=name, grid=(rows.n,), in_specs=list(in_specs), out_specs=out_specs, out_shape=out_shape,
        compiler_params=_params(("arbitrary",)),
    )(*ins)


def _edge_zero(prev, nxt, n_tiles):
    i = pl.program_id(0)
    prev = jnp.where(i == 0, jnp.zeros_like(prev), prev)
    nxt = jnp.where(i == n_tiles - 1, jnp.zeros_like(nxt), nxt)
    return prev, nxt


def _ext(prev, cur, nxt, n_tiles):
    prev, nxt = _edge_zero(prev, nxt, n_tiles)
    return jnp.concatenate([prev.astype(F32), cur.astype(F32), nxt.astype(F32)], axis=0)


def _shift(ext, off):
    n = ext.shape[0]
    t = n - 2 * HALO
    if off == 0:
        return ext[HALO:HALO + t]
    return pltpu.roll(ext, (-off) % n, 0)[HALO:HALO + t]


def _rowsum(v):
    return jnp.sum(v, axis=0, keepdims=True)


CONV_OFFS = (-2, -1, 0, 1)
SC_OFFS = (-1, 0, 1)


def conv_fwd(proj, conv_w, conv_b, rows):
    def fn(p, c, nx, w, b):
        e = _ext(p, c, nx, rows.n)
        pre = b
        for k, off in enumerate(CONV_OFFS):
            pre = pre + w[k:k + 1, :] * _shift(e, off)
        return pre, _silu(pre)

    outs = [rows_call(f"ev_conv_fwd{h}", rows, fn,
                      [proj, proj, proj, conv_w[:, h * 1024:(h + 1) * 1024], conv_b[:, h * 1024:(h + 1) * 1024]],
                      rows.halo(1024, E_XBC + h * 1024) + [Rows.full((4, 1024)), Rows.full((1, 1024))],
                      [(1024, F32), (1024, F32)]) for h in range(2)]
    return outs


def conv_bwd(proj, pre_h, du_h, conv_w, rows):
    res = []
    for h in range(2):
        def fn(*a):
            w = a[-1]
            xe = _ext(a[0], a[1], a[2], rows.n)
            pe = jnp.concatenate([a[3], a[4], a[5]], axis=0)
            g = a[6:-1]
            du = _ext(g[0], g[1], g[2], rows.n) + _ext(g[3], g[4], g[5], rows.n)
            dpre = du * _dsilu(pe)
            dx = jnp.zeros_like(a[1], dtype=F32)
            dws = []
            for k, off in enumerate(CONV_OFFS):
                dx = dx + w[k:k + 1, :] * _shift(dpre, -off)
                dws.append(_rowsum(_shift(dpre, 0) * _shift(xe, off)))
            dw = jnp.concatenate(dws + [jnp.zeros((4, dx.shape[1]), F32)], axis=0)
            return dx, dw, _rowsum(_shift(dpre, 0))

        gi = [du_h[h][0]] * 3 + [du_h[h][1]] * 3
        gs = rows.halo(1024) * 2
        res.append(rows_call(
            f"ev_conv_bwd{h}", rows, fn,
            [proj] * 3 + [pre_h[h]] * 3 + gi + [conv_w[:, h * 1024:(h + 1) * 1024]],
            rows.halo(1024, E_XBC + h * 1024) + rows.halo(1024) + gs + [Rows.full((4, 1024))],
            [(1024, MATMUL_DTYPE)], [(8, 1024), (1, 1024)]))
    dconv_w = jnp.concatenate([res[0][1][:4], res[1][1][:4]], axis=1)
    dconv_b = jnp.concatenate([res[0][2], res[1][2]], axis=1)
    return [res[0][0], res[1][0]], dconv_w, dconv_b


def _head_row(p):
    return jnp.concatenate([p.reshape(1, 2 * SSD_HEADS), jnp.zeros((1, LANES - 2 * SSD_HEADS), F32)], axis=1)


def _head_unrow(r):
    return r[:, :2 * SSD_HEADS].reshape(2, SSD_HEADS)


def _ssd_pre_fn(dtraw, bias_row, alog_row):
    q = dtraw.shape[0]
    dt = jax.nn.softplus(dtraw + bias_row)
    da = dt * (-jnp.exp(alog_row))
    li = lax.broadcasted_iota(jnp.int32, (q, q), 0)
    si = lax.broadcasted_iota(jnp.int32, (q, q), 1)
    tril = (li >= si).astype(F32)
    csf = lax.dot_general(tril, da, _NN, precision=lax.Precision.HIGHEST, preferred_element_type=F32)
    tot = jnp.sum(da, axis=0, keepdims=True)
    lane = lax.broadcasted_iota(jnp.int32, (1, LANES), 1)
    cs = jnp.where(lane < SSD_HEADS, csf, tot - csf + da)
    return dt, cs


def ssd_pre(proj, bias_row, alog_row, s):
    rows = Rows(s, SSD_CHUNK)
    return rows_call("ev_ssd_pre", rows, _ssd_pre_fn, [proj, bias_row, alog_row],
                     [rows.tile(LANES, E_DT), Rows.full((1, LANES)), Rows.full((1, LANES))],
                     [(LANES, F32), (LANES, F32)])


def ssd_pre_bwd(proj, bias_row, alog_row, ddt, dcs, s):
    rows = Rows(s, SSD_CHUNK)

    def fn(dtraw, b, al, g0, g1, c0, c1):
        _, vjp = jax.vjp(_ssd_pre_fn, dtraw, b, al)
        return vjp((g0 + g1, c0 + c1))

    return rows_call("ev_ssd_pre_bwd", rows, fn, [proj, bias_row, alog_row, ddt[0], ddt[1], dcs[0], dcs[1]],
                     [rows.tile(LANES, E_DT), Rows.full((1, LANES)), Rows.full((1, LANES))] + [rows.tile(LANES)] * 4,
                     [(LANES, MATMUL_DTYPE)], [(1, LANES), (1, LANES)])


_N_PAIR = SSD_HEADS // 2
_BC = SSD_GROUPS * SSD_STATE


def _ssd_chunk_fn(x, bc, dt, cs, h_in, dsk_row, d):
    q = x.shape[0]
    lane = lax.broadcasted_iota(jnp.int32, (1, LANES), 1)
    half = lane < SSD_HEAD_DIM
    lo, hi = half.astype(F32), 1.0 - half.astype(F32)
    cst = cs.T
    li = lax.broadcasted_iota(jnp.int32, (q, 1), 0)
    si = lax.broadcasted_iota(jnp.int32, (1, q), 1)
    mask = li >= si if d == 0 else li <= si
    end = q - 1 if d == 0 else 0
    tot = cs[end:end + 1, :]
    mine = (lane >= SSD_HEADS * d) & (lane < SSD_HEADS * (d + 1))
    e_cs, e_dec, e_tot = jnp.exp(cs), jnp.exp(jnp.where(mine, tot - cs, 0.0)), jnp.exp(tot)
    e_dd = e_dec * dt
    dtt = dt.T

    def col(v, l):
        return v[:, l:l + 1]

    def by_head(v, l):
        return jnp.where(half, col(v, l), col(v, l + 1))

    per_group = _N_PAIR // SSD_GROUPS
    ys, hs = [], []
    for g in range(SSD_GROUPS):
        bm = bc[:, g * SSD_STATE:(g + 1) * SSD_STATE]
        cm = bc[:, _BC + g * SSD_STATE:_BC + (g + 1) * SSD_STATE]
        cb = _mm(cm, bm, _NT)
        pairs = range(g * per_group, (g + 1) * per_group)
        h_g = [h_in[j * SSD_STATE:(j + 1) * SSD_STATE, :] for j in pairs]
        y_off = _mm(cm, jnp.concatenate(h_g, axis=1))
        for k, j in enumerate(pairs):
            l0 = SSD_HEADS * d + 2 * j
            xj = x[:, j * LANES:(j + 1) * LANES]
            xcat = jnp.concatenate([xj * lo, xj * hi], axis=0)
            w0 = cb * jnp.exp(jnp.where(mask, col(cs, l0) - cst[l0:l0 + 1, :], -jnp.inf)) * dtt[l0:l0 + 1, :]
            w1 = cb * jnp.exp(jnp.where(mask, col(cs, l0 + 1) - cst[l0 + 1:l0 + 2, :], -jnp.inf)) * dtt[l0 + 1:l0 + 2, :]
            y = _mm(jnp.concatenate([w0, w1], axis=1), xcat)
            st = _mm(jnp.concatenate([bm * col(e_dd, l0), bm * col(e_dd, l0 + 1)], axis=0), xcat, _TN)
            y = y + y_off[:, k * LANES:(k + 1) * LANES] * by_head(e_cs, l0) + by_head(dsk_row, l0) * xj
            ys.append(y)
            hs.append(h_g[k] * by_head(e_tot, l0) + st)
    return jnp.concatenate(ys, axis=1), jnp.concatenate(hs, axis=0)


def _chunk_of(d, ci, nc, backward):
    up = ci if not backward else nc - 1 - ci
    return up + d * (nc - 1 - 2 * up)


_ST_ROWS = _N_PAIR * SSD_STATE


def _ssd_fwd_dir(u_h, dt, cs, dsk_row, s, d):
    nc = s // SSD_CHUNK
    q = SSD_CHUNK

    def body(x_ref, bc_ref, dt_ref, cs_ref, dsk_ref, y_ref, hs_ref, st_ref):
        @pl.when(pl.program_id(0) == 0)
        def _():
            st_ref[...] = jnp.zeros(st_ref.shape, F32)

        h_in = st_ref[...]
        y, h_out = _ssd_chunk_fn(x_ref[...], bc_ref[...], dt_ref[...], cs_ref[...], h_in, dsk_ref[...], d)
        y_ref[...] = y
        hs_ref[...] = h_in
        st_ref[...] = h_out

    ch = lambda ci: _chunk_of(d, ci, nc, False)
    return pl.pallas_call(
        body, name=f"ev_ssd_fwd{d}", grid=(nc,),
        in_specs=[pl.BlockSpec((q, SSD_INNER), lambda ci: (ch(ci), 0)),
                  pl.BlockSpec((q, 2 * _BC), lambda ci: (ch(ci), 0)),
                  pl.BlockSpec((q, LANES), lambda ci: (ch(ci), 0)),
                  pl.BlockSpec((q, LANES), lambda ci: (ch(ci), 0)),
                  pl.BlockSpec((1, LANES), lambda ci: (0, 0))],
        out_specs=[pl.BlockSpec((q, SSD_INNER), lambda ci: (ch(ci), 0)),
                   pl.BlockSpec((None, _ST_ROWS, LANES), lambda ci: (ch(ci), 0, 0))],
        out_shape=[jax.ShapeDtypeStruct((s, SSD_INNER), F32), jax.ShapeDtypeStruct((nc, _ST_ROWS, LANES), F32)],
        scratch_shapes=[pltpu.VMEM((_ST_ROWS, LANES), F32)],
        compiler_params=_params(("arbitrary",)),
    )(u_h[0], u_h[1], dt, cs, dsk_row)


def ssd_fwd(u_h, dt, cs, dsk_row, s):
    ys, hss = zip(*[_ssd_fwd_dir(u_h, dt, cs, dsk_row, s, d) for d in range(2)])
    return ys, hss


def _ssd_bwd_dir(u_h, dt, cs, dsk_row, hs, dy, s, d):
    nc = s // SSD_CHUNK
    q = SSD_CHUNK

    def body(x_ref, bc_ref, dt_ref, cs_ref, dsk_ref, hs_ref, dy_ref,
             dx_ref, dbc_ref, ddt_ref, dcs_ref, ddsk_ref, dst_ref):
        ci = pl.program_id(0)

        @pl.when(ci == 0)
        def _():
            dst_ref[...] = jnp.zeros(dst_ref.shape, F32)

        f = functools.partial(_ssd_chunk_fn, d=d)
        _, vjp = jax.vjp(f, x_ref[...], bc_ref[...], dt_ref[...], cs_ref[...], hs_ref[...], dsk_ref[...])
        dx, dbc, ddt, dcs, dh, ddsk = vjp((dy_ref[...], dst_ref[...]))
        dx_ref[...] = dx
        dbc_ref[...] = dbc
        ddt_ref[...] = ddt
        dcs_ref[...] = dcs
        dst_ref[...] = dh

        @pl.when(ci == 0)
        def _():
            ddsk_ref[...] = ddsk

        @pl.when(ci != 0)
        def _():
            ddsk_ref[...] += ddsk

    ch = lambda ci: _chunk_of(d, ci, nc, True)
    row_blk = lambda w: pl.BlockSpec((q, w), lambda ci: (ch(ci), 0))
    return pl.pallas_call(
        body, name=f"ev_ssd_bwd{d}", grid=(nc,),
        in_specs=[row_blk(SSD_INNER), row_blk(2 * _BC), row_blk(LANES), row_blk(LANES),
                  pl.BlockSpec((1, LANES), lambda ci: (0, 0)),
                  pl.BlockSpec((None, _ST_ROWS, LANES), lambda ci: (ch(ci), 0, 0)), row_blk(SSD_INNER)],
        out_specs=[row_blk(SSD_INNER), row_blk(2 * _BC), row_blk(LANES), row_blk(LANES),
                   pl.BlockSpec((1, LANES), lambda ci: (0, 0))],
        out_shape=[jax.ShapeDtypeStruct((s, SSD_INNER), F32), jax.ShapeDtypeStruct((s, 2 * _BC), F32),
                   jax.ShapeDtypeStruct((s, LANES), F32), jax.ShapeDtypeStruct((s, LANES), F32),
                   jax.ShapeDtypeStruct((1, LANES), F32)],
        scratch_shapes=[pltpu.VMEM((_ST_ROWS, LANES), F32)],
        compiler_params=_params(("arbitrary",)),
    )(u_h[0], u_h[1], dt, cs, dsk_row, hs, dy)


def ssd_bwd(u_h, dt, cs, dsk_row, hs, dy, s):
    return zip(*[_ssd_bwd_dir(u_h, dt, cs, dsk_row, hs[d], dy, s, d) for d in range(2)])


def _gated_rms(ys, z, g):
    t1 = ys * _silu(z)
    return t1 * lax.rsqrt(jnp.mean(t1 * t1, axis=-1, keepdims=True) + EPS) * g


def even_mix_fwd(proj, y2, norm_g, sc_w, rows):
    def fn(yf, yb, z, bg, cgp, cg, cgn, hp, hh, hn, gate, g, w):
        ya = _gated_rms(yf + yb, z, g)
        me = _ext(cgp, cg, cgn, rows.n) * _ext(hp, hh, hn, rows.n)
        cm = sum(w[k:k + 1, :] * _shift(me, off) for k, off in enumerate(SC_OFFS))
        return jnp.concatenate([ya, bg * cm * _silu(gate)], axis=1)

    w = 1024
    return rows_call("ev_mix_fwd", rows, fn,
                     [y2[0], y2[1], proj, proj] + [proj] * 6 + [proj, norm_g, sc_w],
                     [rows.tile(w), rows.tile(w), rows.tile(w, E_Z), rows.tile(w, E_BG)]
                     + rows.halo(w, E_CG) + rows.halo(w, E_H)
                     + [rows.tile(w, E_GATE), Rows.full((1, w)), Rows.full((3, w))],
                     [(2 * w, MATMUL_DTYPE)])[0]


def even_mix_bwd(proj, y2, norm_g, sc_w, dyab, rows):
    w = 1024

    def fn(yf, yb, z, g, sw, dya, *a):
        (dybp, dyb, dybn, bgp, bg, bgn, gtp, gt, gtn, cgp, cg, cgn, hp, hh, hn) = a
        _, vjp = jax.vjp(_gated_rms, yf + yb, z, g)
        dys, dz, dg = vjp(dya.astype(F32))
        n = rows.n
        dye, bge, gte = _ext(dybp, dyb, dybn, n), _ext(bgp, bg, bgn, n), _ext(gtp, gt, gtn, n)
        cge, he = _ext(cgp, cg, cgn, n), _ext(hp, hh, hn, n)
        me = cge * he
        cm = sum(sw[k:k + 1, :] * _shift(me, off) for k, off in enumerate(SC_OFFS))
        dyc = dyb.astype(F32)
        dbg = dyc * cm * _silu(gt)
        dgate = dyc * bg * cm * _dsilu(gt)
        dcme = dye * bge * _silu(gte)
        dm = sum(sw[k:k + 1, :] * _shift(dcme, -off) for k, off in enumerate(SC_OFFS))
        dcm = _shift(dcme, 0)
        dws = [_rowsum(dcm * _shift(me, off)) for off in SC_OFFS]
        dsw = jnp.concatenate(dws + [jnp.zeros((5, w), F32)], axis=0)
        return dys, dz, dbg, dm * hh, dm * cg, dgate, _rowsum(dg), dsw

    return rows_call(
        "ev_mix_bwd", rows, fn,
        [y2[0], y2[1], proj, norm_g, sc_w, dyab] + [dyab] * 3 + [proj] * 12,
        [rows.tile(w), rows.tile(w), rows.tile(w, E_Z), Rows.full((1, w)), Rows.full((3, w)),
         rows.tile(w, 0)] + rows.halo(w, w) + rows.halo(w, E_BG) + rows.halo(w, E_GATE)
        + rows.halo(w, E_CG) + rows.halo(w, E_H),
        [(w, F32)] + [(w, MATMUL_DTYPE)] * 5, [(1, w), (8, w)])


def _res_ln(x, h, g, b):
    v = ALPHA * x + h
    mu = jnp.mean(v, axis=-1, keepdims=True)
    var = jnp.mean(jnp.square(v - mu), axis=-1, keepdims=True)
    return (v - mu) * lax.rsqrt(var + EPS) * g + b


def res_ln_fwd(x, h, g, b, rows, name):
    return rows_call(name, rows, _res_ln, [x, h, g, b],
                     [rows.tile(D_MODEL), rows.tile(D_MODEL), Rows.full((1, D_MODEL)), Rows.full((1, D_MODEL))],
                     [(D_MODEL, F32)])[0]


def res_ln_bwd(x, h, g, b, dy, rows, name):
    def fn(x_, h_, g_, b_, dy_):
        _, vjp = jax.vjp(_res_ln, x_, h_, g_, b_)
        dx, dh, dg, db = vjp(dy_)
        return dx, dh, dg, db

    return rows_call(name, rows, fn, [x, h, g, b, dy],
                     [rows.tile(D_MODEL), rows.tile(D_MODEL), Rows.full((1, D_MODEL)), Rows.full((1, D_MODEL)),
                      rows.tile(D_MODEL)],
                     [(D_MODEL, F32), (D_MODEL, F32)], [(1, D_MODEL), (1, D_MODEL)])


def final_ln_loss(x, h, g, b, target, rows):
    def fn(x_, h_, g_, b_, t_):
        y, vjp = jax.vjp(_res_ln, x_, h_, g_, b_)
        err = y - t_
        dx, dh, dg, db = vjp(err * (1.0 / D_MODEL))
        return dx, dh, dg, db, _rowsum(jnp.square(err)) * (0.5 / D_MODEL)

    return rows_call("od_ln_loss", rows, fn, [x, h, g, b, target],
                     [rows.tile(D_MODEL), rows.tile(D_MODEL), Rows.full((1, D_MODEL)), Rows.full((1, D_MODEL)),
                      rows.tile(D_MODEL)],
                     [(D_MODEL, F32), (D_MODEL, F32)], [(1, D_MODEL), (1, D_MODEL), (1, D_MODEL)])


def pad_even_w_in(w):
    return jnp.concatenate([w[:, :3072], w[:, 3104:], w[:, 3072:3104],
                            jnp.zeros((w.shape[0], EVEN_P - EVEN_PROJ), w.dtype)], axis=1)


def matmul_pieces_nt(pieces, w, name, add):
    m, n, npc = pieces[0].shape[0], w.shape[0], len(pieces)
    widths = [p.shape[1] for p in pieces]
    assert sum(widths) == w.shape[1]
    tm = _pick(m, (256, 128))

    def body(*refs):
        p_refs = refs[:npc]
        w_ref, add_ref, o_ref = refs[npc:]
        acc, off = add_ref[...].astype(F32), 0
        for p_ref, wd in zip(p_refs, widths):
            acc = acc + _mm(p_ref[...], w_ref[:, off:off + wd], _NT)
            off += wd
        o_ref[...] = acc

    return pl.pallas_call(
        body, name=name, grid=(m // tm,),
        in_specs=[pl.BlockSpec((tm, wd), lambda i: (i, 0)) for wd in widths]
        + [pl.BlockSpec(w.shape, lambda i: (0, 0), pipeline_mode=pl.Buffered(1)),
           pl.BlockSpec((tm, n), lambda i: (i, 0))],
        out_specs=pl.BlockSpec((tm, n), lambda i: (i, 0)),
        out_shape=jax.ShapeDtypeStruct((m, n), F32),
        compiler_params=_params(("parallel",)),
    )(*pieces, w, add)


def even_layer(x, w, rows):
    s = rows.s
    xb = x.astype(MATMUL_DTYPE)
    proj = matmul(xb, w["w_in_p"], "nn", F32, "ev_proj")
    (pre0, u0), (pre1, u1) = conv_fwd(proj, w["conv_w"], w["conv_b"], rows)
    dt, cs = ssd_pre(proj, w["bias_row"], w["alog_row"], s)
    y2, hs = ssd_fwd((u0, u1), dt, cs, w["dsk_row"], s)
    yab = even_mix_fwd(proj, y2, w["norm_g"], w["sc_w"], rows)
    h = matmul(yab, w["w_out"], "nn", F32, "ev_out")
    x1 = res_ln_fwd(x, h, w["ln_g"], w["ln_b"], rows, "ev_ln")
    return x1, dict(x=x, xb=xb, proj=proj, pre=(pre0, pre1), u=(u0, u1), dt=dt, cs=cs, y2=y2, hs=hs, yab=yab,
                    h=h)


def even_layer_bwd(dx1, w, sv, rows):
    s = rows.s
    dres, dh, dln_g, dln_b = res_ln_bwd(sv["x"], sv["h"], w["ln_g"], w["ln_b"], dx1, rows, "ev_ln_bwd")
    dyab = matmul(dh, w["w_out"], "nt", F32, "ev_out_dx")
    dw_out = matmul(sv["yab"], dh, "tn", F32, "ev_out_dw")
    dys, dz, dbg, dcg, dhh, dgate, dnorm_g, dsw = even_mix_bwd(sv["proj"], sv["y2"], w["norm_g"], w["sc_w"], dyab,
                                                               rows)
    dxs, dbc, ddt, dcs, ddsk = ssd_bwd(sv["u"], sv["dt"], sv["cs"], w["dsk_row"], sv["hs"], dys, s)
    ddtraw, dbias_row, dalog_row = ssd_pre_bwd(sv["proj"], w["bias_row"], w["alog_row"], ddt, dcs, s)
    (dxbc0, dxbc1), dconv_w, dconv_b = conv_bwd(sv["proj"], sv["pre"], (dxs, dbc), w["conv_w"], rows)
    pieces = [dz, dxbc0, dxbc1, dbg, dcg, dhh, dgate]
    dx0 = matmul_pieces_nt(pieces + [ddtraw], w["w_in_p"], "ev_proj_dx", add=dres)
    dws = [matmul(p, sv["xb"], "tn", F32, f"ev_proj_dw{i}") for i, p in enumerate(pieces)]
    dw_dt = matmul(ddtraw, sv["xb"], "tn", F32, "ev_proj_dw_dt")
    dw_in = jnp.concatenate(dws[:3] + [dw_dt[:2 * SSD_HEADS]] + dws[3:], axis=0)
    g = dict(ev_w_in=dw_in, ev_conv_w=dconv_w, ev_conv_b=dconv_b,
             ev_a_log=_head_unrow(dalog_row), ev_dt_bias=_head_unrow(dbias_row),
             ev_d_skip=_head_unrow(ddsk[0] + ddsk[1]), ev_norm_g=dnorm_g, ev_sc_conv_w=dsw[:3],
             ev_w_out=dw_out, ev_ln_g=dln_g, ev_ln_b=dln_b)
    return dx0, g


def even_weights(ev_w_in, ev_conv_w, ev_conv_b, ev_a_log, ev_dt_bias, ev_d_skip, ev_norm_g, ev_sc_conv_w,
                 ev_w_out, ev_ln_g, ev_ln_b):
    return dict(w_in_p=pad_even_w_in(ev_w_in).astype(MATMUL_DTYPE), conv_w=ev_conv_w, conv_b=ev_conv_b,
                alog_row=_head_row(ev_a_log), bias_row=_head_row(ev_dt_bias), dsk_row=_head_row(ev_d_skip),
                norm_g=ev_norm_g, sc_w=ev_sc_conv_w, w_out=ev_w_out.astype(MATMUL_DTYPE), ln_g=ev_ln_g,
                ln_b=ev_ln_b)


HEAD_BLK = LANES
ROPE_LO = MLA_NOPE
ROPE_HALF = MLA_ROPE // 2


def _rms(v, g):
    return v * lax.rsqrt(jnp.mean(v * v, axis=-1, keepdims=True) + EPS) * g


def latent_norm_fwd(proj, gq, gkv, rows):
    def fn(cq, ckv, gq_, gkv_):
        return _rms(cq, gq_), _rms(ckv, gkv_)

    return rows_call("od_norm_fwd", rows, fn, [proj, proj, gq, gkv],
                     [rows.tile(MLA_Q_RANK, O_CQ), rows.tile(MLA_KV_RANK, O_CKV), Rows.full((1, MLA_Q_RANK)),
                      Rows.full((1, MLA_KV_RANK))],
                     [(MLA_Q_RANK, MATMUL_DTYPE), (MLA_KV_RANK, MATMUL_DTYPE)])


def latent_norm_bwd(proj, gq, gkv, dcqn, dckvn, rows):
    def fn(cq, ckv, gq_, gkv_, d1, d2):
        _, vjp = jax.vjp(_rms, cq, gq_)
        dcq, dgq = vjp(d1)
        _, vjp2 = jax.vjp(_rms, ckv, gkv_)
        dckv, dgkv = vjp2(d2)
        return dcq, dckv, dgq, dgkv

    return rows_call("od_norm_bwd", rows, fn, [proj, proj, gq, gkv, dcqn, dckvn],
                     [rows.tile(MLA_Q_RANK, O_CQ), rows.tile(MLA_KV_RANK, O_CKV), Rows.full((1, MLA_Q_RANK)),
                      Rows.full((1, MLA_KV_RANK)), rows.tile(MLA_Q_RANK), rows.tile(MLA_KV_RANK)],
                     [(MLA_Q_RANK, F32), (MLA_KV_RANK, F32)], [(1, MLA_Q_RANK), (1, MLA_KV_RANK)])


def rope_rows():
    lane = np.arange(LANES)
    inv = ROPE_THETA ** (-jnp.arange(ROPE_HALF, dtype=F32) / ROPE_HALF)
    on = (lane >= ROPE_LO) & (lane < ROPE_LO + MLA_ROPE)
    freq = jnp.where(on, inv[(lane - ROPE_LO) % ROPE_HALF], 0.0).reshape(1, LANES).astype(F32)
    sign = np.where(on, np.where(lane < ROPE_LO + ROPE_HALF, -1.0, 1.0), 0.0).reshape(1, LANES).astype(np.float32)
    return freq, jnp.asarray(sign)


def _rot_tables(pos, freq, sign):
    ang = pos.astype(F32) * freq
    return jnp.cos(ang), jnp.sin(ang) * sign


def _swap_halves(v):
    lane = lax.broadcasted_iota(jnp.int32, (1, LANES), 1)
    return jnp.where(lane < ROPE_LO + ROPE_HALF, pltpu.roll(v, LANES - ROPE_HALF, 1), pltpu.roll(v, ROPE_HALF, 1))


def rope_fwd(qp, kvp, proj, pos, freq, sign, rows):
    def fn(q, k, kr, v, p, f, sg):
        c, sn = _rot_tables(p, f, sg)
        rk = kr * c + _swap_halves(kr) * sn
        one = (lax.broadcasted_iota(jnp.int32, (v.shape[0], HEAD_BLK - MLA_V), 1) == 0).astype(F32)
        qs, ks, vs = [], [], []
        for h in range(MLA_HEADS):
            qh = q[:, h * HEAD_BLK:(h + 1) * HEAD_BLK]
            qs.append((qh * c + _swap_halves(qh) * sn) * QSCALE)
            ks.append(k[:, h * HEAD_BLK:(h + 1) * HEAD_BLK] + rk)
            vs += [v[:, h * MLA_V:(h + 1) * MLA_V], one]
        return jnp.concatenate(qs, axis=1), jnp.concatenate(ks, axis=1), jnp.concatenate(vs, axis=1)

    w = MLA_HEADS * HEAD_BLK
    return rows_call("od_rope_fwd", rows, fn, [qp, kvp, proj, kvp, pos, freq, sign],
                     [rows.tile(w), rows.tile(w, 0), rows.tile(LANES, O_KR), rows.tile(MLA_WIDTH, w),
                      rows.tile(1), Rows.full((1, LANES)), Rows.full((1, LANES))],
                     [(w, MATMUL_DTYPE), (w, MATMUL_DTYPE), (w, MATMUL_DTYPE)])


def rope_bwd(dq, dk, dv, pos, freq, sign, rows):
    def fn(dq_, dk_, dv_, p, f, sg):
        c, sn = _rot_tables(p, f, sg)
        on = jnp.abs(sg)
        outs, dks, dvs, dkr = [], [], [], jnp.zeros((dq_.shape[0], LANES), F32)
        for h in range(MLA_HEADS):
            g = dq_[:, h * HEAD_BLK:(h + 1) * HEAD_BLK] * ATTN_SCALE
            outs.append(g * c + _swap_halves(g * sn) * on)
            gk = dk_[:, h * HEAD_BLK:(h + 1) * HEAD_BLK] * LN2
            dks.append(gk)
            dkr = dkr + gk * c + _swap_halves(gk * sn) * on
            dvs.append(dv_[:, h * HEAD_BLK:h * HEAD_BLK + MLA_V])
        return jnp.concatenate(outs, axis=1), jnp.concatenate(dks + dvs, axis=1), dkr

    w = MLA_HEADS * HEAD_BLK
    return rows_call("od_rope_bwd", rows, fn, [dq, dk, dv, pos, freq, sign],
                     [rows.tile(w), rows.tile(w), rows.tile(w), rows.tile(1), Rows.full((1, LANES)),
                      Rows.full((1, LANES))],
                     [(w, MATMUL_DTYPE), (w + MLA_WIDTH, MATMUL_DTYPE), (LANES, F32)])


_PAIRS = MLA_HEADS // 2
ATT_TQ = 512
ATT_TK = 4096
ATT_BWD_TQ = 1024
ATT_BWD_TK = 1024


def _att_tiles(s, backward=False):
    if backward:
        return min(ATT_BWD_TQ, s), min(ATT_BWD_TK, s)
    return min(ATT_TQ, s), min(ATT_TK, s)


def attention_fwd(qcat, kcat, vcat, s):
    tq, tk = _att_tiles(s)
    nq, nk = s // tq, s // tk

    def body(q_ref, k_ref, v_ref, o_ref, lse_ref, p_ref, mt_ref, m_ref, acc_ref):
        kk = pl.program_id(2)
        half = lax.broadcasted_iota(jnp.int32, (1, LANES), 1) < MLA_V

        @pl.when(kk == 0)
        def _():
            m_ref[...] = jnp.full(m_ref.shape, -jnp.inf, F32)
            acc_ref[...] = jnp.zeros(acc_ref.shape, F32)

        sl = [slice(hh * HEAD_BLK, (hh + 1) * HEAD_BLK) for hh in range(2)]
        sc = [_mm(q_ref[:, sl[hh]], k_ref[:, sl[hh]], _NT) for hh in range(2)]
        for hh in range(2):
            m_prev = m_ref[hh]
            m_new = jnp.maximum(m_prev, jnp.max(sc[hh], axis=1, keepdims=True))
            p = jnp.exp2(sc[hh] - m_new[:, :1]).astype(p_ref.dtype)
            p_ref[hh] = p
            acc_ref[hh] = acc_ref[hh] * jnp.exp2(m_prev - m_new) + _mm(p, v_ref[:, sl[hh]])
            m_ref[hh] = m_new
        mt_ref[...] = jnp.where(half, m_ref[0], m_ref[1])

        @pl.when(kk == nk - 1)
        def _():
            l0, l1 = acc_ref[0][:, MLA_V:MLA_V + 1], acc_ref[1][:, MLA_V:MLA_V + 1]
            o_ref[...] = jnp.where(half, acc_ref[0] / l0, pltpu.roll(acc_ref[1] / l1, MLA_V, 1))
            lse_ref[...] = jnp.where(half, m_ref[0] + jnp.log2(l0), m_ref[1] + jnp.log2(l1))

    return pl.pallas_call(
        body, name="od_attn_fwd", grid=(_PAIRS, nq, nk),
        in_specs=[pl.BlockSpec((tq, 2 * HEAD_BLK), lambda p, i, kk: (i, p)),
                  pl.BlockSpec((tk, 2 * HEAD_BLK), lambda p, i, kk: (kk, p)),
                  pl.BlockSpec((tk, 2 * HEAD_BLK), lambda p, i, kk: (kk, p))],
        out_specs=[pl.BlockSpec((tq, LANES), lambda p, i, kk: (i, p)),
                   pl.BlockSpec((None, tq, LANES), lambda p, i, kk: (p, i, 0)),
                   pl.BlockSpec((2, tq, tk), lambda p, i, kk: (p, i, kk)),
                   pl.BlockSpec((None, None, tq, LANES), lambda p, i, kk: (p, kk, i, 0))],
        out_shape=[jax.ShapeDtypeStruct((s, MLA_WIDTH), F32), jax.ShapeDtypeStruct((_PAIRS, s, LANES), F32),
                   jax.ShapeDtypeStruct((MLA_HEADS, s, s), MATMUL_DTYPE),
                   jax.ShapeDtypeStruct((_PAIRS, nk, s, LANES), F32)],
        scratch_shapes=[pltpu.VMEM((2, tq, LANES), F32)] * 2,
        compiler_params=_params(("parallel", "parallel", "arbitrary")),
    )(qcat, kcat, vcat)


def attention_bwd(qcat, kcat, vcat, o, lse, pst, mt, do, s):
    tq, tk = _att_tiles(s, backward=True)
    nq, nk = s // tq, s // tk
    per_fwd_tile = _att_tiles(s)[1] // tk
    assert per_fwd_tile * tk == _att_tiles(s)[1]

    def body(q_ref, k_ref, v_ref, do_ref, o_ref, lse_ref, p_ref, mt_ref, dq_ref, dk_ref, dv_ref):
        kk, i = pl.program_id(1), pl.program_id(2)
        lane = lax.broadcasted_iota(jnp.int32, (1, LANES), 1)
        half = lane < MLA_V
        c = jnp.exp2(mt_ref[...] - lse_ref[...])
        do_p = do_ref[...] * c
        prod = do_p * o_ref[...]
        rows_i = pl.ds(pl.multiple_of(i * tq, tq), tq)
        for hh in range(2):
            sl = slice(hh * HEAD_BLK, (hh + 1) * HEAD_BLK)
            mine = half if hh == 0 else jnp.logical_not(half)
            delta = jnp.sum(jnp.where(mine, prod, 0.0), axis=1, keepdims=True)
            do_h = jnp.where(half, do_p if hh == 0 else pltpu.roll(do_p, MLA_V, 1), 0.0)
            p = p_ref[hh]
            ds = p.astype(F32) * (_mm(do_h, v_ref[:, sl], _NT) - delta)
            dv_h, dk_h, dq_h = _mm(p, do_h, _TN), _mm(ds, q_ref[:, sl], _TN), _mm(ds, k_ref[:, sl])

            @pl.when(i == 0)
            def _():
                dv_ref[:, sl] = dv_h
                dk_ref[:, sl] = dk_h

            @pl.when(i != 0)
            def _():
                dv_ref[:, sl] += dv_h
                dk_ref[:, sl] += dk_h

            @pl.when(kk == 0)
            def _():
                dq_ref[rows_i, sl] = dq_h

            @pl.when(kk != 0)
            def _():
                dq_ref[rows_i, sl] += dq_h

    w = MLA_HEADS * HEAD_BLK
    return pl.pallas_call(
        body, name="od_attn_bwd", grid=(_PAIRS, nk, nq),
        in_specs=[pl.BlockSpec((tq, 2 * HEAD_BLK), lambda p, kk, i: (i, p)),
                  pl.BlockSpec((tk, 2 * HEAD_BLK), lambda p, kk, i: (kk, p)),
                  pl.BlockSpec((tk, 2 * HEAD_BLK), lambda p, kk, i: (kk, p)),
                  pl.BlockSpec((tq, LANES), lambda p, kk, i: (i, p)),
                  pl.BlockSpec((tq, LANES), lambda p, kk, i: (i, p)),
                  pl.BlockSpec((None, tq, LANES), lambda p, kk, i: (p, i, 0)),
                  pl.BlockSpec((2, tq, tk), lambda p, kk, i: (p, i, kk)),
                  pl.BlockSpec((None, None, tq, LANES), lambda p, kk, i: (p, kk // per_fwd_tile, i, 0))],
        out_specs=[pl.BlockSpec((s, 2 * HEAD_BLK), lambda p, kk, i: (0, p), pipeline_mode=pl.Buffered(1)),
                   pl.BlockSpec((tk, 2 * HEAD_BLK), lambda p, kk, i: (kk, p)),
                   pl.BlockSpec((tk, 2 * HEAD_BLK), lambda p, kk, i: (kk, p))],
        out_shape=[jax.ShapeDtypeStruct((s, w), F32)] * 3,
        compiler_params=_params(("parallel", "arbitrary", "arbitrary")),
    )(qcat, kcat, vcat, do, o, lse, pst, mt)


def _pool_counts(n_rows, first_row, s, w):
    pos = first_row + lax.broadcasted_iota(jnp.int32, (n_rows, 1), 0)
    lo = jnp.clip(pos - w // 2, 0, s)
    hi = jnp.clip(pos + w - w // 2, 0, s)
    return jnp.maximum(hi - lo, 1).astype(F32)


def _window_sum(e, levels, mirrored):
    n = e.shape[0]
    acc = e + pltpu.roll(e, (n - 1) if mirrored else 1, 0)
    step = 1
    for _ in range(levels - 1):
        acc = pltpu.roll(acc, step, 0) + pltpu.roll(acc, n - step, 0)
        step *= 2
    return acc


def _pooled(ue, s, t):
    first = pl.program_id(0) * t
    outs = []
    for gi, w in enumerate(POOL_WINDOWS):
        eg = ue[:, gi * POOL_GROUP:(gi + 1) * POOL_GROUP]
        sm = _window_sum(eg, gi + 1, False)[HALO:HALO + t]
        outs.append(sm / _pool_counts(t, first, s, w) - eg[HALO:HALO + t])
    return outs


def odd_mix_fwd(proj, o, pool_w, pool_scale, rows):
    def fn(o_, gc, up, u, un, gd, pw, ps):
        pooled = _pooled(_ext(up, u, un, rows.n), rows.s, rows.t)
        lin = jnp.concatenate([_mm(pooled[g], pw[g]) for g in range(len(POOL_WINDOWS))], axis=1)
        return jnp.concatenate([o_ * _silu(gc), lin * ps * _silu(gd)], axis=1)

    w = POOL_WIDTH
    return rows_call("od_mix_fwd", rows, fn, [o, proj, proj, proj, proj, proj, pool_w, pool_scale],
                     [rows.tile(w), rows.tile(w, O_GC)] + rows.halo(w, O_UD)
                     + [rows.tile(w, O_GD), Rows.full((4, POOL_GROUP, POOL_GROUP)), Rows.full((1, w))],
                     [(2 * w, MATMUL_DTYPE)])[0]


def odd_mix_bwd(proj, o, pool_w, pool_scale, dycd, rows):
    w = POOL_WIDTH
    ng = len(POOL_WINDOWS)

    def fn(o_, gc, up, u, un, gdp, gd, gdn, pw, ps, dyc, dydp, dyd, dydn):
        n, t, s = rows.n, rows.t, rows.s
        dyc = dyc.astype(F32)
        do = dyc * _silu(gc)
        dgc = dyc * o_ * _dsilu(gc)
        pooled = _pooled(_ext(up, u, un, n), s, t)
        lin = jnp.concatenate([_mm(pooled[g], pw[g]) for g in range(ng)], axis=1)
        dydc = dyd.astype(F32)
        dgd = dydc * lin * ps * _dsilu(gd)
        dps = _rowsum(dydc * lin * _silu(gd))
        dlin_e = _ext(dydp, dyd, dydn, n) * ps * _silu(_ext(gdp, gd, gdn, n))
        first = pl.program_id(0) * t - HALO
        dus, dpws = [], []
        for g, win in enumerate(POOL_WINDOWS):
            sl = slice(g * POOL_GROUP, (g + 1) * POOL_GROUP)
            dle = dlin_e[:, sl]
            dpws.append(_mm(pooled[g], dle[HALO:HALO + t], _TN))
            dpe = _mm(dle, pw[g], _NT)
            gce = dpe / _pool_counts(t + 2 * HALO, first, s, win)
            dus.append(_window_sum(gce, g + 1, True)[HALO:HALO + t] - dpe[HALO:HALO + t])
        return do, dgc, jnp.concatenate(dus, axis=1), dgd, jnp.stack(dpws), dps

    return rows_call("od_mix_bwd", rows, fn,
                     [o, proj, proj, proj, proj, proj, proj, proj, pool_w, pool_scale, dycd, dycd, dycd, dycd],
                     [rows.tile(w), rows.tile(w, O_GC)] + rows.halo(w, O_UD) + rows.halo(w, O_GD)
                     + [Rows.full((ng, POOL_GROUP, POOL_GROUP)), Rows.full((1, w)), rows.tile(w, 0)]
                     + rows.halo(w, w),
                     [(w, F32)] * 4, [(ng, POOL_GROUP, POOL_GROUP), (1, w)])


def pad_odd_w_in(w):
    z = lambda n: jnp.zeros((w.shape[0], n), w.dtype)
    return jnp.concatenate([w[:, :384], z(ROPE_LO), w[:, 384:416], z(LANES - ROPE_LO - MLA_ROPE), w[:, 416:]], axis=1)


def unpad_odd_w_in_t(wpt):
    return jnp.concatenate([wpt[:384], wpt[O_KR + ROPE_LO:O_KR + ROPE_LO + MLA_ROPE], wpt[O_GC:]], axis=0)


def pad_w_uq(w):
    w3 = w.reshape(MLA_Q_RANK, MLA_HEADS, MLA_NOPE + MLA_ROPE)
    w3 = jnp.pad(w3, ((0, 0), (0, 0), (0, HEAD_BLK - MLA_NOPE - MLA_ROPE)))
    return w3.reshape(MLA_Q_RANK, MLA_HEADS * HEAD_BLK)


def unpad_w_uq(wp):
    return wp.reshape(MLA_Q_RANK, MLA_HEADS, HEAD_BLK)[..., :MLA_NOPE + MLA_ROPE].reshape(MLA_Q_RANK, -1)


def pad_w_ukv(w):
    w3 = w.reshape(MLA_KV_RANK, MLA_HEADS, MLA_NOPE + MLA_V)
    kp = jnp.pad(w3[..., :MLA_NOPE], ((0, 0), (0, 0), (0, HEAD_BLK - MLA_NOPE)))
    return jnp.concatenate([kp.reshape(MLA_KV_RANK, -1), w3[..., MLA_NOPE:].reshape(MLA_KV_RANK, -1)], axis=1)


def unpad_w_ukv(wp):
    kp = wp[:, :MLA_HEADS * HEAD_BLK].reshape(MLA_KV_RANK, MLA_HEADS, HEAD_BLK)[..., :MLA_NOPE]
    vp = wp[:, MLA_HEADS * HEAD_BLK:].reshape(MLA_KV_RANK, MLA_HEADS, MLA_V)
    return jnp.concatenate([kp, vp], axis=-1).reshape(MLA_KV_RANK, -1)


def odd_weights(od_w_in, od_q_norm_g, od_w_uq, od_kv_norm_g, od_w_ukv, od_pool_w, od_pool_scale, od_w_out,
                od_ln_g, od_ln_b):
    freq, sign = rope_rows()
    return dict(w_in_p=pad_odd_w_in(od_w_in).astype(MATMUL_DTYPE), gq=od_q_norm_g, gkv=od_kv_norm_g,
                w_uq_p=pad_w_uq(od_w_uq).astype(MATMUL_DTYPE), w_ukv_p=pad_w_ukv(od_w_ukv).astype(MATMUL_DTYPE),
                pool_w=od_pool_w, pool_scale=od_pool_scale, w_out=od_w_out.astype(MATMUL_DTYPE), ln_g=od_ln_g,
                ln_b=od_ln_b, freq=freq, sign=sign)


def odd_layer_loss(x, pos, target, w, rows):
    s = rows.s
    proj = matmul(x, w["w_in_p"], "nn", F32, "od_proj")
    cqn, ckvn = latent_norm_fwd(proj, w["gq"], w["gkv"], rows)
    qp = matmul(cqn, w["w_uq_p"], "nn", F32, "od_q_up")
    kvp = matmul(ckvn, w["w_ukv_p"], "nn", F32, "od_kv_up")
    qcat, kcat, v = rope_fwd(qp, kvp, proj, pos, w["freq"], w["sign"], rows)
    o, lse, pst, mt = attention_fwd(qcat, kcat, v, s)
    ycd = odd_mix_fwd(proj, o, w["pool_w"], w["pool_scale"], rows)
    h = matmul(ycd, w["w_out"], "nn", F32, "od_out")
    dres, dh, dln_g, dln_b, loss_lanes = final_ln_loss(x, h, w["ln_g"], w["ln_b"], target, rows)
    dycd = matmul(dh, w["w_out"], "nt", F32, "od_out_dx")
    dw_out = matmul(ycd, dh, "tn", F32, "od_out_dw")
    do, dgc, dud, dgd, dpool_w, dpool_scale = odd_mix_bwd(proj, o, w["pool_w"], w["pool_scale"], dycd, rows)
    dq, dk, dv = attention_bwd(qcat, kcat, v, o, lse, pst, mt, do, s)
    dqp, dkvp, dkr = rope_bwd(dq, dk, dv, pos, w["freq"], w["sign"], rows)
    dcqn = matmul(dqp, w["w_uq_p"], "nt", F32, "od_q_up_dx")
    dw_uq = unpad_w_uq(matmul(cqn, dqp, "tn", F32, "od_q_up_dw"))
    dckvn = matmul(dkvp, w["w_ukv_p"], "nt", F32, "od_kv_up_dx")
    dw_ukv = unpad_w_ukv(matmul(ckvn, dkvp, "tn", F32, "od_kv_up_dw"))
    dcq, dckv, dgq, dgkv = latent_norm_bwd(proj, w["gq"], w["gkv"], dcqn, dckvn, rows)
    dproj = jnp.concatenate([dcq, dckv, dkr, dgc, dud, dgd], axis=1).astype(MATMUL_DTYPE)
    dx = matmul(dproj, w["w_in_p"], "nt", F32, "od_proj_dx", add=dres)
    dw_in = unpad_odd_w_in_t(matmul(dproj, x, "tn", F32, "od_proj_dw"))
    g = dict(od_w_in=dw_in, od_q_norm_g=dgq, od_w_uq=dw_uq, od_kv_norm_g=dgkv, od_w_ukv=dw_ukv,
             od_pool_w=dpool_w, od_pool_scale=dpool_scale, od_w_out=dw_out, od_ln_g=dln_g, od_ln_b=dln_b)
    return loss_lanes, dx, g


_MESH = pl.DeviceIdType.MESH
_ANY = pl.BlockSpec(memory_space=pl.ANY)
N_CHIPS = 4


def _push_call(name, ins, out_shapes, plan, n_remote, n_local):
    n_in, n_out = len(ins), len(out_shapes)

    def body(*refs):
        in_refs, out_refs = refs[:n_in], refs[n_in:n_in + n_out]
        send_sems, recv_sems, local_sems = refs[n_in + n_out:]
        x, y, c = lax.axis_index("x"), lax.axis_index("y"), lax.axis_index("c")
        remote, local = plan(in_refs, out_refs, x, y, c)
        assert len(remote) == n_remote and len(local) == n_local
        sends = [pltpu.make_async_remote_copy(src_ref=s, dst_ref=d, send_sem=send_sems.at[k], recv_sem=recv_sems.at[k],
                                              device_id=dev, device_id_type=_MESH)
                 for k, (s, d, dev, _) in enumerate(remote)]
        recvs = [pltpu.make_async_remote_copy(src_ref=s, dst_ref=land, send_sem=send_sems.at[k],
                                              recv_sem=recv_sems.at[k], device_id=dev, device_id_type=_MESH)
                 for k, (s, _, dev, land) in enumerate(remote)]
        locs = [pltpu.make_async_copy(s, d, local_sems.at[k]) for k, (s, d) in enumerate(local)]
        for cp in sends + locs:
            cp.start()
        for cp in recvs:
            cp.wait_recv()
        for cp in sends:
            cp.wait_send()
        for cp in locs:
            cp.wait()

    return pl.pallas_call(
        body, name=name, in_specs=[_ANY] * n_in, out_specs=[_ANY] * n_out, out_shape=list(out_shapes),
        scratch_shapes=[pltpu.SemaphoreType.DMA((n_remote,)), pltpu.SemaphoreType.DMA((n_remote,)),
                        pltpu.SemaphoreType.DMA((max(n_local, 1),))],
    )(*ins)


def _other_chips(x, y):
    return [(1 - x, y), (x, 1 - y), (1 - x, 1 - y)]


def chips_allgather(bufs, halved):
    def plan(in_refs, out_refs, x, y, c):
        me = 2 * x + y
        remote, local = [], []
        for src, out, half in zip(in_refs, out_refs, halved):
            rh = src.shape[0] // 2
            for (px, py) in _other_chips(x, y):
                if half:
                    mine = pl.ds(c * rh, rh)
                    remote.append((src.at[mine], out.at[me, mine], (px, py, c), out.at[2 * px + py, mine]))
                else:
                    remote.append((src, out.at[me], (px, py, c), out.at[2 * px + py]))
            local.append((src, out.at[me]))
        return remote, local

    shapes = [jax.ShapeDtypeStruct((N_CHIPS,) + b.shape, b.dtype) for b in bufs]
    return _push_call("weights_allgather", bufs, shapes, plan, 3 * len(bufs), len(bufs))


def siblings_complete(part):
    rh = part.shape[1] // 2

    def plan(in_refs, out_refs, x, y, c):
        me = 2 * x + y
        g, out = in_refs[0], out_refs[0]
        mine, theirs = pl.ds(c * rh, rh), pl.ds((1 - c) * rh, rh)
        remote, local = [], [(g.at[me], out.at[me])]
        for (px, py) in _other_chips(x, y):
            q = 2 * px + py
            remote.append((g.at[q, mine], out.at[q, mine], (x, y, 1 - c), out.at[q, theirs]))
            local.append((g.at[q, mine], out.at[q, mine]))
        return remote, local

    return _push_call("weights_to_sibling", [part], [jax.ShapeDtypeStruct(part.shape, part.dtype)], plan, 3, 4)[0]


def sibling_send(buf, name):
    def plan(in_refs, out_refs, x, y, c):
        return [(in_refs[0], out_refs[0], (x, y, 1 - c), out_refs[0])], []

    return _push_call(name, [buf], [jax.ShapeDtypeStruct(buf.shape, buf.dtype)], plan, 1, 0)[0]


def chips_scatter(buf):
    def plan(in_refs, out_refs, x, y, c):
        me = 2 * x + y
        src, out = in_refs[0], out_refs[0]
        remote = [(src.at[2 * px + py], out.at[me], (px, py, c), out.at[2 * px + py]) for (px, py) in _other_chips(x, y)]
        return remote, [(src.at[me], out.at[me])]

    return _push_call("grads_scatter", [buf], [jax.ShapeDtypeStruct(buf.shape, buf.dtype)], plan, 3, 1)[0]


PACK_W = 1024
PACK_BLK = 128


def _ew_call(name, fn, ins, in_specs, out_shape, out_spec, n_out, steps):
    def body(*refs):
        res = fn(*[r[...] for r in refs[:len(ins)]])
        if not isinstance(res, (tuple, list)):
            res = (res,)
        for r, v in zip(refs[len(ins):], res):
            r[...] = v

    return pl.pallas_call(
        body, name=name, grid=(steps,), in_specs=in_specs, out_specs=[out_spec] * n_out,
        out_shape=[jax.ShapeDtypeStruct(out_shape, F32)] * n_out,
        compiler_params=_params(("parallel",)),
    )(*ins)


def add2(a, b, name):
    n, r, w = a.shape
    blk = pl.BlockSpec((PACK_BLK, w), lambda i: (i, 0))
    out = _ew_call(name, lambda u, v: u + v, [a.reshape(n * r, w), b.reshape(n * r, w)], [blk, blk], (n * r, w), blk,
                   1, n * r // PACK_BLK)[0]
    return out.reshape(a.shape)


def sum_chips(buf):
    _, r, w = buf.shape
    specs = [pl.BlockSpec((None, PACK_BLK, w), lambda i, q=q: (q, i, 0)) for q in range(N_CHIPS)]
    return _ew_call("grads_sum", lambda a, b, c, d: ((a + b) + c) + d, [buf] * N_CHIPS, specs, (r, w),
                    pl.BlockSpec((PACK_BLK, w), lambda i: (i, 0)), 1, r // PACK_BLK)[0]


def adamw(w, g, m, v, name):
    def fn(w_, g_, m_, v_):
        m2 = ADAM_B1 * m_ + (1.0 - ADAM_B1) * g_
        v2 = ADAM_B2 * v_ + (1.0 - ADAM_B2) * jnp.square(g_)
        m_hat = m2 / (1.0 - ADAM_B1 ** ADAM_STEP)
        v_hat = v2 / (1.0 - ADAM_B2 ** ADAM_STEP)
        return -ADAM_LR * (m_hat / (jnp.sqrt(v_hat) + ADAM_EPS) + ADAM_WD * w_), m2, v2

    r, c = w.shape
    br = r if r <= 512 else _pick(r, (256, 128, 64, 32, 16, 8))
    blk = pl.BlockSpec((br, c), lambda i: (i, 0))
    return _ew_call(name, fn, [w, g, m, v], [blk] * 4, (r, c), blk, 3, r // br)


WEIGHTS = (
    ("ev_w_in", (1, 1024, 7200), 2), ("ev_conv_w", (1, 4, 2048), 2), ("ev_conv_b", (1, 2048), None),
    ("ev_a_log", (1, 2, 16), None), ("ev_dt_bias", (1, 2, 16), None), ("ev_d_skip", (1, 2, 16), None),
    ("ev_norm_g", (1, 1024), None), ("ev_sc_conv_w", (1, 3, 1024), 2), ("ev_w_out", (1, 2048, 1024), 1),
    ("ev_ln_g", (1, 1024), None), ("ev_ln_b", (1, 1024), None), ("od_w_in", (1, 1024, 1952), 2),
    ("od_q_norm_g", (1, 256), 1), ("od_w_uq", (1, 256, 768), 2), ("od_kv_norm_g", (1, 128), None),
    ("od_w_ukv", (1, 128, 1024), 2), ("od_pool_w", (1, 4, 128, 128), None), ("od_pool_scale", (1, 512), 1),
    ("od_w_out", (1, 1024, 1024), 1), ("od_ln_g", (1, 1024), 1), ("od_ln_b", (1, 1024), 1),
)
BIG = ("ev_w_in", "ev_w_out", "od_w_in", "od_w_uq", "od_w_ukv", "od_w_out")


def _block_shape(shape, axis):
    if axis is None:
        return tuple(shape)
    return tuple(d // N_CHIPS if i == axis else d for i, d in enumerate(shape))


def _pack(arrs, quantum):
    flat = jnp.concatenate([a.reshape(-1) for a in arrs])
    n = flat.shape[0]
    padded = -(-n // quantum) * quantum
    return jnp.concatenate([flat, jnp.zeros((padded - n,), flat.dtype)]).reshape(-1, LANES)


def _unpack(flat, shapes):
    out, off = [], 0
    for sh in shapes:
        n = int(np.prod(sh))
        out.append(flat[off:off + n].reshape(sh))
        off += n
    return out


def gather_weights(local):
    sharded = [(n, sh, ax) for (n, sh, ax) in WEIGHTS if ax is not None]
    big = [(n, sh, ax) for (n, sh, ax) in sharded if n in BIG]
    small = [(n, sh, ax) for (n, sh, ax) in sharded if n not in BIG]
    pb = _pack([local[n].astype(MATMUL_DTYPE) for n, _, _ in big], 4 * SUBLANES * LANES)
    ps = _pack([local[n] for n, _, _ in small], SUBLANES * LANES)
    gb, gs = chips_allgather([pb, ps], (True, False))
    gb = siblings_complete(gb)
    full = {n: local[n] for (n, sh, ax) in WEIGHTS if ax is None}
    for group, g in ((big, gb), (small, gs)):
        parts = [_unpack(g[q].reshape(-1), [_block_shape(sh, ax) for _, sh, ax in group]) for q in range(N_CHIPS)]
        for i, (n, sh, ax) in enumerate(group):
            full[n] = jnp.concatenate([parts[q][i] for q in range(N_CHIPS)], axis=ax)
    return full


WIDE = (("ev_w_in", True), ("ev_w_out", False), ("od_w_in", True), ("od_w_out", False))


def reduce_and_update(grads, local_w, local_m, local_v):
    c = lax.axis_index("c")
    wide_names = [n for n, _ in WIDE]
    tail = [(n, sh, ax) for (n, sh, ax) in WEIGHTS if n not in wide_names]
    tail_blocks = [_block_shape(sh, ax) for _, sh, ax in tail]
    n_tail = sum(int(np.prod(b)) for b in tail_blocks)
    wide_rows = [grads[n].shape[0] // N_CHIPS for n in wide_names]
    quantum = 2 * PACK_BLK
    total = -(-(sum(wide_rows) + -(-n_tail // PACK_W)) // quantum) * quantum
    tail_rows = total - sum(wide_rows)

    def tail_pack(pieces):
        flat = jnp.concatenate([p.reshape(-1) for p in pieces] + [jnp.zeros((tail_rows * PACK_W - n_tail,), F32)])
        return flat.reshape(tail_rows, PACK_W)

    parts = [grads[n].reshape(N_CHIPS, r, PACK_W) for n, r in zip(wide_names, wide_rows)]
    cols = []
    for (n, sh, ax), bs in zip(tail, tail_blocks):
        g = grads[n].reshape(sh)
        if ax is None:
            cols.append(jnp.broadcast_to(g.reshape(1, -1), (N_CHIPS, g.size)))
        else:
            g = g.reshape(sh[:ax] + (N_CHIPS, bs[ax]) + sh[ax + 1:])
            cols.append(jnp.moveaxis(g, ax, 0).reshape(N_CHIPS, -1))
    cols.append(jnp.zeros((N_CHIPS, tail_rows * PACK_W - n_tail), F32))
    packs = jnp.concatenate(parts + [jnp.concatenate(cols, axis=1).reshape(N_CHIPS, tail_rows, PACK_W)], axis=1)
    rh = total // 2
    keep = lax.dynamic_slice_in_dim(packs, c * rh, rh, axis=1)
    give = lax.dynamic_slice_in_dim(packs, (1 - c) * rh, rh, axis=1)
    chip_half = add2(keep, sibling_send(give, "grads_to_sibling"), "grads_chip_sum")
    total_half = sum_chips(chips_scatter(chip_half))
    other_half = sibling_send(total_half, "grads_from_sibling")
    g_pack = jnp.concatenate([jnp.where(c == 0, total_half, other_half),
                              jnp.where(c == 0, other_half, total_half)], axis=0)
    outs = ({}, {}, {}, {})
    off = 0
    for (n, transposed), r in zip(WIDE, wide_rows):
        g = g_pack[off:off + r]
        off += r
        g = g.T if transposed else g
        shape = local_w[n].shape
        res = adamw(local_w[n].reshape(g.shape), g, local_m[n].reshape(g.shape), local_v[n].reshape(g.shape),
                    "adamw_" + n)
        for d, a in zip(outs, (g, *res)):
            d[n] = a.reshape(shape)
    g_tail = g_pack[off:]
    res = adamw(*[tail_pack([d[n] for n, _, _ in tail]) if d is not None else g_tail
                  for d in (local_w, None, local_m, local_v)], "adamw_small")
    for d, a in zip(outs, (g_tail, *res)):
        d.update(zip([n for n, _, _ in tail], _unpack(a.reshape(-1), tail_blocks)))
    return outs


ROW_TILE = 256


def kernel(x, positions, ev_w_in, ev_conv_w, ev_conv_b, ev_a_log, ev_dt_bias, ev_d_skip, ev_norm_g, ev_sc_conv_w, ev_w_out, ev_ln_g, ev_ln_b, od_w_in, od_q_norm_g, od_w_uq, od_kv_norm_g, od_w_ukv, od_pool_w, od_pool_scale, od_w_out, od_ln_g, od_ln_b, loss_target, m_ev_w_in, m_ev_conv_w, m_ev_conv_b, m_ev_a_log, m_ev_dt_bias, m_ev_d_skip, m_ev_norm_g, m_ev_sc_conv_w, m_ev_w_out, m_ev_ln_g, m_ev_ln_b, m_od_w_in, m_od_q_norm_g, m_od_w_uq, m_od_kv_norm_g, m_od_w_ukv, m_od_pool_w, m_od_pool_scale, m_od_w_out, m_od_ln_g, m_od_ln_b, v_ev_w_in, v_ev_conv_w, v_ev_conv_b, v_ev_a_log, v_ev_dt_bias, v_ev_d_skip, v_ev_norm_g, v_ev_sc_conv_w, v_ev_w_out, v_ev_ln_g, v_ev_ln_b, v_od_w_in, v_od_q_norm_g, v_od_w_uq, v_od_kv_norm_g, v_od_w_ukv, v_od_pool_w, v_od_pool_scale, v_od_w_out, v_od_ln_g, v_od_ln_b):
    names = [n for n, _, _ in WEIGHTS]
    local_w = dict(zip(names, (ev_w_in, ev_conv_w, ev_conv_b, ev_a_log, ev_dt_bias, ev_d_skip, ev_norm_g, ev_sc_conv_w, ev_w_out, ev_ln_g, ev_ln_b, od_w_in, od_q_norm_g, od_w_uq, od_kv_norm_g, od_w_ukv, od_pool_w, od_pool_scale, od_w_out, od_ln_g, od_ln_b)))
    local_m = dict(zip(names, (m_ev_w_in, m_ev_conv_w, m_ev_conv_b, m_ev_a_log, m_ev_dt_bias, m_ev_d_skip, m_ev_norm_g, m_ev_sc_conv_w, m_ev_w_out, m_ev_ln_g, m_ev_ln_b, m_od_w_in, m_od_q_norm_g, m_od_w_uq, m_od_kv_norm_g, m_od_w_ukv, m_od_pool_w, m_od_pool_scale, m_od_w_out, m_od_ln_g, m_od_ln_b)))
    local_v = dict(zip(names, (v_ev_w_in, v_ev_conv_w, v_ev_conv_b, v_ev_a_log, v_ev_dt_bias, v_ev_d_skip, v_ev_norm_g, v_ev_sc_conv_w, v_ev_w_out, v_ev_ln_g, v_ev_ln_b, v_od_w_in, v_od_q_norm_g, v_od_w_uq, v_od_kv_norm_g, v_od_w_ukv, v_od_pool_w, v_od_pool_scale, v_od_w_out, v_od_ln_g, v_od_ln_b)))
    s = x.shape[1]
    rows = Rows(s, min(ROW_TILE, s))
    f = gather_weights(local_w)
    ew = even_weights(f["ev_w_in"][0], f["ev_conv_w"][0], f["ev_conv_b"], f["ev_a_log"][0], f["ev_dt_bias"][0],
                      f["ev_d_skip"][0], f["ev_norm_g"], f["ev_sc_conv_w"][0], f["ev_w_out"][0], f["ev_ln_g"],
                      f["ev_ln_b"])
    ow = odd_weights(f["od_w_in"][0], f["od_q_norm_g"], f["od_w_uq"][0], f["od_kv_norm_g"], f["od_w_ukv"][0],
                     f["od_pool_w"][0], f["od_pool_scale"], f["od_w_out"][0], f["od_ln_g"], f["od_ln_b"])
    x1, saved = even_layer(x[0], ew, rows)
    loss_lanes, dx1, g_odd = odd_layer_loss(x1, positions.reshape(s, 1), loss_target[0], ow, rows)
    dx0, g_even = even_layer_bwd(dx1, ew, saved, rows)
    loss = lax.psum(jnp.sum(loss_lanes), ("x", "y", "c"))
    grad, delta, new_m, new_v = reduce_and_update({**g_even, **g_odd}, local_w, local_m, local_v)
    return (loss, dx0[None], *[grad[n] for n in names], *[delta[n] for n in names],
            *[new_m[n] for n in names], *[new_v[n] for n in names])
```

```python
import functools
import math

import jax
import jax.numpy as jnp
import numpy as np
from jax import lax
from jax.experimental import pallas as pl
from jax.experimental.pallas import tpu as pltpu

F32 = jnp.float32
BF16 = jnp.bfloat16
MATMUL_DTYPE = jnp.bfloat16

D_MODEL = 1024
DEPTH = 2
SSD_HEADS, SSD_HEAD_DIM, SSD_GROUPS, SSD_STATE, SSD_CHUNK = 16, 64, 4, 128, 128
SSD_INNER = SSD_HEADS * SSD_HEAD_DIM
SSD_XBC = SSD_INNER + 2 * SSD_GROUPS * SSD_STATE
SC_WIDTH = 1024
MLA_HEADS, MLA_Q_RANK, MLA_KV_RANK, MLA_NOPE, MLA_ROPE, MLA_V = 8, 256, 128, 64, 32, 64
MLA_WIDTH = MLA_HEADS * MLA_V
ROPE_THETA = 10000.0
ATTN_SCALE = (MLA_NOPE + MLA_ROPE) ** -0.5
QSCALE = ATTN_SCALE * math.log2(math.e)
LN2 = math.log(2.0)
POOL_WINDOWS = (2, 4, 8, 16)
POOL_GROUP = 128
POOL_WIDTH = POOL_GROUP * len(POOL_WINDOWS)
EPS = 1e-5
ALPHA = (2 * DEPTH) ** 0.25
EVEN_PROJ, ODD_PROJ = 7200, 1952
ADAM_LR, ADAM_B1, ADAM_B2, ADAM_EPS, ADAM_WD, ADAM_STEP = 0.001, 0.9, 0.999, 1e-08, 0.01, 10

LANES = 128
SUBLANES = 8
HALO = SUBLANES
VMEM_LIMIT = 56 * 1024 * 1024

EVEN_P = 7296
EVEN_F = 2176
G_Z, G_BG, G_CG, G_H, G_GATE = 0, 1024, 2048, 3072, 4096
ODD_P = 2048
E_XBC, E_DT = 0, 2048
O_CQ, O_CKV, O_KR, O_GC, O_UD, O_GD = 0, 256, 384, 512, 1024, 1536


def _params(sem=None):
    return pltpu.CompilerParams(dimension_semantics=sem, vmem_limit_bytes=VMEM_LIMIT)


def _mm(a, b, dims=(((1,), (0,)), ((), ()))):
    return lax.dot_general(a.astype(MATMUL_DTYPE), b.astype(MATMUL_DTYPE), dims, preferred_element_type=F32)


_NN = (((1,), (0,)), ((), ()))
_NT = (((1,), (1,)), ((), ()))
_TN = (((0,), (0,)), ((), ()))


def _silu(v):
    return v * jax.nn.sigmoid(v)


def _dsilu(v):
    s = jax.nn.sigmoid(v)
    return s * (1.0 + v * (1.0 - s))


def _pick(n, prefs):
    for p in prefs:
        if n % p == 0:
            return p
    return n


def matmul(a, b, mode, out_dtype, name, add=None, tm=None, tn=None, tk=None):
    if mode == "nn":
        (m, k), (k2, n) = a.shape, b.shape
    elif mode == "nt":
        (m, k), (n, k2) = a.shape, b.shape
    else:
        (k, m), (k2, n) = a.shape, b.shape
    assert k == k2, (a.shape, b.shape, mode)
    wide = (2432, 2048, 1536, 1024, 512, 256, 128)
    if mode == "tn":
        tm = tm or _pick(m, (2432, 2048, 1024, 512, 256, 128))
        tn = tn or _pick(n, wide)
        tk = tk or _pick(k, (512, 256, 128))
    else:
        tm = tm or _pick(m, (512, 256, 128))
        tn = tn or _pick(n, wide)
        tk = tk or _pick(k, wide)
    nk = k // tk
    dims = {"nn": _NN, "nt": _NT, "tn": _TN}[mode]

    def body(a_ref, b_ref, *rest):
        o_ref, acc_ref = rest[-2:]
        kk = pl.program_id(2)

        @pl.when(kk == 0)
        def _():
            acc_ref[...] = jnp.zeros_like(acc_ref) if add is None else rest[0][...].astype(F32)

        acc_ref[...] += _mm(a_ref[...], b_ref[...], dims)

        @pl.when(kk == nk - 1)
        def _():
            o_ref[...] = acc_ref[...].astype(o_ref.dtype)

    a_spec = {"nn": pl.BlockSpec((tm, tk), lambda i, j, kk: (i, kk)),
              "nt": pl.BlockSpec((tm, tk), lambda i, j, kk: (i, kk)),
              "tn": pl.BlockSpec((tk, tm), lambda i, j, kk: (kk, i))}[mode]
    b_spec = {"nn": pl.BlockSpec((tk, tn), lambda i, j, kk: (kk, j)),
              "nt": pl.BlockSpec((tn, tk), lambda i, j, kk: (j, kk)),
              "tn": pl.BlockSpec((tk, tn), lambda i, j, kk: (kk, j))}[mode]
    return pl.pallas_call(
        body, name=name, grid=(m // tm, n // tn, nk),
        in_specs=[a_spec, b_spec] + ([] if add is None else [pl.BlockSpec((tm, tn), lambda i, j, kk: (i, j))]),
        out_specs=pl.BlockSpec((tm, tn), lambda i, j, kk: (i, j)),
        out_shape=jax.ShapeDtypeStruct((m, n), out_dtype),
        scratch_shapes=[pltpu.VMEM((tm, tn), F32)],
        compiler_params=_params(("parallel", "parallel", "arbitrary")),
    )(*((a, b) if add is None else (a, b, add)))


class Rows:
    def __init__(self, s, t):
        assert s % t == 0 and t % HALO == 0
        self.s, self.t, self.n = s, t, s // t

    def tile(self, width, col=0, lead=None):
        cb = col // width
        assert col % width == 0
        if lead is None:
            return pl.BlockSpec((self.t, width), lambda i: (i, cb))
        return pl.BlockSpec((None, self.t, width), lambda i: (lead, i, cb))

    def prev(self, width, col=0, lead=None):
        cb, r = col // width, self.t // HALO
        if lead is None:
            return pl.BlockSpec((HALO, width), lambda i: (jnp.maximum(i * r - 1, 0), cb))
        return pl.BlockSpec((None, HALO, width), lambda i: (lead, jnp.maximum(i * r - 1, 0), cb))

    def next(self, width, col=0, lead=None):
        cb, r, last = col // width, self.t // HALO, self.s // HALO - 1
        if lead is None:
            return pl.BlockSpec((HALO, width), lambda i: (jnp.minimum((i + 1) * r, last), cb))
        return pl.BlockSpec((None, HALO, width), lambda i: (lead, jnp.minimum((i + 1) * r, last), cb))

    def halo(self, width, col=0, lead=None):
        return [self.prev(width, col, lead), self.tile(width, col, lead), self.next(width, col, lead)]

    @staticmethod
    def full(shape):
        nd = len(shape)
        return pl.BlockSpec(tuple(shape), lambda i: (0,) * nd)


def rows_call(name, rows, fn, ins, in_specs, row_outs, acc_outs=()):
    n_row = len(row_outs)

    def body(*refs):
        in_refs = refs[:len(ins)]
        out_refs = refs[len(ins):]
        res = fn(*[r[...] for r in in_refs])
        if not isinstance(res, (tuple, list)):
            res = (res,)
        for r, v in zip(out_refs[:n_row], res[:n_row]):
            r[...] = v.astype(r.dtype)
        if acc_outs:
            first = pl.program_id(0) == 0

            @pl.when(first)
            def _():
                for r, v in zip(out_refs[n_row:], res[n_row:]):
                    r[...] = v.astype(F32)

            @pl.when(jnp.logical_not(first))
            def _():
                for r, v in zip(out_refs[n_row:], res[n_row:]):
                    r[...] += v.astype(F32)

    out_shape = [jax.ShapeDtypeStruct((rows.s, w), dt) for (w, dt) in row_outs]
    out_specs = [rows.tile(w) for (w, dt) in row_outs]
    out_shape += [jax.ShapeDtypeStruct(tuple(sh), F32) for sh in acc_outs]
    out_specs += [Rows.full(sh) for sh in acc_outs]
    return pl.pallas_call(
        body, name=name, grid=(rows.n,), in_specs=list(in_specs), out_specs=out_specs, out_shape=out_shape,
        compiler_params=_params(("arbitrary",)),
    )(*ins)


def _edge_zero(prev, nxt, n_tiles):
    i = pl.program_id(0)
    prev = jnp.where(i == 0, jnp.zeros_like(prev), prev)
    nxt = jnp.where(i == n_tiles - 1, jnp.zeros_like(nxt), nxt)
    return prev, nxt


def _ext(prev, cur, nxt, n_tiles):
    prev, nxt = _edge_zero(prev, nxt, n_tiles)
    return jnp.concatenate([prev.astype(F32), cur.astype(F32), nxt.astype(F32)], axis=0)


def _shift(ext, off):
    n = ext.shape[0]
    t = n - 2 * HALO
    if off == 0:
        return ext[HALO:HALO + t]
    return pltpu.roll(ext, (-off) % n, 0)[HALO:HALO + t]


def _rowsum(v):
    return jnp.sum(v, axis=0, keepdims=True)


CONV_OFFS = (-2, -1, 0, 1)
SC_OFFS = (-1, 0, 1)


def conv_fwd(proj, conv_w, conv_b, rows):
    def fn(p, c, nx, w, b):
        e = _ext(p, c, nx, rows.n)
        pre = b
        for k, off in enumerate(CONV_OFFS):
            pre = pre + w[k:k + 1, :] * _shift(e, off)
        return pre, _silu(pre)

    outs = [rows_call(f"ev_conv_fwd{h}", rows, fn,
                      [proj, proj, proj, conv_w[:, h * 1024:(h + 1) * 1024], conv_b[:, h * 1024:(h + 1) * 1024]],
                      rows.halo(1024, E_XBC + h * 1024) + [Rows.full((4, 1024)), Rows.full((1, 1024))],
                      [(1024, MATMUL_DTYPE), (1024, F32)]) for h in range(2)]
    return outs


def conv_bwd(proj, pre_h, du_h, conv_w, rows):
    res = []
    for h in range(2):
        def fn(*a):
            w = a[-1]
            xe = _ext(a[0], a[1], a[2], rows.n)
            pe = jnp.concatenate([a[3], a[4], a[5]], axis=0).astype(F32)
            g = a[6:-1]
            du = _ext(g[0], g[1], g[2], rows.n) + _ext(g[3], g[4], g[5], rows.n)
            dpre = du * _dsilu(pe)
            dx = jnp.zeros_like(a[1], dtype=F32)
            dws = []
            for k, off in enumerate(CONV_OFFS):
                dx = dx + w[k:k + 1, :] * _shift(dpre, -off)
                dws.append(_rowsum(_shift(dpre, 0) * _shift(xe, off)))
            dw = jnp.concatenate(dws + [jnp.zeros((4, dx.shape[1]), F32)], axis=0)
            return dx, dw, _rowsum(_shift(dpre, 0))

        gi = [du_h[h][0]] * 3 + [du_h[h][1]] * 3
        gs = rows.halo(1024) * 2
        res.append(rows_call(
            f"ev_conv_bwd{h}", rows, fn,
            [proj] * 3 + [pre_h[h]] * 3 + gi + [conv_w[:, h * 1024:(h + 1) * 1024]],
            rows.halo(1024, E_XBC + h * 1024) + rows.halo(1024) + gs + [Rows.full((4, 1024))],
            [(1024, MATMUL_DTYPE)], [(8, 1024), (1, 1024)]))
    dconv_w = jnp.concatenate([res[0][1][:4], res[1][1][:4]], axis=1)
    dconv_b = jnp.concatenate([res[0][2], res[1][2]], axis=1)
    return [res[0][0], res[1][0]], dconv_w, dconv_b


def _head_row(p):
    return jnp.concatenate([p.reshape(1, 2 * SSD_HEADS), jnp.zeros((1, LANES - 2 * SSD_HEADS), F32)], axis=1)


def _head_unrow(r):
    return r[:, :2 * SSD_HEADS].reshape(2, SSD_HEADS)


def _ssd_pre_fn(dtraw, bias_row, alog_row):
    q = dtraw.shape[0]
    dt = jax.nn.softplus(dtraw + bias_row)
    da = dt * (-jnp.exp(alog_row))
    li = lax.broadcasted_iota(jnp.int32, (q, q), 0)
    si = lax.broadcasted_iota(jnp.int32, (q, q), 1)
    tril = (li >= si).astype(F32)
    csf = lax.dot_general(tril, da, _NN, precision=lax.Precision.HIGHEST, preferred_element_type=F32)
    tot = jnp.sum(da, axis=0, keepdims=True)
    lane = lax.broadcasted_iota(jnp.int32, (1, LANES), 1)
    cs = jnp.where(lane < SSD_HEADS, csf, tot - csf + da)
    return dt, cs


def ssd_pre(proj, bias_row, alog_row, s):
    rows = Rows(s, SSD_CHUNK)
    return rows_call("ev_ssd_pre", rows, _ssd_pre_fn, [proj, bias_row, alog_row],
                     [rows.tile(LANES, E_DT), Rows.full((1, LANES)), Rows.full((1, LANES))],
                     [(LANES, F32), (LANES, F32)])


def ssd_pre_bwd(proj, bias_row, alog_row, ddt, dcs, s):
    rows = Rows(s, SSD_CHUNK)

    def fn(dtraw, b, al, g0, g1, c0, c1):
        _, vjp = jax.vjp(_ssd_pre_fn, dtraw, b, al)
        return vjp((g0 + g1, c0 + c1))

    return rows_call("ev_ssd_pre_bwd", rows, fn, [proj, bias_row, alog_row, ddt[0], ddt[1], dcs[0], dcs[1]],
                     [rows.tile(LANES, E_DT), Rows.full((1, LANES)), Rows.full((1, LANES))] + [rows.tile(LANES)] * 4,
                     [(LANES, MATMUL_DTYPE)], [(1, LANES), (1, LANES)])


_N_PAIR = SSD_HEADS // 2
_BC = SSD_GROUPS * SSD_STATE


def _ssd_chunk_fn(x, bc, dt, cs, h_in, dsk_row, d):
    q = x.shape[0]
    lane = lax.broadcasted_iota(jnp.int32, (1, LANES), 1)
    half = lane < SSD_HEAD_DIM
    lo, hi = half.astype(F32), 1.0 - half.astype(F32)
    cst = cs.T
    li = lax.broadcasted_iota(jnp.int32, (q, 1), 0)
    si = lax.broadcasted_iota(jnp.int32, (1, q), 1)
    mask = li >= si if d == 0 else li <= si
    end = q - 1 if d == 0 else 0
    tot = cs[end:end + 1, :]
    mine = (lane >= SSD_HEADS * d) & (lane < SSD_HEADS * (d + 1))
    e_cs, e_dec, e_tot = jnp.exp(cs), jnp.exp(jnp.where(mine, tot - cs, 0.0)), jnp.exp(tot)
    e_dd = e_dec * dt
    dtt = dt.T

    def col(v, l):
        return v[:, l:l + 1]

    def by_head(v, l):
        return jnp.where(half, col(v, l), col(v, l + 1))

    per_group = _N_PAIR // SSD_GROUPS
    ys, hs = [], []
    for g in range(SSD_GROUPS):
        bm = bc[:, g * SSD_STATE:(g + 1) * SSD_STATE]
        cm = bc[:, _BC + g * SSD_STATE:_BC + (g + 1) * SSD_STATE]
        cb = _mm(cm, bm, _NT)
        pairs = range(g * per_group, (g + 1) * per_group)
        h_g = [h_in[j * SSD_STATE:(j + 1) * SSD_STATE, :] for j in pairs]
        y_off = _mm(cm, jnp.concatenate(h_g, axis=1))
        for k, j in enumerate(pairs):
            l0 = SSD_HEADS * d + 2 * j
            xj = x[:, j * LANES:(j + 1) * LANES]
            xcat = jnp.concatenate([xj * lo, xj * hi], axis=0)
            w0 = cb * jnp.exp(jnp.where(mask, col(cs, l0) - cst[l0:l0 + 1, :], -jnp.inf)) * dtt[l0:l0 + 1, :]
            w1 = cb * jnp.exp(jnp.where(mask, col(cs, l0 + 1) - cst[l0 + 1:l0 + 2, :], -jnp.inf)) * dtt[l0 + 1:l0 + 2, :]
            y = _mm(jnp.concatenate([w0, w1], axis=1), xcat)
            st = _mm(jnp.concatenate([bm * col(e_dd, l0), bm * col(e_dd, l0 + 1)], axis=0), xcat, _TN)
            y = y + y_off[:, k * LANES:(k + 1) * LANES] * by_head(e_cs, l0) + by_head(dsk_row, l0) * xj
            ys.append(y)
            hs.append(h_g[k] * by_head(e_tot, l0) + st)
    return jnp.concatenate(ys, axis=1), jnp.concatenate(hs, axis=0)


def _chunk_of(d, ci, nc, backward):
    up = ci if not backward else nc - 1 - ci
    return up + d * (nc - 1 - 2 * up)


_ST_ROWS = _N_PAIR * SSD_STATE


def _ssd_fwd_dir(u_h, dt, cs, dsk_row, s, d):
    nc = s // SSD_CHUNK
    q = SSD_CHUNK

    def body(x_ref, bc_ref, dt_ref, cs_ref, dsk_ref, y_ref, hs_ref, st_ref):
        @pl.when(pl.program_id(0) == 0)
        def _():
            st_ref[...] = jnp.zeros(st_ref.shape, F32)

        h_in = st_ref[...]
        y, h_out = _ssd_chunk_fn(x_ref[...], bc_ref[...], dt_ref[...], cs_ref[...], h_in, dsk_ref[...], d)
        y_ref[...] = y.astype(y_ref.dtype)
        hs_ref[...] = h_in
        st_ref[...] = h_out

    ch = lambda ci: _chunk_of(d, ci, nc, False)
    return pl.pallas_call(
        body, name=f"ev_ssd_fwd{d}", grid=(nc,),
        in_specs=[pl.BlockSpec((q, SSD_INNER), lambda ci: (ch(ci), 0)),
                  pl.BlockSpec((q, 2 * _BC), lambda ci: (ch(ci), 0)),
                  pl.BlockSpec((q, LANES), lambda ci: (ch(ci), 0)),
                  pl.BlockSpec((q, LANES), lambda ci: (ch(ci), 0)),
                  pl.BlockSpec((1, LANES), lambda ci: (0, 0))],
        out_specs=[pl.BlockSpec((q, SSD_INNER), lambda ci: (ch(ci), 0)),
                   pl.BlockSpec((None, _ST_ROWS, LANES), lambda ci: (ch(ci), 0, 0))],
        out_shape=[jax.ShapeDtypeStruct((s, SSD_INNER), MATMUL_DTYPE),
                   jax.ShapeDtypeStruct((nc, _ST_ROWS, LANES), F32)],
        scratch_shapes=[pltpu.VMEM((_ST_ROWS, LANES), F32)],
        compiler_params=_params(("arbitrary",)),
    )(u_h[0], u_h[1], dt, cs, dsk_row)


def ssd_fwd(u_h, dt, cs, dsk_row, s):
    ys, hss = zip(*[_ssd_fwd_dir(u_h, dt, cs, dsk_row, s, d) for d in range(2)])
    return ys, hss


def _ssd_bwd_dir(u_h, dt, cs, dsk_row, hs, dy, s, d):
    nc = s // SSD_CHUNK
    q = SSD_CHUNK

    def body(x_ref, bc_ref, dt_ref, cs_ref, dsk_ref, hs_ref, dy_ref,
             dx_ref, dbc_ref, ddt_ref, dcs_ref, ddsk_ref, dst_ref):
        ci = pl.program_id(0)

        @pl.when(ci == 0)
        def _():
            dst_ref[...] = jnp.zeros(dst_ref.shape, F32)

        f = functools.partial(_ssd_chunk_fn, d=d)
        _, vjp = jax.vjp(f, x_ref[...], bc_ref[...], dt_ref[...], cs_ref[...], hs_ref[...], dsk_ref[...])
        dx, dbc, ddt, dcs, dh, ddsk = vjp((dy_ref[...], dst_ref[...]))
        dx_ref[...] = dx
        dbc_ref[...] = dbc
        ddt_ref[...] = ddt
        dcs_ref[...] = dcs
        dst_ref[...] = dh

        @pl.when(ci == 0)
        def _():
            ddsk_ref[...] = ddsk

        @pl.when(ci != 0)
        def _():
            ddsk_ref[...] += ddsk

    ch = lambda ci: _chunk_of(d, ci, nc, True)
    row_blk = lambda w: pl.BlockSpec((q, w), lambda ci: (ch(ci), 0))
    return pl.pallas_call(
        body, name=f"ev_ssd_bwd{d}", grid=(nc,),
        in_specs=[row_blk(SSD_INNER), row_blk(2 * _BC), row_blk(LANES), row_blk(LANES),
                  pl.BlockSpec((1, LANES), lambda ci: (0, 0)),
                  pl.BlockSpec((None, _ST_ROWS, LANES), lambda ci: (ch(ci), 0, 0)), row_blk(SSD_INNER)],
        out_specs=[row_blk(SSD_INNER), row_blk(2 * _BC), row_blk(LANES), row_blk(LANES),
                   pl.BlockSpec((1, LANES), lambda ci: (0, 0))],
        out_shape=[jax.ShapeDtypeStruct((s, SSD_INNER), F32), jax.ShapeDtypeStruct((s, 2 * _BC), F32),
                   jax.ShapeDtypeStruct((s, LANES), F32), jax.ShapeDtypeStruct((s, LANES), F32),
                   jax.ShapeDtypeStruct((1, LANES), F32)],
        scratch_shapes=[pltpu.VMEM((_ST_ROWS, LANES), F32)],
        compiler_params=_params(("arbitrary",)),
    )(u_h[0], u_h[1], dt, cs, dsk_row, hs, dy)


def ssd_bwd(u_h, dt, cs, dsk_row, hs, dy, s):
    return zip(*[_ssd_bwd_dir(u_h, dt, cs, dsk_row, hs[d], dy, s, d) for d in range(2)])


def _gated_rms(ys, z, g):
    t1 = ys * _silu(z)
    return t1 * lax.rsqrt(jnp.mean(t1 * t1, axis=-1, keepdims=True) + EPS) * g


def even_mix_fwd(proj, y2, norm_g, sc_w, rows):
    def fn(yf, yb, z, bg, cgp, cg, cgn, hp, hh, hn, gate, g, w):
        z, bg, gate = z.astype(F32), bg.astype(F32), gate.astype(F32)
        ya = _gated_rms(yf.astype(F32) + yb.astype(F32), z, g)
        me = _ext(cgp, cg, cgn, rows.n) * _ext(hp, hh, hn, rows.n)
        cm = sum(w[k:k + 1, :] * _shift(me, off) for k, off in enumerate(SC_OFFS))
        return jnp.concatenate([ya, bg * cm * _silu(gate)], axis=1)

    w = 1024
    return rows_call("ev_mix_fwd", rows, fn,
                     [y2[0], y2[1], proj, proj] + [proj] * 6 + [proj, norm_g, sc_w],
                     [rows.tile(w), rows.tile(w), rows.tile(w, G_Z), rows.tile(w, G_BG)]
                     + rows.halo(w, G_CG) + rows.halo(w, G_H)
                     + [rows.tile(w, G_GATE), Rows.full((1, w)), Rows.full((3, w))],
                     [(2 * w, MATMUL_DTYPE)])[0]


def even_mix_bwd(proj, y2, norm_g, sc_w, dyab, rows):
    w = 1024

    def fn(yf, yb, z, g, sw, dya, *a):
        (dybp, dyb, dybn, bgp, bg, bgn, gtp, gt, gtn, cgp, cg, cgn, hp, hh, hn) = a
        z, bg, gt, cg, hh = (t.astype(F32) for t in (z, bg, gt, cg, hh))
        _, vjp = jax.vjp(_gated_rms, yf.astype(F32) + yb.astype(F32), z, g)
        dys, dz, dg = vjp(dya.astype(F32))
        n = rows.n
        dye, bge, gte = _ext(dybp, dyb, dybn, n), _ext(bgp, bg, bgn, n), _ext(gtp, gt, gtn, n)
        cge, he = _ext(cgp, cg, cgn, n), _ext(hp, hh, hn, n)
        me = cge * he
        cm = sum(sw[k:k + 1, :] * _shift(me, off) for k, off in enumerate(SC_OFFS))
        dyc = dyb.astype(F32)
        dbg = dyc * cm * _silu(gt)
        dgate = dyc * bg * cm * _dsilu(gt)
        dcme = dye * bge * _silu(gte)
        dm = sum(sw[k:k + 1, :] * _shift(dcme, -off) for k, off in enumerate(SC_OFFS))
        dcm = _shift(dcme, 0)
        dws = [_rowsum(dcm * _shift(me, off)) for off in SC_OFFS]
        dsw = jnp.concatenate(dws + [jnp.zeros((5, w), F32)], axis=0)
        return dys, dz, dbg, dm * hh, dm * cg, dgate, _rowsum(dg), dsw

    return rows_call(
        "ev_mix_bwd", rows, fn,
        [y2[0], y2[1], proj, norm_g, sc_w, dyab] + [dyab] * 3 + [proj] * 12,
        [rows.tile(w), rows.tile(w), rows.tile(w, G_Z), Rows.full((1, w)), Rows.full((3, w)),
         rows.tile(w, 0)] + rows.halo(w, w) + rows.halo(w, G_BG) + rows.halo(w, G_GATE)
        + rows.halo(w, G_CG) + rows.halo(w, G_H),
        [(w, F32)] + [(w, MATMUL_DTYPE)] * 5, [(1, w), (8, w)])


def _res_ln(x, h, g, b):
    v = ALPHA * x + h
    mu = jnp.mean(v, axis=-1, keepdims=True)
    var = jnp.mean(jnp.square(v - mu), axis=-1, keepdims=True)
    return (v - mu) * lax.rsqrt(var + EPS) * g + b


def res_ln_fwd(x, h, g, b, rows, name):
    return rows_call(name, rows, _res_ln, [x, h, g, b],
                     [rows.tile(D_MODEL), rows.tile(D_MODEL), Rows.full((1, D_MODEL)), Rows.full((1, D_MODEL))],
                     [(D_MODEL, F32)])[0]


def res_ln_bwd(x, h, g, b, dy, rows, name):
    def fn(x_, h_, g_, b_, dy_):
        _, vjp = jax.vjp(_res_ln, x_, h_, g_, b_)
        dx, dh, dg, db = vjp(dy_)
        return dx, dh, dg, db

    return rows_call(name, rows, fn, [x, h, g, b, dy],
                     [rows.tile(D_MODEL), rows.tile(D_MODEL), Rows.full((1, D_MODEL)), Rows.full((1, D_MODEL)),
                      rows.tile(D_MODEL)],
                     [(D_MODEL, F32), (D_MODEL, F32)], [(1, D_MODEL), (1, D_MODEL)])


def final_ln_loss(x, h, g, b, target, rows):
    def fn(x_, h_, g_, b_, t_):
        y, vjp = jax.vjp(_res_ln, x_, h_, g_, b_)
        err = y - t_
        dx, dh, dg, db = vjp(err * (1.0 / D_MODEL))
        return dx, dh, dg, db, _rowsum(jnp.square(err)) * (0.5 / D_MODEL)

    return rows_call("od_ln_loss", rows, fn, [x, h, g, b, target],
                     [rows.tile(D_MODEL), rows.tile(D_MODEL), Rows.full((1, D_MODEL)), Rows.full((1, D_MODEL)),
                      rows.tile(D_MODEL)],
                     [(D_MODEL, F32), (D_MODEL, F32)], [(1, D_MODEL), (1, D_MODEL), (1, D_MODEL)])


def pad_even_w_in(w):
    return jnp.concatenate([w[:, 1024:3104], jnp.zeros((w.shape[0], EVEN_P - EVEN_PROJ), w.dtype), w[:, :1024],
                            w[:, 3104:]], axis=1)


def matmul_pieces_nt(pieces, w, name, add):
    m, n, npc = pieces[0].shape[0], w.shape[0], len(pieces)
    widths = [p.shape[1] for p in pieces]
    assert sum(widths) == w.shape[1]
    tm = _pick(m, (256, 128))

    def body(*refs):
        p_refs = refs[:npc]
        w_ref, add_ref, o_ref = refs[npc:]
        acc, off = add_ref[...].astype(F32), 0
        for p_ref, wd in zip(p_refs, widths):
            acc = acc + _mm(p_ref[...], w_ref[:, off:off + wd], _NT)
            off += wd
        o_ref[...] = acc

    return pl.pallas_call(
        body, name=name, grid=(m // tm,),
        in_specs=[pl.BlockSpec((tm, wd), lambda i: (i, 0)) for wd in widths]
        + [pl.BlockSpec(w.shape, lambda i: (0, 0), pipeline_mode=pl.Buffered(1)),
           pl.BlockSpec((tm, n), lambda i: (i, 0))],
        out_specs=pl.BlockSpec((tm, n), lambda i: (i, 0)),
        out_shape=jax.ShapeDtypeStruct((m, n), F32),
        compiler_params=_params(("parallel",)),
    )(*pieces, w, add)


def even_layer(x, w, rows):
    s = rows.s
    xb = x.astype(MATMUL_DTYPE)
    proj = matmul(xb, w["w_in_p"][:, :EVEN_F], "nn", F32, "ev_proj_f", tn=EVEN_F)
    proj_g = matmul(xb, w["w_in_p"][:, EVEN_F:], "nn", MATMUL_DTYPE, "ev_proj_g", tn=(EVEN_P - EVEN_F) // 2)
    (pre0, u0), (pre1, u1) = conv_fwd(proj, w["conv_w"], w["conv_b"], rows)
    dt, cs = ssd_pre(proj, w["bias_row"], w["alog_row"], s)
    y2, hs = ssd_fwd((u0, u1), dt, cs, w["dsk_row"], s)
    yab = even_mix_fwd(proj_g, y2, w["norm_g"], w["sc_w"], rows)
    h = matmul(yab, w["w_out"], "nn", F32, "ev_out")
    x1 = res_ln_fwd(x, h, w["ln_g"], w["ln_b"], rows, "ev_ln")
    return x1, dict(x=x, xb=xb, proj=proj, proj_g=proj_g, pre=(pre0, pre1), u=(u0, u1), dt=dt, cs=cs, y2=y2,
                    hs=hs, yab=yab, h=h)


def even_layer_bwd(dx1, w, sv, rows):
    s = rows.s
    dres, dh, dln_g, dln_b = res_ln_bwd(sv["x"], sv["h"], w["ln_g"], w["ln_b"], dx1, rows, "ev_ln_bwd")
    dyab = matmul(dh, w["w_out"], "nt", MATMUL_DTYPE, "ev_out_dx")
    dw_out = matmul(sv["yab"], dh, "tn", F32, "ev_out_dw")
    dys, dz, dbg, dcg, dhh, dgate, dnorm_g, dsw = even_mix_bwd(sv["proj_g"], sv["y2"], w["norm_g"], w["sc_w"],
                                                               dyab, rows)
    dxs, dbc, ddt, dcs, ddsk = ssd_bwd(sv["u"], sv["dt"], sv["cs"], w["dsk_row"], sv["hs"], dys, s)
    ddtraw, dbias_row, dalog_row = ssd_pre_bwd(sv["proj"], w["bias_row"], w["alog_row"], ddt, dcs, s)
    (dxbc0, dxbc1), dconv_w, dconv_b = conv_bwd(sv["proj"], sv["pre"], (dxs, dbc), w["conv_w"], rows)
    pieces = [dxbc0, dxbc1, ddtraw, dz, dbg, dcg, dhh, dgate]
    dx0 = matmul_pieces_nt(pieces, w["w_in_p"], "ev_proj_dx", add=dres)
    dw0, dw1, dw_dt, dw_z, *dw_gates = [matmul(p, sv["xb"], "tn", F32, f"ev_proj_dw{i}")
                                        for i, p in enumerate(pieces)]
    dw_in = jnp.concatenate([dw_z, dw0, dw1, dw_dt[:2 * SSD_HEADS]] + dw_gates, axis=0)
    g = dict(ev_w_in=dw_in, ev_conv_w=dconv_w, ev_conv_b=dconv_b,
             ev_a_log=_head_unrow(dalog_row), ev_dt_bias=_head_unrow(dbias_row),
             ev_d_skip=_head_unrow(ddsk[0] + ddsk[1]), ev_norm_g=dnorm_g, ev_sc_conv_w=dsw[:3],
             ev_w_out=dw_out, ev_ln_g=dln_g, ev_ln_b=dln_b)
    return dx0, g


def even_weights(ev_w_in, ev_conv_w, ev_conv_b, ev_a_log, ev_dt_bias, ev_d_skip, ev_norm_g, ev_sc_conv_w,
                 ev_w_out, ev_ln_g, ev_ln_b):
    return dict(w_in_p=pad_even_w_in(ev_w_in).astype(MATMUL_DTYPE), conv_w=ev_conv_w, conv_b=ev_conv_b,
                alog_row=_head_row(ev_a_log), bias_row=_head_row(ev_dt_bias), dsk_row=_head_row(ev_d_skip),
                norm_g=ev_norm_g, sc_w=ev_sc_conv_w, w_out=ev_w_out.astype(MATMUL_DTYPE), ln_g=ev_ln_g,
                ln_b=ev_ln_b)


HEAD_BLK = LANES
ROPE_LO = MLA_NOPE
ROPE_HALF = MLA_ROPE // 2


def _rms(v, g):
    return v * lax.rsqrt(jnp.mean(v * v, axis=-1, keepdims=True) + EPS) * g


def latent_norm_fwd(proj, gq, gkv, rows):
    def fn(cq, ckv, gq_, gkv_):
        return _rms(cq, gq_), _rms(ckv, gkv_)

    return rows_call("od_norm_fwd", rows, fn, [proj, proj, gq, gkv],
                     [rows.tile(MLA_Q_RANK, O_CQ), rows.tile(MLA_KV_RANK, O_CKV), Rows.full((1, MLA_Q_RANK)),
                      Rows.full((1, MLA_KV_RANK))],
                     [(MLA_Q_RANK, MATMUL_DTYPE), (MLA_KV_RANK, MATMUL_DTYPE)])


def latent_norm_bwd(proj, gq, gkv, dcqn, dckvn, rows):
    def fn(cq, ckv, gq_, gkv_, d1, d2):
        _, vjp = jax.vjp(_rms, cq, gq_)
        dcq, dgq = vjp(d1)
        _, vjp2 = jax.vjp(_rms, ckv, gkv_)
        dckv, dgkv = vjp2(d2)
        return dcq, dckv, dgq, dgkv

    return rows_call("od_norm_bwd", rows, fn, [proj, proj, gq, gkv, dcqn, dckvn],
                     [rows.tile(MLA_Q_RANK, O_CQ), rows.tile(MLA_KV_RANK, O_CKV), Rows.full((1, MLA_Q_RANK)),
                      Rows.full((1, MLA_KV_RANK)), rows.tile(MLA_Q_RANK), rows.tile(MLA_KV_RANK)],
                     [(MLA_Q_RANK, F32), (MLA_KV_RANK, F32)], [(1, MLA_Q_RANK), (1, MLA_KV_RANK)])


def rope_rows():
    lane = np.arange(LANES)
    inv = ROPE_THETA ** (-jnp.arange(ROPE_HALF, dtype=F32) / ROPE_HALF)
    on = (lane >= ROPE_LO) & (lane < ROPE_LO + MLA_ROPE)
    freq = jnp.where(on, inv[(lane - ROPE_LO) % ROPE_HALF], 0.0).reshape(1, LANES).astype(F32)
    sign = np.where(on, np.where(lane < ROPE_LO + ROPE_HALF, -1.0, 1.0), 0.0).reshape(1, LANES).astype(np.float32)
    return freq, jnp.asarray(sign)


def _rot_tables(pos, freq, sign):
    ang = pos.astype(F32) * freq
    return jnp.cos(ang), jnp.sin(ang) * sign


def _swap_halves(v):
    lane = lax.broadcasted_iota(jnp.int32, (1, LANES), 1)
    return jnp.where(lane < ROPE_LO + ROPE_HALF, pltpu.roll(v, LANES - ROPE_HALF, 1), pltpu.roll(v, ROPE_HALF, 1))


def rope_fwd(qp, kvp, proj, pos, freq, sign, rows):
    def fn(q, k, kr, v, p, f, sg):
        c, sn = _rot_tables(p, f, sg)
        rk = kr * c + _swap_halves(kr) * sn
        one = (lax.broadcasted_iota(jnp.int32, (v.shape[0], HEAD_BLK - MLA_V), 1) == 0).astype(F32)
        qs, ks, vs = [], [], []
        for h in range(MLA_HEADS):
            qh = q[:, h * HEAD_BLK:(h + 1) * HEAD_BLK]
            qs.append((qh * c + _swap_halves(qh) * sn) * QSCALE)
            ks.append(k[:, h * HEAD_BLK:(h + 1) * HEAD_BLK] + rk)
            vs += [v[:, h * MLA_V:(h + 1) * MLA_V], one]
        return jnp.concatenate(qs, axis=1), jnp.concatenate(ks, axis=1), jnp.concatenate(vs, axis=1)

    w = MLA_HEADS * HEAD_BLK
    return rows_call("od_rope_fwd", rows, fn, [qp, kvp, proj, kvp, pos, freq, sign],
                     [rows.tile(w), rows.tile(w, 0), rows.tile(LANES, O_KR), rows.tile(MLA_WIDTH, w),
                      rows.tile(1), Rows.full((1, LANES)), Rows.full((1, LANES))],
                     [(w, MATMUL_DTYPE), (w, MATMUL_DTYPE), (w, MATMUL_DTYPE)])


def rope_bwd(dq, dk, dv, pos, freq, sign, rows):
    def fn(dq_, dk_, dv_, p, f, sg):
        c, sn = _rot_tables(p, f, sg)
        on = jnp.abs(sg)
        outs, dks, dvs, dkr = [], [], [], jnp.zeros((dq_.shape[0], LANES), F32)
        for h in range(MLA_HEADS):
            g = dq_[:, h * HEAD_BLK:(h + 1) * HEAD_BLK] * ATTN_SCALE
            outs.append(g * c + _swap_halves(g * sn) * on)
            gk = dk_[:, h * HEAD_BLK:(h + 1) * HEAD_BLK] * LN2
            dks.append(gk)
            dkr = dkr + gk * c + _swap_halves(gk * sn) * on
            dvs.append(dv_[:, h * HEAD_BLK:h * HEAD_BLK + MLA_V])
        return jnp.concatenate(outs, axis=1), jnp.concatenate(dks + dvs, axis=1), dkr

    w = MLA_HEADS * HEAD_BLK
    return rows_call("od_rope_bwd", rows, fn, [dq, dk, dv, pos, freq, sign],
                     [rows.tile(w), rows.tile(w), rows.tile(w), rows.tile(1), Rows.full((1, LANES)),
                      Rows.full((1, LANES))],
                     [(w, MATMUL_DTYPE), (w + MLA_WIDTH, MATMUL_DTYPE), (LANES, F32)])


_PAIRS = MLA_HEADS // 2
ATT_TQ = 512
ATT_TK = 4096
ATT_BWD_TQ = 1024
ATT_BWD_TK = 1024


def _att_tiles(s, backward=False):
    if backward:
        return min(ATT_BWD_TQ, s), min(ATT_BWD_TK, s)
    return min(ATT_TQ, s), min(ATT_TK, s)


def attention_fwd(qcat, kcat, vcat, s):
    tq, tk = _att_tiles(s)
    nq, nk = s // tq, s // tk

    def body(q_ref, k_ref, v_ref, o_ref, lse_ref, p_ref, mt_ref, m_ref, acc_ref):
        kk = pl.program_id(2)
        half = lax.broadcasted_iota(jnp.int32, (1, LANES), 1) < MLA_V

        @pl.when(kk == 0)
        def _():
            m_ref[...] = jnp.full(m_ref.shape, -jnp.inf, F32)
            acc_ref[...] = jnp.zeros(acc_ref.shape, F32)

        sl = [slice(hh * HEAD_BLK, (hh + 1) * HEAD_BLK) for hh in range(2)]
        sc = [_mm(q_ref[:, sl[hh]], k_ref[:, sl[hh]], _NT) for hh in range(2)]
        for hh in range(2):
            m_prev = m_ref[hh]
            m_new = jnp.maximum(m_prev, jnp.max(sc[hh], axis=1, keepdims=True))
            p = jnp.exp2(sc[hh] - m_new[:, :1]).astype(p_ref.dtype)
            p_ref[hh] = p
            acc_ref[hh] = acc_ref[hh] * jnp.exp2(m_prev - m_new) + _mm(p, v_ref[:, sl[hh]])
            m_ref[hh] = m_new
        mt_ref[...] = jnp.where(half, m_ref[0], m_ref[1])

        @pl.when(kk == nk - 1)
        def _():
            l0, l1 = acc_ref[0][:, MLA_V:MLA_V + 1], acc_ref[1][:, MLA_V:MLA_V + 1]
            o_ref[...] = jnp.where(half, acc_ref[0] / l0, pltpu.roll(acc_ref[1] / l1, MLA_V, 1))
            lse_ref[...] = jnp.where(half, m_ref[0] + jnp.log2(l0), m_ref[1] + jnp.log2(l1))

    return pl.pallas_call(
        body, name="od_attn_fwd", grid=(_PAIRS, nq, nk),
        in_specs=[pl.BlockSpec((tq, 2 * HEAD_BLK), lambda p, i, kk: (i, p)),
                  pl.BlockSpec((tk, 2 * HEAD_BLK), lambda p, i, kk: (kk, p)),
                  pl.BlockSpec((tk, 2 * HEAD_BLK), lambda p, i, kk: (kk, p))],
        out_specs=[pl.BlockSpec((tq, LANES), lambda p, i, kk: (i, p)),
                   pl.BlockSpec((None, tq, LANES), lambda p, i, kk: (p, i, 0)),
                   pl.BlockSpec((2, tq, tk), lambda p, i, kk: (p, i, kk)),
                   pl.BlockSpec((None, None, tq, LANES), lambda p, i, kk: (p, kk, i, 0))],
        out_shape=[jax.ShapeDtypeStruct((s, MLA_WIDTH), F32), jax.ShapeDtypeStruct((_PAIRS, s, LANES), F32),
                   jax.ShapeDtypeStruct((MLA_HEADS, s, s), MATMUL_DTYPE),
                   jax.ShapeDtypeStruct((_PAIRS, nk, s, LANES), F32)],
        scratch_shapes=[pltpu.VMEM((2, tq, LANES), F32)] * 2,
        compiler_params=_params(("parallel", "parallel", "arbitrary")),
    )(qcat, kcat, vcat)


def attention_bwd(qcat, kcat, vcat, o, lse, pst, mt, do, s):
    tq, tk = _att_tiles(s, backward=True)
    nq, nk = s // tq, s // tk
    per_fwd_tile = _att_tiles(s)[1] // tk
    assert per_fwd_tile * tk == _att_tiles(s)[1]

    def body(q_ref, k_ref, v_ref, do_ref, o_ref, lse_ref, p_ref, mt_ref, dq_ref, dk_ref, dv_ref):
        kk, i = pl.program_id(1), pl.program_id(2)
        lane = lax.broadcasted_iota(jnp.int32, (1, LANES), 1)
        half = lane < MLA_V
        c = jnp.exp2(mt_ref[...] - lse_ref[...])
        do_p = do_ref[...] * c
        prod = do_p * o_ref[...]
        rows_i = pl.ds(pl.multiple_of(i * tq, tq), tq)
        for hh in range(2):
            sl = slice(hh * HEAD_BLK, (hh + 1) * HEAD_BLK)
            mine = half if hh == 0 else jnp.logical_not(half)
            delta = jnp.sum(jnp.where(mine, prod, 0.0), axis=1, keepdims=True)
            do_h = jnp.where(half, do_p if hh == 0 else pltpu.roll(do_p, MLA_V, 1), 0.0)
            p = p_ref[hh]
            ds = p.astype(F32) * (_mm(do_h, v_ref[:, sl], _NT) - delta)
            dv_h, dk_h, dq_h = _mm(p, do_h, _TN), _mm(ds, q_ref[:, sl], _TN), _mm(ds, k_ref[:, sl])

            @pl.when(i == 0)
            def _():
                dv_ref[:, sl] = dv_h
                dk_ref[:, sl] = dk_h

            @pl.when(i != 0)
            def _():
                dv_ref[:, sl] += dv_h
                dk_ref[:, sl] += dk_h

            @pl.when(kk == 0)
            def _():
                dq_ref[rows_i, sl] = dq_h

            @pl.when(kk != 0)
            def _():
                dq_ref[rows_i, sl] += dq_h

    w = MLA_HEADS * HEAD_BLK
    return pl.pallas_call(
        body, name="od_attn_bwd", grid=(_PAIRS, nk, nq),
        in_specs=[pl.BlockSpec((tq, 2 * HEAD_BLK), lambda p, kk, i: (i, p)),
                  pl.BlockSpec((tk, 2 * HEAD_BLK), lambda p, kk, i: (kk, p)),
                  pl.BlockSpec((tk, 2 * HEAD_BLK), lambda p, kk, i: (kk, p)),
                  pl.BlockSpec((tq, LANES), lambda p, kk, i: (i, p)),
                  pl.BlockSpec((tq, LANES), lambda p, kk, i: (i, p)),
                  pl.BlockSpec((None, tq, LANES), lambda p, kk, i: (p, i, 0)),
                  pl.BlockSpec((2, tq, tk), lambda p, kk, i: (p, i, kk)),
                  pl.BlockSpec((None, None, tq, LANES), lambda p, kk, i: (p, kk // per_fwd_tile, i, 0))],
        out_specs=[pl.BlockSpec((s, 2 * HEAD_BLK), lambda p, kk, i: (0, p), pipeline_mode=pl.Buffered(1)),
                   pl.BlockSpec((tk, 2 * HEAD_BLK), lambda p, kk, i: (kk, p)),
                   pl.BlockSpec((tk, 2 * HEAD_BLK), lambda p, kk, i: (kk, p))],
        out_shape=[jax.ShapeDtypeStruct((s, w), F32)] * 3,
        compiler_params=_params(("parallel", "arbitrary", "arbitrary")),
    )(qcat, kcat, vcat, do, o, lse, pst, mt)


def _pool_counts(n_rows, first_row, s, w):
    pos = first_row + lax.broadcasted_iota(jnp.int32, (n_rows, 1), 0)
    lo = jnp.clip(pos - w // 2, 0, s)
    hi = jnp.clip(pos + w - w // 2, 0, s)
    return jnp.maximum(hi - lo, 1).astype(F32)


def _window_sum(e, levels, mirrored):
    n = e.shape[0]
    acc = e + pltpu.roll(e, (n - 1) if mirrored else 1, 0)
    step = 1
    for _ in range(levels - 1):
        acc = pltpu.roll(acc, step, 0) + pltpu.roll(acc, n - step, 0)
        step *= 2
    return acc


def _pooled(ue, s, t):
    first = pl.program_id(0) * t
    outs = []
    for gi, w in enumerate(POOL_WINDOWS):
        eg = ue[:, gi * POOL_GROUP:(gi + 1) * POOL_GROUP]
        sm = _window_sum(eg, gi + 1, False)[HALO:HALO + t]
        outs.append(sm / _pool_counts(t, first, s, w) - eg[HALO:HALO + t])
    return outs


def odd_mix_fwd(proj, o, pool_w, pool_scale, rows):
    def fn(o_, gc, up, u, un, gd, pw, ps):
        pooled = _pooled(_ext(up, u, un, rows.n), rows.s, rows.t)
        lin = jnp.concatenate([_mm(pooled[g], pw[g]) for g in range(len(POOL_WINDOWS))], axis=1)
        return jnp.concatenate([o_ * _silu(gc), lin * ps * _silu(gd)], axis=1)

    w = POOL_WIDTH
    return rows_call("od_mix_fwd", rows, fn, [o, proj, proj, proj, proj, proj, pool_w, pool_scale],
                     [rows.tile(w), rows.tile(w, O_GC)] + rows.halo(w, O_UD)
                     + [rows.tile(w, O_GD), Rows.full((4, POOL_GROUP, POOL_GROUP)), Rows.full((1, w))],
                     [(2 * w, MATMUL_DTYPE)])[0]


def odd_mix_bwd(proj, o, pool_w, pool_scale, dycd, rows):
    w = POOL_WIDTH
    ng = len(POOL_WINDOWS)

    def fn(o_, gc, up, u, un, gdp, gd, gdn, pw, ps, dyc, dydp, dyd, dydn):
        n, t, s = rows.n, rows.t, rows.s
        dyc = dyc.astype(F32)
        do = dyc * _silu(gc)
        dgc = dyc * o_ * _dsilu(gc)
        pooled = _pooled(_ext(up, u, un, n), s, t)
        lin = jnp.concatenate([_mm(pooled[g], pw[g]) for g in range(ng)], axis=1)
        dydc = dyd.astype(F32)
        dgd = dydc * lin * ps * _dsilu(gd)
        dps = _rowsum(dydc * lin * _silu(gd))
        dlin_e = _ext(dydp, dyd, dydn, n) * ps * _silu(_ext(gdp, gd, gdn, n))
        first = pl.program_id(0) * t - HALO
        dus, dpws = [], []
        for g, win in enumerate(POOL_WINDOWS):
            sl = slice(g * POOL_GROUP, (g + 1) * POOL_GROUP)
            dle = dlin_e[:, sl]
            dpws.append(_mm(pooled[g], dle[HALO:HALO + t], _TN))
            dpe = _mm(dle, pw[g], _NT)
            gce = dpe / _pool_counts(t + 2 * HALO, first, s, win)
            dus.append(_window_sum(gce, g + 1, True)[HALO:HALO + t] - dpe[HALO:HALO + t])
        return do, dgc, jnp.concatenate(dus, axis=1), dgd, jnp.stack(dpws), dps

    return rows_call("od_mix_bwd", rows, fn,
                     [o, proj, proj, proj, proj, proj, proj, proj, pool_w, pool_scale, dycd, dycd, dycd, dycd],
                     [rows.tile(w), rows.tile(w, O_GC)] + rows.halo(w, O_UD) + rows.halo(w, O_GD)
                     + [Rows.full((ng, POOL_GROUP, POOL_GROUP)), Rows.full((1, w)), rows.tile(w, 0)]
                     + rows.halo(w, w),
                     [(w, F32)] * 4, [(ng, POOL_GROUP, POOL_GROUP), (1, w)])


def pad_odd_w_in(w):
    z = lambda n: jnp.zeros((w.shape[0], n), w.dtype)
    return jnp.concatenate([w[:, :384], z(ROPE_LO), w[:, 384:416], z(LANES - ROPE_LO - MLA_ROPE), w[:, 416:]], axis=1)


def unpad_odd_w_in_t(wpt):
    return jnp.concatenate([wpt[:384], wpt[O_KR + ROPE_LO:O_KR + ROPE_LO + MLA_ROPE], wpt[O_GC:]], axis=0)


def pad_w_uq(w):
    w3 = w.reshape(MLA_Q_RANK, MLA_HEADS, MLA_NOPE + MLA_ROPE)
    w3 = jnp.pad(w3, ((0, 0), (0, 0), (0, HEAD_BLK - MLA_NOPE - MLA_ROPE)))
    return w3.reshape(MLA_Q_RANK, MLA_HEADS * HEAD_BLK)


def unpad_w_uq(wp):
    return wp.reshape(MLA_Q_RANK, MLA_HEADS, HEAD_BLK)[..., :MLA_NOPE + MLA_ROPE].reshape(MLA_Q_RANK, -1)


def pad_w_ukv(w):
    w3 = w.reshape(MLA_KV_RANK, MLA_HEADS, MLA_NOPE + MLA_V)
    kp = jnp.pad(w3[..., :MLA_NOPE], ((0, 0), (0, 0), (0, HEAD_BLK - MLA_NOPE)))
    return jnp.concatenate([kp.reshape(MLA_KV_RANK, -1), w3[..., MLA_NOPE:].reshape(MLA_KV_RANK, -1)], axis=1)


def unpad_w_ukv(wp):
    kp = wp[:, :MLA_HEADS * HEAD_BLK].reshape(MLA_KV_RANK, MLA_HEADS, HEAD_BLK)[..., :MLA_NOPE]
    vp = wp[:, MLA_HEADS * HEAD_BLK:].reshape(MLA_KV_RANK, MLA_HEADS, MLA_V)
    return jnp.concatenate([kp, vp], axis=-1).reshape(MLA_KV_RANK, -1)


def odd_weights(od_w_in, od_q_norm_g, od_w_uq, od_kv_norm_g, od_w_ukv, od_pool_w, od_pool_scale, od_w_out,
                od_ln_g, od_ln_b):
    freq, sign = rope_rows()
    return dict(w_in_p=pad_odd_w_in(od_w_in).astype(MATMUL_DTYPE), gq=od_q_norm_g, gkv=od_kv_norm_g,
                w_uq_p=pad_w_uq(od_w_uq).astype(MATMUL_DTYPE), w_ukv_p=pad_w_ukv(od_w_ukv).astype(MATMUL_DTYPE),
                pool_w=od_pool_w, pool_scale=od_pool_scale, w_out=od_w_out.astype(MATMUL_DTYPE), ln_g=od_ln_g,
                ln_b=od_ln_b, freq=freq, sign=sign)


def odd_layer_loss(x, pos, target, w, rows):
    s = rows.s
    proj = matmul(x, w["w_in_p"], "nn", F32, "od_proj")
    cqn, ckvn = latent_norm_fwd(proj, w["gq"], w["gkv"], rows)
    qp = matmul(cqn, w["w_uq_p"], "nn", F32, "od_q_up")
    kvp = matmul(ckvn, w["w_ukv_p"], "nn", F32, "od_kv_up")
    qcat, kcat, v = rope_fwd(qp, kvp, proj, pos, w["freq"], w["sign"], rows)
    o, lse, pst, mt = attention_fwd(qcat, kcat, v, s)
    ycd = odd_mix_fwd(proj, o, w["pool_w"], w["pool_scale"], rows)
    h = matmul(ycd, w["w_out"], "nn", F32, "od_out")
    dres, dh, dln_g, dln_b, loss_lanes = final_ln_loss(x, h, w["ln_g"], w["ln_b"], target, rows)
    dycd = matmul(dh, w["w_out"], "nt", F32, "od_out_dx")
    dw_out = matmul(ycd, dh, "tn", F32, "od_out_dw")
    do, dgc, dud, dgd, dpool_w, dpool_scale = odd_mix_bwd(proj, o, w["pool_w"], w["pool_scale"], dycd, rows)
    dq, dk, dv = attention_bwd(qcat, kcat, v, o, lse, pst, mt, do, s)
    dqp, dkvp, dkr = rope_bwd(dq, dk, dv, pos, w["freq"], w["sign"], rows)
    dcqn = matmul(dqp, w["w_uq_p"], "nt", F32, "od_q_up_dx")
    dw_uq = unpad_w_uq(matmul(cqn, dqp, "tn", F32, "od_q_up_dw"))
    dckvn = matmul(dkvp, w["w_ukv_p"], "nt", F32, "od_kv_up_dx")
    dw_ukv = unpad_w_ukv(matmul(ckvn, dkvp, "tn", F32, "od_kv_up_dw"))
    dcq, dckv, dgq, dgkv = latent_norm_bwd(proj, w["gq"], w["gkv"], dcqn, dckvn, rows)
    dproj = jnp.concatenate([dcq, dckv, dkr, dgc, dud, dgd], axis=1).astype(MATMUL_DTYPE)
    dx = matmul(dproj, w["w_in_p"], "nt", F32, "od_proj_dx", add=dres)
    dw_in = unpad_odd_w_in_t(matmul(dproj, x, "tn", F32, "od_proj_dw"))
    g = dict(od_w_in=dw_in, od_q_norm_g=dgq, od_w_uq=dw_uq, od_kv_norm_g=dgkv, od_w_ukv=dw_ukv,
             od_pool_w=dpool_w, od_pool_scale=dpool_scale, od_w_out=dw_out, od_ln_g=dln_g, od_ln_b=dln_b)
    return loss_lanes, dx, g


_MESH = pl.DeviceIdType.MESH
_ANY = pl.BlockSpec(memory_space=pl.ANY)
N_CHIPS = 4


def _push_call(name, ins, out_shapes, plan, n_remote, n_local):
    n_in, n_out = len(ins), len(out_shapes)

    def body(*refs):
        in_refs, out_refs = refs[:n_in], refs[n_in:n_in + n_out]
        send_sems, recv_sems, local_sems = refs[n_in + n_out:]
        x, y, c = lax.axis_index("x"), lax.axis_index("y"), lax.axis_index("c")
        remote, local = plan(in_refs, out_refs, x, y, c)
        assert len(remote) == n_remote and len(local) == n_local
        sends = [pltpu.make_async_remote_copy(src_ref=s, dst_ref=d, send_sem=send_sems.at[k], recv_sem=recv_sems.at[k],
                                              device_id=dev, device_id_type=_MESH)
                 for k, (s, d, dev, _) in enumerate(remote)]
        recvs = [pltpu.make_async_remote_copy(src_ref=s, dst_ref=land, send_sem=send_sems.at[k],
                                              recv_sem=recv_sems.at[k], device_id=dev, device_id_type=_MESH)
                 for k, (s, _, dev, land) in enumerate(remote)]
        locs = [pltpu.make_async_copy(s, d, local_sems.at[k]) for k, (s, d) in enumerate(local)]
        for cp in sends + locs:
            cp.start()
        for cp in recvs:
            cp.wait_recv()
        for cp in sends:
            cp.wait_send()
        for cp in locs:
            cp.wait()

    return pl.pallas_call(
        body, name=name, in_specs=[_ANY] * n_in, out_specs=[_ANY] * n_out, out_shape=list(out_shapes),
        scratch_shapes=[pltpu.SemaphoreType.DMA((n_remote,)), pltpu.SemaphoreType.DMA((n_remote,)),
                        pltpu.SemaphoreType.DMA((max(n_local, 1),))],
    )(*ins)


def _other_chips(x, y):
    return [(1 - x, y), (x, 1 - y), (1 - x, 1 - y)]


def chips_allgather(bufs):
    def plan(in_refs, out_refs, x, y, c):
        me = 2 * x + y
        remote, local = [], []
        for src, out in zip(in_refs, out_refs):
            for (px, py) in _other_chips(x, y):
                remote.append((src, out.at[me], (px, py, c), out.at[2 * px + py]))
            local.append((src, out.at[me]))
        return remote, local

    shapes = [jax.ShapeDtypeStruct((N_CHIPS,) + b.shape, b.dtype) for b in bufs]
    return _push_call("weights_allgather", bufs, shapes, plan, 3 * len(bufs), len(bufs))


def sibling_send(buf, name):
    def plan(in_refs, out_refs, x, y, c):
        return [(in_refs[0], out_refs[0], (x, y, 1 - c), out_refs[0])], []

    return _push_call(name, [buf], [jax.ShapeDtypeStruct(buf.shape, buf.dtype)], plan, 1, 0)[0]


def chips_scatter(buf):
    def plan(in_refs, out_refs, x, y, c):
        me = 2 * x + y
        src, out = in_refs[0], out_refs[0]
        remote = [(src.at[2 * px + py], out.at[me], (px, py, c), out.at[2 * px + py]) for (px, py) in _other_chips(x, y)]
        return remote, [(src.at[me], out.at[me])]

    return _push_call("grads_scatter", [buf], [jax.ShapeDtypeStruct(buf.shape, buf.dtype)], plan, 3, 1)[0]


PACK_W = 1024
PACK_BLK = 128


def _ew_call(name, fn, ins, in_specs, out_shape, out_spec, n_out, steps):
    def body(*refs):
        res = fn(*[r[...] for r in refs[:len(ins)]])
        if not isinstance(res, (tuple, list)):
            res = (res,)
        for r, v in zip(refs[len(ins):], res):
            r[...] = v

    return pl.pallas_call(
        body, name=name, grid=(steps,), in_specs=in_specs, out_specs=[out_spec] * n_out,
        out_shape=[jax.ShapeDtypeStruct(out_shape, F32)] * n_out,
        compiler_params=_params(("parallel",)),
    )(*ins)


def add2(a, b, name):
    n, r, w = a.shape
    blk = pl.BlockSpec((PACK_BLK, w), lambda i: (i, 0))
    out = _ew_call(name, lambda u, v: u + v, [a.reshape(n * r, w), b.reshape(n * r, w)], [blk, blk], (n * r, w), blk,
                   1, n * r // PACK_BLK)[0]
    return out.reshape(a.shape)


def sum_chips(buf):
    _, r, w = buf.shape
    specs = [pl.BlockSpec((None, PACK_BLK, w), lambda i, q=q: (q, i, 0)) for q in range(N_CHIPS)]
    return _ew_call("grads_sum", lambda a, b, c, d: ((a + b) + c) + d, [buf] * N_CHIPS, specs, (r, w),
                    pl.BlockSpec((PACK_BLK, w), lambda i: (i, 0)), 1, r // PACK_BLK)[0]


def adamw(w, g, m, v, name):
    def fn(w_, g_, m_, v_):
        m2 = ADAM_B1 * m_ + (1.0 - ADAM_B1) * g_
        v2 = ADAM_B2 * v_ + (1.0 - ADAM_B2) * jnp.square(g_)
        m_hat = m2 / (1.0 - ADAM_B1 ** ADAM_STEP)
        v_hat = v2 / (1.0 - ADAM_B2 ** ADAM_STEP)
        return -ADAM_LR * (m_hat / (jnp.sqrt(v_hat) + ADAM_EPS) + ADAM_WD * w_), m2, v2

    r, c = w.shape
    br = r if r <= 512 else _pick(r, (256, 128, 64, 32, 16, 8))
    blk = pl.BlockSpec((br, c), lambda i: (i, 0))
    return _ew_call(name, fn, [w, g, m, v], [blk] * 4, (r, c), blk, 3, r // br)


WEIGHTS = (
    ("ev_w_in", (1, 1024, 7200), 2), ("ev_conv_w", (1, 4, 2048), 2), ("ev_conv_b", (1, 2048), None),
    ("ev_a_log", (1, 2, 16), None), ("ev_dt_bias", (1, 2, 16), None), ("ev_d_skip", (1, 2, 16), None),
    ("ev_norm_g", (1, 1024), None), ("ev_sc_conv_w", (1, 3, 1024), 2), ("ev_w_out", (1, 2048, 1024), 1),
    ("ev_ln_g", (1, 1024), None), ("ev_ln_b", (1, 1024), None), ("od_w_in", (1, 1024, 1952), 2),
    ("od_q_norm_g", (1, 256), 1), ("od_w_uq", (1, 256, 768), 2), ("od_kv_norm_g", (1, 128), None),
    ("od_w_ukv", (1, 128, 1024), 2), ("od_pool_w", (1, 4, 128, 128), None), ("od_pool_scale", (1, 512), 1),
    ("od_w_out", (1, 1024, 1024), 1), ("od_ln_g", (1, 1024), 1), ("od_ln_b", (1, 1024), 1),
)
BIG = ("ev_w_in", "ev_w_out", "od_w_in", "od_w_uq", "od_w_ukv", "od_w_out")


def _block_shape(shape, axis):
    if axis is None:
        return tuple(shape)
    return tuple(d // N_CHIPS if i == axis else d for i, d in enumerate(shape))


def _pack(arrs, quantum):
    flat = jnp.concatenate([a.reshape(-1) for a in arrs])
    n = flat.shape[0]
    padded = -(-n // quantum) * quantum
    return jnp.concatenate([flat, jnp.zeros((padded - n,), flat.dtype)]).reshape(-1, LANES)


def _unpack(flat, shapes):
    out, off = [], 0
    for sh in shapes:
        n = int(np.prod(sh))
        out.append(flat[off:off + n].reshape(sh))
        off += n
    return out


def gather_weights(local):
    sharded = [(n, sh, ax) for (n, sh, ax) in WEIGHTS if ax is not None]
    big = [(n, sh, ax) for (n, sh, ax) in sharded if n in BIG]
    small = [(n, sh, ax) for (n, sh, ax) in sharded if n not in BIG]
    pb = _pack([local[n].astype(MATMUL_DTYPE) for n, _, _ in big], 2 * SUBLANES * LANES)
    ps = _pack([local[n] for n, _, _ in small], SUBLANES * LANES)
    gb, gs = chips_allgather([pb, ps])
    full = {n: local[n] for (n, sh, ax) in WEIGHTS if ax is None}
    for group, g in ((big, gb), (small, gs)):
        parts = [_unpack(g[q].reshape(-1), [_block_shape(sh, ax) for _, sh, ax in group]) for q in range(N_CHIPS)]
        for i, (n, sh, ax) in enumerate(group):
            full[n] = jnp.concatenate([parts[q][i] for q in range(N_CHIPS)], axis=ax)
    return full


WIDE = (("ev_w_in", True), ("ev_w_out", False), ("od_w_in", True), ("od_w_out", False))


def reduce_and_update(grads, local_w, local_m, local_v):
    c = lax.axis_index("c")
    wide_names = [n for n, _ in WIDE]
    tail = [(n, sh, ax) for (n, sh, ax) in WEIGHTS if n not in wide_names]
    tail_blocks = [_block_shape(sh, ax) for _, sh, ax in tail]
    n_tail = sum(int(np.prod(b)) for b in tail_blocks)
    wide_rows = [grads[n].shape[0] // N_CHIPS for n in wide_names]
    quantum = 2 * PACK_BLK
    total = -(-(sum(wide_rows) + -(-n_tail // PACK_W)) // quantum) * quantum
    tail_rows = total - sum(wide_rows)

    def tail_pack(pieces):
        flat = jnp.concatenate([p.reshape(-1) for p in pieces] + [jnp.zeros((tail_rows * PACK_W - n_tail,), F32)])
        return flat.reshape(tail_rows, PACK_W)

    parts = [grads[n].reshape(N_CHIPS, r, PACK_W) for n, r in zip(wide_names, wide_rows)]
    cols = []
    for (n, sh, ax), bs in zip(tail, tail_blocks):
        g = grads[n].reshape(sh)
        if ax is None:
            cols.append(jnp.broadcast_to(g.reshape(1, -1), (N_CHIPS, g.size)))
        else:
            g = g.reshape(sh[:ax] + (N_CHIPS, bs[ax]) + sh[ax + 1:])
            cols.append(jnp.moveaxis(g, ax, 0).reshape(N_CHIPS, -1))
    cols.append(jnp.zeros((N_CHIPS, tail_rows * PACK_W - n_tail), F32))
    packs = jnp.concatenate(parts + [jnp.concatenate(cols, axis=1).reshape(N_CHIPS, tail_rows, PACK_W)], axis=1)
    rh = total // 2
    keep = lax.dynamic_slice_in_dim(packs, c * rh, rh, axis=1)
    give = lax.dynamic_slice_in_dim(packs, (1 - c) * rh, rh, axis=1)
    chip_half = add2(keep, sibling_send(give, "grads_to_sibling"), "grads_chip_sum")
    total_half = sum_chips(chips_scatter(chip_half))
    other_half = sibling_send(total_half, "grads_from_sibling")
    g_pack = jnp.concatenate([jnp.where(c == 0, total_half, other_half),
                              jnp.where(c == 0, other_half, total_half)], axis=0)
    outs = ({}, {}, {}, {})
    off = 0
    for (n, transposed), r in zip(WIDE, wide_rows):
        g = g_pack[off:off + r]
        off += r
        g = g.T if transposed else g
        shape = local_w[n].shape
        res = adamw(local_w[n].reshape(g.shape), g, local_m[n].reshape(g.shape), local_v[n].reshape(g.shape),
                    "adamw_" + n)
        for d, a in zip(outs, (g, *res)):
            d[n] = a.reshape(shape)
    g_tail = g_pack[off:]
    res = adamw(*[tail_pack([d[n] for n, _, _ in tail]) if d is not None else g_tail
                  for d in (local_w, None, local_m, local_v)], "adamw_small")
    for d, a in zip(outs, (g_tail, *res)):
        d.update(zip([n for n, _, _ in tail], _unpack(a.reshape(-1), tail_blocks)))
    return outs


ROW_TILE = 256


def kernel(x, positions, ev_w_in, ev_conv_w, ev_conv_b, ev_a_log, ev_dt_bias, ev_d_skip, ev_norm_g, ev_sc_conv_w, ev_w_out, ev_ln_g, ev_ln_b, od_w_in, od_q_norm_g, od_w_uq, od_kv_norm_g, od_w_ukv, od_pool_w, od_pool_scale, od_w_out, od_ln_g, od_ln_b, loss_target, m_ev_w_in, m_ev_conv_w, m_ev_conv_b, m_ev_a_log, m_ev_dt_bias, m_ev_d_skip, m_ev_norm_g, m_ev_sc_conv_w, m_ev_w_out, m_ev_ln_g, m_ev_ln_b, m_od_w_in, m_od_q_norm_g, m_od_w_uq, m_od_kv_norm_g, m_od_w_ukv, m_od_pool_w, m_od_pool_scale, m_od_w_out, m_od_ln_g, m_od_ln_b, v_ev_w_in, v_ev_conv_w, v_ev_conv_b, v_ev_a_log, v_ev_dt_bias, v_ev_d_skip, v_ev_norm_g, v_ev_sc_conv_w, v_ev_w_out, v_ev_ln_g, v_ev_ln_b, v_od_w_in, v_od_q_norm_g, v_od_w_uq, v_od_kv_norm_g, v_od_w_ukv, v_od_pool_w, v_od_pool_scale, v_od_w_out, v_od_ln_g, v_od_ln_b):
    names = [n for n, _, _ in WEIGHTS]
    local_w = dict(zip(names, (ev_w_in, ev_conv_w, ev_conv_b, ev_a_log, ev_dt_bias, ev_d_skip, ev_norm_g, ev_sc_conv_w, ev_w_out, ev_ln_g, ev_ln_b, od_w_in, od_q_norm_g, od_w_uq, od_kv_norm_g, od_w_ukv, od_pool_w, od_pool_scale, od_w_out, od_ln_g, od_ln_b)))
    local_m = dict(zip(names, (m_ev_w_in, m_ev_conv_w, m_ev_conv_b, m_ev_a_log, m_ev_dt_bias, m_ev_d_skip, m_ev_norm_g, m_ev_sc_conv_w, m_ev_w_out, m_ev_ln_g, m_ev_ln_b, m_od_w_in, m_od_q_norm_g, m_od_w_uq, m_od_kv_norm_g, m_od_w_ukv, m_od_pool_w, m_od_pool_scale, m_od_w_out, m_od_ln_g, m_od_ln_b)))
    local_v = dict(zip(names, (v_ev_w_in, v_ev_conv_w, v_ev_conv_b, v_ev_a_log, v_ev_dt_bias, v_ev_d_skip, v_ev_norm_g, v_ev_sc_conv_w, v_ev_w_out, v_ev_ln_g, v_ev_ln_b, v_od_w_in, v_od_q_norm_g, v_od_w_uq, v_od_kv_norm_g, v_od_w_ukv, v_od_pool_w, v_od_pool_scale, v_od_w_out, v_od_ln_g, v_od_ln_b)))
    s = x.shape[1]
    rows = Rows(s, min(ROW_TILE, s))
    f = gather_weights(local_w)
    ew = even_weights(f["ev_w_in"][0], f["ev_conv_w"][0], f["ev_conv_b"], f["ev_a_log"][0], f["ev_dt_bias"][0],
                      f["ev_d_skip"][0], f["ev_norm_g"], f["ev_sc_conv_w"][0], f["ev_w_out"][0], f["ev_ln_g"],
                      f["ev_ln_b"])
    ow = odd_weights(f["od_w_in"][0], f["od_q_norm_g"], f["od_w_uq"][0], f["od_kv_norm_g"], f["od_w_ukv"][0],
                     f["od_pool_w"][0], f["od_pool_scale"], f["od_w_out"][0], f["od_ln_g"], f["od_ln_b"])
    x1, saved = even_layer(x[0], ew, rows)
    loss_lanes, dx1, g_odd = odd_layer_loss(x1, positions.reshape(s, 1), loss_target[0], ow, rows)
    dx0, g_even = even_layer_bwd(dx1, ew, saved, rows)
    loss = lax.psum(jnp.sum(loss_lanes), ("x", "y", "c"))
    grad, delta, new_m, new_v = reduce_and_update({**g_even, **g_odd}, local_w, local_m, local_v)
    return (loss, dx0[None], *[grad[n] for n in names], *[delta[n] for n in names],
            *[new_m[n] for n in names], *[new_v[n] for n in names])
```

```python
import functools
import math

import jax
import jax.numpy as jnp
import numpy as np
from jax import lax
from jax.experimental import pallas as pl
from jax.experimental.pallas import tpu as pltpu

F32 = jnp.float32
BF16 = jnp.bfloat16
MATMUL_DTYPE = jnp.bfloat16

D_MODEL = 1024
DEPTH = 2
SSD_HEADS, SSD_HEAD_DIM, SSD_GROUPS, SSD_STATE, SSD_CHUNK = 16, 64, 4, 128, 128
SSD_INNER = SSD_HEADS * SSD_HEAD_DIM
SSD_XBC = SSD_INNER + 2 * SSD_GROUPS * SSD_STATE
SC_WIDTH = 1024
MLA_HEADS, MLA_Q_RANK, MLA_KV_RANK, MLA_NOPE, MLA_ROPE, MLA_V = 8, 256, 128, 64, 32, 64
MLA_WIDTH = MLA_HEADS * MLA_V
ROPE_THETA = 10000.0
ATTN_SCALE = (MLA_NOPE + MLA_ROPE) ** -0.5
QSCALE = ATTN_SCALE * math.log2(math.e)
LN2 = math.log(2.0)
POOL_WINDOWS = (2, 4, 8, 16)
POOL_GROUP = 128
POOL_WIDTH = POOL_GROUP * len(POOL_WINDOWS)
EPS = 1e-5
ALPHA = (2 * DEPTH) ** 0.25
EVEN_PROJ, ODD_PROJ = 7200, 1952
ADAM_LR, ADAM_B1, ADAM_B2, ADAM_EPS, ADAM_WD, ADAM_STEP = 0.001, 0.9, 0.999, 1e-08, 0.01, 10

LANES = 128
SUBLANES = 8
HALO = SUBLANES
VMEM_LIMIT = 56 * 1024 * 1024

EVEN_P = 7296
EVEN_F = 2176
G_Z, G_BG, G_CG, G_H, G_GATE = 0, 1024, 2048, 3072, 4096
ODD_P = 2048
E_XBC, E_DT = 0, 2048
O_CQ, O_CKV, O_KR, O_GC, O_UD, O_GD = 0, 256, 384, 512, 1024, 1536


def _params(sem=None):
    return pltpu.CompilerParams(dimension_semantics=sem, vmem_limit_bytes=VMEM_LIMIT)


def _mm(a, b, dims=(((1,), (0,)), ((), ()))):
    return lax.dot_general(a.astype(MATMUL_DTYPE), b.astype(MATMUL_DTYPE), dims, preferred_element_type=F32)


_NN = (((1,), (0,)), ((), ()))
_NT = (((1,), (1,)), ((), ()))
_TN = (((0,), (0,)), ((), ()))


def _silu(v):
    return v * jax.nn.sigmoid(v)


def _dsilu(v):
    s = jax.nn.sigmoid(v)
    return s * (1.0 + v * (1.0 - s))


def _pick(n, prefs):
    for p in prefs:
        if n % p == 0:
            return p
    return n


def matmul(a, b, mode, out_dtype, name, add=None, tm=None, tn=None, tk=None):
    if mode == "nn":
        (m, k), (k2, n) = a.shape, b.shape
    elif mode == "nt":
        (m, k), (n, k2) = a.shape, b.shape
    else:
        (k, m), (k2, n) = a.shape, b.shape
    assert k == k2, (a.shape, b.shape, mode)
    wide = (2432, 2048, 1536, 1024, 512, 256, 128)
    if mode == "tn":
        tm = tm or _pick(m, (2432, 2048, 1024, 512, 256, 128))
        tn = tn or _pick(n, wide)
        tk = tk or _pick(k, (512, 256, 128))
    else:
        tm = tm or _pick(m, (512, 256, 128))
        tn = tn or _pick(n, wide)
        tk = tk or _pick(k, wide)
    nk = k // tk
    dims = {"nn": _NN, "nt": _NT, "tn": _TN}[mode]

    def body(a_ref, b_ref, *rest):
        o_ref, acc_ref = rest[-2:]
        kk = pl.program_id(2)

        @pl.when(kk == 0)
        def _():
            acc_ref[...] = jnp.zeros_like(acc_ref) if add is None else rest[0][...].astype(F32)

        acc_ref[...] += _mm(a_ref[...], b_ref[...], dims)

        @pl.when(kk == nk - 1)
        def _():
            o_ref[...] = acc_ref[...].astype(o_ref.dtype)

    a_spec = {"nn": pl.BlockSpec((tm, tk), lambda i, j, kk: (i, kk)),
              "nt": pl.BlockSpec((tm, tk), lambda i, j, kk: (i, kk)),
              "tn": pl.BlockSpec((tk, tm), lambda i, j, kk: (kk, i))}[mode]
    b_spec = {"nn": pl.BlockSpec((tk, tn), lambda i, j, kk: (kk, j)),
              "nt": pl.BlockSpec((tn, tk), lambda i, j, kk: (j, kk)),
              "tn": pl.BlockSpec((tk, tn), lambda i, j, kk: (kk, j))}[mode]
    return pl.pallas_call(
        body, name=name, grid=(m // tm, n // tn, nk),
        in_specs=[a_spec, b_spec] + ([] if add is None else [pl.BlockSpec((tm, tn), lambda i, j, kk: (i, j))]),
        out_specs=pl.BlockSpec((tm, tn), lambda i, j, kk: (i, j)),
        out_shape=jax.ShapeDtypeStruct((m, n), out_dtype),
        scratch_shapes=[pltpu.VMEM((tm, tn), F32)],
        compiler_params=_params(("parallel", "parallel", "arbitrary")),
    )(*((a, b) if add is None else (a, b, add)))


class Rows:
    def __init__(self, s, t):
        assert s % t == 0 and t % HALO == 0
        self.s, self.t, self.n = s, t, s // t

    def tile(self, width, col=0, lead=None):
        cb = col // width
        assert col % width == 0
        if lead is None:
            return pl.BlockSpec((self.t, width), lambda i: (i, cb))
        return pl.BlockSpec((None, self.t, width), lambda i: (lead, i, cb))

    def prev(self, width, col=0, lead=None):
        cb, r = col // width, self.t // HALO
        if lead is None:
            return pl.BlockSpec((HALO, width), lambda i: (jnp.maximum(i * r - 1, 0), cb))
        return pl.BlockSpec((None, HALO, width), lambda i: (lead, jnp.maximum(i * r - 1, 0), cb))

    def next(self, width, col=0, lead=None):
        cb, r, last = col // width, self.t // HALO, self.s // HALO - 1
        if lead is None:
            return pl.BlockSpec((HALO, width), lambda i: (jnp.minimum((i + 1) * r, last), cb))
        return pl.BlockSpec((None, HALO, width), lambda i: (lead, jnp.minimum((i + 1) * r, last), cb))

    def halo(self, width, col=0, lead=None):
        return [self.prev(width, col, lead), self.tile(width, col, lead), self.next(width, col, lead)]

    @staticmethod
    def full(shape):
        nd = len(shape)
        return pl.BlockSpec(tuple(shape), lambda i: (0,) * nd)


def rows_call(name, rows, fn, ins, in_specs, row_outs, acc_outs=()):
    n_row = len(row_outs)

    def body(*refs):
        in_refs = refs[:len(ins)]
        out_refs = refs[len(ins):]
        res = fn(*[r[...] for r in in_refs])
        if not isinstance(res, (tuple, list)):
            res = (res,)
        for r, v in zip(out_refs[:n_row], res[:n_row]):
            r[...] = v.astype(r.dtype)
        if acc_outs:
            first = pl.program_id(0) == 0

            @pl.when(first)
            def _():
                for r, v in zip(out_refs[n_row:], res[n_row:]):
                    r[...] = v.astype(F32)

            @pl.when(jnp.logical_not(first))
            def _():
                for r, v in zip(out_refs[n_row:], res[n_row:]):
                    r[...] += v.astype(F32)

    out_shape = [jax.ShapeDtypeStruct((rows.s, w), dt) for (w, dt) in row_outs]
    out_specs = [rows.tile(w) for (w, dt) in row_outs]
    out_shape += [jax.ShapeDtypeStruct(tuple(sh), F32) for sh in acc_outs]
    out_specs += [Rows.full(sh) for sh in acc_outs]
    return pl.pallas_call(
        body, name=name, grid=(rows.n,), in_specs=list(in_specs), out_specs=out_specs, out_shape=out_shape,
        compiler_params=_params(("arbitrary",)),
    )(*ins)


def _edge_zero(prev, nxt, n_tiles, axis=0):
    i = pl.program_id(axis)
    prev = jnp.where(i == 0, jnp.zeros_like(prev), prev)
    nxt = jnp.where(i == n_tiles - 1, jnp.zeros_like(nxt), nxt)
    return prev, nxt


def _ext(prev, cur, nxt, n_tiles, axis=0):
    prev, nxt = _edge_zero(prev, nxt, n_tiles, axis)
    return jnp.concatenate([prev.astype(F32), cur.astype(F32), nxt.astype(F32)], axis=0)


def _shift(ext, off):
    n = ext.shape[0]
    t = n - 2 * HALO
    if off == 0:
        return ext[HALO:HALO + t]
    return pltpu.roll(ext, (-off) % n, 0)[HALO:HALO + t]


def _rowsum(v):
    return jnp.sum(v, axis=0, keepdims=True)


CONV_OFFS = (-2, -1, 0, 1)
SC_OFFS = (-1, 0, 1)


def conv_fwd(proj, conv_w, conv_b, rows):
    def fn(p, c, nx, w, b):
        e = _ext(p, c, nx, rows.n)
        pre = b
        for k, off in enumerate(CONV_OFFS):
            pre = pre + w[k:k + 1, :] * _shift(e, off)
        return pre, _silu(pre)

    outs = [rows_call(f"ev_conv_fwd{h}", rows, fn,
                      [proj, proj, proj, conv_w[:, h * 1024:(h + 1) * 1024], conv_b[:, h * 1024:(h + 1) * 1024]],
                      rows.halo(1024, E_XBC + h * 1024) + [Rows.full((4, 1024)), Rows.full((1, 1024))],
                      [(1024, MATMUL_DTYPE), (1024, F32)]) for h in range(2)]
    return outs


def conv_bwd(proj, pre_h, du_h, conv_w, rows):
    res = []
    for h in range(2):
        def fn(*a):
            w = a[-1]
            xe = _ext(a[0], a[1], a[2], rows.n)
            pe = jnp.concatenate([a[3], a[4], a[5]], axis=0).astype(F32)
            g = a[6:-1]
            du = _ext(g[0], g[1], g[2], rows.n) + _ext(g[3], g[4], g[5], rows.n)
            dpre = du * _dsilu(pe)
            dx = jnp.zeros_like(a[1], dtype=F32)
            dws = []
            for k, off in enumerate(CONV_OFFS):
                dx = dx + w[k:k + 1, :] * _shift(dpre, -off)
                dws.append(_rowsum(_shift(dpre, 0) * _shift(xe, off)))
            dw = jnp.concatenate(dws + [jnp.zeros((4, dx.shape[1]), F32)], axis=0)
            return dx, dw, _rowsum(_shift(dpre, 0))

        gi = [du_h[h][0]] * 3 + [du_h[h][1]] * 3
        gs = rows.halo(1024) * 2
        res.append(rows_call(
            f"ev_conv_bwd{h}", rows, fn,
            [proj] * 3 + [pre_h[h]] * 3 + gi + [conv_w[:, h * 1024:(h + 1) * 1024]],
            rows.halo(1024, E_XBC + h * 1024) + rows.halo(1024) + gs + [Rows.full((4, 1024))],
            [(1024, MATMUL_DTYPE)], [(8, 1024), (1, 1024)]))
    dconv_w = jnp.concatenate([res[0][1][:4], res[1][1][:4]], axis=1)
    dconv_b = jnp.concatenate([res[0][2], res[1][2]], axis=1)
    return [res[0][0], res[1][0]], dconv_w, dconv_b


def _head_row(p):
    return jnp.concatenate([p.reshape(1, 2 * SSD_HEADS), jnp.zeros((1, LANES - 2 * SSD_HEADS), F32)], axis=1)


def _head_unrow(r):
    return r[:, :2 * SSD_HEADS].reshape(2, SSD_HEADS)


def _ssd_pre_fn(dtraw, bias_row, alog_row):
    q = dtraw.shape[0]
    dt = jax.nn.softplus(dtraw + bias_row)
    da = dt * (-jnp.exp(alog_row))
    li = lax.broadcasted_iota(jnp.int32, (q, q), 0)
    si = lax.broadcasted_iota(jnp.int32, (q, q), 1)
    tril = (li >= si).astype(F32)
    csf = lax.dot_general(tril, da, _NN, precision=lax.Precision.HIGHEST, preferred_element_type=F32)
    tot = jnp.sum(da, axis=0, keepdims=True)
    lane = lax.broadcasted_iota(jnp.int32, (1, LANES), 1)
    cs = jnp.where(lane < SSD_HEADS, csf, tot - csf + da)
    return dt, cs


def ssd_pre(proj, bias_row, alog_row, s):
    rows = Rows(s, SSD_CHUNK)
    return rows_call("ev_ssd_pre", rows, _ssd_pre_fn, [proj, bias_row, alog_row],
                     [rows.tile(LANES, E_DT), Rows.full((1, LANES)), Rows.full((1, LANES))],
                     [(LANES, F32), (LANES, F32)])


def ssd_pre_bwd(proj, bias_row, alog_row, ddt, dcs, s):
    rows = Rows(s, SSD_CHUNK)

    def fn(dtraw, b, al, g0, g1, c0, c1):
        _, vjp = jax.vjp(_ssd_pre_fn, dtraw, b, al)
        return vjp((g0 + g1, c0 + c1))

    return rows_call("ev_ssd_pre_bwd", rows, fn, [proj, bias_row, alog_row, ddt[0], ddt[1], dcs[0], dcs[1]],
                     [rows.tile(LANES, E_DT), Rows.full((1, LANES)), Rows.full((1, LANES))] + [rows.tile(LANES)] * 4,
                     [(LANES, MATMUL_DTYPE)], [(1, LANES), (1, LANES)])


_N_PAIR = SSD_HEADS // 2
_BC = SSD_GROUPS * SSD_STATE


def _ssd_chunk_fn(x, bc, dt, cs, h_in, dsk_row, d):
    q = x.shape[0]
    lane = lax.broadcasted_iota(jnp.int32, (1, LANES), 1)
    half = lane < SSD_HEAD_DIM
    lo, hi = half.astype(F32), 1.0 - half.astype(F32)
    cst = cs.T
    li = lax.broadcasted_iota(jnp.int32, (q, 1), 0)
    si = lax.broadcasted_iota(jnp.int32, (1, q), 1)
    mask = li >= si if d == 0 else li <= si
    end = q - 1 if d == 0 else 0
    tot = cs[end:end + 1, :]
    mine = (lane >= SSD_HEADS * d) & (lane < SSD_HEADS * (d + 1))
    e_cs, e_dec, e_tot = jnp.exp(cs), jnp.exp(jnp.where(mine, tot - cs, 0.0)), jnp.exp(tot)
    e_dd = e_dec * dt
    dtt = dt.T

    def col(v, l):
        return v[:, l:l + 1]

    def by_head(v, l):
        return jnp.where(half, col(v, l), col(v, l + 1))

    per_group = _N_PAIR // SSD_GROUPS
    ys, hs = [], []
    for g in range(SSD_GROUPS):
        bm = bc[:, g * SSD_STATE:(g + 1) * SSD_STATE]
        cm = bc[:, _BC + g * SSD_STATE:_BC + (g + 1) * SSD_STATE]
        cb = _mm(cm, bm, _NT)
        pairs = range(g * per_group, (g + 1) * per_group)
        h_g = [h_in[j * SSD_STATE:(j + 1) * SSD_STATE, :] for j in pairs]
        y_off = _mm(cm, jnp.concatenate(h_g, axis=1))
        for k, j in enumerate(pairs):
            l0 = SSD_HEADS * d + 2 * j
            xj = x[:, j * LANES:(j + 1) * LANES]
            xcat = jnp.concatenate([xj * lo, xj * hi], axis=0)
            w0 = cb * jnp.exp(jnp.where(mask, col(cs, l0) - cst[l0:l0 + 1, :], -jnp.inf)) * dtt[l0:l0 + 1, :]
            w1 = cb * jnp.exp(jnp.where(mask, col(cs, l0 + 1) - cst[l0 + 1:l0 + 2, :], -jnp.inf)) * dtt[l0 + 1:l0 + 2, :]
            y = _mm(jnp.concatenate([w0, w1], axis=1), xcat)
            st = _mm(jnp.concatenate([bm * col(e_dd, l0), bm * col(e_dd, l0 + 1)], axis=0), xcat, _TN)
            y = y + y_off[:, k * LANES:(k + 1) * LANES] * by_head(e_cs, l0) + by_head(dsk_row, l0) * xj
            ys.append(y)
            hs.append(h_g[k] * by_head(e_tot, l0) + st)
    return jnp.concatenate(ys, axis=1), jnp.concatenate(hs, axis=0)


def _chunk_of(d, ci, nc, backward):
    up = ci if not backward else nc - 1 - ci
    return up + d * (nc - 1 - 2 * up)


_ST_ROWS = _N_PAIR * SSD_STATE


def _ssd_fwd_dir(u_h, dt, cs, dsk_row, s, d):
    nc = s // SSD_CHUNK
    q = SSD_CHUNK

    def body(x_ref, bc_ref, dt_ref, cs_ref, dsk_ref, y_ref, hs_ref, st_ref):
        @pl.when(pl.program_id(0) == 0)
        def _():
            st_ref[...] = jnp.zeros(st_ref.shape, F32)

        h_in = st_ref[...]
        y, h_out = _ssd_chunk_fn(x_ref[...], bc_ref[...], dt_ref[...], cs_ref[...], h_in, dsk_ref[...], d)
        y_ref[...] = y.astype(y_ref.dtype)
        hs_ref[...] = h_in
        st_ref[...] = h_out

    ch = lambda ci: _chunk_of(d, ci, nc, False)
    return pl.pallas_call(
        body, name=f"ev_ssd_fwd{d}", grid=(nc,),
        in_specs=[pl.BlockSpec((q, SSD_INNER), lambda ci: (ch(ci), 0)),
                  pl.BlockSpec((q, 2 * _BC), lambda ci: (ch(ci), 0)),
                  pl.BlockSpec((q, LANES), lambda ci: (ch(ci), 0)),
                  pl.BlockSpec((q, LANES), lambda ci: (ch(ci), 0)),
                  pl.BlockSpec((1, LANES), lambda ci: (0, 0))],
        out_specs=[pl.BlockSpec((q, SSD_INNER), lambda ci: (ch(ci), 0)),
                   pl.BlockSpec((None, _ST_ROWS, LANES), lambda ci: (ch(ci), 0, 0))],
        out_shape=[jax.ShapeDtypeStruct((s, SSD_INNER), MATMUL_DTYPE),
                   jax.ShapeDtypeStruct((nc, _ST_ROWS, LANES), F32)],
        scratch_shapes=[pltpu.VMEM((_ST_ROWS, LANES), F32)],
        compiler_params=_params(("arbitrary",)),
    )(u_h[0], u_h[1], dt, cs, dsk_row)


def ssd_fwd(u_h, dt, cs, dsk_row, s):
    ys, hss = zip(*[_ssd_fwd_dir(u_h, dt, cs, dsk_row, s, d) for d in range(2)])
    return ys, hss


def _ssd_bwd_dir(u_h, dt, cs, dsk_row, hs, dy, s, d):
    nc = s // SSD_CHUNK
    q = SSD_CHUNK

    def body(x_ref, bc_ref, dt_ref, cs_ref, dsk_ref, hs_ref, dy_ref,
             dx_ref, dbc_ref, ddt_ref, dcs_ref, ddsk_ref, dst_ref):
        ci = pl.program_id(0)

        @pl.when(ci == 0)
        def _():
            dst_ref[...] = jnp.zeros(dst_ref.shape, F32)

        f = functools.partial(_ssd_chunk_fn, d=d)
        _, vjp = jax.vjp(f, x_ref[...], bc_ref[...], dt_ref[...], cs_ref[...], hs_ref[...], dsk_ref[...])
        dx, dbc, ddt, dcs, dh, ddsk = vjp((dy_ref[...], dst_ref[...]))
        dx_ref[...] = dx
        dbc_ref[...] = dbc
        ddt_ref[...] = ddt
        dcs_ref[...] = dcs
        dst_ref[...] = dh

        @pl.when(ci == 0)
        def _():
            ddsk_ref[...] = ddsk

        @pl.when(ci != 0)
        def _():
            ddsk_ref[...] += ddsk

    ch = lambda ci: _chunk_of(d, ci, nc, True)
    row_blk = lambda w: pl.BlockSpec((q, w), lambda ci: (ch(ci), 0))
    return pl.pallas_call(
        body, name=f"ev_ssd_bwd{d}", grid=(nc,),
        in_specs=[row_blk(SSD_INNER), row_blk(2 * _BC), row_blk(LANES), row_blk(LANES),
                  pl.BlockSpec((1, LANES), lambda ci: (0, 0)),
                  pl.BlockSpec((None, _ST_ROWS, LANES), lambda ci: (ch(ci), 0, 0)), row_blk(SSD_INNER)],
        out_specs=[row_blk(SSD_INNER), row_blk(2 * _BC), row_blk(LANES), row_blk(LANES),
                   pl.BlockSpec((1, LANES), lambda ci: (0, 0))],
        out_shape=[jax.ShapeDtypeStruct((s, SSD_INNER), F32), jax.ShapeDtypeStruct((s, 2 * _BC), F32),
                   jax.ShapeDtypeStruct((s, LANES), F32), jax.ShapeDtypeStruct((s, LANES), F32),
                   jax.ShapeDtypeStruct((1, LANES), F32)],
        scratch_shapes=[pltpu.VMEM((_ST_ROWS, LANES), F32)],
        compiler_params=_params(("arbitrary",)),
    )(u_h[0], u_h[1], dt, cs, dsk_row, hs, dy)


def ssd_bwd(u_h, dt, cs, dsk_row, hs, dy, s):
    return zip(*[_ssd_bwd_dir(u_h, dt, cs, dsk_row, hs[d], dy, s, d) for d in range(2)])


def _gated_rms(ys, z, g):
    t1 = ys * _silu(z)
    return t1 * lax.rsqrt(jnp.mean(t1 * t1, axis=-1, keepdims=True) + EPS) * g


def even_mix_fwd(proj, y2, norm_g, sc_w, rows):
    def fn(yf, yb, z, bg, cgp, cg, cgn, hp, hh, hn, gate, g, w):
        z, bg, gate = z.astype(F32), bg.astype(F32), gate.astype(F32)
        ya = _gated_rms(yf.astype(F32) + yb.astype(F32), z, g)
        me = _ext(cgp, cg, cgn, rows.n) * _ext(hp, hh, hn, rows.n)
        cm = sum(w[k:k + 1, :] * _shift(me, off) for k, off in enumerate(SC_OFFS))
        return jnp.concatenate([ya, bg * cm * _silu(gate)], axis=1)

    w = 1024
    return rows_call("ev_mix_fwd", rows, fn,
                     [y2[0], y2[1], proj, proj] + [proj] * 6 + [proj, norm_g, sc_w],
                     [rows.tile(w), rows.tile(w), rows.tile(w, G_Z), rows.tile(w, G_BG)]
                     + rows.halo(w, G_CG) + rows.halo(w, G_H)
                     + [rows.tile(w, G_GATE), Rows.full((1, w)), Rows.full((3, w))],
                     [(2 * w, MATMUL_DTYPE)])[0]


MIX_COLS = 256


def even_mix_bwd(proj, y2, norm_g, sc_w, dyab, rows):
    w = 1024

    def norm_fn(yf, yb, z, g, dya):
        _, vjp = jax.vjp(_gated_rms, yf.astype(F32) + yb.astype(F32), z.astype(F32), g)
        dys, dz, dg = vjp(dya.astype(F32))
        return dys, dz, _rowsum(dg)

    dys, dz, dnorm_g = rows_call(
        "ev_mix_bwd_norm", rows, norm_fn, [y2[0], y2[1], proj, norm_g, dyab],
        [rows.tile(w), rows.tile(w), rows.tile(w, G_Z), Rows.full((1, w)), rows.tile(w, 0)],
        [(w, F32), (w, MATMUL_DTYPE)], [(1, w)])

    cw, n, t = MIX_COLS, rows.n, rows.t
    r, last = t // HALO, rows.s // HALO - 1

    def body(*refs):
        dsw_ref = refs[-1]
        dbg_ref, dcg_ref, dh_ref, dgate_ref = refs[-5:-1]
        sw = refs[15][...]
        dye, bge, gte, cge, he = [_ext(refs[3 * k][...], refs[3 * k + 1][...], refs[3 * k + 2][...], n, axis=1)
                                  for k in range(5)]
        me = cge * he
        cm = sum(sw[k:k + 1, :] * _shift(me, off) for k, off in enumerate(SC_OFFS))
        dyc, bgc, gtc = _shift(dye, 0), _shift(bge, 0), _shift(gte, 0)
        dcme = dye * bge * _silu(gte)
        dm = sum(sw[k:k + 1, :] * _shift(dcme, -off) for k, off in enumerate(SC_OFFS))
        dcm = _shift(dcme, 0)
        dsw = jnp.concatenate([_rowsum(dcm * _shift(me, off)) for off in SC_OFFS] + [jnp.zeros((5, cw), F32)], axis=0)
        dbg_ref[...] = (dyc * cm * _silu(gtc)).astype(dbg_ref.dtype)
        dcg_ref[...] = (dm * _shift(he, 0)).astype(dcg_ref.dtype)
        dh_ref[...] = (dm * _shift(cge, 0)).astype(dh_ref.dtype)
        dgate_ref[...] = (dyc * bgc * cm * _dsilu(gtc)).astype(dgate_ref.dtype)
        first = pl.program_id(1) == 0

        @pl.when(first)
        def _():
            dsw_ref[...] = dsw

        @pl.when(jnp.logical_not(first))
        def _():
            dsw_ref[...] += dsw

    def halo(col):
        cb = col // cw
        return [pl.BlockSpec((HALO, cw), lambda j, i: (jnp.maximum(i * r - 1, 0), cb + j)),
                pl.BlockSpec((t, cw), lambda j, i: (i, cb + j)),
                pl.BlockSpec((HALO, cw), lambda j, i: (jnp.minimum((i + 1) * r, last), cb + j))]

    tile_out = pl.BlockSpec((t, cw), lambda j, i: (i, j))
    dbg, dcg, dhh, dgate, dsw = pl.pallas_call(
        body, name="ev_mix_bwd_conv", grid=(w // cw, n),
        in_specs=halo(w) + halo(G_BG) + halo(G_GATE) + halo(G_CG) + halo(G_H)
        + [pl.BlockSpec((3, cw), lambda j, i: (0, j))],
        out_specs=[tile_out] * 4 + [pl.BlockSpec((8, cw), lambda j, i: (0, j))],
        out_shape=[jax.ShapeDtypeStruct((rows.s, w), MATMUL_DTYPE)] * 4 + [jax.ShapeDtypeStruct((8, w), F32)],
        compiler_params=_params(("parallel", "arbitrary")),
    )(*([dyab] * 3 + [proj] * 12 + [sc_w]))
    return dys, dz, dbg, dcg, dhh, dgate, dnorm_g, dsw


def _res_ln(x, h, g, b):
    v = ALPHA * x + h
    mu = jnp.mean(v, axis=-1, keepdims=True)
    var = jnp.mean(jnp.square(v - mu), axis=-1, keepdims=True)
    return (v - mu) * lax.rsqrt(var + EPS) * g + b


def res_ln_fwd(x, h, g, b, rows, name):
    return rows_call(name, rows, _res_ln, [x, h, g, b],
                     [rows.tile(D_MODEL), rows.tile(D_MODEL), Rows.full((1, D_MODEL)), Rows.full((1, D_MODEL))],
                     [(D_MODEL, F32)])[0]


def res_ln_bwd(x, h, g, b, dy, rows, name):
    def fn(x_, h_, g_, b_, dy_):
        _, vjp = jax.vjp(_res_ln, x_, h_, g_, b_)
        dx, dh, dg, db = vjp(dy_)
        return dx, dh, dg, db

    return rows_call(name, rows, fn, [x, h, g, b, dy],
                     [rows.tile(D_MODEL), rows.tile(D_MODEL), Rows.full((1, D_MODEL)), Rows.full((1, D_MODEL)),
                      rows.tile(D_MODEL)],
                     [(D_MODEL, F32), (D_MODEL, F32)], [(1, D_MODEL), (1, D_MODEL)])


def final_ln_loss(x, h, g, b, target, rows):
    def fn(x_, h_, g_, b_, t_):
        y, vjp = jax.vjp(_res_ln, x_, h_, g_, b_)
        err = y - t_
        dx, dh, dg, db = vjp(err * (1.0 / D_MODEL))
        return dx, dh, dg, db, _rowsum(jnp.square(err)) * (0.5 / D_MODEL)

    return rows_call("od_ln_loss", rows, fn, [x, h, g, b, target],
                     [rows.tile(D_MODEL), rows.tile(D_MODEL), Rows.full((1, D_MODEL)), Rows.full((1, D_MODEL)),
                      rows.tile(D_MODEL)],
                     [(D_MODEL, F32), (D_MODEL, F32)], [(1, D_MODEL), (1, D_MODEL), (1, D_MODEL)])


def pad_even_w_in(w):
    return jnp.concatenate([w[:, 1024:3104], jnp.zeros((w.shape[0], EVEN_P - EVEN_PROJ), w.dtype), w[:, :1024],
                            w[:, 3104:]], axis=1)


def matmul_pieces_nt(pieces, w, name, add):
    m, n, npc = pieces[0].shape[0], w.shape[0], len(pieces)
    widths = [p.shape[1] for p in pieces]
    assert sum(widths) == w.shape[1]
    tm = _pick(m, (256, 128))

    def body(*refs):
        p_refs = refs[:npc]
        w_ref, add_ref, o_ref = refs[npc:]
        acc, off = add_ref[...].astype(F32), 0
        for p_ref, wd in zip(p_refs, widths):
            acc = acc + _mm(p_ref[...], w_ref[:, off:off + wd], _NT)
            off += wd
        o_ref[...] = acc

    return pl.pallas_call(
        body, name=name, grid=(m // tm,),
        in_specs=[pl.BlockSpec((tm, wd), lambda i: (i, 0)) for wd in widths]
        + [pl.BlockSpec(w.shape, lambda i: (0, 0), pipeline_mode=pl.Buffered(1)),
           pl.BlockSpec((tm, n), lambda i: (i, 0))],
        out_specs=pl.BlockSpec((tm, n), lambda i: (i, 0)),
        out_shape=jax.ShapeDtypeStruct((m, n), F32),
        compiler_params=_params(("parallel",)),
    )(*pieces, w, add)


def even_layer(x, w, rows):
    s = rows.s
    xb = x.astype(MATMUL_DTYPE)
    proj = matmul(xb, w["w_in_p"][:, :EVEN_F], "nn", F32, "ev_proj_f", tn=EVEN_F)
    proj_g = matmul(xb, w["w_in_p"][:, EVEN_F:], "nn", MATMUL_DTYPE, "ev_proj_g", tn=(EVEN_P - EVEN_F) // 2)
    (pre0, u0), (pre1, u1) = conv_fwd(proj, w["conv_w"], w["conv_b"], rows)
    dt, cs = ssd_pre(proj, w["bias_row"], w["alog_row"], s)
    y2, hs = ssd_fwd((u0, u1), dt, cs, w["dsk_row"], s)
    yab = even_mix_fwd(proj_g, y2, w["norm_g"], w["sc_w"], rows)
    h = matmul(yab, w["w_out"], "nn", F32, "ev_out")
    x1 = res_ln_fwd(x, h, w["ln_g"], w["ln_b"], rows, "ev_ln")
    return x1, dict(x=x, xb=xb, proj=proj, proj_g=proj_g, pre=(pre0, pre1), u=(u0, u1), dt=dt, cs=cs, y2=y2,
                    hs=hs, yab=yab, h=h)


def even_layer_bwd(dx1, w, sv, rows):
    s = rows.s
    dres, dh, dln_g, dln_b = res_ln_bwd(sv["x"], sv["h"], w["ln_g"], w["ln_b"], dx1, rows, "ev_ln_bwd")
    dyab = matmul(dh, w["w_out"], "nt", MATMUL_DTYPE, "ev_out_dx")
    dw_out = matmul(sv["yab"], dh, "tn", F32, "ev_out_dw")
    dys, dz, dbg, dcg, dhh, dgate, dnorm_g, dsw = even_mix_bwd(sv["proj_g"], sv["y2"], w["norm_g"], w["sc_w"],
                                                               dyab, rows)
    dxs, dbc, ddt, dcs, ddsk = ssd_bwd(sv["u"], sv["dt"], sv["cs"], w["dsk_row"], sv["hs"], dys, s)
    ddtraw, dbias_row, dalog_row = ssd_pre_bwd(sv["proj"], w["bias_row"], w["alog_row"], ddt, dcs, s)
    (dxbc0, dxbc1), dconv_w, dconv_b = conv_bwd(sv["proj"], sv["pre"], (dxs, dbc), w["conv_w"], rows)
    pieces = [dxbc0, dxbc1, ddtraw, dz, dbg, dcg, dhh, dgate]
    dx0 = matmul_pieces_nt(pieces, w["w_in_p"], "ev_proj_dx", add=dres)
    dw0, dw1, dw_dt, dw_z, *dw_gates = [matmul(p, sv["xb"], "tn", F32, f"ev_proj_dw{i}")
                                        for i, p in enumerate(pieces)]
    dw_in = jnp.concatenate([dw_z, dw0, dw1, dw_dt[:2 * SSD_HEADS]] + dw_gates, axis=0)
    g = dict(ev_w_in=dw_in, ev_conv_w=dconv_w, ev_conv_b=dconv_b,
             ev_a_log=_head_unrow(dalog_row), ev_dt_bias=_head_unrow(dbias_row),
             ev_d_skip=_head_unrow(ddsk[0] + ddsk[1]), ev_norm_g=dnorm_g, ev_sc_conv_w=dsw[:3],
             ev_w_out=dw_out, ev_ln_g=dln_g, ev_ln_b=dln_b)
    return dx0, g


def even_weights(ev_w_in, ev_conv_w, ev_conv_b, ev_a_log, ev_dt_bias, ev_d_skip, ev_norm_g, ev_sc_conv_w,
                 ev_w_out, ev_ln_g, ev_ln_b):
    return dict(w_in_p=pad_even_w_in(ev_w_in).astype(MATMUL_DTYPE), conv_w=ev_conv_w, conv_b=ev_conv_b,
                alog_row=_head_row(ev_a_log), bias_row=_head_row(ev_dt_bias), dsk_row=_head_row(ev_d_skip),
                norm_g=ev_norm_g, sc_w=ev_sc_conv_w, w_out=ev_w_out.astype(MATMUL_DTYPE), ln_g=ev_ln_g,
                ln_b=ev_ln_b)


HEAD_BLK = LANES
ROPE_LO = MLA_NOPE
ROPE_HALF = MLA_ROPE // 2


def _rms(v, g):
    return v * lax.rsqrt(jnp.mean(v * v, axis=-1, keepdims=True) + EPS) * g


def latent_norm_fwd(proj, gq, gkv, rows):
    def fn(cq, ckv, gq_, gkv_):
        return _rms(cq, gq_), _rms(ckv, gkv_)

    return rows_call("od_norm_fwd", rows, fn, [proj, proj, gq, gkv],
                     [rows.tile(MLA_Q_RANK, O_CQ), rows.tile(MLA_KV_RANK, O_CKV), Rows.full((1, MLA_Q_RANK)),
                      Rows.full((1, MLA_KV_RANK))],
                     [(MLA_Q_RANK, MATMUL_DTYPE), (MLA_KV_RANK, MATMUL_DTYPE)])


def latent_norm_bwd(proj, gq, gkv, dcqn, dckvn, rows):
    def fn(cq, ckv, gq_, gkv_, d1, d2):
        _, vjp = jax.vjp(_rms, cq, gq_)
        dcq, dgq = vjp(d1)
        _, vjp2 = jax.vjp(_rms, ckv, gkv_)
        dckv, dgkv = vjp2(d2)
        return dcq, dckv, dgq, dgkv

    return rows_call("od_norm_bwd", rows, fn, [proj, proj, gq, gkv, dcqn, dckvn],
                     [rows.tile(MLA_Q_RANK, O_CQ), rows.tile(MLA_KV_RANK, O_CKV), Rows.full((1, MLA_Q_RANK)),
                      Rows.full((1, MLA_KV_RANK)), rows.tile(MLA_Q_RANK), rows.tile(MLA_KV_RANK)],
                     [(MLA_Q_RANK, F32), (MLA_KV_RANK, F32)], [(1, MLA_Q_RANK), (1, MLA_KV_RANK)])


def rope_rows():
    lane = np.arange(LANES)
    inv = ROPE_THETA ** (-jnp.arange(ROPE_HALF, dtype=F32) / ROPE_HALF)
    on = (lane >= ROPE_LO) & (lane < ROPE_LO + MLA_ROPE)
    freq = jnp.where(on, inv[(lane - ROPE_LO) % ROPE_HALF], 0.0).reshape(1, LANES).astype(F32)
    sign = np.where(on, np.where(lane < ROPE_LO + ROPE_HALF, -1.0, 1.0), 0.0).reshape(1, LANES).astype(np.float32)
    return freq, jnp.asarray(sign)


def _rot_tables(pos, freq, sign):
    ang = pos.astype(F32) * freq
    return jnp.cos(ang), jnp.sin(ang) * sign


def _swap_halves(v):
    lane = lax.broadcasted_iota(jnp.int32, (1, LANES), 1)
    return jnp.where(lane < ROPE_LO + ROPE_HALF, pltpu.roll(v, LANES - ROPE_HALF, 1), pltpu.roll(v, ROPE_HALF, 1))


def rope_fwd(qp, kvp, proj, pos, freq, sign, rows):
    def fn(q, k, kr, v, p, f, sg):
        c, sn = _rot_tables(p, f, sg)
        rk = kr * c + _swap_halves(kr) * sn
        one = (lax.broadcasted_iota(jnp.int32, (v.shape[0], HEAD_BLK - MLA_V), 1) == 0).astype(F32)
        qs, ks, vs = [], [], []
        for h in range(MLA_HEADS):
            qh = q[:, h * HEAD_BLK:(h + 1) * HEAD_BLK]
            qs.append((qh * c + _swap_halves(qh) * sn) * QSCALE)
            ks.append(k[:, h * HEAD_BLK:(h + 1) * HEAD_BLK] + rk)
            vs += [v[:, h * MLA_V:(h + 1) * MLA_V], one]
        return jnp.concatenate(qs, axis=1), jnp.concatenate(ks, axis=1), jnp.concatenate(vs, axis=1)

    w = MLA_HEADS * HEAD_BLK
    return rows_call("od_rope_fwd", rows, fn, [qp, kvp, proj, kvp, pos, freq, sign],
                     [rows.tile(w), rows.tile(w, 0), rows.tile(LANES, O_KR), rows.tile(MLA_WIDTH, w),
                      rows.tile(1), Rows.full((1, LANES)), Rows.full((1, LANES))],
                     [(w, MATMUL_DTYPE), (w, MATMUL_DTYPE), (w, MATMUL_DTYPE)])


def rope_bwd(dq, dk, dv, pos, freq, sign, rows):
    def fn(dq_, dk_, dv_, p, f, sg):
        c, sn = _rot_tables(p, f, sg)
        on = jnp.abs(sg)
        outs, dks, dvs, dkr = [], [], [], jnp.zeros((dq_.shape[0], LANES), F32)
        for h in range(MLA_HEADS):
            g = dq_[:, h * HEAD_BLK:(h + 1) * HEAD_BLK] * ATTN_SCALE
            outs.append(g * c + _swap_halves(g * sn) * on)
            gk = dk_[:, h * HEAD_BLK:(h + 1) * HEAD_BLK] * LN2
            dks.append(gk)
            dkr = dkr + gk * c + _swap_halves(gk * sn) * on
            dvs.append(dv_[:, h * HEAD_BLK:h * HEAD_BLK + MLA_V])
        return jnp.concatenate(outs, axis=1), jnp.concatenate(dks + dvs, axis=1), dkr

    w = MLA_HEADS * HEAD_BLK
    return rows_call("od_rope_bwd", rows, fn, [dq, dk, dv, pos, freq, sign],
                     [rows.tile(w), rows.tile(w), rows.tile(w), rows.tile(1), Rows.full((1, LANES)),
                      Rows.full((1, LANES))],
                     [(w, MATMUL_DTYPE), (w + MLA_WIDTH, MATMUL_DTYPE), (LANES, F32)])


_PAIRS = MLA_HEADS // 2
ATT_TQ = 512
ATT_TK = 4096
ATT_BWD_TQ = 1024
ATT_BWD_TK = 1024


def _att_tiles(s, backward=False):
    if backward:
        return min(ATT_BWD_TQ, s), min(ATT_BWD_TK, s)
    return min(ATT_TQ, s), min(ATT_TK, s)


def attention_fwd(qcat, kcat, vcat, s):
    tq, tk = _att_tiles(s)
    nq, nk = s // tq, s // tk

    def body(q_ref, k_ref, v_ref, o_ref, lse_ref, p_ref, mt_ref, m_ref, acc_ref):
        kk = pl.program_id(2)
        half = lax.broadcasted_iota(jnp.int32, (1, LANES), 1) < MLA_V

        @pl.when(kk == 0)
        def _():
            m_ref[...] = jnp.full(m_ref.shape, -jnp.inf, F32)
            acc_ref[...] = jnp.zeros(acc_ref.shape, F32)

        sl = [slice(hh * HEAD_BLK, (hh + 1) * HEAD_BLK) for hh in range(2)]
        sc = [_mm(q_ref[:, sl[hh]], k_ref[:, sl[hh]], _NT) for hh in range(2)]
        for hh in range(2):
            m_prev = m_ref[hh]
            m_new = jnp.maximum(m_prev, jnp.max(sc[hh], axis=1, keepdims=True))
            p = jnp.exp2(sc[hh] - m_new[:, :1]).astype(p_ref.dtype)
            p_ref[hh] = p
            acc_ref[hh] = acc_ref[hh] * jnp.exp2(m_prev - m_new) + _mm(p, v_ref[:, sl[hh]])
            m_ref[hh] = m_new
        mt_ref[...] = jnp.where(half, m_ref[0], m_ref[1])

        @pl.when(kk == nk - 1)
        def _():
            l0, l1 = acc_ref[0][:, MLA_V:MLA_V + 1], acc_ref[1][:, MLA_V:MLA_V + 1]
            o_ref[...] = jnp.where(half, acc_ref[0] / l0, pltpu.roll(acc_ref[1] / l1, MLA_V, 1))
            lse_ref[...] = jnp.where(half, m_ref[0] + jnp.log2(l0), m_ref[1] + jnp.log2(l1))

    return pl.pallas_call(
        body, name="od_attn_fwd", grid=(_PAIRS, nq, nk),
        in_specs=[pl.BlockSpec((tq, 2 * HEAD_BLK), lambda p, i, kk: (i, p)),
                  pl.BlockSpec((tk, 2 * HEAD_BLK), lambda p, i, kk: (kk, p)),
                  pl.BlockSpec((tk, 2 * HEAD_BLK), lambda p, i, kk: (kk, p))],
        out_specs=[pl.BlockSpec((tq, LANES), lambda p, i, kk: (i, p)),
                   pl.BlockSpec((None, tq, LANES), lambda p, i, kk: (p, i, 0)),
                   pl.BlockSpec((2, tq, tk), lambda p, i, kk: (p, i, kk)),
                   pl.BlockSpec((None, None, tq, LANES), lambda p, i, kk: (p, kk, i, 0))],
        out_shape=[jax.ShapeDtypeStruct((s, MLA_WIDTH), F32), jax.ShapeDtypeStruct((_PAIRS, s, LANES), F32),
                   jax.ShapeDtypeStruct((MLA_HEADS, s, s), MATMUL_DTYPE),
                   jax.ShapeDtypeStruct((_PAIRS, nk, s, LANES), F32)],
        scratch_shapes=[pltpu.VMEM((2, tq, LANES), F32)] * 2,
        compiler_params=_params(("parallel", "parallel", "arbitrary")),
    )(qcat, kcat, vcat)


def attention_bwd(qcat, kcat, vcat, o, lse, pst, mt, do, s):
    tq, tk = _att_tiles(s, backward=True)
    nq, nk = s // tq, s // tk
    per_fwd_tile = _att_tiles(s)[1] // tk
    assert per_fwd_tile * tk == _att_tiles(s)[1]

    def body(q_ref, k_ref, v_ref, do_ref, o_ref, lse_ref, p_ref, mt_ref, dq_ref, dk_ref, dv_ref):
        kk, i = pl.program_id(1), pl.program_id(2)
        lane = lax.broadcasted_iota(jnp.int32, (1, LANES), 1)
        half = lane < MLA_V
        c = jnp.exp2(mt_ref[...] - lse_ref[...])
        do_p = do_ref[...] * c
        prod = do_p * o_ref[...]
        rows_i = pl.ds(pl.multiple_of(i * tq, tq), tq)
        for hh in range(2):
            sl = slice(hh * HEAD_BLK, (hh + 1) * HEAD_BLK)
            mine = half if hh == 0 else jnp.logical_not(half)
            delta = jnp.sum(jnp.where(mine, prod, 0.0), axis=1, keepdims=True)
            do_h = jnp.where(half, do_p if hh == 0 else pltpu.roll(do_p, MLA_V, 1), 0.0)
            p = p_ref[hh]
            ds = p.astype(F32) * (_mm(do_h, v_ref[:, sl], _NT) - delta)
            dv_h, dk_h, dq_h = _mm(p, do_h, _TN), _mm(ds, q_ref[:, sl], _TN), _mm(ds, k_ref[:, sl])

            @pl.when(i == 0)
            def _():
                dv_ref[:, sl] = dv_h
                dk_ref[:, sl] = dk_h

            @pl.when(i != 0)
            def _():
                dv_ref[:, sl] += dv_h
                dk_ref[:, sl] += dk_h

            @pl.when(kk == 0)
            def _():
                dq_ref[rows_i, sl] = dq_h

            @pl.when(kk != 0)
            def _():
                dq_ref[rows_i, sl] += dq_h

    w = MLA_HEADS * HEAD_BLK
    return pl.pallas_call(
        body, name="od_attn_bwd", grid=(_PAIRS, nk, nq),
        in_specs=[pl.BlockSpec((tq, 2 * HEAD_BLK), lambda p, kk, i: (i, p)),
                  pl.BlockSpec((tk, 2 * HEAD_BLK), lambda p, kk, i: (kk, p)),
                  pl.BlockSpec((tk, 2 * HEAD_BLK), lambda p, kk, i: (kk, p)),
                  pl.BlockSpec((tq, LANES), lambda p, kk, i: (i, p)),
                  pl.BlockSpec((tq, LANES), lambda p, kk, i: (i, p)),
                  pl.BlockSpec((None, tq, LANES), lambda p, kk, i: (p, i, 0)),
                  pl.BlockSpec((2, tq, tk), lambda p, kk, i: (p, i, kk)),
                  pl.BlockSpec((None, None, tq, LANES), lambda p, kk, i: (p, kk // per_fwd_tile, i, 0))],
        out_specs=[pl.BlockSpec((s, 2 * HEAD_BLK), lambda p, kk, i: (0, p), pipeline_mode=pl.Buffered(1)),
                   pl.BlockSpec((tk, 2 * HEAD_BLK), lambda p, kk, i: (kk, p)),
                   pl.BlockSpec((tk, 2 * HEAD_BLK), lambda p, kk, i: (kk, p))],
        out_shape=[jax.ShapeDtypeStruct((s, w), F32)] * 3,
        compiler_params=_params(("parallel", "arbitrary", "arbitrary")),
    )(qcat, kcat, vcat, do, o, lse, pst, mt)


def _pool_counts(n_rows, first_row, s, w):
    pos = first_row + lax.broadcasted_iota(jnp.int32, (n_rows, 1), 0)
    lo = jnp.clip(pos - w // 2, 0, s)
    hi = jnp.clip(pos + w - w // 2, 0, s)
    return jnp.maximum(hi - lo, 1).astype(F32)


def _window_sum(e, levels, mirrored):
    n = e.shape[0]
    acc = e + pltpu.roll(e, (n - 1) if mirrored else 1, 0)
    step = 1
    for _ in range(levels - 1):
        acc = pltpu.roll(acc, step, 0) + pltpu.roll(acc, n - step, 0)
        step *= 2
    return acc


def _pooled(ue, s, t):
    first = pl.program_id(0) * t
    outs = []
    for gi, w in enumerate(POOL_WINDOWS):
        eg = ue[:, gi * POOL_GROUP:(gi + 1) * POOL_GROUP]
        sm = _window_sum(eg, gi + 1, False)[HALO:HALO + t]
        outs.append(sm / _pool_counts(t, first, s, w) - eg[HALO:HALO + t])
    return outs


def odd_mix_fwd(proj, o, pool_w, pool_scale, rows):
    def fn(o_, gc, up, u, un, gd, pw, ps):
        pooled = _pooled(_ext(up, u, un, rows.n), rows.s, rows.t)
        lin = jnp.concatenate([_mm(pooled[g], pw[g]) for g in range(len(POOL_WINDOWS))], axis=1)
        return jnp.concatenate([o_ * _silu(gc), lin * ps * _silu(gd)], axis=1)

    w = POOL_WIDTH
    return rows_call("od_mix_fwd", rows, fn, [o, proj, proj, proj, proj, proj, pool_w, pool_scale],
                     [rows.tile(w), rows.tile(w, O_GC)] + rows.halo(w, O_UD)
                     + [rows.tile(w, O_GD), Rows.full((4, POOL_GROUP, POOL_GROUP)), Rows.full((1, w))],
                     [(2 * w, MATMUL_DTYPE)])[0]


def odd_mix_bwd(proj, o, pool_w, pool_scale, dycd, rows):
    w = POOL_WIDTH
    ng = len(POOL_WINDOWS)

    def fn(o_, gc, up, u, un, gdp, gd, gdn, pw, ps, dyc, dydp, dyd, dydn):
        n, t, s = rows.n, rows.t, rows.s
        dyc = dyc.astype(F32)
        do = dyc * _silu(gc)
        dgc = dyc * o_ * _dsilu(gc)
        pooled = _pooled(_ext(up, u, un, n), s, t)
        lin = jnp.concatenate([_mm(pooled[g], pw[g]) for g in range(ng)], axis=1)
        dydc = dyd.astype(F32)
        dgd = dydc * lin * ps * _dsilu(gd)
        dps = _rowsum(dydc * lin * _silu(gd))
        dlin_e = _ext(dydp, dyd, dydn, n) * ps * _silu(_ext(gdp, gd, gdn, n))
        first = pl.program_id(0) * t - HALO
        dus, dpws = [], []
        for g, win in enumerate(POOL_WINDOWS):
            sl = slice(g * POOL_GROUP, (g + 1) * POOL_GROUP)
            dle = dlin_e[:, sl]
            dpws.append(_mm(pooled[g], dle[HALO:HALO + t], _TN))
            dpe = _mm(dle, pw[g], _NT)
            gce = dpe / _pool_counts(t + 2 * HALO, first, s, win)
            dus.append(_window_sum(gce, g + 1, True)[HALO:HALO + t] - dpe[HALO:HALO + t])
        return do, dgc, jnp.concatenate(dus, axis=1), dgd, jnp.stack(dpws), dps

    return rows_call("od_mix_bwd", rows, fn,
                     [o, proj, proj, proj, proj, proj, proj, proj, pool_w, pool_scale, dycd, dycd, dycd, dycd],
                     [rows.tile(w), rows.tile(w, O_GC)] + rows.halo(w, O_UD) + rows.halo(w, O_GD)
                     + [Rows.full((ng, POOL_GROUP, POOL_GROUP)), Rows.full((1, w)), rows.tile(w, 0)]
                     + rows.halo(w, w),
                     [(w, F32)] * 4, [(ng, POOL_GROUP, POOL_GROUP), (1, w)])


def pad_odd_w_in(w):
    z = lambda n: jnp.zeros((w.shape[0], n), w.dtype)
    return jnp.concatenate([w[:, :384], z(ROPE_LO), w[:, 384:416], z(LANES - ROPE_LO - MLA_ROPE), w[:, 416:]], axis=1)


def unpad_odd_w_in_t(wpt):
    return jnp.concatenate([wpt[:384], wpt[O_KR + ROPE_LO:O_KR + ROPE_LO + MLA_ROPE], wpt[O_GC:]], axis=0)


def pad_w_uq(w):
    w3 = w.reshape(MLA_Q_RANK, MLA_HEADS, MLA_NOPE + MLA_ROPE)
    w3 = jnp.pad(w3, ((0, 0), (0, 0), (0, HEAD_BLK - MLA_NOPE - MLA_ROPE)))
    return w3.reshape(MLA_Q_RANK, MLA_HEADS * HEAD_BLK)


def unpad_w_uq(wp):
    return wp.reshape(MLA_Q_RANK, MLA_HEADS, HEAD_BLK)[..., :MLA_NOPE + MLA_ROPE].reshape(MLA_Q_RANK, -1)


def pad_w_ukv(w):
    w3 = w.reshape(MLA_KV_RANK, MLA_HEADS, MLA_NOPE + MLA_V)
    kp = jnp.pad(w3[..., :MLA_NOPE], ((0, 0), (0, 0), (0, HEAD_BLK - MLA_NOPE)))
    return jnp.concatenate([kp.reshape(MLA_KV_RANK, -1), w3[..., MLA_NOPE:].reshape(MLA_KV_RANK, -1)], axis=1)


def unpad_w_ukv(wp):
    kp = wp[:, :MLA_HEADS * HEAD_BLK].reshape(MLA_KV_RANK, MLA_HEADS, HEAD_BLK)[..., :MLA_NOPE]
    vp = wp[:, MLA_HEADS * HEAD_BLK:].reshape(MLA_KV_RANK, MLA_HEADS, MLA_V)
    return jnp.concatenate([kp, vp], axis=-1).reshape(MLA_KV_RANK, -1)


def odd_weights(od_w_in, od_q_norm_g, od_w_uq, od_kv_norm_g, od_w_ukv, od_pool_w, od_pool_scale, od_w_out,
                od_ln_g, od_ln_b):
    freq, sign = rope_rows()
    return dict(w_in_p=pad_odd_w_in(od_w_in).astype(MATMUL_DTYPE), gq=od_q_norm_g, gkv=od_kv_norm_g,
                w_uq_p=pad_w_uq(od_w_uq).astype(MATMUL_DTYPE), w_ukv_p=pad_w_ukv(od_w_ukv).astype(MATMUL_DTYPE),
                pool_w=od_pool_w, pool_scale=od_pool_scale, w_out=od_w_out.astype(MATMUL_DTYPE), ln_g=od_ln_g,
                ln_b=od_ln_b, freq=freq, sign=sign)


def odd_layer_loss(x, pos, target, w, rows):
    s = rows.s
    proj = matmul(x, w["w_in_p"], "nn", F32, "od_proj")
    cqn, ckvn = latent_norm_fwd(proj, w["gq"], w["gkv"], rows)
    qp = matmul(cqn, w["w_uq_p"], "nn", F32, "od_q_up")
    kvp = matmul(ckvn, w["w_ukv_p"], "nn", F32, "od_kv_up")
    qcat, kcat, v = rope_fwd(qp, kvp, proj, pos, w["freq"], w["sign"], rows)
    o, lse, pst, mt = attention_fwd(qcat, kcat, v, s)
    ycd = odd_mix_fwd(proj, o, w["pool_w"], w["pool_scale"], rows)
    h = matmul(ycd, w["w_out"], "nn", F32, "od_out")
    dres, dh, dln_g, dln_b, loss_lanes = final_ln_loss(x, h, w["ln_g"], w["ln_b"], target, rows)
    dycd = matmul(dh, w["w_out"], "nt", F32, "od_out_dx")
    dw_out = matmul(ycd, dh, "tn", F32, "od_out_dw")
    do, dgc, dud, dgd, dpool_w, dpool_scale = odd_mix_bwd(proj, o, w["pool_w"], w["pool_scale"], dycd, rows)
    dq, dk, dv = attention_bwd(qcat, kcat, v, o, lse, pst, mt, do, s)
    dqp, dkvp, dkr = rope_bwd(dq, dk, dv, pos, w["freq"], w["sign"], rows)
    dcqn = matmul(dqp, w["w_uq_p"], "nt", F32, "od_q_up_dx")
    dw_uq = unpad_w_uq(matmul(cqn, dqp, "tn", F32, "od_q_up_dw"))
    dckvn = matmul(dkvp, w["w_ukv_p"], "nt", F32, "od_kv_up_dx")
    dw_ukv = unpad_w_ukv(matmul(ckvn, dkvp, "tn", F32, "od_kv_up_dw"))
    dcq, dckv, dgq, dgkv = latent_norm_bwd(proj, w["gq"], w["gkv"], dcqn, dckvn, rows)
    dproj = jnp.concatenate([dcq, dckv, dkr, dgc, dud, dgd], axis=1).astype(MATMUL_DTYPE)
    dx = matmul(dproj, w["w_in_p"], "nt", F32, "od_proj_dx", add=dres)
    dw_in = unpad_odd_w_in_t(matmul(dproj, x, "tn", F32, "od_proj_dw"))
    g = dict(od_w_in=dw_in, od_q_norm_g=dgq, od_w_uq=dw_uq, od_kv_norm_g=dgkv, od_w_ukv=dw_ukv,
             od_pool_w=dpool_w, od_pool_scale=dpool_scale, od_w_out=dw_out, od_ln_g=dln_g, od_ln_b=dln_b)
    return loss_lanes, dx, g


_MESH = pl.DeviceIdType.MESH
_ANY = pl.BlockSpec(memory_space=pl.ANY)
N_CHIPS = 4


def _push_call(name, ins, out_shapes, plan, n_remote, n_local):
    n_in, n_out = len(ins), len(out_shapes)

    def body(*refs):
        in_refs, out_refs = refs[:n_in], refs[n_in:n_in + n_out]
        send_sems, recv_sems, local_sems = refs[n_in + n_out:]
        x, y, c = lax.axis_index("x"), lax.axis_index("y"), lax.axis_index("c")
        remote, local = plan(in_refs, out_refs, x, y, c)
        assert len(remote) == n_remote and len(local) == n_local
        sends = [pltpu.make_async_remote_copy(src_ref=s, dst_ref=d, send_sem=send_sems.at[k], recv_sem=recv_sems.at[k],
                                              device_id=dev, device_id_type=_MESH)
                 for k, (s, d, dev, _) in enumerate(remote)]
        recvs = [pltpu.make_async_remote_copy(src_ref=s, dst_ref=land, send_sem=send_sems.at[k],
                                              recv_sem=recv_sems.at[k], device_id=dev, device_id_type=_MESH)
                 for k, (s, _, dev, land) in enumerate(remote)]
        locs = [pltpu.make_async_copy(s, d, local_sems.at[k]) for k, (s, d) in enumerate(local)]
        for cp in sends + locs:
            cp.start()
        for cp in recvs:
            cp.wait_recv()
        for cp in sends:
            cp.wait_send()
        for cp in locs:
            cp.wait()

    return pl.pallas_call(
        body, name=name, in_specs=[_ANY] * n_in, out_specs=[_ANY] * n_out, out_shape=list(out_shapes),
        scratch_shapes=[pltpu.SemaphoreType.DMA((n_remote,)), pltpu.SemaphoreType.DMA((n_remote,)),
                        pltpu.SemaphoreType.DMA((max(n_local, 1),))],
    )(*ins)


def _other_chips(x, y):
    return [(1 - x, y), (x, 1 - y), (1 - x, 1 - y)]


def chips_allgather(bufs):
    def plan(in_refs, out_refs, x, y, c):
        me = 2 * x + y
        remote, local = [], []
        for src, out in zip(in_refs, out_refs):
            for (px, py) in _other_chips(x, y):
                remote.append((src, out.at[me], (px, py, c), out.at[2 * px + py]))
            local.append((src, out.at[me]))
        return remote, local

    shapes = [jax.ShapeDtypeStruct((N_CHIPS,) + b.shape, b.dtype) for b in bufs]
    return _push_call("weights_allgather", bufs, shapes, plan, 3 * len(bufs), len(bufs))


def sibling_send(buf, name):
    def plan(in_refs, out_refs, x, y, c):
        return [(in_refs[0], out_refs[0], (x, y, 1 - c), out_refs[0])], []

    return _push_call(name, [buf], [jax.ShapeDtypeStruct(buf.shape, buf.dtype)], plan, 1, 0)[0]


def chips_scatter(buf):
    def plan(in_refs, out_refs, x, y, c):
        me = 2 * x + y
        src, out = in_refs[0], out_refs[0]
        remote = [(src.at[2 * px + py], out.at[me], (px, py, c), out.at[2 * px + py]) for (px, py) in _other_chips(x, y)]
        return remote, [(src.at[me], out.at[me])]

    return _push_call("grads_scatter", [buf], [jax.ShapeDtypeStruct(buf.shape, buf.dtype)], plan, 3, 1)[0]


PACK_W = 1024
PACK_BLK = 128


def _ew_call(name, fn, ins, in_specs, out_shape, out_spec, n_out, steps):
    def body(*refs):
        res = fn(*[r[...] for r in refs[:len(ins)]])
        if not isinstance(res, (tuple, list)):
            res = (res,)
        for r, v in zip(refs[len(ins):], res):
            r[...] = v

    return pl.pallas_call(
        body, name=name, grid=(steps,), in_specs=in_specs, out_specs=[out_spec] * n_out,
        out_shape=[jax.ShapeDtypeStruct(out_shape, F32)] * n_out,
        compiler_params=_params(("parallel",)),
    )(*ins)


def add2(a, b, name):
    n, r, w = a.shape
    blk = pl.BlockSpec((PACK_BLK, w), lambda i: (i, 0))
    out = _ew_call(name, lambda u, v: u + v, [a.reshape(n * r, w), b.reshape(n * r, w)], [blk, blk], (n * r, w), blk,
                   1, n * r // PACK_BLK)[0]
    return out.reshape(a.shape)


def sum_chips(buf):
    _, r, w = buf.shape
    specs = [pl.BlockSpec((None, PACK_BLK, w), lambda i, q=q: (q, i, 0)) for q in range(N_CHIPS)]
    return _ew_call("grads_sum", lambda a, b, c, d: ((a + b) + c) + d, [buf] * N_CHIPS, specs, (r, w),
                    pl.BlockSpec((PACK_BLK, w), lambda i: (i, 0)), 1, r // PACK_BLK)[0]


def adamw(w, g, m, v, name):
    def fn(w_, g_, m_, v_):
        m2 = ADAM_B1 * m_ + (1.0 - ADAM_B1) * g_
        v2 = ADAM_B2 * v_ + (1.0 - ADAM_B2) * jnp.square(g_)
        m_hat = m2 / (1.0 - ADAM_B1 ** ADAM_STEP)
        v_hat = v2 / (1.0 - ADAM_B2 ** ADAM_STEP)
        return -ADAM_LR * (m_hat / (jnp.sqrt(v_hat) + ADAM_EPS) + ADAM_WD * w_), m2, v2

    r, c = w.shape
    br = r if r <= 512 else _pick(r, (256, 128, 64, 32, 16, 8))
    blk = pl.BlockSpec((br, c), lambda i: (i, 0))
    return _ew_call(name, fn, [w, g, m, v], [blk] * 4, (r, c), blk, 3, r // br)


WEIGHTS = (
    ("ev_w_in", (1, 1024, 7200), 2), ("ev_conv_w", (1, 4, 2048), 2), ("ev_conv_b", (1, 2048), None),
    ("ev_a_log", (1, 2, 16), None), ("ev_dt_bias", (1, 2, 16), None), ("ev_d_skip", (1, 2, 16), None),
    ("ev_norm_g", (1, 1024), None), ("ev_sc_conv_w", (1, 3, 1024), 2), ("ev_w_out", (1, 2048, 1024), 1),
    ("ev_ln_g", (1, 1024), None), ("ev_ln_b", (1, 1024), None), ("od_w_in", (1, 1024, 1952), 2),
    ("od_q_norm_g", (1, 256), 1), ("od_w_uq", (1, 256, 768), 2), ("od_kv_norm_g", (1, 128), None),
    ("od_w_ukv", (1, 128, 1024), 2), ("od_pool_w", (1, 4, 128, 128), None), ("od_pool_scale", (1, 512), 1),
    ("od_w_out", (1, 1024, 1024), 1), ("od_ln_g", (1, 1024), 1), ("od_ln_b", (1, 1024), 1),
)
BIG = ("ev_w_in", "ev_w_out", "od_w_in", "od_w_uq", "od_w_ukv", "od_w_out")


def _block_shape(shape, axis):
    if axis is None:
        return tuple(shape)
    return tuple(d // N_CHIPS if i == axis else d for i, d in enumerate(shape))


def _pack(arrs, quantum):
    flat = jnp.concatenate([a.reshape(-1) for a in arrs])
    n = flat.shape[0]
    padded = -(-n // quantum) * quantum
    return jnp.concatenate([flat, jnp.zeros((padded - n,), flat.dtype)]).reshape(-1, LANES)


def _unpack(flat, shapes):
    out, off = [], 0
    for sh in shapes:
        n = int(np.prod(sh))
        out.append(flat[off:off + n].reshape(sh))
        off += n
    return out


def gather_weights(local):
    sharded = [(n, sh, ax) for (n, sh, ax) in WEIGHTS if ax is not None]
    big = [(n, sh, ax) for (n, sh, ax) in sharded if n in BIG]
    small = [(n, sh, ax) for (n, sh, ax) in sharded if n not in BIG]
    pb = _pack([local[n].astype(MATMUL_DTYPE) for n, _, _ in big], 2 * SUBLANES * LANES)
    ps = _pack([local[n] for n, _, _ in small], SUBLANES * LANES)
    gb, gs = chips_allgather([pb, ps])
    full = {n: local[n] for (n, sh, ax) in WEIGHTS if ax is None}
    for group, g in ((big, gb), (small, gs)):
        parts = [_unpack(g[q].reshape(-1), [_block_shape(sh, ax) for _, sh, ax in group]) for q in range(N_CHIPS)]
        for i, (n, sh, ax) in enumerate(group):
            full[n] = jnp.concatenate([parts[q][i] for q in range(N_CHIPS)], axis=ax)
    return full


WIDE = (("ev_w_in", True), ("ev_w_out", False), ("od_w_in", True), ("od_w_out", False))


def reduce_and_update(grads, local_w, local_m, local_v):
    c = lax.axis_index("c")
    wide_names = [n for n, _ in WIDE]
    tail = [(n, sh, ax) for (n, sh, ax) in WEIGHTS if n not in wide_names]
    tail_blocks = [_block_shape(sh, ax) for _, sh, ax in tail]
    n_tail = sum(int(np.prod(b)) for b in tail_blocks)
    wide_rows = [grads[n].shape[0] // N_CHIPS for n in wide_names]
    quantum = 2 * PACK_BLK
    total = -(-(sum(wide_rows) + -(-n_tail // PACK_W)) // quantum) * quantum
    tail_rows = total - sum(wide_rows)

    def tail_pack(pieces):
        flat = jnp.concatenate([p.reshape(-1) for p in pieces] + [jnp.zeros((tail_rows * PACK_W - n_tail,), F32)])
        return flat.reshape(tail_rows, PACK_W)

    parts = [grads[n].reshape(N_CHIPS, r, PACK_W) for n, r in zip(wide_names, wide_rows)]
    cols = []
    for (n, sh, ax), bs in zip(tail, tail_blocks):
        g = grads[n].reshape(sh)
        if ax is None:
            cols.append(jnp.broadcast_to(g.reshape(1, -1), (N_CHIPS, g.size)))
        else:
            g = g.reshape(sh[:ax] + (N_CHIPS, bs[ax]) + sh[ax + 1:])
            cols.append(jnp.moveaxis(g, ax, 0).reshape(N_CHIPS, -1))
    cols.append(jnp.zeros((N_CHIPS, tail_rows * PACK_W - n_tail), F32))
    packs = jnp.concatenate(parts + [jnp.concatenate(cols, axis=1).reshape(N_CHIPS, tail_rows, PACK_W)], axis=1)
    rh = total // 2
    keep = lax.dynamic_slice_in_dim(packs, c * rh, rh, axis=1)
    give = lax.dynamic_slice_in_dim(packs, (1 - c) * rh, rh, axis=1)
    chip_half = add2(keep, sibling_send(give, "grads_to_sibling"), "grads_chip_sum")
    total_half = sum_chips(chips_scatter(chip_half))
    other_half = sibling_send(total_half, "grads_from_sibling")
    g_pack = jnp.concatenate([jnp.where(c == 0, total_half, other_half),
                              jnp.where(c == 0, other_half, total_half)], axis=0)
    outs = ({}, {}, {}, {})
    off = 0
    for (n, transposed), r in zip(WIDE, wide_rows):
        g = g_pack[off:off + r]
        off += r
        g = g.T if transposed else g
        shape = local_w[n].shape
        res = adamw(local_w[n].reshape(g.shape), g, local_m[n].reshape(g.shape), local_v[n].reshape(g.shape),
                    "adamw_" + n)
        for d, a in zip(outs, (g, *res)):
            d[n] = a.reshape(shape)
    g_tail = g_pack[off:]
    res = adamw(*[tail_pack([d[n] for n, _, _ in tail]) if d is not None else g_tail
                  for d in (local_w, None, local_m, local_v)], "adamw_small")
    for d, a in zip(outs, (g_tail, *res)):
        d.update(zip([n for n, _, _ in tail], _unpack(a.reshape(-1), tail_blocks)))
    return outs


ROW_TILE = 256


def kernel(x, positions, ev_w_in, ev_conv_w, ev_conv_b, ev_a_log, ev_dt_bias, ev_d_skip, ev_norm_g, ev_sc_conv_w, ev_w_out, ev_ln_g, ev_ln_b, od_w_in, od_q_norm_g, od_w_uq, od_kv_norm_g, od_w_ukv, od_pool_w, od_pool_scale, od_w_out, od_ln_g, od_ln_b, loss_target, m_ev_w_in, m_ev_conv_w, m_ev_conv_b, m_ev_a_log, m_ev_dt_bias, m_ev_d_skip, m_ev_norm_g, m_ev_sc_conv_w, m_ev_w_out, m_ev_ln_g, m_ev_ln_b, m_od_w_in, m_od_q_norm_g, m_od_w_uq, m_od_kv_norm_g, m_od_w_ukv, m_od_pool_w, m_od_pool_scale, m_od_w_out, m_od_ln_g, m_od_ln_b, v_ev_w_in, v_ev_conv_w, v_ev_conv_b, v_ev_a_log, v_ev_dt_bias, v_ev_d_skip, v_ev_norm_g, v_ev_sc_conv_w, v_ev_w_out, v_ev_ln_g, v_ev_ln_b, v_od_w_in, v_od_q_norm_g, v_od_w_uq, v_od_kv_norm_g, v_od_w_ukv, v_od_pool_w, v_od_pool_scale, v_od_w_out, v_od_ln_g, v_od_ln_b):
    names = [n for n, _, _ in WEIGHTS]
    local_w = dict(zip(names, (ev_w_in, ev_conv_w, ev_conv_b, ev_a_log, ev_dt_bias, ev_d_skip, ev_norm_g, ev_sc_conv_w, ev_w_out, ev_ln_g, ev_ln_b, od_w_in, od_q_norm_g, od_w_uq, od_kv_norm_g, od_w_ukv, od_pool_w, od_pool_scale, od_w_out, od_ln_g, od_ln_b)))
    local_m = dict(zip(names, (m_ev_w_in, m_ev_conv_w, m_ev_conv_b, m_ev_a_log, m_ev_dt_bias, m_ev_d_skip, m_ev_norm_g, m_ev_sc_conv_w, m_ev_w_out, m_ev_ln_g, m_ev_ln_b, m_od_w_in, m_od_q_norm_g, m_od_w_uq, m_od_kv_norm_g, m_od_w_ukv, m_od_pool_w, m_od_pool_scale, m_od_w_out, m_od_ln_g, m_od_ln_b)))
    local_v = dict(zip(names, (v_ev_w_in, v_ev_conv_w, v_ev_conv_b, v_ev_a_log, v_ev_dt_bias, v_ev_d_skip, v_ev_norm_g, v_ev_sc_conv_w, v_ev_w_out, v_ev_ln_g, v_ev_ln_b, v_od_w_in, v_od_q_norm_g, v_od_w_uq, v_od_kv_norm_g, v_od_w_ukv, v_od_pool_w, v_od_pool_scale, v_od_w_out, v_od_ln_g, v_od_ln_b)))
    s = x.shape[1]
    rows = Rows(s, min(ROW_TILE, s))
    f = gather_weights(local_w)
    ew = even_weights(f["ev_w_in"][0], f["ev_conv_w"][0], f["ev_conv_b"], f["ev_a_log"][0], f["ev_dt_bias"][0],
                      f["ev_d_skip"][0], f["ev_norm_g"], f["ev_sc_conv_w"][0], f["ev_w_out"][0], f["ev_ln_g"],
                      f["ev_ln_b"])
    ow = odd_weights(f["od_w_in"][0], f["od_q_norm_g"], f["od_w_uq"][0], f["od_kv_norm_g"], f["od_w_ukv"][0],
                     f["od_pool_w"][0], f["od_pool_scale"], f["od_w_out"][0], f["od_ln_g"], f["od_ln_b"])
    x1, saved = even_layer(x[0], ew, rows)
    loss_lanes, dx1, g_odd = odd_layer_loss(x1, positions.reshape(s, 1), loss_target[0], ow, rows)
    dx0, g_even = even_layer_bwd(dx1, ew, saved, rows)
    loss = lax.psum(jnp.sum(loss_lanes), ("x", "y", "c"))
    grad, delta, new_m, new_v = reduce_and_update({**g_even, **g_odd}, local_w, local_m, local_v)
    return (loss, dx0[None], *[grad[n] for n in names], *[delta[n] for n in names],
            *[new_m[n] for n in names], *[new_v[n] for n in names])
```

```python
import functools
import math

import jax
import jax.numpy as jnp
import numpy as np
from jax import lax
from jax.experimental import pallas as pl
from jax.experimental.pallas import tpu as pltpu

F32 = jnp.float32
BF16 = jnp.bfloat16
MATMUL_DTYPE = jnp.bfloat16

D_MODEL = 1024
DEPTH = 2
SSD_HEADS, SSD_HEAD_DIM, SSD_GROUPS, SSD_STATE, SSD_CHUNK = 16, 64, 4, 128, 128
SSD_INNER = SSD_HEADS * SSD_HEAD_DIM
SSD_XBC = SSD_INNER + 2 * SSD_GROUPS * SSD_STATE
SC_WIDTH = 1024
MLA_HEADS, MLA_Q_RANK, MLA_KV_RANK, MLA_NOPE, MLA_ROPE, MLA_V = 8, 256, 128, 64, 32, 64
MLA_WIDTH = MLA_HEADS * MLA_V
ROPE_THETA = 10000.0
ATTN_SCALE = (MLA_NOPE + MLA_ROPE) ** -0.5
QSCALE = ATTN_SCALE * math.log2(math.e)
LN2 = math.log(2.0)
POOL_WINDOWS = (2, 4, 8, 16)
POOL_GROUP = 128
POOL_WIDTH = POOL_GROUP * len(POOL_WINDOWS)
EPS = 1e-5
ALPHA = (2 * DEPTH) ** 0.25
EVEN_PROJ, ODD_PROJ = 7200, 1952
ADAM_LR, ADAM_B1, ADAM_B2, ADAM_EPS, ADAM_WD, ADAM_STEP = 0.001, 0.9, 0.999, 1e-08, 0.01, 10

LANES = 128
SUBLANES = 8
HALO = SUBLANES
VMEM_LIMIT = 56 * 1024 * 1024

EVEN_P = 7296
EVEN_F = 2176
G_Z, G_BG, G_CG, G_H, G_GATE = 0, 1024, 2048, 3072, 4096
ODD_P = 2048
E_XBC, E_DT = 0, 2048
O_CQ, O_CKV, O_KR, O_GC, O_UD, O_GD = 0, 256, 384, 512, 1024, 1536


def _params(sem=None, vmem=VMEM_LIMIT):
    return pltpu.CompilerParams(dimension_semantics=sem, vmem_limit_bytes=vmem)


def _mm(a, b, dims=(((1,), (0,)), ((), ()))):
    return lax.dot_general(a.astype(MATMUL_DTYPE), b.astype(MATMUL_DTYPE), dims, preferred_element_type=F32)


_NN = (((1,), (0,)), ((), ()))
_NT = (((1,), (1,)), ((), ()))
_TN = (((0,), (0,)), ((), ()))


def _silu(v):
    return v * jax.nn.sigmoid(v)


def _dsilu(v):
    s = jax.nn.sigmoid(v)
    return s * (1.0 + v * (1.0 - s))


def _pick(n, prefs):
    for p in prefs:
        if n % p == 0:
            return p
    return n


def matmul(a, b, mode, out_dtype, name, add=None, tm=None, tn=None, tk=None):
    if mode == "nn":
        (m, k), (k2, n) = a.shape, b.shape
    elif mode == "nt":
        (m, k), (n, k2) = a.shape, b.shape
    else:
        (k, m), (k2, n) = a.shape, b.shape
    assert k == k2, (a.shape, b.shape, mode)
    wide = (2432, 2048, 1536, 1024, 512, 256, 128)
    if mode == "tn":
        tm = tm or _pick(m, (2432, 2048, 1024, 512, 256, 128))
        tn = tn or _pick(n, wide)
        tk = tk or _pick(k, (512, 256, 128))
    else:
        tm = tm or _pick(m, (512, 256, 128))
        tn = tn or _pick(n, wide)
        tk = tk or _pick(k, wide)
    nk = k // tk
    dims = {"nn": _NN, "nt": _NT, "tn": _TN}[mode]

    def body(a_ref, b_ref, *rest):
        o_ref, acc_ref = rest[-2:]
        kk = pl.program_id(2)

        @pl.when(kk == 0)
        def _():
            acc_ref[...] = jnp.zeros_like(acc_ref) if add is None else rest[0][...].astype(F32)

        acc_ref[...] += _mm(a_ref[...], b_ref[...], dims)

        @pl.when(kk == nk - 1)
        def _():
            o_ref[...] = acc_ref[...].astype(o_ref.dtype)

    a_spec = {"nn": pl.BlockSpec((tm, tk), lambda i, j, kk: (i, kk)),
              "nt": pl.BlockSpec((tm, tk), lambda i, j, kk: (i, kk)),
              "tn": pl.BlockSpec((tk, tm), lambda i, j, kk: (kk, i))}[mode]
    b_spec = {"nn": pl.BlockSpec((tk, tn), lambda i, j, kk: (kk, j)),
              "nt": pl.BlockSpec((tn, tk), lambda i, j, kk: (j, kk)),
              "tn": pl.BlockSpec((tk, tn), lambda i, j, kk: (kk, j))}[mode]
    return pl.pallas_call(
        body, name=name, grid=(m // tm, n // tn, nk),
        in_specs=[a_spec, b_spec] + ([] if add is None else [pl.BlockSpec((tm, tn), lambda i, j, kk: (i, j))]),
        out_specs=pl.BlockSpec((tm, tn), lambda i, j, kk: (i, j)),
        out_shape=jax.ShapeDtypeStruct((m, n), out_dtype),
        scratch_shapes=[pltpu.VMEM((tm, tn), F32)],
        compiler_params=_params(("parallel", "parallel", "arbitrary")),
    )(*((a, b) if add is None else (a, b, add)))


class Rows:
    def __init__(self, s, t):
        assert s % t == 0 and t % HALO == 0
        self.s, self.t, self.n = s, t, s // t

    def tile(self, width, col=0, lead=None):
        cb = col // width
        assert col % width == 0
        if lead is None:
            return pl.BlockSpec((self.t, width), lambda i: (i, cb))
        return pl.BlockSpec((None, self.t, width), lambda i: (lead, i, cb))

    def prev(self, width, col=0, lead=None):
        cb, r = col // width, self.t // HALO
        if lead is None:
            return pl.BlockSpec((HALO, width), lambda i: (jnp.maximum(i * r - 1, 0), cb))
        return pl.BlockSpec((None, HALO, width), lambda i: (lead, jnp.maximum(i * r - 1, 0), cb))

    def next(self, width, col=0, lead=None):
        cb, r, last = col // width, self.t // HALO, self.s // HALO - 1
        if lead is None:
            return pl.BlockSpec((HALO, width), lambda i: (jnp.minimum((i + 1) * r, last), cb))
        return pl.BlockSpec((None, HALO, width), lambda i: (lead, jnp.minimum((i + 1) * r, last), cb))

    def halo(self, width, col=0, lead=None):
        return [self.prev(width, col, lead), self.tile(width, col, lead), self.next(width, col, lead)]

    @staticmethod
    def full(shape):
        nd = len(shape)
        return pl.BlockSpec(tuple(shape), lambda i: (0,) * nd)


def rows_call(name, rows, fn, ins, in_specs, row_outs, acc_outs=()):
    n_row = len(row_outs)

    def body(*refs):
        in_refs = refs[:len(ins)]
        out_refs = refs[len(ins):]
        res = fn(*[r[...] for r in in_refs])
        if not isinstance(res, (tuple, list)):
            res = (res,)
        for r, v in zip(out_refs[:n_row], res[:n_row]):
            r[...] = v.astype(r.dtype)
        if acc_outs:
            first = pl.program_id(0) == 0

            @pl.when(first)
            def _():
                for r, v in zip(out_refs[n_row:], res[n_row:]):
                    r[...] = v.astype(F32)

            @pl.when(jnp.logical_not(first))
            def _():
                for r, v in zip(out_refs[n_row:], res[n_row:]):
                    r[...] += v.astype(F32)

    out_shape = [jax.ShapeDtypeStruct((rows.s, w), dt) for (w, dt) in row_outs]
    out_specs = [rows.tile(w) for (w, dt) in row_outs]
    out_shape += [jax.ShapeDtypeStruct(tuple(sh), F32) for sh in acc_outs]
    out_specs += [Rows.full(sh) for sh in acc_outs]
    return pl.pallas_call(
        body, name=name, grid=(rows.n,), in_specs=list(in_specs), out_specs=out_specs, out_shape=out_shape,
        compiler_params=_params(("arbitrary",)),
    )(*ins)


def _edge_zero(prev, nxt, n_tiles, axis=0):
    i = pl.program_id(axis)
    prev = jnp.where(i == 0, jnp.zeros_like(prev), prev)
    nxt = jnp.where(i == n_tiles - 1, jnp.zeros_like(nxt), nxt)
    return prev, nxt


def _ext(prev, cur, nxt, n_tiles, axis=0):
    prev, nxt = _edge_zero(prev, nxt, n_tiles, axis)
    return jnp.concatenate([prev.astype(F32), cur.astype(F32), nxt.astype(F32)], axis=0)


def _shift(ext, off):
    n = ext.shape[0]
    t = n - 2 * HALO
    if off == 0:
        return ext[HALO:HALO + t]
    return pltpu.roll(ext, (-off) % n, 0)[HALO:HALO + t]


def _rowsum(v):
    return jnp.sum(v, axis=0, keepdims=True)


CONV_OFFS = (-2, -1, 0, 1)
SC_OFFS = (-1, 0, 1)


def conv_fwd(proj, conv_w, conv_b, rows):
    def fn(p, c, nx, w, b):
        e = _ext(p, c, nx, rows.n)
        pre = b
        for k, off in enumerate(CONV_OFFS):
            pre = pre + w[k:k + 1, :] * _shift(e, off)
        return pre, _silu(pre)

    outs = [rows_call(f"ev_conv_fwd{h}", rows, fn,
                      [proj, proj, proj, conv_w[:, h * 1024:(h + 1) * 1024], conv_b[:, h * 1024:(h + 1) * 1024]],
                      rows.halo(1024, E_XBC + h * 1024) + [Rows.full((4, 1024)), Rows.full((1, 1024))],
                      [(1024, MATMUL_DTYPE), (1024, F32)]) for h in range(2)]
    return outs


def conv_bwd(proj, pre_h, du_h, conv_w, rows):
    res = []
    for h in range(2):
        def fn(*a):
            w = a[-1]
            xe = _ext(a[0], a[1], a[2], rows.n)
            pe = jnp.concatenate([a[3], a[4], a[5]], axis=0).astype(F32)
            g = a[6:-1]
            du = _ext(g[0], g[1], g[2], rows.n) + _ext(g[3], g[4], g[5], rows.n)
            dpre = du * _dsilu(pe)
            dx = jnp.zeros_like(a[1], dtype=F32)
            dws = []
            for k, off in enumerate(CONV_OFFS):
                dx = dx + w[k:k + 1, :] * _shift(dpre, -off)
                dws.append(_rowsum(_shift(dpre, 0) * _shift(xe, off)))
            dw = jnp.concatenate(dws + [jnp.zeros((4, dx.shape[1]), F32)], axis=0)
            return dx, dw, _rowsum(_shift(dpre, 0))

        gi = [du_h[h][0]] * 3 + [du_h[h][1]] * 3
        gs = rows.halo(1024) * 2
        res.append(rows_call(
            f"ev_conv_bwd{h}", rows, fn,
            [proj] * 3 + [pre_h[h]] * 3 + gi + [conv_w[:, h * 1024:(h + 1) * 1024]],
            rows.halo(1024, E_XBC + h * 1024) + rows.halo(1024) + gs + [Rows.full((4, 1024))],
            [(1024, MATMUL_DTYPE)], [(8, 1024), (1, 1024)]))
    dconv_w = jnp.concatenate([res[0][1][:4], res[1][1][:4]], axis=1)
    dconv_b = jnp.concatenate([res[0][2], res[1][2]], axis=1)
    return [res[0][0], res[1][0]], dconv_w, dconv_b


def _head_row(p):
    return jnp.concatenate([p.reshape(1, 2 * SSD_HEADS), jnp.zeros((1, LANES - 2 * SSD_HEADS), F32)], axis=1)


def _head_unrow(r):
    return r[:, :2 * SSD_HEADS].reshape(2, SSD_HEADS)


def _ssd_pre_fn(dtraw, bias_row, alog_row):
    q = dtraw.shape[0]
    dt = jax.nn.softplus(dtraw + bias_row)
    da = dt * (-jnp.exp(alog_row))
    li = lax.broadcasted_iota(jnp.int32, (q, q), 0)
    si = lax.broadcasted_iota(jnp.int32, (q, q), 1)
    tril = (li >= si).astype(F32)
    csf = lax.dot_general(tril, da, _NN, precision=lax.Precision.HIGHEST, preferred_element_type=F32)
    tot = jnp.sum(da, axis=0, keepdims=True)
    lane = lax.broadcasted_iota(jnp.int32, (1, LANES), 1)
    cs = jnp.where(lane < SSD_HEADS, csf, tot - csf + da)
    return dt, cs


def ssd_pre(proj, bias_row, alog_row, s):
    rows = Rows(s, SSD_CHUNK)
    return rows_call("ev_ssd_pre", rows, _ssd_pre_fn, [proj, bias_row, alog_row],
                     [rows.tile(LANES, E_DT), Rows.full((1, LANES)), Rows.full((1, LANES))],
                     [(LANES, F32), (LANES, F32)])


def ssd_pre_bwd(proj, bias_row, alog_row, ddt, dcs, s):
    rows = Rows(s, SSD_CHUNK)

    def fn(dtraw, b, al, g0, g1, c0, c1):
        _, vjp = jax.vjp(_ssd_pre_fn, dtraw, b, al)
        return vjp((g0 + g1, c0 + c1))

    return rows_call("ev_ssd_pre_bwd", rows, fn, [proj, bias_row, alog_row, ddt[0], ddt[1], dcs[0], dcs[1]],
                     [rows.tile(LANES, E_DT), Rows.full((1, LANES)), Rows.full((1, LANES))] + [rows.tile(LANES)] * 4,
                     [(LANES, MATMUL_DTYPE)], [(1, LANES), (1, LANES)])


_N_PAIR = SSD_HEADS // 2
_BC = SSD_GROUPS * SSD_STATE


def _ssd_chunk_fn(x, bc, dt, cs, h_in, dsk_row, d):
    q = x.shape[0]
    lane = lax.broadcasted_iota(jnp.int32, (1, LANES), 1)
    half = lane < SSD_HEAD_DIM
    lo, hi = half.astype(F32), 1.0 - half.astype(F32)
    cst = cs.T
    li = lax.broadcasted_iota(jnp.int32, (q, 1), 0)
    si = lax.broadcasted_iota(jnp.int32, (1, q), 1)
    mask = li >= si if d == 0 else li <= si
    end = q - 1 if d == 0 else 0
    tot = cs[end:end + 1, :]
    mine = (lane >= SSD_HEADS * d) & (lane < SSD_HEADS * (d + 1))
    e_cs, e_dec, e_tot = jnp.exp(cs), jnp.exp(jnp.where(mine, tot - cs, 0.0)), jnp.exp(tot)
    e_dd = e_dec * dt
    dtt = dt.T

    def col(v, l):
        return v[:, l:l + 1]

    def by_head(v, l):
        return jnp.where(half, col(v, l), col(v, l + 1))

    per_group = _N_PAIR // SSD_GROUPS
    ys, hs = [], []
    for g in range(SSD_GROUPS):
        bm = bc[:, g * SSD_STATE:(g + 1) * SSD_STATE]
        cm = bc[:, _BC + g * SSD_STATE:_BC + (g + 1) * SSD_STATE]
        cb = _mm(cm, bm, _NT)
        pairs = range(g * per_group, (g + 1) * per_group)
        h_g = [h_in[j * SSD_STATE:(j + 1) * SSD_STATE, :] for j in pairs]
        y_off = _mm(cm, jnp.concatenate(h_g, axis=1))
        for k, j in enumerate(pairs):
            l0 = SSD_HEADS * d + 2 * j
            xj = x[:, j * LANES:(j + 1) * LANES]
            xcat = jnp.concatenate([xj * lo, xj * hi], axis=0)
            w0 = cb * jnp.exp(jnp.where(mask, col(cs, l0) - cst[l0:l0 + 1, :], -jnp.inf)) * dtt[l0:l0 + 1, :]
            w1 = cb * jnp.exp(jnp.where(mask, col(cs, l0 + 1) - cst[l0 + 1:l0 + 2, :], -jnp.inf)) * dtt[l0 + 1:l0 + 2, :]
            y = _mm(jnp.concatenate([w0, w1], axis=1), xcat)
            st = _mm(jnp.concatenate([bm * col(e_dd, l0), bm * col(e_dd, l0 + 1)], axis=0), xcat, _TN)
            y = y + y_off[:, k * LANES:(k + 1) * LANES] * by_head(e_cs, l0) + by_head(dsk_row, l0) * xj
            ys.append(y)
            hs.append(h_g[k] * by_head(e_tot, l0) + st)
    return jnp.concatenate(ys, axis=1), jnp.concatenate(hs, axis=0)


def _chunk_of(d, ci, nc, backward):
    up = ci if not backward else nc - 1 - ci
    return up + d * (nc - 1 - 2 * up)


_ST_ROWS = _N_PAIR * SSD_STATE


def _ssd_fwd_dir(u_h, dt, cs, dsk_row, s, d):
    nc = s // SSD_CHUNK
    q = SSD_CHUNK

    def body(x_ref, bc_ref, dt_ref, cs_ref, dsk_ref, y_ref, hs_ref, st_ref):
        @pl.when(pl.program_id(0) == 0)
        def _():
            st_ref[...] = jnp.zeros(st_ref.shape, F32)

        h_in = st_ref[...]
        y, h_out = _ssd_chunk_fn(x_ref[...], bc_ref[...], dt_ref[...], cs_ref[...], h_in, dsk_ref[...], d)
        y_ref[...] = y.astype(y_ref.dtype)
        hs_ref[...] = h_in
        st_ref[...] = h_out

    ch = lambda ci: _chunk_of(d, ci, nc, False)
    return pl.pallas_call(
        body, name=f"ev_ssd_fwd{d}", grid=(nc,),
        in_specs=[pl.BlockSpec((q, SSD_INNER), lambda ci: (ch(ci), 0)),
                  pl.BlockSpec((q, 2 * _BC), lambda ci: (ch(ci), 0)),
                  pl.BlockSpec((q, LANES), lambda ci: (ch(ci), 0)),
                  pl.BlockSpec((q, LANES), lambda ci: (ch(ci), 0)),
                  pl.BlockSpec((1, LANES), lambda ci: (0, 0))],
        out_specs=[pl.BlockSpec((q, SSD_INNER), lambda ci: (ch(ci), 0)),
                   pl.BlockSpec((None, _ST_ROWS, LANES), lambda ci: (ch(ci), 0, 0))],
        out_shape=[jax.ShapeDtypeStruct((s, SSD_INNER), MATMUL_DTYPE),
                   jax.ShapeDtypeStruct((nc, _ST_ROWS, LANES), F32)],
        scratch_shapes=[pltpu.VMEM((_ST_ROWS, LANES), F32)],
        compiler_params=_params(("arbitrary",)),
    )(u_h[0], u_h[1], dt, cs, dsk_row)


def ssd_fwd(u_h, dt, cs, dsk_row, s):
    ys, hss = zip(*[_ssd_fwd_dir(u_h, dt, cs, dsk_row, s, d) for d in range(2)])
    return ys, hss


def _ssd_bwd_dir(u_h, dt, cs, dsk_row, hs, dy, s, d):
    nc = s // SSD_CHUNK
    q = SSD_CHUNK

    def body(x_ref, bc_ref, dt_ref, cs_ref, dsk_ref, hs_ref, dy_ref,
             dx_ref, dbc_ref, ddt_ref, dcs_ref, ddsk_ref, dst_ref):
        ci = pl.program_id(0)

        @pl.when(ci == 0)
        def _():
            dst_ref[...] = jnp.zeros(dst_ref.shape, F32)

        f = functools.partial(_ssd_chunk_fn, d=d)
        _, vjp = jax.vjp(f, x_ref[...], bc_ref[...], dt_ref[...], cs_ref[...], hs_ref[...], dsk_ref[...])
        dx, dbc, ddt, dcs, dh, ddsk = vjp((dy_ref[...], dst_ref[...]))
        dx_ref[...] = dx
        dbc_ref[...] = dbc
        ddt_ref[...] = ddt
        dcs_ref[...] = dcs
        dst_ref[...] = dh

        @pl.when(ci == 0)
        def _():
            ddsk_ref[...] = ddsk

        @pl.when(ci != 0)
        def _():
            ddsk_ref[...] += ddsk

    ch = lambda ci: _chunk_of(d, ci, nc, True)
    row_blk = lambda w: pl.BlockSpec((q, w), lambda ci: (ch(ci), 0))
    return pl.pallas_call(
        body, name=f"ev_ssd_bwd{d}", grid=(nc,),
        in_specs=[row_blk(SSD_INNER), row_blk(2 * _BC), row_blk(LANES), row_blk(LANES),
                  pl.BlockSpec((1, LANES), lambda ci: (0, 0)),
                  pl.BlockSpec((None, _ST_ROWS, LANES), lambda ci: (ch(ci), 0, 0)), row_blk(SSD_INNER)],
        out_specs=[row_blk(SSD_INNER), row_blk(2 * _BC), row_blk(LANES), row_blk(LANES),
                   pl.BlockSpec((1, LANES), lambda ci: (0, 0))],
        out_shape=[jax.ShapeDtypeStruct((s, SSD_INNER), F32), jax.ShapeDtypeStruct((s, 2 * _BC), F32),
                   jax.ShapeDtypeStruct((s, LANES), F32), jax.ShapeDtypeStruct((s, LANES), F32),
                   jax.ShapeDtypeStruct((1, LANES), F32)],
        scratch_shapes=[pltpu.VMEM((_ST_ROWS, LANES), F32)],
        compiler_params=_params(("arbitrary",)),
    )(u_h[0], u_h[1], dt, cs, dsk_row, hs, dy)


def ssd_bwd(u_h, dt, cs, dsk_row, hs, dy, s):
    return zip(*[_ssd_bwd_dir(u_h, dt, cs, dsk_row, hs[d], dy, s, d) for d in range(2)])


def _gated_rms(ys, z, g):
    t1 = ys * _silu(z)
    return t1 * lax.rsqrt(jnp.mean(t1 * t1, axis=-1, keepdims=True) + EPS) * g


def even_mix_fwd(proj, y2, norm_g, sc_w, rows):
    def fn(yf, yb, z, bg, cgp, cg, cgn, hp, hh, hn, gate, g, w):
        z, bg, gate = z.astype(F32), bg.astype(F32), gate.astype(F32)
        ya = _gated_rms(yf.astype(F32) + yb.astype(F32), z, g)
        me = _ext(cgp, cg, cgn, rows.n) * _ext(hp, hh, hn, rows.n)
        cm = sum(w[k:k + 1, :] * _shift(me, off) for k, off in enumerate(SC_OFFS))
        return jnp.concatenate([ya, bg * cm * _silu(gate)], axis=1)

    w = 1024
    return rows_call("ev_mix_fwd", rows, fn,
                     [y2[0], y2[1], proj, proj] + [proj] * 6 + [proj, norm_g, sc_w],
                     [rows.tile(w), rows.tile(w), rows.tile(w, G_Z), rows.tile(w, G_BG)]
                     + rows.halo(w, G_CG) + rows.halo(w, G_H)
                     + [rows.tile(w, G_GATE), Rows.full((1, w)), Rows.full((3, w))],
                     [(2 * w, MATMUL_DTYPE)])[0]


MIX_COLS = 256


def even_mix_bwd(proj, y2, norm_g, sc_w, dyab, rows):
    w = 1024

    def norm_fn(yf, yb, z, g, dya):
        _, vjp = jax.vjp(_gated_rms, yf.astype(F32) + yb.astype(F32), z.astype(F32), g)
        dys, dz, dg = vjp(dya.astype(F32))
        return dys, dz, _rowsum(dg)

    dys, dz, dnorm_g = rows_call(
        "ev_mix_bwd_norm", rows, norm_fn, [y2[0], y2[1], proj, norm_g, dyab],
        [rows.tile(w), rows.tile(w), rows.tile(w, G_Z), Rows.full((1, w)), rows.tile(w, 0)],
        [(w, F32), (w, MATMUL_DTYPE)], [(1, w)])

    cw, n, t = MIX_COLS, rows.n, rows.t
    r, last = t // HALO, rows.s // HALO - 1

    def body(*refs):
        dsw_ref = refs[-1]
        dbg_ref, dcg_ref, dh_ref, dgate_ref = refs[-5:-1]
        sw = refs[15][...]
        dye, bge, gte, cge, he = [_ext(refs[3 * k][...], refs[3 * k + 1][...], refs[3 * k + 2][...], n, axis=1)
                                  for k in range(5)]
        me = cge * he
        cm = sum(sw[k:k + 1, :] * _shift(me, off) for k, off in enumerate(SC_OFFS))
        dyc, bgc, gtc = _shift(dye, 0), _shift(bge, 0), _shift(gte, 0)
        dcme = dye * bge * _silu(gte)
        dm = sum(sw[k:k + 1, :] * _shift(dcme, -off) for k, off in enumerate(SC_OFFS))
        dcm = _shift(dcme, 0)
        dsw = jnp.concatenate([_rowsum(dcm * _shift(me, off)) for off in SC_OFFS] + [jnp.zeros((5, cw), F32)], axis=0)
        dbg_ref[...] = (dyc * cm * _silu(gtc)).astype(dbg_ref.dtype)
        dcg_ref[...] = (dm * _shift(he, 0)).astype(dcg_ref.dtype)
        dh_ref[...] = (dm * _shift(cge, 0)).astype(dh_ref.dtype)
        dgate_ref[...] = (dyc * bgc * cm * _dsilu(gtc)).astype(dgate_ref.dtype)
        first = pl.program_id(1) == 0

        @pl.when(first)
        def _():
            dsw_ref[...] = dsw

        @pl.when(jnp.logical_not(first))
        def _():
            dsw_ref[...] += dsw

    def halo(col):
        cb = col // cw
        return [pl.BlockSpec((HALO, cw), lambda j, i: (jnp.maximum(i * r - 1, 0), cb + j)),
                pl.BlockSpec((t, cw), lambda j, i: (i, cb + j)),
                pl.BlockSpec((HALO, cw), lambda j, i: (jnp.minimum((i + 1) * r, last), cb + j))]

    tile_out = pl.BlockSpec((t, cw), lambda j, i: (i, j))
    dbg, dcg, dhh, dgate, dsw = pl.pallas_call(
        body, name="ev_mix_bwd_conv", grid=(w // cw, n),
        in_specs=halo(w) + halo(G_BG) + halo(G_GATE) + halo(G_CG) + halo(G_H)
        + [pl.BlockSpec((3, cw), lambda j, i: (0, j))],
        out_specs=[tile_out] * 4 + [pl.BlockSpec((8, cw), lambda j, i: (0, j))],
        out_shape=[jax.ShapeDtypeStruct((rows.s, w), MATMUL_DTYPE)] * 4 + [jax.ShapeDtypeStruct((8, w), F32)],
        compiler_params=_params(("parallel", "arbitrary")),
    )(*([dyab] * 3 + [proj] * 12 + [sc_w]))
    return dys, dz, dbg, dcg, dhh, dgate, dnorm_g, dsw


def _res_ln(x, h, g, b):
    v = ALPHA * x + h
    mu = jnp.mean(v, axis=-1, keepdims=True)
    var = jnp.mean(jnp.square(v - mu), axis=-1, keepdims=True)
    return (v - mu) * lax.rsqrt(var + EPS) * g + b


def res_ln_fwd(x, h, g, b, rows, name):
    return rows_call(name, rows, _res_ln, [x, h, g, b],
                     [rows.tile(D_MODEL), rows.tile(D_MODEL), Rows.full((1, D_MODEL)), Rows.full((1, D_MODEL))],
                     [(D_MODEL, F32)])[0]


def res_ln_bwd(x, h, g, b, dy, rows, name):
    def fn(x_, h_, g_, b_, dy_):
        _, vjp = jax.vjp(_res_ln, x_, h_, g_, b_)
        dx, dh, dg, db = vjp(dy_)
        return dx, dh, dg, db

    return rows_call(name, rows, fn, [x, h, g, b, dy],
                     [rows.tile(D_MODEL), rows.tile(D_MODEL), Rows.full((1, D_MODEL)), Rows.full((1, D_MODEL)),
                      rows.tile(D_MODEL)],
                     [(D_MODEL, F32), (D_MODEL, F32)], [(1, D_MODEL), (1, D_MODEL)])


def final_ln_loss(x, h, g, b, target, rows):
    def fn(x_, h_, g_, b_, t_):
        y, vjp = jax.vjp(_res_ln, x_, h_, g_, b_)
        err = y - t_
        dx, dh, dg, db = vjp(err * (1.0 / D_MODEL))
        return dx, dh, dg, db, _rowsum(jnp.square(err)) * (0.5 / D_MODEL)

    return rows_call("od_ln_loss", rows, fn, [x, h, g, b, target],
                     [rows.tile(D_MODEL), rows.tile(D_MODEL), Rows.full((1, D_MODEL)), Rows.full((1, D_MODEL)),
                      rows.tile(D_MODEL)],
                     [(D_MODEL, F32), (D_MODEL, F32)], [(1, D_MODEL), (1, D_MODEL), (1, D_MODEL)])


def pad_even_w_in(w):
    return jnp.concatenate([w[:, 1024:3104], jnp.zeros((w.shape[0], EVEN_P - EVEN_PROJ), w.dtype), w[:, :1024],
                            w[:, 3104:]], axis=1)


def matmul_pieces_nt(pieces, w, name, add):
    m, n, npc = pieces[0].shape[0], w.shape[0], len(pieces)
    widths = [p.shape[1] for p in pieces]
    assert sum(widths) == w.shape[1]
    tm = _pick(m, (256, 128))

    def body(*refs):
        p_refs = refs[:npc]
        w_ref, add_ref, o_ref = refs[npc:]
        acc, off = add_ref[...].astype(F32), 0
        for p_ref, wd in zip(p_refs, widths):
            acc = acc + _mm(p_ref[...], w_ref[:, off:off + wd], _NT)
            off += wd
        o_ref[...] = acc

    return pl.pallas_call(
        body, name=name, grid=(m // tm,),
        in_specs=[pl.BlockSpec((tm, wd), lambda i: (i, 0)) for wd in widths]
        + [pl.BlockSpec(w.shape, lambda i: (0, 0), pipeline_mode=pl.Buffered(1)),
           pl.BlockSpec((tm, n), lambda i: (i, 0))],
        out_specs=pl.BlockSpec((tm, n), lambda i: (i, 0)),
        out_shape=jax.ShapeDtypeStruct((m, n), F32),
        compiler_params=_params(("parallel",)),
    )(*pieces, w, add)


def even_layer(x, w, rows):
    s = rows.s
    xb = x.astype(MATMUL_DTYPE)
    proj = matmul(xb, w["w_in_p"][:, :EVEN_F], "nn", F32, "ev_proj_f", tn=EVEN_F)
    proj_g = matmul(xb, w["w_in_p"][:, EVEN_F:], "nn", MATMUL_DTYPE, "ev_proj_g", tn=(EVEN_P - EVEN_F) // 2)
    (pre0, u0), (pre1, u1) = conv_fwd(proj, w["conv_w"], w["conv_b"], rows)
    dt, cs = ssd_pre(proj, w["bias_row"], w["alog_row"], s)
    y2, hs = ssd_fwd((u0, u1), dt, cs, w["dsk_row"], s)
    yab = even_mix_fwd(proj_g, y2, w["norm_g"], w["sc_w"], rows)
    h = matmul(yab, w["w_out"], "nn", F32, "ev_out")
    x1 = res_ln_fwd(x, h, w["ln_g"], w["ln_b"], rows, "ev_ln")
    return x1, dict(x=x, xb=xb, proj=proj, proj_g=proj_g, pre=(pre0, pre1), u=(u0, u1), dt=dt, cs=cs, y2=y2,
                    hs=hs, yab=yab, h=h)


def even_layer_bwd(dx1, w, sv, rows):
    s = rows.s
    dres, dh, dln_g, dln_b = res_ln_bwd(sv["x"], sv["h"], w["ln_g"], w["ln_b"], dx1, rows, "ev_ln_bwd")
    dyab = matmul(dh, w["w_out"], "nt", MATMUL_DTYPE, "ev_out_dx")
    dw_out = matmul(sv["yab"], dh, "tn", F32, "ev_out_dw")
    dys, dz, dbg, dcg, dhh, dgate, dnorm_g, dsw = even_mix_bwd(sv["proj_g"], sv["y2"], w["norm_g"], w["sc_w"],
                                                               dyab, rows)
    dxs, dbc, ddt, dcs, ddsk = ssd_bwd(sv["u"], sv["dt"], sv["cs"], w["dsk_row"], sv["hs"], dys, s)
    ddtraw, dbias_row, dalog_row = ssd_pre_bwd(sv["proj"], w["bias_row"], w["alog_row"], ddt, dcs, s)
    (dxbc0, dxbc1), dconv_w, dconv_b = conv_bwd(sv["proj"], sv["pre"], (dxs, dbc), w["conv_w"], rows)
    pieces = [dxbc0, dxbc1, ddtraw, dz, dbg, dcg, dhh, dgate]
    dx0 = matmul_pieces_nt(pieces, w["w_in_p"], "ev_proj_dx", add=dres)
    dw0, dw1, dw_dt, dw_z, *dw_gates = [matmul(p, sv["xb"], "tn", F32, f"ev_proj_dw{i}")
                                        for i, p in enumerate(pieces)]
    dw_in = jnp.concatenate([dw_z, dw0, dw1, dw_dt[:2 * SSD_HEADS]] + dw_gates, axis=0)
    g = dict(ev_w_in=dw_in, ev_conv_w=dconv_w, ev_conv_b=dconv_b,
             ev_a_log=_head_unrow(dalog_row), ev_dt_bias=_head_unrow(dbias_row),
             ev_d_skip=_head_unrow(ddsk[0] + ddsk[1]), ev_norm_g=dnorm_g, ev_sc_conv_w=dsw[:3],
             ev_w_out=dw_out, ev_ln_g=dln_g, ev_ln_b=dln_b)
    return dx0, g


def even_weights(ev_w_in, ev_conv_w, ev_conv_b, ev_a_log, ev_dt_bias, ev_d_skip, ev_norm_g, ev_sc_conv_w,
                 ev_w_out, ev_ln_g, ev_ln_b):
    return dict(w_in_p=pad_even_w_in(ev_w_in).astype(MATMUL_DTYPE), conv_w=ev_conv_w, conv_b=ev_conv_b,
                alog_row=_head_row(ev_a_log), bias_row=_head_row(ev_dt_bias), dsk_row=_head_row(ev_d_skip),
                norm_g=ev_norm_g, sc_w=ev_sc_conv_w, w_out=ev_w_out.astype(MATMUL_DTYPE), ln_g=ev_ln_g,
                ln_b=ev_ln_b)


HEAD_BLK = LANES
ROPE_LO = MLA_NOPE
ROPE_HALF = MLA_ROPE // 2


def _rms(v, g):
    return v * lax.rsqrt(jnp.mean(v * v, axis=-1, keepdims=True) + EPS) * g


def latent_norm_fwd(proj, gq, gkv, rows):
    def fn(cq, ckv, gq_, gkv_):
        return _rms(cq, gq_), _rms(ckv, gkv_)

    return rows_call("od_norm_fwd", rows, fn, [proj, proj, gq, gkv],
                     [rows.tile(MLA_Q_RANK, O_CQ), rows.tile(MLA_KV_RANK, O_CKV), Rows.full((1, MLA_Q_RANK)),
                      Rows.full((1, MLA_KV_RANK))],
                     [(MLA_Q_RANK, MATMUL_DTYPE), (MLA_KV_RANK, MATMUL_DTYPE)])


def latent_norm_bwd(proj, gq, gkv, dcqn, dckvn, rows):
    def fn(cq, ckv, gq_, gkv_, d1, d2):
        _, vjp = jax.vjp(_rms, cq, gq_)
        dcq, dgq = vjp(d1)
        _, vjp2 = jax.vjp(_rms, ckv, gkv_)
        dckv, dgkv = vjp2(d2)
        return dcq, dckv, dgq, dgkv

    return rows_call("od_norm_bwd", rows, fn, [proj, proj, gq, gkv, dcqn, dckvn],
                     [rows.tile(MLA_Q_RANK, O_CQ), rows.tile(MLA_KV_RANK, O_CKV), Rows.full((1, MLA_Q_RANK)),
                      Rows.full((1, MLA_KV_RANK)), rows.tile(MLA_Q_RANK), rows.tile(MLA_KV_RANK)],
                     [(MLA_Q_RANK, F32), (MLA_KV_RANK, F32)], [(1, MLA_Q_RANK), (1, MLA_KV_RANK)])


def rope_rows():
    lane = np.arange(LANES)
    inv = ROPE_THETA ** (-jnp.arange(ROPE_HALF, dtype=F32) / ROPE_HALF)
    on = (lane >= ROPE_LO) & (lane < ROPE_LO + MLA_ROPE)
    freq = jnp.where(on, inv[(lane - ROPE_LO) % ROPE_HALF], 0.0).reshape(1, LANES).astype(F32)
    sign = np.where(on, np.where(lane < ROPE_LO + ROPE_HALF, -1.0, 1.0), 0.0).reshape(1, LANES).astype(np.float32)
    return freq, jnp.asarray(sign)


def _rot_tables(pos, freq, sign):
    ang = pos.astype(F32) * freq
    return jnp.cos(ang), jnp.sin(ang) * sign


def _swap_halves(v):
    lane = lax.broadcasted_iota(jnp.int32, (1, LANES), 1)
    return jnp.where(lane < ROPE_LO + ROPE_HALF, pltpu.roll(v, LANES - ROPE_HALF, 1), pltpu.roll(v, ROPE_HALF, 1))


def rope_fwd(qp, kvp, proj, pos, freq, sign, rows):
    def fn(q, k, kr, v, p, f, sg):
        c, sn = _rot_tables(p, f, sg)
        rk = kr * c + _swap_halves(kr) * sn
        one = (lax.broadcasted_iota(jnp.int32, (v.shape[0], HEAD_BLK - MLA_V), 1) == 0).astype(F32)
        qs, ks, vs = [], [], []
        for h in range(MLA_HEADS):
            qh = q[:, h * HEAD_BLK:(h + 1) * HEAD_BLK]
            qs.append((qh * c + _swap_halves(qh) * sn) * QSCALE)
            ks.append(k[:, h * HEAD_BLK:(h + 1) * HEAD_BLK] + rk)
            vs += [v[:, h * MLA_V:(h + 1) * MLA_V], one]
        return jnp.concatenate(qs, axis=1), jnp.concatenate(ks, axis=1), jnp.concatenate(vs, axis=1)

    w = MLA_HEADS * HEAD_BLK
    return rows_call("od_rope_fwd", rows, fn, [qp, kvp, proj, kvp, pos, freq, sign],
                     [rows.tile(w), rows.tile(w, 0), rows.tile(LANES, O_KR), rows.tile(MLA_WIDTH, w),
                      rows.tile(1), Rows.full((1, LANES)), Rows.full((1, LANES))],
                     [(w, MATMUL_DTYPE), (w, MATMUL_DTYPE), (w, MATMUL_DTYPE)])


def rope_bwd(dq, dk, dv, pos, freq, sign, rows):
    def fn(dq_, dk_, dv_, p, f, sg):
        c, sn = _rot_tables(p, f, sg)
        on = jnp.abs(sg)
        outs, dks, dvs, dkr = [], [], [], jnp.zeros((dq_.shape[0], LANES), F32)
        for h in range(MLA_HEADS):
            g = dq_[:, h * HEAD_BLK:(h + 1) * HEAD_BLK] * ATTN_SCALE
            outs.append(g * c + _swap_halves(g * sn) * on)
            gk = dk_[:, h * HEAD_BLK:(h + 1) * HEAD_BLK] * LN2
            dks.append(gk)
            dkr = dkr + gk * c + _swap_halves(gk * sn) * on
            dvs.append(dv_[:, h * HEAD_BLK:h * HEAD_BLK + MLA_V])
        return jnp.concatenate(outs, axis=1), jnp.concatenate(dks + dvs, axis=1), dkr

    w = MLA_HEADS * HEAD_BLK
    return rows_call("od_rope_bwd", rows, fn, [dq, dk, dv, pos, freq, sign],
                     [rows.tile(w), rows.tile(w), rows.tile(w), rows.tile(1), Rows.full((1, LANES)),
                      Rows.full((1, LANES))],
                     [(w, MATMUL_DTYPE), (w + MLA_WIDTH, MATMUL_DTYPE), (LANES, F32)])


_PAIRS = MLA_HEADS // 2
ATT_TQ = 512
ATT_TK = 4096
ATT_BWD_TQ = 2048
ATT_BWD_TK = 1024
ATT_BWD_VMEM = 62 * 1024 * 1024


def _att_tiles(s, backward=False):
    if backward:
        return min(ATT_BWD_TQ, s), min(ATT_BWD_TK, s)
    return min(ATT_TQ, s), min(ATT_TK, s)


def attention_fwd(qcat, kcat, vcat, s):
    tq, tk = _att_tiles(s)
    nq, nk = s // tq, s // tk

    def body(q_ref, k_ref, v_ref, o_ref, lse_ref, p_ref, mt_ref, m_ref, acc_ref):
        kk = pl.program_id(2)
        half = lax.broadcasted_iota(jnp.int32, (1, LANES), 1) < MLA_V

        @pl.when(kk == 0)
        def _():
            m_ref[...] = jnp.full(m_ref.shape, -jnp.inf, F32)
            acc_ref[...] = jnp.zeros(acc_ref.shape, F32)

        sl = [slice(hh * HEAD_BLK, (hh + 1) * HEAD_BLK) for hh in range(2)]
        sc = [_mm(q_ref[:, sl[hh]], k_ref[:, sl[hh]], _NT) for hh in range(2)]
        for hh in range(2):
            m_prev = m_ref[hh]
            m_new = jnp.maximum(m_prev, jnp.max(sc[hh], axis=1, keepdims=True))
            p = jnp.exp2(sc[hh] - m_new[:, :1]).astype(p_ref.dtype)
            p_ref[hh] = p
            acc_ref[hh] = acc_ref[hh] * jnp.exp2(m_prev - m_new) + _mm(p, v_ref[:, sl[hh]])
            m_ref[hh] = m_new
        mt_ref[...] = jnp.where(half, m_ref[0], m_ref[1])

        @pl.when(kk == nk - 1)
        def _():
            l0, l1 = acc_ref[0][:, MLA_V:MLA_V + 1], acc_ref[1][:, MLA_V:MLA_V + 1]
            o_ref[...] = jnp.where(half, acc_ref[0] / l0, pltpu.roll(acc_ref[1] / l1, MLA_V, 1))
            lse_ref[...] = jnp.where(half, m_ref[0] + jnp.log2(l0), m_ref[1] + jnp.log2(l1))

    return pl.pallas_call(
        body, name="od_attn_fwd", grid=(_PAIRS, nq, nk),
        in_specs=[pl.BlockSpec((tq, 2 * HEAD_BLK), lambda p, i, kk: (i, p)),
                  pl.BlockSpec((tk, 2 * HEAD_BLK), lambda p, i, kk: (kk, p)),
                  pl.BlockSpec((tk, 2 * HEAD_BLK), lambda p, i, kk: (kk, p))],
        out_specs=[pl.BlockSpec((tq, LANES), lambda p, i, kk: (i, p)),
                   pl.BlockSpec((None, tq, LANES), lambda p, i, kk: (p, i, 0)),
                   pl.BlockSpec((2, tq, tk), lambda p, i, kk: (p, i, kk)),
                   pl.BlockSpec((None, None, tq, LANES), lambda p, i, kk: (p, kk, i, 0))],
        out_shape=[jax.ShapeDtypeStruct((s, MLA_WIDTH), F32), jax.ShapeDtypeStruct((_PAIRS, s, LANES), F32),
                   jax.ShapeDtypeStruct((MLA_HEADS, s, s), MATMUL_DTYPE),
                   jax.ShapeDtypeStruct((_PAIRS, nk, s, LANES), F32)],
        scratch_shapes=[pltpu.VMEM((2, tq, LANES), F32)] * 2,
        compiler_params=_params(("parallel", "parallel", "arbitrary")),
    )(qcat, kcat, vcat)


def attention_bwd(qcat, kcat, vcat, o, lse, pst, mt, do, s):
    tq, tk = _att_tiles(s, backward=True)
    nq, nk = s // tq, s // tk
    per_fwd_tile = _att_tiles(s)[1] // tk
    assert per_fwd_tile * tk == _att_tiles(s)[1]

    def body(q_ref, k_ref, v_ref, do_ref, o_ref, lse_ref, p_ref, mt_ref, dq_ref, dk_ref, dv_ref):
        kk, i = pl.program_id(1), pl.program_id(2)
        lane = lax.broadcasted_iota(jnp.int32, (1, LANES), 1)
        half = lane < MLA_V
        c = jnp.exp2(mt_ref[...] - lse_ref[...])
        do_p = do_ref[...] * c
        prod = do_p * o_ref[...]
        rows_i = pl.ds(pl.multiple_of(i * tq, tq), tq)
        for hh in range(2):
            sl = slice(hh * HEAD_BLK, (hh + 1) * HEAD_BLK)
            mine = half if hh == 0 else jnp.logical_not(half)
            delta = jnp.sum(jnp.where(mine, prod, 0.0), axis=1, keepdims=True)
            do_h = jnp.where(half, do_p if hh == 0 else pltpu.roll(do_p, MLA_V, 1), 0.0)
            p = p_ref[hh]
            ds = p.astype(F32) * (_mm(do_h, v_ref[:, sl], _NT) - delta)
            dv_h, dk_h, dq_h = _mm(p, do_h, _TN), _mm(ds, q_ref[:, sl], _TN), _mm(ds, k_ref[:, sl])

            @pl.when(i == 0)
            def _():
                dv_ref[:, sl] = dv_h
                dk_ref[:, sl] = dk_h

            @pl.when(i != 0)
            def _():
                dv_ref[:, sl] += dv_h
                dk_ref[:, sl] += dk_h

            @pl.when(kk == 0)
            def _():
                dq_ref[rows_i, sl] = dq_h

            @pl.when(kk != 0)
            def _():
                dq_ref[rows_i, sl] += dq_h

    w = MLA_HEADS * HEAD_BLK
    return pl.pallas_call(
        body, name="od_attn_bwd", grid=(_PAIRS, nk, nq),
        in_specs=[pl.BlockSpec((tq, 2 * HEAD_BLK), lambda p, kk, i: (i, p)),
                  pl.BlockSpec((tk, 2 * HEAD_BLK), lambda p, kk, i: (kk, p)),
                  pl.BlockSpec((tk, 2 * HEAD_BLK), lambda p, kk, i: (kk, p)),
                  pl.BlockSpec((tq, LANES), lambda p, kk, i: (i, p)),
                  pl.BlockSpec((tq, LANES), lambda p, kk, i: (i, p)),
                  pl.BlockSpec((None, tq, LANES), lambda p, kk, i: (p, i, 0)),
                  pl.BlockSpec((2, tq, tk), lambda p, kk, i: (p, i, kk)),
                  pl.BlockSpec((None, None, tq, LANES), lambda p, kk, i: (p, kk // per_fwd_tile, i, 0))],
        out_specs=[pl.BlockSpec((s, 2 * HEAD_BLK), lambda p, kk, i: (0, p), pipeline_mode=pl.Buffered(1)),
                   pl.BlockSpec((tk, 2 * HEAD_BLK), lambda p, kk, i: (kk, p)),
                   pl.BlockSpec((tk, 2 * HEAD_BLK), lambda p, kk, i: (kk, p))],
        out_shape=[jax.ShapeDtypeStruct((s, w), F32)] * 3,
        compiler_params=_params(("parallel", "arbitrary", "arbitrary"), vmem=ATT_BWD_VMEM),
    )(qcat, kcat, vcat, do, o, lse, pst, mt)


def _pool_counts(n_rows, first_row, s, w):
    pos = first_row + lax.broadcasted_iota(jnp.int32, (n_rows, 1), 0)
    lo = jnp.clip(pos - w // 2, 0, s)
    hi = jnp.clip(pos + w - w // 2, 0, s)
    return jnp.maximum(hi - lo, 1).astype(F32)


def _window_sum(e, levels, mirrored):
    n = e.shape[0]
    acc = e + pltpu.roll(e, (n - 1) if mirrored else 1, 0)
    step = 1
    for _ in range(levels - 1):
        acc = pltpu.roll(acc, step, 0) + pltpu.roll(acc, n - step, 0)
        step *= 2
    return acc


def _pooled(ue, s, t):
    first = pl.program_id(0) * t
    outs = []
    for gi, w in enumerate(POOL_WINDOWS):
        eg = ue[:, gi * POOL_GROUP:(gi + 1) * POOL_GROUP]
        sm = _window_sum(eg, gi + 1, False)[HALO:HALO + t]
        outs.append(sm / _pool_counts(t, first, s, w) - eg[HALO:HALO + t])
    return outs


def odd_mix_fwd(proj, o, pool_w, pool_scale, rows):
    def fn(o_, gc, up, u, un, gd, pw, ps):
        pooled = _pooled(_ext(up, u, un, rows.n), rows.s, rows.t)
        lin = jnp.concatenate([_mm(pooled[g], pw[g]) for g in range(len(POOL_WINDOWS))], axis=1)
        return jnp.concatenate([o_ * _silu(gc), lin * ps * _silu(gd)], axis=1)

    w = POOL_WIDTH
    return rows_call("od_mix_fwd", rows, fn, [o, proj, proj, proj, proj, proj, pool_w, pool_scale],
                     [rows.tile(w), rows.tile(w, O_GC)] + rows.halo(w, O_UD)
                     + [rows.tile(w, O_GD), Rows.full((4, POOL_GROUP, POOL_GROUP)), Rows.full((1, w))],
                     [(2 * w, MATMUL_DTYPE)])[0]


def odd_mix_bwd(proj, o, pool_w, pool_scale, dycd, rows):
    w = POOL_WIDTH
    ng = len(POOL_WINDOWS)

    def fn(o_, gc, up, u, un, gdp, gd, gdn, pw, ps, dyc, dydp, dyd, dydn):
        n, t, s = rows.n, rows.t, rows.s
        dyc = dyc.astype(F32)
        do = dyc * _silu(gc)
        dgc = dyc * o_ * _dsilu(gc)
        pooled = _pooled(_ext(up, u, un, n), s, t)
        lin = jnp.concatenate([_mm(pooled[g], pw[g]) for g in range(ng)], axis=1)
        dydc = dyd.astype(F32)
        dgd = dydc * lin * ps * _dsilu(gd)
        dps = _rowsum(dydc * lin * _silu(gd))
        dlin_e = _ext(dydp, dyd, dydn, n) * ps * _silu(_ext(gdp, gd, gdn, n))
        first = pl.program_id(0) * t - HALO
        dus, dpws = [], []
        for g, win in enumerate(POOL_WINDOWS):
            sl = slice(g * POOL_GROUP, (g + 1) * POOL_GROUP)
            dle = dlin_e[:, sl]
            dpws.append(_mm(pooled[g], dle[HALO:HALO + t], _TN))
            dpe = _mm(dle, pw[g], _NT)
            gce = dpe / _pool_counts(t + 2 * HALO, first, s, win)
            dus.append(_window_sum(gce, g + 1, True)[HALO:HALO + t] - dpe[HALO:HALO + t])
        return do, dgc, jnp.concatenate(dus, axis=1), dgd, jnp.stack(dpws), dps

    return rows_call("od_mix_bwd", rows, fn,
                     [o, proj, proj, proj, proj, proj, proj, proj, pool_w, pool_scale, dycd, dycd, dycd, dycd],
                     [rows.tile(w), rows.tile(w, O_GC)] + rows.halo(w, O_UD) + rows.halo(w, O_GD)
                     + [Rows.full((ng, POOL_GROUP, POOL_GROUP)), Rows.full((1, w)), rows.tile(w, 0)]
                     + rows.halo(w, w),
                     [(w, F32)] * 4, [(ng, POOL_GROUP, POOL_GROUP), (1, w)])


def pad_odd_w_in(w):
    z = lambda n: jnp.zeros((w.shape[0], n), w.dtype)
    return jnp.concatenate([w[:, :384], z(ROPE_LO), w[:, 384:416], z(LANES - ROPE_LO - MLA_ROPE), w[:, 416:]], axis=1)


def unpad_odd_w_in_t(wpt):
    return jnp.concatenate([wpt[:384], wpt[O_KR + ROPE_LO:O_KR + ROPE_LO + MLA_ROPE], wpt[O_GC:]], axis=0)


def pad_w_uq(w):
    w3 = w.reshape(MLA_Q_RANK, MLA_HEADS, MLA_NOPE + MLA_ROPE)
    w3 = jnp.pad(w3, ((0, 0), (0, 0), (0, HEAD_BLK - MLA_NOPE - MLA_ROPE)))
    return w3.reshape(MLA_Q_RANK, MLA_HEADS * HEAD_BLK)


def unpad_w_uq(wp):
    return wp.reshape(MLA_Q_RANK, MLA_HEADS, HEAD_BLK)[..., :MLA_NOPE + MLA_ROPE].reshape(MLA_Q_RANK, -1)


def pad_w_ukv(w):
    w3 = w.reshape(MLA_KV_RANK, MLA_HEADS, MLA_NOPE + MLA_V)
    kp = jnp.pad(w3[..., :MLA_NOPE], ((0, 0), (0, 0), (0, HEAD_BLK - MLA_NOPE)))
    return jnp.concatenate([kp.reshape(MLA_KV_RANK, -1), w3[..., MLA_NOPE:].reshape(MLA_KV_RANK, -1)], axis=1)


def unpad_w_ukv(wp):
    kp = wp[:, :MLA_HEADS * HEAD_BLK].reshape(MLA_KV_RANK, MLA_HEADS, HEAD_BLK)[..., :MLA_NOPE]
    vp = wp[:, MLA_HEADS * HEAD_BLK:].reshape(MLA_KV_RANK, MLA_HEADS, MLA_V)
    return jnp.concatenate([kp, vp], axis=-1).reshape(MLA_KV_RANK, -1)


def odd_weights(od_w_in, od_q_norm_g, od_w_uq, od_kv_norm_g, od_w_ukv, od_pool_w, od_pool_scale, od_w_out,
                od_ln_g, od_ln_b):
    freq, sign = rope_rows()
    return dict(w_in_p=pad_odd_w_in(od_w_in).astype(MATMUL_DTYPE), gq=od_q_norm_g, gkv=od_kv_norm_g,
                w_uq_p=pad_w_uq(od_w_uq).astype(MATMUL_DTYPE), w_ukv_p=pad_w_ukv(od_w_ukv).astype(MATMUL_DTYPE),
                pool_w=od_pool_w, pool_scale=od_pool_scale, w_out=od_w_out.astype(MATMUL_DTYPE), ln_g=od_ln_g,
                ln_b=od_ln_b, freq=freq, sign=sign)


def odd_layer_loss(x, pos, target, w, rows):
    s = rows.s
    proj = matmul(x, w["w_in_p"], "nn", F32, "od_proj")
    cqn, ckvn = latent_norm_fwd(proj, w["gq"], w["gkv"], rows)
    qp = matmul(cqn, w["w_uq_p"], "nn", F32, "od_q_up")
    kvp = matmul(ckvn, w["w_ukv_p"], "nn", F32, "od_kv_up")
    qcat, kcat, v = rope_fwd(qp, kvp, proj, pos, w["freq"], w["sign"], rows)
    o, lse, pst, mt = attention_fwd(qcat, kcat, v, s)
    ycd = odd_mix_fwd(proj, o, w["pool_w"], w["pool_scale"], rows)
    h = matmul(ycd, w["w_out"], "nn", F32, "od_out")
    dres, dh, dln_g, dln_b, loss_lanes = final_ln_loss(x, h, w["ln_g"], w["ln_b"], target, rows)
    dycd = matmul(dh, w["w_out"], "nt", F32, "od_out_dx")
    dw_out = matmul(ycd, dh, "tn", F32, "od_out_dw")
    do, dgc, dud, dgd, dpool_w, dpool_scale = odd_mix_bwd(proj, o, w["pool_w"], w["pool_scale"], dycd, rows)
    dq, dk, dv = attention_bwd(qcat, kcat, v, o, lse, pst, mt, do, s)
    dqp, dkvp, dkr = rope_bwd(dq, dk, dv, pos, w["freq"], w["sign"], rows)
    dcqn = matmul(dqp, w["w_uq_p"], "nt", F32, "od_q_up_dx")
    dw_uq = unpad_w_uq(matmul(cqn, dqp, "tn", F32, "od_q_up_dw"))
    dckvn = matmul(dkvp, w["w_ukv_p"], "nt", F32, "od_kv_up_dx")
    dw_ukv = unpad_w_ukv(matmul(ckvn, dkvp, "tn", F32, "od_kv_up_dw"))
    dcq, dckv, dgq, dgkv = latent_norm_bwd(proj, w["gq"], w["gkv"], dcqn, dckvn, rows)
    dproj = jnp.concatenate([dcq, dckv, dkr, dgc, dud, dgd], axis=1).astype(MATMUL_DTYPE)
    dx = matmul(dproj, w["w_in_p"], "nt", F32, "od_proj_dx", add=dres)
    dw_in = unpad_odd_w_in_t(matmul(dproj, x, "tn", F32, "od_proj_dw"))
    g = dict(od_w_in=dw_in, od_q_norm_g=dgq, od_w_uq=dw_uq, od_kv_norm_g=dgkv, od_w_ukv=dw_ukv,
             od_pool_w=dpool_w, od_pool_scale=dpool_scale, od_w_out=dw_out, od_ln_g=dln_g, od_ln_b=dln_b)
    return loss_lanes, dx, g


_MESH = pl.DeviceIdType.MESH
_ANY = pl.BlockSpec(memory_space=pl.ANY)
N_CHIPS = 4


def _push_call(name, ins, out_shapes, plan, n_remote, n_local):
    n_in, n_out = len(ins), len(out_shapes)

    def body(*refs):
        in_refs, out_refs = refs[:n_in], refs[n_in:n_in + n_out]
        send_sems, recv_sems, local_sems = refs[n_in + n_out:]
        x, y, c = lax.axis_index("x"), lax.axis_index("y"), lax.axis_index("c")
        remote, local = plan(in_refs, out_refs, x, y, c)
        assert len(remote) == n_remote and len(local) == n_local
        sends = [pltpu.make_async_remote_copy(src_ref=s, dst_ref=d, send_sem=send_sems.at[k], recv_sem=recv_sems.at[k],
                                              device_id=dev, device_id_type=_MESH)
                 for k, (s, d, dev, _) in enumerate(remote)]
        recvs = [pltpu.make_async_remote_copy(src_ref=s, dst_ref=land, send_sem=send_sems.at[k],
                                              recv_sem=recv_sems.at[k], device_id=dev, device_id_type=_MESH)
                 for k, (s, _, dev, land) in enumerate(remote)]
        locs = [pltpu.make_async_copy(s, d, local_sems.at[k]) for k, (s, d) in enumerate(local)]
        for cp in sends + locs:
            cp.start()
        for cp in recvs:
            cp.wait_recv()
        for cp in sends:
            cp.wait_send()
        for cp in locs:
            cp.wait()

    return pl.pallas_call(
        body, name=name, in_specs=[_ANY] * n_in, out_specs=[_ANY] * n_out, out_shape=list(out_shapes),
        scratch_shapes=[pltpu.SemaphoreType.DMA((n_remote,)), pltpu.SemaphoreType.DMA((n_remote,)),
                        pltpu.SemaphoreType.DMA((max(n_local, 1),))],
    )(*ins)


def _other_chips(x, y):
    return [(1 - x, y), (x, 1 - y), (1 - x, 1 - y)]


def chips_allgather(bufs):
    def plan(in_refs, out_refs, x, y, c):
        me = 2 * x + y
        remote, local = [], []
        for src, out in zip(in_refs, out_refs):
            for (px, py) in _other_chips(x, y):
                remote.append((src, out.at[me], (px, py, c), out.at[2 * px + py]))
            local.append((src, out.at[me]))
        return remote, local

    shapes = [jax.ShapeDtypeStruct((N_CHIPS,) + b.shape, b.dtype) for b in bufs]
    return _push_call("weights_allgather", bufs, shapes, plan, 3 * len(bufs), len(bufs))


def sibling_send(buf, name):
    def plan(in_refs, out_refs, x, y, c):
        return [(in_refs[0], out_refs[0], (x, y, 1 - c), out_refs[0])], []

    return _push_call(name, [buf], [jax.ShapeDtypeStruct(buf.shape, buf.dtype)], plan, 1, 0)[0]


def chips_scatter(buf):
    def plan(in_refs, out_refs, x, y, c):
        me = 2 * x + y
        src, out = in_refs[0], out_refs[0]
        remote = [(src.at[2 * px + py], out.at[me], (px, py, c), out.at[2 * px + py]) for (px, py) in _other_chips(x, y)]
        return remote, [(src.at[me], out.at[me])]

    return _push_call("grads_scatter", [buf], [jax.ShapeDtypeStruct(buf.shape, buf.dtype)], plan, 3, 1)[0]


PACK_W = 1024
PACK_BLK = 128


def _ew_call(name, fn, ins, in_specs, out_shape, out_spec, n_out, steps):
    def body(*refs):
        res = fn(*[r[...] for r in refs[:len(ins)]])
        if not isinstance(res, (tuple, list)):
            res = (res,)
        for r, v in zip(refs[len(ins):], res):
            r[...] = v

    return pl.pallas_call(
        body, name=name, grid=(steps,), in_specs=in_specs, out_specs=[out_spec] * n_out,
        out_shape=[jax.ShapeDtypeStruct(out_shape, F32)] * n_out,
        compiler_params=_params(("parallel",)),
    )(*ins)


def add2(a, b, name):
    n, r, w = a.shape
    blk = pl.BlockSpec((PACK_BLK, w), lambda i: (i, 0))
    out = _ew_call(name, lambda u, v: u + v, [a.reshape(n * r, w), b.reshape(n * r, w)], [blk, blk], (n * r, w), blk,
                   1, n * r // PACK_BLK)[0]
    return out.reshape(a.shape)


def sum_chips(buf):
    _, r, w = buf.shape
    specs = [pl.BlockSpec((None, PACK_BLK, w), lambda i, q=q: (q, i, 0)) for q in range(N_CHIPS)]
    return _ew_call("grads_sum", lambda a, b, c, d: ((a + b) + c) + d, [buf] * N_CHIPS, specs, (r, w),
                    pl.BlockSpec((PACK_BLK, w), lambda i: (i, 0)), 1, r // PACK_BLK)[0]


def adamw(w, g, m, v, name):
    def fn(w_, g_, m_, v_):
        m2 = ADAM_B1 * m_ + (1.0 - ADAM_B1) * g_
        v2 = ADAM_B2 * v_ + (1.0 - ADAM_B2) * jnp.square(g_)
        m_hat = m2 / (1.0 - ADAM_B1 ** ADAM_STEP)
        v_hat = v2 / (1.0 - ADAM_B2 ** ADAM_STEP)
        return -ADAM_LR * (m_hat / (jnp.sqrt(v_hat) + ADAM_EPS) + ADAM_WD * w_), m2, v2

    r, c = w.shape
    br = r if r <= 512 else _pick(r, (256, 128, 64, 32, 16, 8))
    blk = pl.BlockSpec((br, c), lambda i: (i, 0))
    return _ew_call(name, fn, [w, g, m, v], [blk] * 4, (r, c), blk, 3, r // br)


WEIGHTS = (
    ("ev_w_in", (1, 1024, 7200), 2), ("ev_conv_w", (1, 4, 2048), 2), ("ev_conv_b", (1, 2048), None),
    ("ev_a_log", (1, 2, 16), None), ("ev_dt_bias", (1, 2, 16), None), ("ev_d_skip", (1, 2, 16), None),
    ("ev_norm_g", (1, 1024), None), ("ev_sc_conv_w", (1, 3, 1024), 2), ("ev_w_out", (1, 2048, 1024), 1),
    ("ev_ln_g", (1, 1024), None), ("ev_ln_b", (1, 1024), None), ("od_w_in", (1, 1024, 1952), 2),
    ("od_q_norm_g", (1, 256), 1), ("od_w_uq", (1, 256, 768), 2), ("od_kv_norm_g", (1, 128), None),
    ("od_w_ukv", (1, 128, 1024), 2), ("od_pool_w", (1, 4, 128, 128), None), ("od_pool_scale", (1, 512), 1),
    ("od_w_out", (1, 1024, 1024), 1), ("od_ln_g", (1, 1024), 1), ("od_ln_b", (1, 1024), 1),
)
BIG = ("ev_w_in", "ev_w_out", "od_w_in", "od_w_uq", "od_w_ukv", "od_w_out")


def _block_shape(shape, axis):
    if axis is None:
        return tuple(shape)
    return tuple(d // N_CHIPS if i == axis else d for i, d in enumerate(shape))


def _pack(arrs, quantum):
    flat = jnp.concatenate([a.reshape(-1) for a in arrs])
    n = flat.shape[0]
    padded = -(-n // quantum) * quantum
    return jnp.concatenate([flat, jnp.zeros((padded - n,), flat.dtype)]).reshape(-1, LANES)


def _unpack(flat, shapes):
    out, off = [], 0
    for sh in shapes:
        n = int(np.prod(sh))
        out.append(flat[off:off + n].reshape(sh))
        off += n
    return out


def gather_weights(local):
    sharded = [(n, sh, ax) for (n, sh, ax) in WEIGHTS if ax is not None]
    big = [(n, sh, ax) for (n, sh, ax) in sharded if n in BIG]
    small = [(n, sh, ax) for (n, sh, ax) in sharded if n not in BIG]
    pb = _pack([local[n].astype(MATMUL_DTYPE) for n, _, _ in big], 2 * SUBLANES * LANES)
    ps = _pack([local[n] for n, _, _ in small], SUBLANES * LANES)
    gb, gs = chips_allgather([pb, ps])
    full = {n: local[n] for (n, sh, ax) in WEIGHTS if ax is None}
    for group, g in ((big, gb), (small, gs)):
        parts = [_unpack(g[q].reshape(-1), [_block_shape(sh, ax) for _, sh, ax in group]) for q in range(N_CHIPS)]
        for i, (n, sh, ax) in enumerate(group):
            full[n] = jnp.concatenate([parts[q][i] for q in range(N_CHIPS)], axis=ax)
    return full


WIDE = (("ev_w_in", True), ("ev_w_out", False), ("od_w_in", True), ("od_w_out", False))


def reduce_and_update(grads, local_w, local_m, local_v):
    c = lax.axis_index("c")
    wide_names = [n for n, _ in WIDE]
    tail = [(n, sh, ax) for (n, sh, ax) in WEIGHTS if n not in wide_names]
    tail_blocks = [_block_shape(sh, ax) for _, sh, ax in tail]
    n_tail = sum(int(np.prod(b)) for b in tail_blocks)
    wide_rows = [grads[n].shape[0] // N_CHIPS for n in wide_names]
    quantum = 2 * PACK_BLK
    total = -(-(sum(wide_rows) + -(-n_tail // PACK_W)) // quantum) * quantum
    tail_rows = total - sum(wide_rows)

    def tail_pack(pieces):
        flat = jnp.concatenate([p.reshape(-1) for p in pieces] + [jnp.zeros((tail_rows * PACK_W - n_tail,), F32)])
        return flat.reshape(tail_rows, PACK_W)

    parts = [grads[n].reshape(N_CHIPS, r, PACK_W) for n, r in zip(wide_names, wide_rows)]
    cols = []
    for (n, sh, ax), bs in zip(tail, tail_blocks):
        g = grads[n].reshape(sh)
        if ax is None:
            cols.append(jnp.broadcast_to(g.reshape(1, -1), (N_CHIPS, g.size)))
        else:
            g = g.reshape(sh[:ax] + (N_CHIPS, bs[ax]) + sh[ax + 1:])
            cols.append(jnp.moveaxis(g, ax, 0).reshape(N_CHIPS, -1))
    cols.append(jnp.zeros((N_CHIPS, tail_rows * PACK_W - n_tail), F32))
    packs = jnp.concatenate(parts + [jnp.concatenate(cols, axis=1).reshape(N_CHIPS, tail_rows, PACK_W)], axis=1)
    rh = total // 2
    keep = lax.dynamic_slice_in_dim(packs, c * rh, rh, axis=1)
    give = lax.dynamic_slice_in_dim(packs, (1 - c) * rh, rh, axis=1)
    chip_half = add2(keep, sibling_send(give, "grads_to_sibling"), "grads_chip_sum")
    total_half = sum_chips(chips_scatter(chip_half))
    other_half = sibling_send(total_half, "grads_from_sibling")
    g_pack = jnp.concatenate([jnp.where(c == 0, total_half, other_half),
                              jnp.where(c == 0, other_half, total_half)], axis=0)
    outs = ({}, {}, {}, {})
    off = 0
    for (n, transposed), r in zip(WIDE, wide_rows):
        g = g_pack[off:off + r]
        off += r
        g = g.T if transposed else g
        shape = local_w[n].shape
        res = adamw(local_w[n].reshape(g.shape), g, local_m[n].reshape(g.shape), local_v[n].reshape(g.shape),
                    "adamw_" + n)
        for d, a in zip(outs, (g, *res)):
            d[n] = a.reshape(shape)
    g_tail = g_pack[off:]
    res = adamw(*[tail_pack([d[n] for n, _, _ in tail]) if d is not None else g_tail
                  for d in (local_w, None, local_m, local_v)], "adamw_small")
    for d, a in zip(outs, (g_tail, *res)):
        d.update(zip([n for n, _, _ in tail], _unpack(a.reshape(-1), tail_blocks)))
    return outs


ROW_TILE = 256


def kernel(x, positions, ev_w_in, ev_conv_w, ev_conv_b, ev_a_log, ev_dt_bias, ev_d_skip, ev_norm_g, ev_sc_conv_w, ev_w_out, ev_ln_g, ev_ln_b, od_w_in, od_q_norm_g, od_w_uq, od_kv_norm_g, od_w_ukv, od_pool_w, od_pool_scale, od_w_out, od_ln_g, od_ln_b, loss_target, m_ev_w_in, m_ev_conv_w, m_ev_conv_b, m_ev_a_log, m_ev_dt_bias, m_ev_d_skip, m_ev_norm_g, m_ev_sc_conv_w, m_ev_w_out, m_ev_ln_g, m_ev_ln_b, m_od_w_in, m_od_q_norm_g, m_od_w_uq, m_od_kv_norm_g, m_od_w_ukv, m_od_pool_w, m_od_pool_scale, m_od_w_out, m_od_ln_g, m_od_ln_b, v_ev_w_in, v_ev_conv_w, v_ev_conv_b, v_ev_a_log, v_ev_dt_bias, v_ev_d_skip, v_ev_norm_g, v_ev_sc_conv_w, v_ev_w_out, v_ev_ln_g, v_ev_ln_b, v_od_w_in, v_od_q_norm_g, v_od_w_uq, v_od_kv_norm_g, v_od_w_ukv, v_od_pool_w, v_od_pool_scale, v_od_w_out, v_od_ln_g, v_od_ln_b):
    names = [n for n, _, _ in WEIGHTS]
    local_w = dict(zip(names, (ev_w_in, ev_conv_w, ev_conv_b, ev_a_log, ev_dt_bias, ev_d_skip, ev_norm_g, ev_sc_conv_w, ev_w_out, ev_ln_g, ev_ln_b, od_w_in, od_q_norm_g, od_w_uq, od_kv_norm_g, od_w_ukv, od_pool_w, od_pool_scale, od_w_out, od_ln_g, od_ln_b)))
    local_m = dict(zip(names, (m_ev_w_in, m_ev_conv_w, m_ev_conv_b, m_ev_a_log, m_ev_dt_bias, m_ev_d_skip, m_ev_norm_g, m_ev_sc_conv_w, m_ev_w_out, m_ev_ln_g, m_ev_ln_b, m_od_w_in, m_od_q_norm_g, m_od_w_uq, m_od_kv_norm_g, m_od_w_ukv, m_od_pool_w, m_od_pool_scale, m_od_w_out, m_od_ln_g, m_od_ln_b)))
    local_v = dict(zip(names, (v_ev_w_in, v_ev_conv_w, v_ev_conv_b, v_ev_a_log, v_ev_dt_bias, v_ev_d_skip, v_ev_norm_g, v_ev_sc_conv_w, v_ev_w_out, v_ev_ln_g, v_ev_ln_b, v_od_w_in, v_od_q_norm_g, v_od_w_uq, v_od_kv_norm_g, v_od_w_ukv, v_od_pool_w, v_od_pool_scale, v_od_w_out, v_od_ln_g, v_od_ln_b)))
    s = x.shape[1]
    rows = Rows(s, min(ROW_TILE, s))
    f = gather_weights(local_w)
    ew = even_weights(f["ev_w_in"][0], f["ev_conv_w"][0], f["ev_conv_b"], f["ev_a_log"][0], f["ev_dt_bias"][0],
                      f["ev_d_skip"][0], f["ev_norm_g"], f["ev_sc_conv_w"][0], f["ev_w_out"][0], f["ev_ln_g"],
                      f["ev_ln_b"])
    ow = odd_weights(f["od_w_in"][0], f["od_q_norm_g"], f["od_w_uq"][0], f["od_kv_norm_g"], f["od_w_ukv"][0],
                     f["od_pool_w"][0], f["od_pool_scale"], f["od_w_out"][0], f["od_ln_g"], f["od_ln_b"])
    x1, saved = even_layer(x[0], ew, rows)
    loss_lanes, dx1, g_odd = odd_layer_loss(x1, positions.reshape(s, 1), loss_target[0], ow, rows)
    dx0, g_even = even_layer_bwd(dx1, ew, saved, rows)
    loss = lax.psum(jnp.sum(loss_lanes), ("x", "y", "c"))
    grad, delta, new_m, new_v = reduce_and_update({**g_even, **g_odd}, local_w, local_m, local_v)
    return (loss, dx0[None], *[grad[n] for n in names], *[delta[n] for n in names],
            *[new_m[n] for n in names], *[new_v[n] for n in names])
```

```python
import functools
import math

import jax
import jax.numpy as jnp
import numpy as np
from jax import lax
from jax.experimental import pallas as pl
from jax.experimental.pallas import tpu as pltpu

F32 = jnp.float32
BF16 = jnp.bfloat16
MATMUL_DTYPE = jnp.bfloat16

D_MODEL = 1024
DEPTH = 2
SSD_HEADS, SSD_HEAD_DIM, SSD_GROUPS, SSD_STATE, SSD_CHUNK = 16, 64, 4, 128, 128
SSD_INNER = SSD_HEADS * SSD_HEAD_DIM
SSD_XBC = SSD_INNER + 2 * SSD_GROUPS * SSD_STATE
SC_WIDTH = 1024
MLA_HEADS, MLA_Q_RANK, MLA_KV_RANK, MLA_NOPE, MLA_ROPE, MLA_V = 8, 256, 128, 64, 32, 64
MLA_WIDTH = MLA_HEADS * MLA_V
ROPE_THETA = 10000.0
ATTN_SCALE = (MLA_NOPE + MLA_ROPE) ** -0.5
QSCALE = ATTN_SCALE * math.log2(math.e)
LN2 = math.log(2.0)
POOL_WINDOWS = (2, 4, 8, 16)
POOL_GROUP = 128
POOL_WIDTH = POOL_GROUP * len(POOL_WINDOWS)
EPS = 1e-5
ALPHA = (2 * DEPTH) ** 0.25
EVEN_PROJ, ODD_PROJ = 7200, 1952
ADAM_LR, ADAM_B1, ADAM_B2, ADAM_EPS, ADAM_WD, ADAM_STEP = 0.001, 0.9, 0.999, 1e-08, 0.01, 10

LANES = 128
SUBLANES = 8
HALO = SUBLANES
VMEM_LIMIT = 56 * 1024 * 1024

EVEN_P = 7296
EVEN_F = 2176
G_Z, G_BG, G_CG, G_H, G_GATE = 0, 1024, 2048, 3072, 4096
ODD_P = 2048
E_XBC, E_DT = 0, 2048
O_CQ, O_CKV, O_KR, O_GC, O_UD, O_GD = 0, 256, 384, 512, 1024, 1536


def _params(sem=None, vmem=VMEM_LIMIT):
    return pltpu.CompilerParams(dimension_semantics=sem, vmem_limit_bytes=vmem)


def _mm(a, b, dims=(((1,), (0,)), ((), ()))):
    return lax.dot_general(a.astype(MATMUL_DTYPE), b.astype(MATMUL_DTYPE), dims, preferred_element_type=F32)


_NN = (((1,), (0,)), ((), ()))
_NT = (((1,), (1,)), ((), ()))
_TN = (((0,), (0,)), ((), ()))


def _silu(v):
    return v * jax.nn.sigmoid(v)


def _dsilu(v):
    s = jax.nn.sigmoid(v)
    return s * (1.0 + v * (1.0 - s))


def _pick(n, prefs):
    for p in prefs:
        if n % p == 0:
            return p
    return n


def matmul(a, b, mode, out_dtype, name, add=None, tm=None, tn=None, tk=None):
    if mode == "nn":
        (m, k), (k2, n) = a.shape, b.shape
    elif mode == "nt":
        (m, k), (n, k2) = a.shape, b.shape
    else:
        (k, m), (k2, n) = a.shape, b.shape
    assert k == k2, (a.shape, b.shape, mode)
    wide = (2432, 2048, 1536, 1024, 512, 256, 128)
    if mode == "tn":
        tm = tm or _pick(m, (2432, 2048, 1024, 512, 256, 128))
        tn = tn or _pick(n, wide)
        tk = tk or _pick(k, (512, 256, 128))
    else:
        tm = tm or _pick(m, (512, 256, 128))
        tn = tn or _pick(n, wide)
        tk = tk or _pick(k, wide)
    nk = k // tk
    dims = {"nn": _NN, "nt": _NT, "tn": _TN}[mode]

    def body(a_ref, b_ref, *rest):
        o_ref, acc_ref = rest[-2:]
        kk = pl.program_id(2)

        @pl.when(kk == 0)
        def _():
            acc_ref[...] = jnp.zeros_like(acc_ref) if add is None else rest[0][...].astype(F32)

        acc_ref[...] += _mm(a_ref[...], b_ref[...], dims)

        @pl.when(kk == nk - 1)
        def _():
            o_ref[...] = acc_ref[...].astype(o_ref.dtype)

    a_spec = {"nn": pl.BlockSpec((tm, tk), lambda i, j, kk: (i, kk)),
              "nt": pl.BlockSpec((tm, tk), lambda i, j, kk: (i, kk)),
              "tn": pl.BlockSpec((tk, tm), lambda i, j, kk: (kk, i))}[mode]
    b_spec = {"nn": pl.BlockSpec((tk, tn), lambda i, j, kk: (kk, j)),
              "nt": pl.BlockSpec((tn, tk), lambda i, j, kk: (j, kk)),
              "tn": pl.BlockSpec((tk, tn), lambda i, j, kk: (kk, j))}[mode]
    return pl.pallas_call(
        body, name=name, grid=(m // tm, n // tn, nk),
        in_specs=[a_spec, b_spec] + ([] if add is None else [pl.BlockSpec((tm, tn), lambda i, j, kk: (i, j))]),
        out_specs=pl.BlockSpec((tm, tn), lambda i, j, kk: (i, j)),
        out_shape=jax.ShapeDtypeStruct((m, n), out_dtype),
        scratch_shapes=[pltpu.VMEM((tm, tn), F32)],
        compiler_params=_params(("parallel", "parallel", "arbitrary")),
    )(*((a, b) if add is None else (a, b, add)))


class Rows:
    def __init__(self, s, t):
        assert s % t == 0 and t % HALO == 0
        self.s, self.t, self.n = s, t, s // t

    def tile(self, width, col=0, lead=None):
        cb = col // width
        assert col % width == 0
        if lead is None:
            return pl.BlockSpec((self.t, width), lambda i: (i, cb))
        return pl.BlockSpec((None, self.t, width), lambda i: (lead, i, cb))

    def prev(self, width, col=0, lead=None):
        cb, r = col // width, self.t // HALO
        if lead is None:
            return pl.BlockSpec((HALO, width), lambda i: (jnp.maximum(i * r - 1, 0), cb))
        return pl.BlockSpec((None, HALO, width), lambda i: (lead, jnp.maximum(i * r - 1, 0), cb))

    def next(self, width, col=0, lead=None):
        cb, r, last = col // width, self.t // HALO, self.s // HALO - 1
        if lead is None:
            return pl.BlockSpec((HALO, width), lambda i: (jnp.minimum((i + 1) * r, last), cb))
        return pl.BlockSpec((None, HALO, width), lambda i: (lead, jnp.minimum((i + 1) * r, last), cb))

    def halo(self, width, col=0, lead=None):
        return [self.prev(width, col, lead), self.tile(width, col, lead), self.next(width, col, lead)]

    @staticmethod
    def full(shape):
        nd = len(shape)
        return pl.BlockSpec(tuple(shape), lambda i: (0,) * nd)


def rows_call(name, rows, fn, ins, in_specs, row_outs, acc_outs=()):
    n_row = len(row_outs)

    def body(*refs):
        in_refs = refs[:len(ins)]
        out_refs = refs[len(ins):]
        res = fn(*[r[...] for r in in_refs])
        if not isinstance(res, (tuple, list)):
            res = (res,)
        for r, v in zip(out_refs[:n_row], res[:n_row]):
            r[...] = v.astype(r.dtype)
        if acc_outs:
            first = pl.program_id(0) == 0

            @pl.when(first)
            def _():
                for r, v in zip(out_refs[n_row:], res[n_row:]):
                    r[...] = v.astype(F32)

            @pl.when(jnp.logical_not(first))
            def _():
                for r, v in zip(out_refs[n_row:], res[n_row:]):
                    r[...] += v.astype(F32)

    out_shape = [jax.ShapeDtypeStruct((rows.s, w), dt) for (w, dt) in row_outs]
    out_specs = [rows.tile(w) for (w, dt) in row_outs]
    out_shape += [jax.ShapeDtypeStruct(tuple(sh), F32) for sh in acc_outs]
    out_specs += [Rows.full(sh) for sh in acc_outs]
    return pl.pallas_call(
        body, name=name, grid=(rows.n,), in_specs=list(in_specs), out_specs=out_specs, out_shape=out_shape,
        compiler_params=_params(("arbitrary",)),
    )(*ins)


def _edge_zero(prev, nxt, n_tiles, axis=0):
    i = pl.program_id(axis)
    prev = jnp.where(i == 0, jnp.zeros_like(prev), prev)
    nxt = jnp.where(i == n_tiles - 1, jnp.zeros_like(nxt), nxt)
    return prev, nxt


def _ext(prev, cur, nxt, n_tiles, axis=0):
    prev, nxt = _edge_zero(prev, nxt, n_tiles, axis)
    return jnp.concatenate([prev.astype(F32), cur.astype(F32), nxt.astype(F32)], axis=0)


def _shift(ext, off):
    n = ext.shape[0]
    t = n - 2 * HALO
    if off == 0:
        return ext[HALO:HALO + t]
    return pltpu.roll(ext, (-off) % n, 0)[HALO:HALO + t]


def _rowsum(v):
    return jnp.sum(v, axis=0, keepdims=True)


CONV_OFFS = (-2, -1, 0, 1)
SC_OFFS = (-1, 0, 1)


def conv_fwd(proj, conv_w, conv_b, rows):
    def fn(p, c, nx, w, b):
        e = _ext(p, c, nx, rows.n)
        pre = b
        for k, off in enumerate(CONV_OFFS):
            pre = pre + w[k:k + 1, :] * _shift(e, off)
        return pre, _silu(pre)

    outs = [rows_call(f"ev_conv_fwd{h}", rows, fn,
                      [proj, proj, proj, conv_w[:, h * 1024:(h + 1) * 1024], conv_b[:, h * 1024:(h + 1) * 1024]],
                      rows.halo(1024, E_XBC + h * 1024) + [Rows.full((4, 1024)), Rows.full((1, 1024))],
                      [(1024, MATMUL_DTYPE), (1024, F32)]) for h in range(2)]
    return outs


def conv_bwd(proj, pre_h, du_h, conv_w, rows):
    res = []
    for h in range(2):
        def fn(*a):
            w = a[-1]
            xe = _ext(a[0], a[1], a[2], rows.n)
            pe = jnp.concatenate([a[3], a[4], a[5]], axis=0).astype(F32)
            g = a[6:-1]
            du = _ext(g[0], g[1], g[2], rows.n) + _ext(g[3], g[4], g[5], rows.n)
            dpre = du * _dsilu(pe)
            dx = jnp.zeros_like(a[1], dtype=F32)
            dws = []
            for k, off in enumerate(CONV_OFFS):
                dx = dx + w[k:k + 1, :] * _shift(dpre, -off)
                dws.append(_rowsum(_shift(dpre, 0) * _shift(xe, off)))
            dw = jnp.concatenate(dws + [jnp.zeros((4, dx.shape[1]), F32)], axis=0)
            return dx, dw, _rowsum(_shift(dpre, 0))

        gi = [du_h[h][0]] * 3 + [du_h[h][1]] * 3
        gs = rows.halo(1024) * 2
        res.append(rows_call(
            f"ev_conv_bwd{h}", rows, fn,
            [proj] * 3 + [pre_h[h]] * 3 + gi + [conv_w[:, h * 1024:(h + 1) * 1024]],
            rows.halo(1024, E_XBC + h * 1024) + rows.halo(1024) + gs + [Rows.full((4, 1024))],
            [(1024, MATMUL_DTYPE)], [(8, 1024), (1, 1024)]))
    dconv_w = jnp.concatenate([res[0][1][:4], res[1][1][:4]], axis=1)
    dconv_b = jnp.concatenate([res[0][2], res[1][2]], axis=1)
    return [res[0][0], res[1][0]], dconv_w, dconv_b


def _head_row(p):
    return jnp.concatenate([p.reshape(1, 2 * SSD_HEADS), jnp.zeros((1, LANES - 2 * SSD_HEADS), F32)], axis=1)


def _head_unrow(r):
    return r[:, :2 * SSD_HEADS].reshape(2, SSD_HEADS)


def _ssd_pre_fn(dtraw, bias_row, alog_row):
    q = dtraw.shape[0]
    dt = jax.nn.softplus(dtraw + bias_row)
    da = dt * (-jnp.exp(alog_row))
    li = lax.broadcasted_iota(jnp.int32, (q, q), 0)
    si = lax.broadcasted_iota(jnp.int32, (q, q), 1)
    tril = (li >= si).astype(F32)
    csf = lax.dot_general(tril, da, _NN, precision=lax.Precision.HIGHEST, preferred_element_type=F32)
    tot = jnp.sum(da, axis=0, keepdims=True)
    lane = lax.broadcasted_iota(jnp.int32, (1, LANES), 1)
    cs = jnp.where(lane < SSD_HEADS, csf, tot - csf + da)
    return dt, cs


def ssd_pre(proj, bias_row, alog_row, s):
    rows = Rows(s, SSD_CHUNK)
    return rows_call("ev_ssd_pre", rows, _ssd_pre_fn, [proj, bias_row, alog_row],
                     [rows.tile(LANES, E_DT), Rows.full((1, LANES)), Rows.full((1, LANES))],
                     [(LANES, F32), (LANES, F32)])


def ssd_pre_bwd(proj, bias_row, alog_row, ddt, dcs, s):
    rows = Rows(s, SSD_CHUNK)

    def fn(dtraw, b, al, g0, g1, c0, c1):
        _, vjp = jax.vjp(_ssd_pre_fn, dtraw, b, al)
        return vjp((g0 + g1, c0 + c1))

    return rows_call("ev_ssd_pre_bwd", rows, fn, [proj, bias_row, alog_row, ddt[0], ddt[1], dcs[0], dcs[1]],
                     [rows.tile(LANES, E_DT), Rows.full((1, LANES)), Rows.full((1, LANES))] + [rows.tile(LANES)] * 4,
                     [(LANES, MATMUL_DTYPE)], [(1, LANES), (1, LANES)])


_N_PAIR = SSD_HEADS // 2
_BC = SSD_GROUPS * SSD_STATE


def _ssd_chunk_fn(x, bc, dt, cs, h_in, dsk_row, d):
    q = x.shape[0]
    lane = lax.broadcasted_iota(jnp.int32, (1, LANES), 1)
    half = lane < SSD_HEAD_DIM
    lo, hi = half.astype(F32), 1.0 - half.astype(F32)
    cst = cs.T
    li = lax.broadcasted_iota(jnp.int32, (q, 1), 0)
    si = lax.broadcasted_iota(jnp.int32, (1, q), 1)
    mask = li >= si if d == 0 else li <= si
    end = q - 1 if d == 0 else 0
    tot = cs[end:end + 1, :]
    mine = (lane >= SSD_HEADS * d) & (lane < SSD_HEADS * (d + 1))
    e_cs, e_dec, e_tot = jnp.exp(cs), jnp.exp(jnp.where(mine, tot - cs, 0.0)), jnp.exp(tot)
    e_dd = e_dec * dt
    dtt = dt.T

    def col(v, l):
        return v[:, l:l + 1]

    def by_head(v, l):
        return jnp.where(half, col(v, l), col(v, l + 1))

    per_group = _N_PAIR // SSD_GROUPS
    ys, hs = [], []
    for g in range(SSD_GROUPS):
        bm = bc[:, g * SSD_STATE:(g + 1) * SSD_STATE]
        cm = bc[:, _BC + g * SSD_STATE:_BC + (g + 1) * SSD_STATE]
        cb = _mm(cm, bm, _NT)
        pairs = range(g * per_group, (g + 1) * per_group)
        h_g = [h_in[j * SSD_STATE:(j + 1) * SSD_STATE, :] for j in pairs]
        y_off = _mm(cm, jnp.concatenate(h_g, axis=1))
        for k, j in enumerate(pairs):
            l0 = SSD_HEADS * d + 2 * j
            xj = x[:, j * LANES:(j + 1) * LANES]
            xcat = jnp.concatenate([xj * lo, xj * hi], axis=0)
            w0 = cb * jnp.exp(jnp.where(mask, col(cs, l0) - cst[l0:l0 + 1, :], -jnp.inf)) * dtt[l0:l0 + 1, :]
            w1 = cb * jnp.exp(jnp.where(mask, col(cs, l0 + 1) - cst[l0 + 1:l0 + 2, :], -jnp.inf)) * dtt[l0 + 1:l0 + 2, :]
            y = _mm(jnp.concatenate([w0, w1], axis=1), xcat)
            st = _mm(jnp.concatenate([bm * col(e_dd, l0), bm * col(e_dd, l0 + 1)], axis=0), xcat, _TN)
            y = y + y_off[:, k * LANES:(k + 1) * LANES] * by_head(e_cs, l0) + by_head(dsk_row, l0) * xj
            ys.append(y)
            hs.append(h_g[k] * by_head(e_tot, l0) + st)
    return jnp.concatenate(ys, axis=1), jnp.concatenate(hs, axis=0)


def _chunk_of(d, ci, nc, backward):
    up = ci if not backward else nc - 1 - ci
    return up + d * (nc - 1 - 2 * up)


_ST_ROWS = _N_PAIR * SSD_STATE


def _ssd_fwd_dir(u_h, dt, cs, dsk_row, s, d):
    nc = s // SSD_CHUNK
    q = SSD_CHUNK

    def body(x_ref, bc_ref, dt_ref, cs_ref, dsk_ref, y_ref, hs_ref, st_ref):
        @pl.when(pl.program_id(0) == 0)
        def _():
            st_ref[...] = jnp.zeros(st_ref.shape, F32)

        h_in = st_ref[...]
        y, h_out = _ssd_chunk_fn(x_ref[...], bc_ref[...], dt_ref[...], cs_ref[...], h_in, dsk_ref[...], d)
        y_ref[...] = y.astype(y_ref.dtype)
        hs_ref[...] = h_in
        st_ref[...] = h_out

    ch = lambda ci: _chunk_of(d, ci, nc, False)
    return pl.pallas_call(
        body, name=f"ev_ssd_fwd{d}", grid=(nc,),
        in_specs=[pl.BlockSpec((q, SSD_INNER), lambda ci: (ch(ci), 0)),
                  pl.BlockSpec((q, 2 * _BC), lambda ci: (ch(ci), 0)),
                  pl.BlockSpec((q, LANES), lambda ci: (ch(ci), 0)),
                  pl.BlockSpec((q, LANES), lambda ci: (ch(ci), 0)),
                  pl.BlockSpec((1, LANES), lambda ci: (0, 0))],
        out_specs=[pl.BlockSpec((q, SSD_INNER), lambda ci: (ch(ci), 0)),
                   pl.BlockSpec((None, _ST_ROWS, LANES), lambda ci: (ch(ci), 0, 0))],
        out_shape=[jax.ShapeDtypeStruct((s, SSD_INNER), MATMUL_DTYPE),
                   jax.ShapeDtypeStruct((nc, _ST_ROWS, LANES), F32)],
        scratch_shapes=[pltpu.VMEM((_ST_ROWS, LANES), F32)],
        compiler_params=_params(("arbitrary",)),
    )(u_h[0], u_h[1], dt, cs, dsk_row)


def ssd_fwd(u_h, dt, cs, dsk_row, s):
    ys, hss = zip(*[_ssd_fwd_dir(u_h, dt, cs, dsk_row, s, d) for d in range(2)])
    return ys, hss


def _ssd_bwd_dir(u_h, dt, cs, dsk_row, hs, dy, s, d):
    nc = s // SSD_CHUNK
    q = SSD_CHUNK

    def body(x_ref, bc_ref, dt_ref, cs_ref, dsk_ref, hs_ref, dy_ref,
             dx_ref, dbc_ref, ddt_ref, dcs_ref, ddsk_ref, dst_ref):
        ci = pl.program_id(0)

        @pl.when(ci == 0)
        def _():
            dst_ref[...] = jnp.zeros(dst_ref.shape, F32)

        f = functools.partial(_ssd_chunk_fn, d=d)
        _, vjp = jax.vjp(f, x_ref[...], bc_ref[...], dt_ref[...], cs_ref[...], hs_ref[...], dsk_ref[...])
        dx, dbc, ddt, dcs, dh, ddsk = vjp((dy_ref[...], dst_ref[...]))
        dx_ref[...] = dx
        dbc_ref[...] = dbc
        ddt_ref[...] = ddt
        dcs_ref[...] = dcs
        dst_ref[...] = dh

        @pl.when(ci == 0)
        def _():
            ddsk_ref[...] = ddsk

        @pl.when(ci != 0)
        def _():
            ddsk_ref[...] += ddsk

    ch = lambda ci: _chunk_of(d, ci, nc, True)
    row_blk = lambda w: pl.BlockSpec((q, w), lambda ci: (ch(ci), 0))
    return pl.pallas_call(
        body, name=f"ev_ssd_bwd{d}", grid=(nc,),
        in_specs=[row_blk(SSD_INNER), row_blk(2 * _BC), row_blk(LANES), row_blk(LANES),
                  pl.BlockSpec((1, LANES), lambda ci: (0, 0)),
                  pl.BlockSpec((None, _ST_ROWS, LANES), lambda ci: (ch(ci), 0, 0)), row_blk(SSD_INNER)],
        out_specs=[row_blk(SSD_INNER), row_blk(2 * _BC), row_blk(LANES), row_blk(LANES),
                   pl.BlockSpec((1, LANES), lambda ci: (0, 0))],
        out_shape=[jax.ShapeDtypeStruct((s, SSD_INNER), F32), jax.ShapeDtypeStruct((s, 2 * _BC), F32),
                   jax.ShapeDtypeStruct((s, LANES), F32), jax.ShapeDtypeStruct((s, LANES), F32),
                   jax.ShapeDtypeStruct((1, LANES), F32)],
        scratch_shapes=[pltpu.VMEM((_ST_ROWS, LANES), F32)],
        compiler_params=_params(("arbitrary",)),
    )(u_h[0], u_h[1], dt, cs, dsk_row, hs, dy)


def ssd_bwd(u_h, dt, cs, dsk_row, hs, dy, s):
    return zip(*[_ssd_bwd_dir(u_h, dt, cs, dsk_row, hs[d], dy, s, d) for d in range(2)])


def _gated_rms(ys, z, g):
    t1 = ys * _silu(z)
    return t1 * lax.rsqrt(jnp.mean(t1 * t1, axis=-1, keepdims=True) + EPS) * g


def even_mix_fwd(proj, y2, norm_g, sc_w, rows):
    def fn(yf, yb, z, bg, cgp, cg, cgn, hp, hh, hn, gate, g, w):
        z, bg, gate = z.astype(F32), bg.astype(F32), gate.astype(F32)
        ya = _gated_rms(yf.astype(F32) + yb.astype(F32), z, g)
        me = _ext(cgp, cg, cgn, rows.n) * _ext(hp, hh, hn, rows.n)
        cm = sum(w[k:k + 1, :] * _shift(me, off) for k, off in enumerate(SC_OFFS))
        return jnp.concatenate([ya, bg * cm * _silu(gate)], axis=1)

    w = 1024
    return rows_call("ev_mix_fwd", rows, fn,
                     [y2[0], y2[1], proj, proj] + [proj] * 6 + [proj, norm_g, sc_w],
                     [rows.tile(w), rows.tile(w), rows.tile(w, G_Z), rows.tile(w, G_BG)]
                     + rows.halo(w, G_CG) + rows.halo(w, G_H)
                     + [rows.tile(w, G_GATE), Rows.full((1, w)), Rows.full((3, w))],
                     [(2 * w, MATMUL_DTYPE)])[0]


MIX_COLS = 256


def even_mix_bwd(proj, y2, norm_g, sc_w, dyab, rows):
    w = 1024

    def norm_fn(yf, yb, z, g, dya):
        _, vjp = jax.vjp(_gated_rms, yf.astype(F32) + yb.astype(F32), z.astype(F32), g)
        dys, dz, dg = vjp(dya.astype(F32))
        return dys, dz, _rowsum(dg)

    dys, dz, dnorm_g = rows_call(
        "ev_mix_bwd_norm", rows, norm_fn, [y2[0], y2[1], proj, norm_g, dyab],
        [rows.tile(w), rows.tile(w), rows.tile(w, G_Z), Rows.full((1, w)), rows.tile(w, 0)],
        [(w, F32), (w, MATMUL_DTYPE)], [(1, w)])

    cw, n, t = MIX_COLS, rows.n, rows.t
    r, last = t // HALO, rows.s // HALO - 1

    def body(*refs):
        dsw_ref = refs[-1]
        dbg_ref, dcg_ref, dh_ref, dgate_ref = refs[-5:-1]
        sw = refs[15][...]
        dye, bge, gte, cge, he = [_ext(refs[3 * k][...], refs[3 * k + 1][...], refs[3 * k + 2][...], n, axis=1)
                                  for k in range(5)]
        me = cge * he
        cm = sum(sw[k:k + 1, :] * _shift(me, off) for k, off in enumerate(SC_OFFS))
        dyc, bgc, gtc = _shift(dye, 0), _shift(bge, 0), _shift(gte, 0)
        dcme = dye * bge * _silu(gte)
        dm = sum(sw[k:k + 1, :] * _shift(dcme, -off) for k, off in enumerate(SC_OFFS))
        dcm = _shift(dcme, 0)
        dsw = jnp.concatenate([_rowsum(dcm * _shift(me, off)) for off in SC_OFFS] + [jnp.zeros((5, cw), F32)], axis=0)
        dbg_ref[...] = (dyc * cm * _silu(gtc)).astype(dbg_ref.dtype)
        dcg_ref[...] = (dm * _shift(he, 0)).astype(dcg_ref.dtype)
        dh_ref[...] = (dm * _shift(cge, 0)).astype(dh_ref.dtype)
        dgate_ref[...] = (dyc * bgc * cm * _dsilu(gtc)).astype(dgate_ref.dtype)
        first = pl.program_id(1) == 0

        @pl.when(first)
        def _():
            dsw_ref[...] = dsw

        @pl.when(jnp.logical_not(first))
        def _():
            dsw_ref[...] += dsw

    def halo(col):
        cb = col // cw
        return [pl.BlockSpec((HALO, cw), lambda j, i: (jnp.maximum(i * r - 1, 0), cb + j)),
                pl.BlockSpec((t, cw), lambda j, i: (i, cb + j)),
                pl.BlockSpec((HALO, cw), lambda j, i: (jnp.minimum((i + 1) * r, last), cb + j))]

    tile_out = pl.BlockSpec((t, cw), lambda j, i: (i, j))
    dbg, dcg, dhh, dgate, dsw = pl.pallas_call(
        body, name="ev_mix_bwd_conv", grid=(w // cw, n),
        in_specs=halo(w) + halo(G_BG) + halo(G_GATE) + halo(G_CG) + halo(G_H)
        + [pl.BlockSpec((3, cw), lambda j, i: (0, j))],
        out_specs=[tile_out] * 4 + [pl.BlockSpec((8, cw), lambda j, i: (0, j))],
        out_shape=[jax.ShapeDtypeStruct((rows.s, w), MATMUL_DTYPE)] * 4 + [jax.ShapeDtypeStruct((8, w), F32)],
        compiler_params=_params(("parallel", "arbitrary")),
    )(*([dyab] * 3 + [proj] * 12 + [sc_w]))
    return dys, dz, dbg, dcg, dhh, dgate, dnorm_g, dsw


def _res_ln(x, h, g, b):
    v = ALPHA * x + h
    mu = jnp.mean(v, axis=-1, keepdims=True)
    var = jnp.mean(jnp.square(v - mu), axis=-1, keepdims=True)
    return (v - mu) * lax.rsqrt(var + EPS) * g + b


def res_ln_fwd(x, h, g, b, rows, name):
    return rows_call(name, rows, _res_ln, [x, h, g, b],
                     [rows.tile(D_MODEL), rows.tile(D_MODEL), Rows.full((1, D_MODEL)), Rows.full((1, D_MODEL))],
                     [(D_MODEL, F32)])[0]


def res_ln_bwd(x, h, g, b, dy, rows, name):
    def fn(x_, h_, g_, b_, dy_):
        _, vjp = jax.vjp(_res_ln, x_, h_, g_, b_)
        dx, dh, dg, db = vjp(dy_)
        return dx, dh, dg, db

    return rows_call(name, rows, fn, [x, h, g, b, dy],
                     [rows.tile(D_MODEL), rows.tile(D_MODEL), Rows.full((1, D_MODEL)), Rows.full((1, D_MODEL)),
                      rows.tile(D_MODEL)],
                     [(D_MODEL, F32), (D_MODEL, F32)], [(1, D_MODEL), (1, D_MODEL)])


def final_ln_loss(x, h, g, b, target, rows):
    def fn(x_, h_, g_, b_, t_):
        y, vjp = jax.vjp(_res_ln, x_, h_, g_, b_)
        err = y - t_
        dx, dh, dg, db = vjp(err * (1.0 / D_MODEL))
        return dx, dh, dg, db, _rowsum(jnp.square(err)) * (0.5 / D_MODEL)

    return rows_call("od_ln_loss", rows, fn, [x, h, g, b, target],
                     [rows.tile(D_MODEL), rows.tile(D_MODEL), Rows.full((1, D_MODEL)), Rows.full((1, D_MODEL)),
                      rows.tile(D_MODEL)],
                     [(D_MODEL, F32), (D_MODEL, F32)], [(1, D_MODEL), (1, D_MODEL), (1, D_MODEL)])


def pad_even_w_in(w):
    return jnp.concatenate([w[:, 1024:3104], jnp.zeros((w.shape[0], EVEN_P - EVEN_PROJ), w.dtype), w[:, :1024],
                            w[:, 3104:]], axis=1)


def matmul_pieces_nt(pieces, w, name, add):
    m, n, npc = pieces[0].shape[0], w.shape[0], len(pieces)
    widths = [p.shape[1] for p in pieces]
    assert sum(widths) == w.shape[1]
    tm = _pick(m, (256, 128))

    def body(*refs):
        p_refs = refs[:npc]
        w_ref, add_ref, o_ref = refs[npc:]
        acc, off = add_ref[...].astype(F32), 0
        for p_ref, wd in zip(p_refs, widths):
            acc = acc + _mm(p_ref[...], w_ref[:, off:off + wd], _NT)
            off += wd
        o_ref[...] = acc

    return pl.pallas_call(
        body, name=name, grid=(m // tm,),
        in_specs=[pl.BlockSpec((tm, wd), lambda i: (i, 0)) for wd in widths]
        + [pl.BlockSpec(w.shape, lambda i: (0, 0), pipeline_mode=pl.Buffered(1)),
           pl.BlockSpec((tm, n), lambda i: (i, 0))],
        out_specs=pl.BlockSpec((tm, n), lambda i: (i, 0)),
        out_shape=jax.ShapeDtypeStruct((m, n), F32),
        compiler_params=_params(("parallel",)),
    )(*pieces, w, add)


def even_layer(x, w, rows):
    s = rows.s
    xb = x.astype(MATMUL_DTYPE)
    proj = matmul(xb, w["w_in_p"][:, :EVEN_F], "nn", F32, "ev_proj_f", tn=EVEN_F)
    proj_g = matmul(xb, w["w_in_p"][:, EVEN_F:], "nn", MATMUL_DTYPE, "ev_proj_g", tn=(EVEN_P - EVEN_F) // 2)
    (pre0, u0), (pre1, u1) = conv_fwd(proj, w["conv_w"], w["conv_b"], rows)
    dt, cs = ssd_pre(proj, w["bias_row"], w["alog_row"], s)
    y2, hs = ssd_fwd((u0, u1), dt, cs, w["dsk_row"], s)
    yab = even_mix_fwd(proj_g, y2, w["norm_g"], w["sc_w"], rows)
    h = matmul(yab, w["w_out"], "nn", F32, "ev_out")
    x1 = res_ln_fwd(x, h, w["ln_g"], w["ln_b"], rows, "ev_ln")
    return x1, dict(x=x, xb=xb, proj=proj, proj_g=proj_g, pre=(pre0, pre1), u=(u0, u1), dt=dt, cs=cs, y2=y2,
                    hs=hs, yab=yab, h=h)


def even_layer_bwd(dx1, w, sv, rows):
    s = rows.s
    dres, dh, dln_g, dln_b = res_ln_bwd(sv["x"], sv["h"], w["ln_g"], w["ln_b"], dx1, rows, "ev_ln_bwd")
    dyab = matmul(dh, w["w_out"], "nt", MATMUL_DTYPE, "ev_out_dx")
    dw_out = matmul(sv["yab"], dh, "tn", F32, "ev_out_dw")
    dys, dz, dbg, dcg, dhh, dgate, dnorm_g, dsw = even_mix_bwd(sv["proj_g"], sv["y2"], w["norm_g"], w["sc_w"],
                                                               dyab, rows)
    dxs, dbc, ddt, dcs, ddsk = ssd_bwd(sv["u"], sv["dt"], sv["cs"], w["dsk_row"], sv["hs"], dys, s)
    ddtraw, dbias_row, dalog_row = ssd_pre_bwd(sv["proj"], w["bias_row"], w["alog_row"], ddt, dcs, s)
    (dxbc0, dxbc1), dconv_w, dconv_b = conv_bwd(sv["proj"], sv["pre"], (dxs, dbc), w["conv_w"], rows)
    pieces = [dxbc0, dxbc1, ddtraw, dz, dbg, dcg, dhh, dgate]
    dx0 = matmul_pieces_nt(pieces, w["w_in_p"], "ev_proj_dx", add=dres)
    dw0, dw1, dw_dt, dw_z, *dw_gates = [matmul(p, sv["xb"], "tn", F32, f"ev_proj_dw{i}")
                                        for i, p in enumerate(pieces)]
    dw_in = jnp.concatenate([dw_z, dw0, dw1, dw_dt[:2 * SSD_HEADS]] + dw_gates, axis=0)
    g = dict(ev_w_in=dw_in, ev_conv_w=dconv_w, ev_conv_b=dconv_b,
             ev_a_log=_head_unrow(dalog_row), ev_dt_bias=_head_unrow(dbias_row),
             ev_d_skip=_head_unrow(ddsk[0] + ddsk[1]), ev_norm_g=dnorm_g, ev_sc_conv_w=dsw[:3],
             ev_w_out=dw_out, ev_ln_g=dln_g, ev_ln_b=dln_b)
    return dx0, g


def even_weights(ev_w_in, ev_conv_w, ev_conv_b, ev_a_log, ev_dt_bias, ev_d_skip, ev_norm_g, ev_sc_conv_w,
                 ev_w_out, ev_ln_g, ev_ln_b):
    return dict(w_in_p=pad_even_w_in(ev_w_in).astype(MATMUL_DTYPE), conv_w=ev_conv_w, conv_b=ev_conv_b,
                alog_row=_head_row(ev_a_log), bias_row=_head_row(ev_dt_bias), dsk_row=_head_row(ev_d_skip),
                norm_g=ev_norm_g, sc_w=ev_sc_conv_w, w_out=ev_w_out.astype(MATMUL_DTYPE), ln_g=ev_ln_g,
                ln_b=ev_ln_b)


HEAD_BLK = LANES
ROPE_HALF = MLA_ROPE // 2
ROPE_A = 48
ROPE_B = ROPE_A + LANES // 2
NOPE_SPLIT = ROPE_A


def _head_lanes(nope, ra, rb):
    z = jnp.zeros(nope.shape[:-1] + (ROPE_B - ROPE_A - ROPE_HALF - (MLA_NOPE - NOPE_SPLIT),), nope.dtype)
    return jnp.concatenate([nope[..., :NOPE_SPLIT], ra, nope[..., NOPE_SPLIT:], z, rb], axis=-1)


def _head_unlanes(blk):
    nope = jnp.concatenate([blk[..., :NOPE_SPLIT], blk[..., ROPE_A + ROPE_HALF:ROPE_A + ROPE_HALF + MLA_NOPE - NOPE_SPLIT]],
                           axis=-1)
    return nope, blk[..., ROPE_A:ROPE_A + ROPE_HALF], blk[..., ROPE_B:ROPE_B + ROPE_HALF]


def _rms(v, g):
    return v * lax.rsqrt(jnp.mean(v * v, axis=-1, keepdims=True) + EPS) * g


def latent_norm_fwd(proj, gq, gkv, rows):
    def fn(cq, ckv, gq_, gkv_):
        return _rms(cq, gq_), _rms(ckv, gkv_)

    return rows_call("od_norm_fwd", rows, fn, [proj, proj, gq, gkv],
                     [rows.tile(MLA_Q_RANK, O_CQ), rows.tile(MLA_KV_RANK, O_CKV), Rows.full((1, MLA_Q_RANK)),
                      Rows.full((1, MLA_KV_RANK))],
                     [(MLA_Q_RANK, MATMUL_DTYPE), (MLA_KV_RANK, MATMUL_DTYPE)])


def latent_norm_bwd(proj, gq, gkv, dcqn, dckvn, rows):
    def fn(cq, ckv, gq_, gkv_, d1, d2):
        _, vjp = jax.vjp(_rms, cq, gq_)
        dcq, dgq = vjp(d1)
        _, vjp2 = jax.vjp(_rms, ckv, gkv_)
        dckv, dgkv = vjp2(d2)
        return dcq, dckv, dgq, dgkv

    return rows_call("od_norm_bwd", rows, fn, [proj, proj, gq, gkv, dcqn, dckvn],
                     [rows.tile(MLA_Q_RANK, O_CQ), rows.tile(MLA_KV_RANK, O_CKV), Rows.full((1, MLA_Q_RANK)),
                      Rows.full((1, MLA_KV_RANK)), rows.tile(MLA_Q_RANK), rows.tile(MLA_KV_RANK)],
                     [(MLA_Q_RANK, F32), (MLA_KV_RANK, F32)], [(1, MLA_Q_RANK), (1, MLA_KV_RANK)])


def rope_rows():
    lane = np.arange(LANES)
    inv = ROPE_THETA ** (-jnp.arange(ROPE_HALF, dtype=F32) / ROPE_HALF)
    in_a = (lane >= ROPE_A) & (lane < ROPE_A + ROPE_HALF)
    in_b = (lane >= ROPE_B) & (lane < ROPE_B + ROPE_HALF)
    freq = jnp.where(in_a | in_b, inv[(lane - ROPE_A) % ROPE_HALF], 0.0).reshape(1, LANES).astype(F32)
    sign = np.where(in_a, -1.0, np.where(in_b, 1.0, 0.0)).reshape(1, LANES).astype(np.float32)
    return freq, jnp.asarray(sign)


def _rot_tables(pos, freq, sign):
    ang = pos.astype(F32) * freq
    return jnp.cos(ang), jnp.sin(ang) * sign


def _swap_halves(v):
    return pltpu.roll(v, LANES // 2, 1)


def rope_fwd(qp, kvp, proj, pos, freq, sign, rows):
    def fn(q, k, kr, v, p, f, sg):
        c, sn = _rot_tables(p, f, sg)
        rk = kr * c + _swap_halves(kr) * sn
        one = (lax.broadcasted_iota(jnp.int32, (v.shape[0], HEAD_BLK - MLA_V), 1) == 0).astype(F32)
        qs, ks, vs = [], [], []
        for h in range(MLA_HEADS):
            qh = q[:, h * HEAD_BLK:(h + 1) * HEAD_BLK]
            qs.append((qh * c + _swap_halves(qh) * sn) * QSCALE)
            ks.append(k[:, h * HEAD_BLK:(h + 1) * HEAD_BLK] + rk)
            vs += [v[:, h * MLA_V:(h + 1) * MLA_V], one]
        return jnp.concatenate(qs, axis=1), jnp.concatenate(ks, axis=1), jnp.concatenate(vs, axis=1)

    w = MLA_HEADS * HEAD_BLK
    return rows_call("od_rope_fwd", rows, fn, [qp, kvp, proj, kvp, pos, freq, sign],
                     [rows.tile(w), rows.tile(w, 0), rows.tile(LANES, O_KR), rows.tile(MLA_WIDTH, w),
                      rows.tile(1), Rows.full((1, LANES)), Rows.full((1, LANES))],
                     [(w, MATMUL_DTYPE), (w, MATMUL_DTYPE), (w, MATMUL_DTYPE)])


def rope_bwd(dq, dk, dv, pos, freq, sign, rows):
    def fn(dq_, dk_, dv_, p, f, sg):
        c, sn = _rot_tables(p, f, sg)
        on = jnp.abs(sg)
        outs, dks, dvs, dkr = [], [], [], jnp.zeros((dq_.shape[0], LANES), F32)
        for h in range(MLA_HEADS):
            g = dq_[:, h * HEAD_BLK:(h + 1) * HEAD_BLK] * ATTN_SCALE
            outs.append(g * c + _swap_halves(g * sn) * on)
            gk = dk_[:, h * HEAD_BLK:(h + 1) * HEAD_BLK] * LN2
            dks.append(gk)
            dkr = dkr + gk * c + _swap_halves(gk * sn) * on
            dvs.append(dv_[:, h * HEAD_BLK:h * HEAD_BLK + MLA_V])
        return jnp.concatenate(outs, axis=1), jnp.concatenate(dks + dvs, axis=1), dkr

    w = MLA_HEADS * HEAD_BLK
    return rows_call("od_rope_bwd", rows, fn, [dq, dk, dv, pos, freq, sign],
                     [rows.tile(w), rows.tile(w), rows.tile(w), rows.tile(1), Rows.full((1, LANES)),
                      Rows.full((1, LANES))],
                     [(w, MATMUL_DTYPE), (w + MLA_WIDTH, MATMUL_DTYPE), (LANES, F32)])


_PAIRS = MLA_HEADS // 2
ATT_TQ = 512
ATT_TK = 4096
ATT_BWD_TQ = 2048
ATT_BWD_TK = 1024
ATT_BWD_VMEM = 62 * 1024 * 1024


def _att_tiles(s, backward=False):
    if backward:
        return min(ATT_BWD_TQ, s), min(ATT_BWD_TK, s)
    return min(ATT_TQ, s), min(ATT_TK, s)


def attention_fwd(qcat, kcat, vcat, s):
    tq, tk = _att_tiles(s)
    nq, nk = s // tq, s // tk

    def body(q_ref, k_ref, v_ref, o_ref, lse_ref, p_ref, mt_ref, m_ref, acc_ref):
        kk = pl.program_id(2)
        half = lax.broadcasted_iota(jnp.int32, (1, LANES), 1) < MLA_V

        @pl.when(kk == 0)
        def _():
            m_ref[...] = jnp.full(m_ref.shape, -jnp.inf, F32)
            acc_ref[...] = jnp.zeros(acc_ref.shape, F32)

        sl = [slice(hh * HEAD_BLK, (hh + 1) * HEAD_BLK) for hh in range(2)]
        sc = [_mm(q_ref[:, sl[hh]], k_ref[:, sl[hh]], _NT) for hh in range(2)]
        for hh in range(2):
            m_prev = m_ref[hh]
            m_new = jnp.maximum(m_prev, jnp.max(sc[hh], axis=1, keepdims=True))
            p = jnp.exp2(sc[hh] - m_new[:, :1]).astype(p_ref.dtype)
            p_ref[hh] = p
            acc_ref[hh] = acc_ref[hh] * jnp.exp2(m_prev - m_new) + _mm(p, v_ref[:, sl[hh]])
            m_ref[hh] = m_new
        mt_ref[...] = jnp.where(half, m_ref[0], m_ref[1])

        @pl.when(kk == nk - 1)
        def _():
            l0, l1 = acc_ref[0][:, MLA_V:MLA_V + 1], acc_ref[1][:, MLA_V:MLA_V + 1]
            o_ref[...] = jnp.where(half, acc_ref[0] / l0, pltpu.roll(acc_ref[1] / l1, MLA_V, 1))
            lse_ref[...] = jnp.where(half, m_ref[0] + jnp.log2(l0), m_ref[1] + jnp.log2(l1))

    return pl.pallas_call(
        body, name="od_attn_fwd", grid=(_PAIRS, nq, nk),
        in_specs=[pl.BlockSpec((tq, 2 * HEAD_BLK), lambda p, i, kk: (i, p)),
                  pl.BlockSpec((tk, 2 * HEAD_BLK), lambda p, i, kk: (kk, p)),
                  pl.BlockSpec((tk, 2 * HEAD_BLK), lambda p, i, kk: (kk, p))],
        out_specs=[pl.BlockSpec((tq, LANES), lambda p, i, kk: (i, p)),
                   pl.BlockSpec((None, tq, LANES), lambda p, i, kk: (p, i, 0)),
                   pl.BlockSpec((2, tq, tk), lambda p, i, kk: (p, i, kk)),
                   pl.BlockSpec((None, None, tq, LANES), lambda p, i, kk: (p, kk, i, 0))],
        out_shape=[jax.ShapeDtypeStruct((s, MLA_WIDTH), F32), jax.ShapeDtypeStruct((_PAIRS, s, LANES), F32),
                   jax.ShapeDtypeStruct((MLA_HEADS, s, s), MATMUL_DTYPE),
                   jax.ShapeDtypeStruct((_PAIRS, nk, s, LANES), F32)],
        scratch_shapes=[pltpu.VMEM((2, tq, LANES), F32)] * 2,
        compiler_params=_params(("parallel", "parallel", "arbitrary")),
    )(qcat, kcat, vcat)


def attention_bwd(qcat, kcat, vcat, o, lse, pst, mt, do, s):
    tq, tk = _att_tiles(s, backward=True)
    nq, nk = s // tq, s // tk
    per_fwd_tile = _att_tiles(s)[1] // tk
    assert per_fwd_tile * tk == _att_tiles(s)[1]

    def body(q_ref, k_ref, v_ref, do_ref, o_ref, lse_ref, p_ref, mt_ref, dq_ref, dk_ref, dv_ref):
        kk, i = pl.program_id(1), pl.program_id(2)
        lane = lax.broadcasted_iota(jnp.int32, (1, LANES), 1)
        half = lane < MLA_V
        c = jnp.exp2(mt_ref[...] - lse_ref[...])
        do_p = do_ref[...] * c
        prod = do_p * o_ref[...]
        rows_i = pl.ds(pl.multiple_of(i * tq, tq), tq)
        for hh in range(2):
            sl = slice(hh * HEAD_BLK, (hh + 1) * HEAD_BLK)
            mine = half if hh == 0 else jnp.logical_not(half)
            delta = jnp.sum(jnp.where(mine, prod, 0.0), axis=1, keepdims=True)
            do_h = jnp.where(half, do_p if hh == 0 else pltpu.roll(do_p, MLA_V, 1), 0.0)
            p = p_ref[hh]
            ds = p.astype(F32) * (_mm(do_h, v_ref[:, sl], _NT) - delta)
            dv_h, dk_h, dq_h = _mm(p, do_h, _TN), _mm(ds, q_ref[:, sl], _TN), _mm(ds, k_ref[:, sl])

            @pl.when(i == 0)
            def _():
                dv_ref[:, sl] = dv_h
                dk_ref[:, sl] = dk_h

            @pl.when(i != 0)
            def _():
                dv_ref[:, sl] += dv_h
                dk_ref[:, sl] += dk_h

            @pl.when(kk == 0)
            def _():
                dq_ref[rows_i, sl] = dq_h

            @pl.when(kk != 0)
            def _():
                dq_ref[rows_i, sl] += dq_h

    w = MLA_HEADS * HEAD_BLK
    return pl.pallas_call(
        body, name="od_attn_bwd", grid=(_PAIRS, nk, nq),
        in_specs=[pl.BlockSpec((tq, 2 * HEAD_BLK), lambda p, kk, i: (i, p)),
                  pl.BlockSpec((tk, 2 * HEAD_BLK), lambda p, kk, i: (kk, p)),
                  pl.BlockSpec((tk, 2 * HEAD_BLK), lambda p, kk, i: (kk, p)),
                  pl.BlockSpec((tq, LANES), lambda p, kk, i: (i, p)),
                  pl.BlockSpec((tq, LANES), lambda p, kk, i: (i, p)),
                  pl.BlockSpec((None, tq, LANES), lambda p, kk, i: (p, i, 0)),
                  pl.BlockSpec((2, tq, tk), lambda p, kk, i: (p, i, kk)),
                  pl.BlockSpec((None, None, tq, LANES), lambda p, kk, i: (p, kk // per_fwd_tile, i, 0))],
        out_specs=[pl.BlockSpec((s, 2 * HEAD_BLK), lambda p, kk, i: (0, p), pipeline_mode=pl.Buffered(1)),
                   pl.BlockSpec((tk, 2 * HEAD_BLK), lambda p, kk, i: (kk, p)),
                   pl.BlockSpec((tk, 2 * HEAD_BLK), lambda p, kk, i: (kk, p))],
        out_shape=[jax.ShapeDtypeStruct((s, w), F32)] * 3,
        compiler_params=_params(("parallel", "arbitrary", "arbitrary"), vmem=ATT_BWD_VMEM),
    )(qcat, kcat, vcat, do, o, lse, pst, mt)


def _pool_counts(n_rows, first_row, s, w):
    pos = first_row + lax.broadcasted_iota(jnp.int32, (n_rows, 1), 0)
    lo = jnp.clip(pos - w // 2, 0, s)
    hi = jnp.clip(pos + w - w // 2, 0, s)
    return jnp.maximum(hi - lo, 1).astype(F32)


def _window_sum(e, levels, mirrored):
    n = e.shape[0]
    acc = e + pltpu.roll(e, (n - 1) if mirrored else 1, 0)
    step = 1
    for _ in range(levels - 1):
        acc = pltpu.roll(acc, step, 0) + pltpu.roll(acc, n - step, 0)
        step *= 2
    return acc


def _pooled(ue, s, t):
    first = pl.program_id(0) * t
    outs = []
    for gi, w in enumerate(POOL_WINDOWS):
        eg = ue[:, gi * POOL_GROUP:(gi + 1) * POOL_GROUP]
        sm = _window_sum(eg, gi + 1, False)[HALO:HALO + t]
        outs.append(sm / _pool_counts(t, first, s, w) - eg[HALO:HALO + t])
    return outs


def odd_mix_fwd(proj, o, pool_w, pool_scale, rows):
    def fn(o_, gc, up, u, un, gd, pw, ps):
        pooled = _pooled(_ext(up, u, un, rows.n), rows.s, rows.t)
        lin = jnp.concatenate([_mm(pooled[g], pw[g]) for g in range(len(POOL_WINDOWS))], axis=1)
        return jnp.concatenate([o_ * _silu(gc), lin * ps * _silu(gd)], axis=1)

    w = POOL_WIDTH
    return rows_call("od_mix_fwd", rows, fn, [o, proj, proj, proj, proj, proj, pool_w, pool_scale],
                     [rows.tile(w), rows.tile(w, O_GC)] + rows.halo(w, O_UD)
                     + [rows.tile(w, O_GD), Rows.full((4, POOL_GROUP, POOL_GROUP)), Rows.full((1, w))],
                     [(2 * w, MATMUL_DTYPE)])[0]


def odd_mix_bwd(proj, o, pool_w, pool_scale, dycd, rows):
    w = POOL_WIDTH
    ng = len(POOL_WINDOWS)

    def fn(o_, gc, up, u, un, gdp, gd, gdn, pw, ps, dyc, dydp, dyd, dydn):
        n, t, s = rows.n, rows.t, rows.s
        dyc = dyc.astype(F32)
        do = dyc * _silu(gc)
        dgc = dyc * o_ * _dsilu(gc)
        pooled = _pooled(_ext(up, u, un, n), s, t)
        lin = jnp.concatenate([_mm(pooled[g], pw[g]) for g in range(ng)], axis=1)
        dydc = dyd.astype(F32)
        dgd = dydc * lin * ps * _dsilu(gd)
        dps = _rowsum(dydc * lin * _silu(gd))
        dlin_e = _ext(dydp, dyd, dydn, n) * ps * _silu(_ext(gdp, gd, gdn, n))
        first = pl.program_id(0) * t - HALO
        dus, dpws = [], []
        for g, win in enumerate(POOL_WINDOWS):
            sl = slice(g * POOL_GROUP, (g + 1) * POOL_GROUP)
            dle = dlin_e[:, sl]
            dpws.append(_mm(pooled[g], dle[HALO:HALO + t], _TN))
            dpe = _mm(dle, pw[g], _NT)
            gce = dpe / _pool_counts(t + 2 * HALO, first, s, win)
            dus.append(_window_sum(gce, g + 1, True)[HALO:HALO + t] - dpe[HALO:HALO + t])
        return do, dgc, jnp.concatenate(dus, axis=1), dgd, jnp.stack(dpws), dps

    return rows_call("od_mix_bwd", rows, fn,
                     [o, proj, proj, proj, proj, proj, proj, proj, pool_w, pool_scale, dycd, dycd, dycd, dycd],
                     [rows.tile(w), rows.tile(w, O_GC)] + rows.halo(w, O_UD) + rows.halo(w, O_GD)
                     + [Rows.full((ng, POOL_GROUP, POOL_GROUP)), Rows.full((1, w)), rows.tile(w, 0)]
                     + rows.halo(w, w),
                     [(w, F32)] * 4, [(ng, POOL_GROUP, POOL_GROUP), (1, w)])


def pad_odd_w_in(w):
    kr = w[:, 384:416]
    blk = _head_lanes(jnp.zeros((w.shape[0], MLA_NOPE), w.dtype), kr[:, :ROPE_HALF], kr[:, ROPE_HALF:])
    return jnp.concatenate([w[:, :384], blk, w[:, 416:]], axis=1)


def unpad_odd_w_in_t(wpt):
    return jnp.concatenate([wpt[:384], wpt[O_KR + ROPE_A:O_KR + ROPE_A + ROPE_HALF],
                            wpt[O_KR + ROPE_B:O_KR + ROPE_B + ROPE_HALF], wpt[O_GC:]], axis=0)


def pad_w_uq(w):
    w3 = w.reshape(MLA_Q_RANK, MLA_HEADS, MLA_NOPE + MLA_ROPE)
    blk = _head_lanes(w3[..., :MLA_NOPE], w3[..., MLA_NOPE:MLA_NOPE + ROPE_HALF], w3[..., MLA_NOPE + ROPE_HALF:])
    return blk.reshape(MLA_Q_RANK, MLA_HEADS * HEAD_BLK)


def unpad_w_uq(wp):
    parts = _head_unlanes(wp.reshape(MLA_Q_RANK, MLA_HEADS, HEAD_BLK))
    return jnp.concatenate(parts, axis=-1).reshape(MLA_Q_RANK, -1)


def pad_w_ukv(w):
    w3 = w.reshape(MLA_KV_RANK, MLA_HEADS, MLA_NOPE + MLA_V)
    z = jnp.zeros(w3.shape[:-1] + (ROPE_HALF,), w.dtype)
    kp = _head_lanes(w3[..., :MLA_NOPE], z, z)
    return jnp.concatenate([kp.reshape(MLA_KV_RANK, -1), w3[..., MLA_NOPE:].reshape(MLA_KV_RANK, -1)], axis=1)


def unpad_w_ukv(wp):
    kp = _head_unlanes(wp[:, :MLA_HEADS * HEAD_BLK].reshape(MLA_KV_RANK, MLA_HEADS, HEAD_BLK))[0]
    vp = wp[:, MLA_HEADS * HEAD_BLK:].reshape(MLA_KV_RANK, MLA_HEADS, MLA_V)
    return jnp.concatenate([kp, vp], axis=-1).reshape(MLA_KV_RANK, -1)


def odd_weights(od_w_in, od_q_norm_g, od_w_uq, od_kv_norm_g, od_w_ukv, od_pool_w, od_pool_scale, od_w_out,
                od_ln_g, od_ln_b):
    freq, sign = rope_rows()
    return dict(w_in_p=pad_odd_w_in(od_w_in).astype(MATMUL_DTYPE), gq=od_q_norm_g, gkv=od_kv_norm_g,
                w_uq_p=pad_w_uq(od_w_uq).astype(MATMUL_DTYPE), w_ukv_p=pad_w_ukv(od_w_ukv).astype(MATMUL_DTYPE),
                pool_w=od_pool_w, pool_scale=od_pool_scale, w_out=od_w_out.astype(MATMUL_DTYPE), ln_g=od_ln_g,
                ln_b=od_ln_b, freq=freq, sign=sign)


def odd_layer_loss(x, pos, target, w, rows):
    s = rows.s
    proj = matmul(x, w["w_in_p"], "nn", F32, "od_proj")
    cqn, ckvn = latent_norm_fwd(proj, w["gq"], w["gkv"], rows)
    qp = matmul(cqn, w["w_uq_p"], "nn", F32, "od_q_up")
    kvp = matmul(ckvn, w["w_ukv_p"], "nn", F32, "od_kv_up")
    qcat, kcat, v = rope_fwd(qp, kvp, proj, pos, w["freq"], w["sign"], rows)
    o, lse, pst, mt = attention_fwd(qcat, kcat, v, s)
    ycd = odd_mix_fwd(proj, o, w["pool_w"], w["pool_scale"], rows)
    h = matmul(ycd, w["w_out"], "nn", F32, "od_out")
    dres, dh, dln_g, dln_b, loss_lanes = final_ln_loss(x, h, w["ln_g"], w["ln_b"], target, rows)
    dycd = matmul(dh, w["w_out"], "nt", F32, "od_out_dx")
    dw_out = matmul(ycd, dh, "tn", F32, "od_out_dw")
    do, dgc, dud, dgd, dpool_w, dpool_scale = odd_mix_bwd(proj, o, w["pool_w"], w["pool_scale"], dycd, rows)
    dq, dk, dv = attention_bwd(qcat, kcat, v, o, lse, pst, mt, do, s)
    dqp, dkvp, dkr = rope_bwd(dq, dk, dv, pos, w["freq"], w["sign"], rows)
    dcqn = matmul(dqp, w["w_uq_p"], "nt", F32, "od_q_up_dx")
    dw_uq = unpad_w_uq(matmul(cqn, dqp, "tn", F32, "od_q_up_dw"))
    dckvn = matmul(dkvp, w["w_ukv_p"], "nt", F32, "od_kv_up_dx")
    dw_ukv = unpad_w_ukv(matmul(ckvn, dkvp, "tn", F32, "od_kv_up_dw"))
    dcq, dckv, dgq, dgkv = latent_norm_bwd(proj, w["gq"], w["gkv"], dcqn, dckvn, rows)
    dproj = jnp.concatenate([dcq, dckv, dkr, dgc, dud, dgd], axis=1).astype(MATMUL_DTYPE)
    dx = matmul(dproj, w["w_in_p"], "nt", F32, "od_proj_dx", add=dres)
    dw_in = unpad_odd_w_in_t(matmul(dproj, x, "tn", F32, "od_proj_dw"))
    g = dict(od_w_in=dw_in, od_q_norm_g=dgq, od_w_uq=dw_uq, od_kv_norm_g=dgkv, od_w_ukv=dw_ukv,
             od_pool_w=dpool_w, od_pool_scale=dpool_scale, od_w_out=dw_out, od_ln_g=dln_g, od_ln_b=dln_b)
    return loss_lanes, dx, g


_MESH = pl.DeviceIdType.MESH
_ANY = pl.BlockSpec(memory_space=pl.ANY)
N_CHIPS = 4


def _push_call(name, ins, out_shapes, plan, n_remote, n_local):
    n_in, n_out = len(ins), len(out_shapes)

    def body(*refs):
        in_refs, out_refs = refs[:n_in], refs[n_in:n_in + n_out]
        send_sems, recv_sems, local_sems = refs[n_in + n_out:]
        x, y, c = lax.axis_index("x"), lax.axis_index("y"), lax.axis_index("c")
        remote, local = plan(in_refs, out_refs, x, y, c)
        assert len(remote) == n_remote and len(local) == n_local
        sends = [pltpu.make_async_remote_copy(src_ref=s, dst_ref=d, send_sem=send_sems.at[k], recv_sem=recv_sems.at[k],
                                              device_id=dev, device_id_type=_MESH)
                 for k, (s, d, dev, _) in enumerate(remote)]
        recvs = [pltpu.make_async_remote_copy(src_ref=s, dst_ref=land, send_sem=send_sems.at[k],
                                              recv_sem=recv_sems.at[k], device_id=dev, device_id_type=_MESH)
                 for k, (s, _, dev, land) in enumerate(remote)]
        locs = [pltpu.make_async_copy(s, d, local_sems.at[k]) for k, (s, d) in enumerate(local)]
        for cp in sends + locs:
            cp.start()
        for cp in recvs:
            cp.wait_recv()
        for cp in sends:
            cp.wait_send()
        for cp in locs:
            cp.wait()

    return pl.pallas_call(
        body, name=name, in_specs=[_ANY] * n_in, out_specs=[_ANY] * n_out, out_shape=list(out_shapes),
        scratch_shapes=[pltpu.SemaphoreType.DMA((n_remote,)), pltpu.SemaphoreType.DMA((n_remote,)),
                        pltpu.SemaphoreType.DMA((max(n_local, 1),))],
    )(*ins)


def _other_chips(x, y):
    return [(1 - x, y), (x, 1 - y), (1 - x, 1 - y)]


def chips_allgather(bufs):
    def plan(in_refs, out_refs, x, y, c):
        me = 2 * x + y
        remote, local = [], []
        for src, out in zip(in_refs, out_refs):
            for (px, py) in _other_chips(x, y):
                remote.append((src, out.at[me], (px, py, c), out.at[2 * px + py]))
            local.append((src, out.at[me]))
        return remote, local

    shapes = [jax.ShapeDtypeStruct((N_CHIPS,) + b.shape, b.dtype) for b in bufs]
    return _push_call("weights_allgather", bufs, shapes, plan, 3 * len(bufs), len(bufs))


def sibling_send(buf, name):
    def plan(in_refs, out_refs, x, y, c):
        return [(in_refs[0], out_refs[0], (x, y, 1 - c), out_refs[0])], []

    return _push_call(name, [buf], [jax.ShapeDtypeStruct(buf.shape, buf.dtype)], plan, 1, 0)[0]


def chips_scatter(buf):
    def plan(in_refs, out_refs, x, y, c):
        me = 2 * x + y
        src, out = in_refs[0], out_refs[0]
        remote = [(src.at[2 * px + py], out.at[me], (px, py, c), out.at[2 * px + py]) for (px, py) in _other_chips(x, y)]
        return remote, [(src.at[me], out.at[me])]

    return _push_call("grads_scatter", [buf], [jax.ShapeDtypeStruct(buf.shape, buf.dtype)], plan, 3, 1)[0]


PACK_W = 1024
PACK_BLK = 128


def _ew_call(name, fn, ins, in_specs, out_shape, out_spec, n_out, steps):
    def body(*refs):
        res = fn(*[r[...] for r in refs[:len(ins)]])
        if not isinstance(res, (tuple, list)):
            res = (res,)
        for r, v in zip(refs[len(ins):], res):
            r[...] = v

    return pl.pallas_call(
        body, name=name, grid=(steps,), in_specs=in_specs, out_specs=[out_spec] * n_out,
        out_shape=[jax.ShapeDtypeStruct(out_shape, F32)] * n_out,
        compiler_params=_params(("parallel",)),
    )(*ins)


def add2(a, b, name):
    n, r, w = a.shape
    blk = pl.BlockSpec((PACK_BLK, w), lambda i: (i, 0))
    out = _ew_call(name, lambda u, v: u + v, [a.reshape(n * r, w), b.reshape(n * r, w)], [blk, blk], (n * r, w), blk,
                   1, n * r // PACK_BLK)[0]
    return out.reshape(a.shape)


def sum_chips(buf):
    _, r, w = buf.shape
    specs = [pl.BlockSpec((None, PACK_BLK, w), lambda i, q=q: (q, i, 0)) for q in range(N_CHIPS)]
    return _ew_call("grads_sum", lambda a, b, c, d: ((a + b) + c) + d, [buf] * N_CHIPS, specs, (r, w),
                    pl.BlockSpec((PACK_BLK, w), lambda i: (i, 0)), 1, r // PACK_BLK)[0]


def adamw(w, g, m, v, name):
    def fn(w_, g_, m_, v_):
        m2 = ADAM_B1 * m_ + (1.0 - ADAM_B1) * g_
        v2 = ADAM_B2 * v_ + (1.0 - ADAM_B2) * jnp.square(g_)
        m_hat = m2 / (1.0 - ADAM_B1 ** ADAM_STEP)
        v_hat = v2 / (1.0 - ADAM_B2 ** ADAM_STEP)
        return -ADAM_LR * (m_hat / (jnp.sqrt(v_hat) + ADAM_EPS) + ADAM_WD * w_), m2, v2

    r, c = w.shape
    br = r if r <= 512 else _pick(r, (256, 128, 64, 32, 16, 8))
    blk = pl.BlockSpec((br, c), lambda i: (i, 0))
    return _ew_call(name, fn, [w, g, m, v], [blk] * 4, (r, c), blk, 3, r // br)


WEIGHTS = (
    ("ev_w_in", (1, 1024, 7200), 2), ("ev_conv_w", (1, 4, 2048), 2), ("ev_conv_b", (1, 2048), None),
    ("ev_a_log", (1, 2, 16), None), ("ev_dt_bias", (1, 2, 16), None), ("ev_d_skip", (1, 2, 16), None),
    ("ev_norm_g", (1, 1024), None), ("ev_sc_conv_w", (1, 3, 1024), 2), ("ev_w_out", (1, 2048, 1024), 1),
    ("ev_ln_g", (1, 1024), None), ("ev_ln_b", (1, 1024), None), ("od_w_in", (1, 1024, 1952), 2),
    ("od_q_norm_g", (1, 256), 1), ("od_w_uq", (1, 256, 768), 2), ("od_kv_norm_g", (1, 128), None),
    ("od_w_ukv", (1, 128, 1024), 2), ("od_pool_w", (1, 4, 128, 128), None), ("od_pool_scale", (1, 512), 1),
    ("od_w_out", (1, 1024, 1024), 1), ("od_ln_g", (1, 1024), 1), ("od_ln_b", (1, 1024), 1),
)
BIG = ("ev_w_in", "ev_w_out", "od_w_in", "od_w_uq", "od_w_ukv", "od_w_out")


def _block_shape(shape, axis):
    if axis is None:
        return tuple(shape)
    return tuple(d // N_CHIPS if i == axis else d for i, d in enumerate(shape))


def _pack(arrs, quantum):
    flat = jnp.concatenate([a.reshape(-1) for a in arrs])
    n = flat.shape[0]
    padded = -(-n // quantum) * quantum
    return jnp.concatenate([flat, jnp.zeros((padded - n,), flat.dtype)]).reshape(-1, LANES)


def _unpack(flat, shapes):
    out, off = [], 0
    for sh in shapes:
        n = int(np.prod(sh))
        out.append(flat[off:off + n].reshape(sh))
        off += n
    return out


def gather_weights(local):
    sharded = [(n, sh, ax) for (n, sh, ax) in WEIGHTS if ax is not None]
    big = [(n, sh, ax) for (n, sh, ax) in sharded if n in BIG]
    small = [(n, sh, ax) for (n, sh, ax) in sharded if n not in BIG]
    pb = _pack([local[n].astype(MATMUL_DTYPE) for n, _, _ in big], 2 * SUBLANES * LANES)
    ps = _pack([local[n] for n, _, _ in small], SUBLANES * LANES)
    gb, gs = chips_allgather([pb, ps])
    full = {n: local[n] for (n, sh, ax) in WEIGHTS if ax is None}
    for group, g in ((big, gb), (small, gs)):
        parts = [_unpack(g[q].reshape(-1), [_block_shape(sh, ax) for _, sh, ax in group]) for q in range(N_CHIPS)]
        for i, (n, sh, ax) in enumerate(group):
            full[n] = jnp.concatenate([parts[q][i] for q in range(N_CHIPS)], axis=ax)
    return full


WIDE = (("ev_w_in", True), ("ev_w_out", False), ("od_w_in", True), ("od_w_out", False))


def reduce_and_update(grads, local_w, local_m, local_v):
    c = lax.axis_index("c")
    wide_names = [n for n, _ in WIDE]
    tail = [(n, sh, ax) for (n, sh, ax) in WEIGHTS if n not in wide_names]
    tail_blocks = [_block_shape(sh, ax) for _, sh, ax in tail]
    n_tail = sum(int(np.prod(b)) for b in tail_blocks)
    wide_rows = [grads[n].shape[0] // N_CHIPS for n in wide_names]
    quantum = 2 * PACK_BLK
    total = -(-(sum(wide_rows) + -(-n_tail // PACK_W)) // quantum) * quantum
    tail_rows = total - sum(wide_rows)

    def tail_pack(pieces):
        flat = jnp.concatenate([p.reshape(-1) for p in pieces] + [jnp.zeros((tail_rows * PACK_W - n_tail,), F32)])
        return flat.reshape(tail_rows, PACK_W)

    parts = [grads[n].reshape(N_CHIPS, r, PACK_W) for n, r in zip(wide_names, wide_rows)]
    cols = []
    for (n, sh, ax), bs in zip(tail, tail_blocks):
        g = grads[n].reshape(sh)
        if ax is None:
            cols.append(jnp.broadcast_to(g.reshape(1, -1), (N_CHIPS, g.size)))
        else:
            g = g.reshape(sh[:ax] + (N_CHIPS, bs[ax]) + sh[ax + 1:])
            cols.append(jnp.moveaxis(g, ax, 0).reshape(N_CHIPS, -1))
    cols.append(jnp.zeros((N_CHIPS, tail_rows * PACK_W - n_tail), F32))
    packs = jnp.concatenate(parts + [jnp.concatenate(cols, axis=1).reshape(N_CHIPS, tail_rows, PACK_W)], axis=1)
    rh = total // 2
    keep = lax.dynamic_slice_in_dim(packs, c * rh, rh, axis=1)
    give = lax.dynamic_slice_in_dim(packs, (1 - c) * rh, rh, axis=1)
    chip_half = add2(keep, sibling_send(give, "grads_to_sibling"), "grads_chip_sum")
    total_half = sum_chips(chips_scatter(chip_half))
    other_half = sibling_send(total_half, "grads_from_sibling")
    g_pack = jnp.concatenate([jnp.where(c == 0, total_half, other_half),
                              jnp.where(c == 0, other_half, total_half)], axis=0)
    outs = ({}, {}, {}, {})
    off = 0
    for (n, transposed), r in zip(WIDE, wide_rows):
        g = g_pack[off:off + r]
        off += r
        g = g.T if transposed else g
        shape = local_w[n].shape
        res = adamw(local_w[n].reshape(g.shape), g, local_m[n].reshape(g.shape), local_v[n].reshape(g.shape),
                    "adamw_" + n)
        for d, a in zip(outs, (g, *res)):
            d[n] = a.reshape(shape)
    g_tail = g_pack[off:]
    res = adamw(*[tail_pack([d[n] for n, _, _ in tail]) if d is not None else g_tail
                  for d in (local_w, None, local_m, local_v)], "adamw_small")
    for d, a in zip(outs, (g_tail, *res)):
        d.update(zip([n for n, _, _ in tail], _unpack(a.reshape(-1), tail_blocks)))
    return outs


ROW_TILE = 256


def kernel(x, positions, ev_w_in, ev_conv_w, ev_conv_b, ev_a_log, ev_dt_bias, ev_d_skip, ev_norm_g, ev_sc_conv_w, ev_w_out, ev_ln_g, ev_ln_b, od_w_in, od_q_norm_g, od_w_uq, od_kv_norm_g, od_w_ukv, od_pool_w, od_pool_scale, od_w_out, od_ln_g, od_ln_b, loss_target, m_ev_w_in, m_ev_conv_w, m_ev_conv_b, m_ev_a_log, m_ev_dt_bias, m_ev_d_skip, m_ev_norm_g, m_ev_sc_conv_w, m_ev_w_out, m_ev_ln_g, m_ev_ln_b, m_od_w_in, m_od_q_norm_g, m_od_w_uq, m_od_kv_norm_g, m_od_w_ukv, m_od_pool_w, m_od_pool_scale, m_od_w_out, m_od_ln_g, m_od_ln_b, v_ev_w_in, v_ev_conv_w, v_ev_conv_b, v_ev_a_log, v_ev_dt_bias, v_ev_d_skip, v_ev_norm_g, v_ev_sc_conv_w, v_ev_w_out, v_ev_ln_g, v_ev_ln_b, v_od_w_in, v_od_q_norm_g, v_od_w_uq, v_od_kv_norm_g, v_od_w_ukv, v_od_pool_w, v_od_pool_scale, v_od_w_out, v_od_ln_g, v_od_ln_b):
    names = [n for n, _, _ in WEIGHTS]
    local_w = dict(zip(names, (ev_w_in, ev_conv_w, ev_conv_b, ev_a_log, ev_dt_bias, ev_d_skip, ev_norm_g, ev_sc_conv_w, ev_w_out, ev_ln_g, ev_ln_b, od_w_in, od_q_norm_g, od_w_uq, od_kv_norm_g, od_w_ukv, od_pool_w, od_pool_scale, od_w_out, od_ln_g, od_ln_b)))
    local_m = dict(zip(names, (m_ev_w_in, m_ev_conv_w, m_ev_conv_b, m_ev_a_log, m_ev_dt_bias, m_ev_d_skip, m_ev_norm_g, m_ev_sc_conv_w, m_ev_w_out, m_ev_ln_g, m_ev_ln_b, m_od_w_in, m_od_q_norm_g, m_od_w_uq, m_od_kv_norm_g, m_od_w_ukv, m_od_pool_w, m_od_pool_scale, m_od_w_out, m_od_ln_g, m_od_ln_b)))
    local_v = dict(zip(names, (v_ev_w_in, v_ev_conv_w, v_ev_conv_b, v_ev_a_log, v_ev_dt_bias, v_ev_d_skip, v_ev_norm_g, v_ev_sc_conv_w, v_ev_w_out, v_ev_ln_g, v_ev_ln_b, v_od_w_in, v_od_q_norm_g, v_od_w_uq, v_od_kv_norm_g, v_od_w_ukv, v_od_pool_w, v_od_pool_scale, v_od_w_out, v_od_ln_g, v_od_ln_b)))
    s = x.shape[1]
    rows = Rows(s, min(ROW_TILE, s))
    f = gather_weights(local_w)
    ew = even_weights(f["ev_w_in"][0], f["ev_conv_w"][0], f["ev_conv_b"], f["ev_a_log"][0], f["ev_dt_bias"][0],
                      f["ev_d_skip"][0], f["ev_norm_g"], f["ev_sc_conv_w"][0], f["ev_w_out"][0], f["ev_ln_g"],
                      f["ev_ln_b"])
    ow = odd_weights(f["od_w_in"][0], f["od_q_norm_g"], f["od_w_uq"][0], f["od_kv_norm_g"], f["od_w_ukv"][0],
                     f["od_pool_w"][0], f["od_pool_scale"], f["od_w_out"][0], f["od_ln_g"], f["od_ln_b"])
    x1, saved = even_layer(x[0], ew, rows)
    loss_lanes, dx1, g_odd = odd_layer_loss(x1, positions.reshape(s, 1), loss_target[0], ow, rows)
    dx0, g_even = even_layer_bwd(dx1, ew, saved, rows)
    loss = lax.psum(jnp.sum(loss_lanes), ("x", "y", "c"))
    grad, delta, new_m, new_v = reduce_and_update({**g_even, **g_odd}, local_w, local_m, local_v)
    return (loss, dx0[None], *[grad[n] for n in names], *[delta[n] for n in names],
            *[new_m[n] for n in names], *[new_v[n] for n in names])
```
